```python
import jax, jax.numpy as jnp
from jax import lax
import numpy as np

D_MODEL = 1024
BATCH = 16
SEQ = 2048
DEPTH = 1

CONV_WIDTH = D_MODEL // 2
CONV_K = 3
NSA_HEADS = 8
NSA_KV_GROUPS = 2
NSA_HEAD_DIM = 64
NSA_Q_WIDTH = NSA_HEADS * NSA_HEAD_DIM
NSA_KV_WIDTH = NSA_KV_GROUPS * NSA_HEAD_DIM
MIX_WIDTH = CONV_WIDTH + NSA_Q_WIDTH
N_BRANCH = 3
CMP_BLOCK = 32
CMP_STRIDE = 16
SEL_BLOCK = 64
SEL_TOPN = 16
WINDOW = 512
Q_BLOCK = 64
SPLIT_SIZES = (CONV_WIDTH, CONV_WIDTH, CONV_WIDTH, NSA_Q_WIDTH,
               N_BRANCH * 2 * NSA_KV_WIDTH, N_BRANCH * NSA_HEADS)
IN_COLS = sum(SPLIT_SIZES)

MEM_TOKENS = 256
X_HEADS = 4
X_HEAD_DIM = D_MODEL // X_HEADS

PEER_HEADS = 8
PEER_NKEYS = 128
PEER_EXPERTS = PEER_NKEYS * PEER_NKEYS
PEER_QDIM = 256
PEER_TOPK = 16
PEER_TOKEN_CHUNK = 128

EPS = 1e-6

kernel_name = "hymba_conv_nsa_peer_layer"


def rms_norm(x, w):
    xf = x.astype(jnp.float32)
    y = xf * lax.rsqrt(jnp.mean(xf * xf, axis=-1, keepdims=True) + EPS)
    return (y * w.astype(jnp.float32)).astype(x.dtype)


def masked_softmax(s, mask):
    s = jnp.where(mask, s.astype(jnp.float32), -jnp.inf)
    m = jnp.max(s, axis=-1, keepdims=True)
    m = jnp.where(jnp.isfinite(m), m, 0.0)
    e = jnp.exp(s - m)
    d = jnp.sum(e, axis=-1, keepdims=True)
    return e / jnp.where(d > 0, d, 1.0)


def short_conv_mixer(b_gate, c_gate, h, conv_w, conv_b):
    u = c_gate * h
    y = lax.conv_general_dilated(u, conv_w[:, None, :], window_strides=(1,),
                                 padding=((CONV_K - 1, 0),),
                                 dimension_numbers=('NWC', 'WIO', 'NWC'),
                                 feature_group_count=CONV_WIDTH)
    return b_gate * (y + conv_b)


def compress_blocks(t, pe, w1, w2):
    B, S, G, dh = t.shape
    n_cmp = (S - CMP_BLOCK) // CMP_STRIDE + 1
    idx = np.arange(n_cmp)[:, None] * CMP_STRIDE + np.arange(CMP_BLOCK)[None, :]
    blk = t[:, idx] + pe[None, None, :, None, :]
    blk = jnp.moveaxis(blk, 3, 2).reshape(B, n_cmp, G, CMP_BLOCK * dh)
    hid = jax.nn.gelu(blk @ w1, approximate=False)
    return hid @ w2


def nsa_mixer(q, kv, gates, cmp_pe, cmp_w1, cmp_w2, q_norm_w, k_norm_w):
    B, S = q.shape[0], q.shape[1]
    G, R, dh = NSA_KV_GROUPS, NSA_HEADS // NSA_KV_GROUPS, NSA_HEAD_DIM
    n_sel = S // SEL_BLOCK
    n_top = min(SEL_TOPN, n_sel)
    n_qb = S // Q_BLOCK
    q = rms_norm(q.reshape(B, S, G, R, dh), q_norm_w) * dh ** -0.5
    gates = gates.reshape(B, S, G, R, N_BRANCH)
    k_c, v_c, k_s, v_s, k_w, v_w = [t.reshape(B, S, G, dh)
                                    for t in jnp.split(kv, 2 * N_BRANCH, axis=-1)]
    k_cmp = rms_norm(compress_blocks(k_c, cmp_pe[0], cmp_w1[0], cmp_w2[0]), k_norm_w[0])
    v_cmp = compress_blocks(v_c, cmp_pe[1], cmp_w1[1], cmp_w2[1])
    n_cmp = k_cmp.shape[1]
    cmp_start = np.arange(n_cmp) * CMP_STRIDE
    cmp_last = cmp_start + CMP_BLOCK - 1
    sel_start = np.arange(n_sel) * SEL_BLOCK
    sel_map = jnp.asarray(((cmp_start[:, None] < sel_start[None, :] + SEL_BLOCK) &
                           (cmp_start[:, None] + CMP_BLOCK > sel_start[None, :])).astype(np.float32))
    to_blocks = lambda t: t.reshape(B, n_sel, SEL_BLOCK, G, dh).transpose(0, 3, 1, 2, 4)
    k_sb = to_blocks(rms_norm(k_s, k_norm_w[1]))
    v_sb = to_blocks(v_s)
    pad = ((0, 0), (WINDOW, 0), (0, 0), (0, 0))
    k_wp = jnp.pad(rms_norm(k_w, k_norm_w[2]), pad)
    v_wp = jnp.pad(v_w, pad)
    bi = jnp.arange(B)[:, None, None, None]
    gi = jnp.arange(G)[None, :, None, None]
    blk_ids = jnp.arange(n_sel)

    def query_block(i):
        t0 = i * Q_BLOCK
        tq = t0 + jnp.arange(Q_BLOCK)
        qb = lax.dynamic_slice_in_dim(q, t0, Q_BLOCK, axis=1)
        gb = lax.dynamic_slice_in_dim(gates, t0, Q_BLOCK, axis=1)
        s_c = jnp.einsum('bqgrd,bngd->bgrqn', qb, k_cmp)
        p_c = masked_softmax(s_c, cmp_last[None, :] <= tq[:, None])
        o_c = jnp.einsum('bgrqn,bngd->bqgrd', p_c.astype(v_cmp.dtype), v_cmp)
        imp = jnp.einsum('bgrqn,nj->bgqj', p_c, sel_map)
        cur = (tq // SEL_BLOCK)[:, None]
        future = blk_ids[None, :] > cur
        forced = (blk_ids[None, :] == 0) | (blk_ids[None, :] == cur) | (blk_ids[None, :] == cur - 1)
        imp = jnp.where(future, -jnp.inf, jnp.where(forced, jnp.inf, imp))
        _, sel = lax.top_k(imp, n_top)
        ks = k_sb[bi, gi, sel]
        vs = v_sb[bi, gi, sel]
        s_s = jnp.einsum('bqgrd,bgqkld->bgrqkl', qb, ks).reshape(B, G, R, Q_BLOCK, n_top * SEL_BLOCK)
        pos = sel[..., None] * SEL_BLOCK + jnp.arange(SEL_BLOCK)
        m_s = (pos <= tq[None, None, :, None, None]).reshape(B, G, 1, Q_BLOCK, n_top * SEL_BLOCK)
        p_s = masked_softmax(s_s, m_s).reshape(B, G, R, Q_BLOCK, n_top, SEL_BLOCK)
        o_s = jnp.einsum('bgrqkl,bgqkld->bqgrd', p_s.astype(vs.dtype), vs)
        kw = lax.dynamic_slice_in_dim(k_wp, t0, WINDOW + Q_BLOCK, axis=1)
        vw = lax.dynamic_slice_in_dim(v_wp, t0, WINDOW + Q_BLOCK, axis=1)
        sk = t0 - WINDOW + jnp.arange(WINDOW + Q_BLOCK)
        diff = tq[:, None] - sk[None, :]
        m_w = (sk[None, :] >= 0) & (diff >= 0) & (diff < WINDOW)
        s_w = jnp.einsum('bqgrd,bkgd->bgrqk', qb, kw)
        p_w = masked_softmax(s_w, m_w)
        o_w = jnp.einsum('bgrqk,bkgd->bqgrd', p_w.astype(vw.dtype), vw)
        return o_c * gb[..., 0:1] + o_s * gb[..., 1:2] + o_w * gb[..., 2:3]

    out = lax.map(query_block, jnp.arange(n_qb))
    return out.transpose(1, 0, 2, 3, 4, 5).reshape(B, S, NSA_Q_WIDTH)


def memory_cross_attention(h, mem, mem_norm_w, wq, wk, wv, wo, q_norm_w, k_norm_w):
    B, S, D = h.shape
    M = mem.shape[1]
    q = rms_norm((h @ wq).reshape(B, S, X_HEADS, X_HEAD_DIM), q_norm_w) * X_HEAD_DIM ** -0.5
    mn = rms_norm(mem, mem_norm_w)
    k = rms_norm((mn @ wk).reshape(B, M, X_HEADS, X_HEAD_DIM), k_norm_w)
    v = (mn @ wv).reshape(B, M, X_HEADS, X_HEAD_DIM)
    s = jnp.einsum('bshd,bmhd->bhsm', q, k).astype(jnp.float32)
    p = jax.nn.softmax(s, axis=-1).astype(v.dtype)
    o = jnp.einsum('bhsm,bmhd->bshd', p, v).reshape(B, S, D)
    return o @ wo


def peer_ffn(h, wq, keys, w_down, w_up):
    B, S, D = h.shape
    K = PEER_TOPK
    q = (h @ wq).reshape(B, S, PEER_HEADS, 2, PEER_QDIM // 2)
    sc = jnp.einsum('bshcd,hcnd->bshcn', q, keys).astype(jnp.float32)
    top_s, top_i = lax.top_k(sc, K)
    cand_s = (top_s[..., 0, :, None] + top_s[..., 1, None, :]).reshape(B, S, PEER_HEADS, K * K)
    cand_i = (top_i[..., 0, :, None] * PEER_NKEYS + top_i[..., 1, None, :]).reshape(B, S, PEER_HEADS, K * K)
    best_s, best_j = lax.top_k(cand_s, K)
    experts = jnp.take_along_axis(cand_i, best_j, axis=-1)
    gates = jax.nn.softmax(best_s, axis=-1).astype(h.dtype)
    n_chunks = (B * S) // PEER_TOKEN_CHUNK
    hc = h.reshape(n_chunks, PEER_TOKEN_CHUNK, D)
    ec = experts.reshape(n_chunks, PEER_TOKEN_CHUNK, PEER_HEADS, K)
    gc = gates.reshape(n_chunks, PEER_TOKEN_CHUNK, PEER_HEADS, K)

    def chunk(args):
        xt, e, g = args
        u = w_down[e]
        a = jax.nn.gelu(jnp.einsum('td,thkd->thk', xt, u), approximate=False)
        v = w_up[e]
        return jnp.einsum('thk,thkd->td', g * a, v)

    return lax.map(chunk, (hc, ec, gc)).reshape(B, S, D)


def setup_inputs(seed: int = 0) -> dict:
    key = jax.random.key(seed)
    ks = jax.random.split(key, 28)
    L, D = DEPTH, D_MODEL
    nrm = lambda k, shape, scale: jax.random.normal(k, shape, jnp.float32) * scale
    gain = lambda k, shape: 1.0 + 0.01 * jax.random.normal(k, shape, jnp.float32)
    return {
        "x": nrm(ks[0], (BATCH, SEQ, D), 1.0),
        "mem": nrm(ks[1], (BATCH, MEM_TOKENS, D), 1.0),
        "mix_norm_w": gain(ks[2], (L, D)),
        "w_in": nrm(ks[3], (L, D, IN_COLS), D ** -0.5),
        "conv_w": nrm(ks[4], (L, CONV_K, CONV_WIDTH), CONV_K ** -0.5),
        "conv_b": nrm(ks[5], (L, CONV_WIDTH), 0.01),
        "cmp_pe": nrm(ks[6], (L, 2, CMP_BLOCK, NSA_HEAD_DIM), 0.1),
        "cmp_w1": nrm(ks[7], (L, 2, CMP_BLOCK * NSA_HEAD_DIM, NSA_HEAD_DIM), (CMP_BLOCK * NSA_HEAD_DIM) ** -0.5),
        "cmp_w2": nrm(ks[8], (L, 2, NSA_HEAD_DIM, NSA_HEAD_DIM), NSA_HEAD_DIM ** -0.5),
        "q_norm_w": gain(ks[9], (L, NSA_HEAD_DIM)),
        "k_norm_w": gain(ks[10], (L, N_BRANCH, NSA_HEAD_DIM)),
        "w_out": nrm(ks[11], (L, MIX_WIDTH, D), MIX_WIDTH ** -0.5),
        "xattn_norm_w": gain(ks[12], (L, D)),
        "mem_norm_w": gain(ks[13], (L, D)),
        "xq": nrm(ks[14], (L, D, D), D ** -0.5),
        "xk": nrm(ks[15], (L, D, D), D ** -0.5),
        "xv": nrm(ks[16], (L, D, D), D ** -0.5),
        "xo": nrm(ks[17], (L, D, D), D ** -0.5),
        "xq_norm_w": gain(ks[18], (L, X_HEAD_DIM)),
        "xk_norm_w": gain(ks[19], (L, X_HEAD_DIM)),
        "ffn_norm_w": gain(ks[20], (L, D)),
        "peer_wq": nrm(ks[21], (L, D, PEER_HEADS * PEER_QDIM), D ** -0.5),
        "peer_keys": nrm(ks[22], (L, PEER_HEADS, 2, PEER_NKEYS, PEER_QDIM // 2), (PEER_QDIM // 2) ** -0.5),
        "peer_down": nrm(ks[23], (L, PEER_EXPERTS, D), D ** -0.5),
        "peer_up": nrm(ks[24], (L, PEER_EXPERTS, D), PEER_HEADS ** -0.5),
    }


def reference(x, mem, mix_norm_w, w_in, conv_w, conv_b, cmp_pe, cmp_w1, cmp_w2, q_norm_w, k_norm_w,
              w_out, xattn_norm_w, mem_norm_w, xq, xk, xv, xo, xq_norm_w, xk_norm_w,
              ffn_norm_w, peer_wq, peer_keys, peer_down, peer_up):
    offsets = np.cumsum(SPLIT_SIZES)[:-1].tolist()
    for l in range(DEPTH):
        hn = rms_norm(x, mix_norm_w[l])
        b_g, c_g, h_c, q, kv, g_lin = jnp.split(hn @ w_in[l], offsets, axis=-1)
        y_conv = short_conv_mixer(b_g, c_g, h_c, conv_w[l], conv_b[l])
        y_nsa = nsa_mixer(q, kv, jax.nn.sigmoid(g_lin), cmp_pe[l], cmp_w1[l], cmp_w2[l],
                          q_norm_w[l], k_norm_w[l])
        x = x + jnp.concatenate([y_conv, y_nsa], axis=-1) @ w_out[l]
        x = x + memory_cross_attention(rms_norm(x, xattn_norm_w[l]), mem, mem_norm_w[l],
                                       xq[l], xk[l], xv[l], xo[l], xq_norm_w[l], xk_norm_w[l])
        x = x + peer_ffn(rms_norm(x, ffn_norm_w[l]), peer_wq[l], peer_keys[l], peer_down[l], peer_up[l])
    return x
```

```python
import functools

import jax
import jax.numpy as jnp
from jax import lax
from jax.experimental import pallas as pl
from jax.experimental.pallas import tpu as pltpu

F32 = jnp.float32
BF16 = jnp.bfloat16
I32 = jnp.int32

EPS = 1e-6
NEG = -1e30
LANES = 128
SUBLANES = 8

D_MODEL = 1024
CONV_WIDTH = 512
NSA_HEADS = 8
NSA_GROUPS = 2
NSA_REP = NSA_HEADS // NSA_GROUPS
NSA_DH = 64
CMP_BLOCK = 32
CMP_STRIDE = 16
SEL_BLOCK = 64
SEL_TOPN = 16
WINDOW = 512
X_HEADS = 4
X_DH = D_MODEL // X_HEADS
PEER_HEADS = 8
PEER_NKEYS = 128
PEER_HALF = 128
PEER_TOPK = 16
PEER_EXPERTS = PEER_NKEYS * PEER_NKEYS

NT_DIMS = (((1,), (1,)), ((), ()))


def _dot(a, b):
    return jnp.dot(a, b, preferred_element_type=F32)


def _dot_nt(a, b):
    return lax.dot_general(a, b, NT_DIMS, preferred_element_type=F32)


def _gelu(x):
    return 0.5 * x * (1.0 + lax.erf(x * (2.0 ** -0.5)))


def _rms(x, w):
    ms = jnp.mean(x * x, axis=-1, keepdims=True)
    return x * lax.rsqrt(ms + EPS) * w


def _cparams(sem, vmem_mb):
    return pltpu.CompilerParams(dimension_semantics=sem, vmem_limit_bytes=vmem_mb * 1024 * 1024)


C_CONV = 3 * CONV_WIDTH
C_Q = NSA_HEADS * LANES
C_KV = 6 * NSA_GROUPS * NSA_DH
C_G = NSA_GROUPS * LANES
C_ALL = C_CONV + C_Q + C_KV + C_G


def _in_proj_kernel(tiles_per_seq, x_ref, nw_ref, w_ref, cw_ref, cb_ref, qw_ref, kw_ref,
                    yconv_ref, q_ref, kvc_ref, kvn_ref, g_ref, carry_ref):
    tm = x_ref.shape[0]
    hn = _rms(x_ref[...], nw_ref[...]).astype(BF16)

    p = _dot(hn, w_ref[:, 0:C_CONV])
    b_g = p[:, 0:CONV_WIDTH]
    u = p[:, CONV_WIDTH:2 * CONV_WIDTH] * p[:, 2 * CONV_WIDTH:3 * CONV_WIDTH]

    @pl.when((pl.program_id(0) % tiles_per_seq) == 0)
    def _sequence_start():
        carry_ref[...] = jnp.zeros(carry_ref.shape, F32)

    prev = carry_ref[...]
    p1 = prev[SUBLANES - 1:SUBLANES, :]
    p2 = prev[SUBLANES - 2:SUBLANES - 1, :]
    row = lax.broadcasted_iota(I32, u.shape, 0)
    u1 = jnp.where(row == 0, p1, pltpu.roll(u, 1, axis=0))
    u2 = jnp.where(row == 0, p2, jnp.where(row == 1, p1, pltpu.roll(u, 2, axis=0)))
    carry_ref[...] = u[tm - SUBLANES:tm, :]
    cw = cw_ref[...]
    y = b_g * (cw[0:1, :] * u2 + cw[1:2, :] * u1 + cw[2:3, :] * u + cb_ref[...])
    yconv_ref[...] = y.astype(BF16)

    pq = _dot(hn, w_ref[:, C_CONV:C_CONV + C_Q])
    for h in range(NSA_HEADS):
        blk = pq[:, h * LANES:(h + 1) * LANES]
        ms = jnp.sum(blk * blk, axis=-1, keepdims=True) * (1.0 / NSA_DH)
        q_ref[:, h * LANES:(h + 1) * LANES] = (
            blk * lax.rsqrt(ms + EPS) * qw_ref[:, h * LANES:(h + 1) * LANES]).astype(BF16)

    pkv = _dot(hn, w_ref[:, C_CONV + C_Q:C_CONV + C_Q + C_KV])
    kvc_ref[...] = pkv[:, 0:2 * LANES]
    lane = lax.broadcasted_iota(I32, (tm, LANES), 1)
    lo = lane < NSA_DH
    for j, normed in enumerate((True, False, True, False)):
        blk = pkv[:, (2 + j) * LANES:(3 + j) * LANES]
        if normed:
            sq = blk * blk
            ms_lo = jnp.sum(jnp.where(lo, sq, 0.0), axis=-1, keepdims=True) * (1.0 / NSA_DH)
            ms_hi = jnp.sum(jnp.where(lo, 0.0, sq), axis=-1, keepdims=True) * (1.0 / NSA_DH)
            scale = jnp.where(lo, lax.rsqrt(ms_lo + EPS), lax.rsqrt(ms_hi + EPS))
            blk = blk * scale * kw_ref[j // 2:j // 2 + 1, :]
        kvn_ref[:, j * LANES:(j + 1) * LANES] = blk.astype(BF16)

    pg = _dot(hn, w_ref[:, C_CONV + C_Q + C_KV:C_ALL])
    g_ref[...] = jax.nn.sigmoid(pg)


def _in_proj(x2d, nw, w_all, cw, cb, qw, kw, seq, tm=256):
    T = x2d.shape[0]
    const = lambda i: (0, 0)
    tile = lambda i: (i, 0)
    return pl.pallas_call(
        functools.partial(_in_proj_kernel, seq // tm),
        grid=(T // tm,),
        in_specs=[
            pl.BlockSpec((tm, D_MODEL), tile),
            pl.BlockSpec((1, D_MODEL), const),
            pl.BlockSpec((D_MODEL, C_ALL), const),
            pl.BlockSpec((3, CONV_WIDTH), const),
            pl.BlockSpec((1, CONV_WIDTH), const),
            pl.BlockSpec((1, C_Q), const),
            pl.BlockSpec((2, LANES), const),
        ],
        out_specs=[
            pl.BlockSpec((tm, CONV_WIDTH), tile),
            pl.BlockSpec((tm, C_Q), tile),
            pl.BlockSpec((tm, 2 * LANES), tile),
            pl.BlockSpec((tm, 4 * LANES), tile),
            pl.BlockSpec((tm, C_G), tile),
        ],
        out_shape=[
            jax.ShapeDtypeStruct((T, CONV_WIDTH), BF16),
            jax.ShapeDtypeStruct((T, C_Q), BF16),
            jax.ShapeDtypeStruct((T, 2 * LANES), F32),
            jax.ShapeDtypeStruct((T, 4 * LANES), BF16),
            jax.ShapeDtypeStruct((T, C_G), F32),
        ],
        scratch_shapes=[pltpu.VMEM((SUBLANES, CONV_WIDTH), F32)],
        compiler_params=_cparams(("arbitrary",), 48),
    )(x2d, nw, w_all, cw, cb, qw, kw)


HALF_FEATS = CMP_STRIDE * NSA_DH


def _compress_kernel(kc_ref, pe_ref, w1_ref, w2_ref, knw_ref, out_ref):
    n_rows = kc_ref.shape[3]
    row = lax.broadcasted_iota(I32, (n_rows, LANES), 0)
    for kv in range(2):
        for g in range(NSA_GROUPS):
            kr = kc_ref[0, kv, g]
            a = (kr + pe_ref[kv, 0:1, :]).astype(BF16)
            b = (kr + pe_ref[kv, 1:2, :]).astype(BF16)
            ha = _dot(a, w1_ref[kv, 0:HALF_FEATS, :])
            hb = _dot(b, w1_ref[kv, HALF_FEATS:2 * HALF_FEATS, :])
            hid = _gelu(ha + pltpu.roll(hb, n_rows - 1, axis=0))
            out = _dot(hid.astype(BF16), w2_ref[kv, g])
            if kv == 0:
                ms = jnp.sum(out * out, axis=-1, keepdims=True) * (1.0 / NSA_DH)
                out = out * lax.rsqrt(ms + EPS) * knw_ref[g]
            out_ref[0, kv, g] = jnp.where(row < n_rows - 1, out, 0.0).astype(BF16)


def _compress(kc, pe, w1, w2, knw):
    B, n_rows = kc.shape[0], kc.shape[3]
    return pl.pallas_call(
        _compress_kernel,
        grid=(B,),
        in_specs=[
            pl.BlockSpec((1, 2, NSA_GROUPS, n_rows, HALF_FEATS), lambda b: (b, 0, 0, 0, 0)),
            pl.BlockSpec((2, 2, HALF_FEATS), lambda b: (0, 0, 0)),
            pl.BlockSpec((2, 2 * HALF_FEATS, LANES), lambda b: (0, 0, 0)),
            pl.BlockSpec((2, NSA_GROUPS, LANES, LANES), lambda b: (0, 0, 0, 0)),
            pl.BlockSpec((NSA_GROUPS, 1, LANES), lambda b: (0, 0, 0)),
        ],
        out_specs=pl.BlockSpec((1, 2, NSA_GROUPS, n_rows, LANES), lambda b: (b, 0, 0, 0, 0)),
        out_shape=jax.ShapeDtypeStruct((B, 2, NSA_GROUPS, n_rows, LANES), BF16),
        compiler_params=_cparams(("arbitrary",), 32),
    )(kc, pe, w1, w2, knw)


SEL_CHUNK = 256
WIN_CHUNK = 128


def _nsa_kernel(q_ref, kvn_ref, cmp_ref, g_ref, selmap_ref, expand_ref, y_ref, m_ref, l_ref, acc_ref):
    tq = q_ref.shape[1]
    rows = NSA_REP * tq
    i = pl.program_id(2)
    t0 = i * tq
    qb = q_ref[0]
    qs = jnp.concatenate([qb[:, r * LANES:(r + 1) * LANES] for r in range(NSA_REP)], axis=0)
    qpos = t0 + lax.broadcasted_iota(I32, (tq, 1), 0)

    n_cmp = cmp_ref.shape[3]
    s = _dot_nt(qs, cmp_ref[0, 0, 0])
    cmp_last = lax.broadcasted_iota(I32, (tq, n_cmp), 1) * CMP_STRIDE + (CMP_BLOCK - 1)
    vis_c = cmp_last <= qpos
    p_parts = []
    for r in range(NSA_REP):
        sr = jnp.where(vis_c, s[r * tq:(r + 1) * tq], NEG)
        m = jnp.max(sr, axis=-1, keepdims=True)
        m = jnp.where(m > 0.5 * NEG, m, 0.0)
        e = jnp.where(vis_c, jnp.exp(sr - m), 0.0)
        d = jnp.sum(e, axis=-1, keepdims=True)
        p_parts.append(e / jnp.where(d > 0.0, d, 1.0))
    p_c = jnp.concatenate(p_parts, axis=0)
    o_c = _dot(p_c.astype(BF16), cmp_ref[0, 1, 0])

    psum = p_parts[0] + p_parts[1] + p_parts[2] + p_parts[3]
    p_hi = psum.astype(BF16)
    p_lo = (psum - p_hi.astype(F32)).astype(BF16)
    imp = _dot_nt(selmap_ref[...], p_hi) + _dot_nt(selmap_ref[...], p_lo)
    n_sel = imp.shape[0]
    blk = lax.broadcasted_iota(I32, (n_sel, tq), 0)
    cur = jnp.right_shift(t0 + lax.broadcasted_iota(I32, (n_sel, tq), 1), 6)
    forced = (blk == 0) | (blk == cur) | (blk == cur - 1)
    imp = jnp.where(blk > cur, -jnp.inf, jnp.where(forced, jnp.inf, imp))
    rank = jnp.zeros((n_sel, tq), I32)
    for jp in range(n_sel):
        other = imp[jp:jp + 1, :]
        beats = (other > imp) | ((other == imp) & (blk > jp))
        rank = rank + beats.astype(I32)
    sel_t = (rank < min(SEL_TOPN, n_sel)).astype(F32)
    sel = sel_t.T.astype(BF16)

    def attend(k_lane, v_lane, c_lo, c_hi, chunk, visible):
        m_ref[...] = jnp.full(m_ref.shape, NEG, F32)
        l_ref[...] = jnp.zeros(l_ref.shape, F32)
        acc_ref[...] = jnp.zeros(acc_ref.shape, F32)

        def body(c, carry):
            start = pl.multiple_of(c * chunk, chunk)
            k = kvn_ref[0, pl.ds(start, chunk), k_lane:k_lane + LANES]
            v = kvn_ref[0, pl.ds(start, chunk), v_lane:v_lane + LANES]
            sc = _dot_nt(qs, k)
            kpos = start + lax.broadcasted_iota(I32, (tq, chunk), 1)
            vis = visible(c, kpos)
            sc = jnp.concatenate(
                [jnp.where(vis, sc[r * tq:(r + 1) * tq], NEG) for r in range(NSA_REP)], axis=0)
            m_old = m_ref[...]
            m_new = jnp.maximum(m_old, jnp.max(sc, axis=-1, keepdims=True))
            alpha = jnp.exp(m_old - m_new)
            p = jnp.exp(sc - m_new)
            l_ref[...] = alpha * l_ref[...] + jnp.sum(p, axis=-1, keepdims=True)
            acc_ref[...] = alpha * acc_ref[...] + _dot(p.astype(BF16), v)
            m_ref[...] = m_new
            return carry

        lax.fori_loop(c_lo, c_hi, body, 0)
        return acc_ref[...] / l_ref[...]

    def vis_sel(c, kpos):
        chosen = _dot(sel, expand_ref[c])
        return (chosen > 0.5) & (kpos <= qpos)

    o_s = attend(0, LANES, 0, (t0 + tq + SEL_CHUNK - 1) // SEL_CHUNK, SEL_CHUNK, vis_sel)

    def vis_win(c, kpos):
        diff = qpos - kpos
        return (diff >= 0) & (diff < WINDOW)

    w_lo = jnp.maximum(t0 - WINDOW, 0) // WIN_CHUNK
    o_w = attend(2 * LANES, 3 * LANES, w_lo, (t0 + tq) // WIN_CHUNK, WIN_CHUNK, vis_win)

    g = g_ref[0]
    for r in range(NSA_REP):
        sl = slice(r * tq, (r + 1) * tq)
        y = (o_c[sl] * g[:, 3 * r:3 * r + 1] + o_s[sl] * g[:, 3 * r + 1:3 * r + 2]
             + o_w[sl] * g[:, 3 * r + 2:3 * r + 3])
        y_ref[0, :, r * LANES:(r + 1) * LANES] = y.astype(BF16)


def _nsa(q, kvn, kvcmp, gates, selmap_t, expand, tq=128):
    B, S = q.shape[0], q.shape[1]
    n_sel = S // SEL_BLOCK
    rows = NSA_REP * tq
    return pl.pallas_call(
        _nsa_kernel,
        grid=(B, NSA_GROUPS, S // tq),
        in_specs=[
            pl.BlockSpec((1, tq, NSA_REP * LANES), lambda b, g, i: (b, i, g)),
            pl.BlockSpec((1, S, 4 * LANES), lambda b, g, i: (b, 0, 0)),
            pl.BlockSpec((1, 2, 1, kvcmp.shape[3], LANES), lambda b, g, i: (b, 0, g, 0, 0)),
            pl.BlockSpec((1, tq, LANES), lambda b, g, i: (b, i, g)),
            pl.BlockSpec((n_sel, kvcmp.shape[3]), lambda b, g, i: (0, 0)),
            pl.BlockSpec((S // SEL_CHUNK, n_sel, SEL_CHUNK), lambda b, g, i: (0, 0, 0)),
        ],
        out_specs=pl.BlockSpec((1, tq, NSA_REP * LANES), lambda b, g, i: (b, i, g)),
        out_shape=jax.ShapeDtypeStruct((B, S, C_Q), BF16),
        scratch_shapes=[
            pltpu.VMEM((rows, 1), F32),
            pltpu.VMEM((rows, 1), F32),
            pltpu.VMEM((rows, LANES), F32),
        ],
        compiler_params=_cparams(("arbitrary", "arbitrary", "arbitrary"), 48),
    )(q, kvn, kvcmp, gates, selmap_t, expand)


def _mem_kv_kernel(mem_ref, nw_ref, wk_ref, wv_ref, kw_ref, k_ref, v_ref):
    mn = _rms(mem_ref[0], nw_ref[...]).astype(BF16)
    k = _dot(mn, wk_ref[...])
    for h in range(X_HEADS):
        sl = slice(h * X_DH, (h + 1) * X_DH)
        k_ref[0, :, sl] = _rms(k[:, sl], kw_ref[...]).astype(BF16)
    v_ref[0] = _dot(mn, wv_ref[...]).astype(BF16)


def _mem_kv(mem, nw, wk, wv, kw):
    B, M, D = mem.shape
    blk = pl.BlockSpec((1, M, D), lambda b: (b, 0, 0))
    const = lambda b: (0, 0)
    return pl.pallas_call(
        _mem_kv_kernel,
        grid=(B,),
        in_specs=[blk, pl.BlockSpec((1, D), const), pl.BlockSpec((D, D), const),
                  pl.BlockSpec((D, D), const), pl.BlockSpec((1, X_DH), const)],
        out_specs=[blk, blk],
        out_shape=[jax.ShapeDtypeStruct((B, M, D), BF16)] * 2,
        compiler_params=_cparams(("arbitrary",), 32),
    )(mem, nw, wk, wv, kw)


def _post_mix_kernel(x_ref, yc_ref, yn_ref, woc_ref, won_ref, xnw_ref, xq_ref, xqw_ref, km_ref, vm_ref,
                     xo_ref, fnw_ref, pwq_ref, keys_ref, x2_ref, h3_ref, sc_ref):
    x1 = x_ref[...] + _dot(yc_ref[...], woc_ref[...]) + _dot(yn_ref[...], won_ref[...])

    h2 = _rms(x1, xnw_ref[...]).astype(BF16)
    qx = _dot(h2, xq_ref[...])
    heads = []
    for h in range(X_HEADS):
        sl = slice(h * X_DH, (h + 1) * X_DH)
        qh = _rms(qx[:, sl], xqw_ref[...]).astype(BF16)
        s = _dot_nt(qh, km_ref[0, :, sl])
        e = jnp.exp(s - jnp.max(s, axis=-1, keepdims=True))
        p = e / jnp.sum(e, axis=-1, keepdims=True)
        heads.append(_dot(p.astype(BF16), vm_ref[0, :, sl]))
    o = jnp.concatenate(heads, axis=1).astype(BF16)
    x2 = x1 + _dot(o, xo_ref[...])
    x2_ref[...] = x2

    h3 = _rms(x2, fnw_ref[...]).astype(BF16)
    h3_ref[...] = h3
    qp = _dot(h3, pwq_ref[...]).astype(BF16)
    for c in range(2 * PEER_HEADS):
        sl = slice(c * PEER_HALF, (c + 1) * PEER_HALF)
        sc_ref[:, sl] = _dot_nt(qp[:, sl], keys_ref[c])


def _post_mix(x2d, yconv, ynsa, woc, won, xnw, xq, xqw, kmem, vmem, xo, fnw, pwq, keys, seq, tm=256):
    T, D = x2d.shape
    M = kmem.shape[1]
    n_sc = 2 * PEER_HEADS * PEER_NKEYS
    tile = lambda i: (i, 0)
    const = lambda i: (0, 0)
    per_batch = lambda i: ((i * tm) // seq, 0, 0)
    return pl.pallas_call(
        _post_mix_kernel,
        grid=(T // tm,),
        in_specs=[
            pl.BlockSpec((tm, D), tile),
            pl.BlockSpec((tm, CONV_WIDTH), tile),
            pl.BlockSpec((tm, C_Q), tile),
            pl.BlockSpec((CONV_WIDTH, D), const),
            pl.BlockSpec((C_Q, D), const),
            pl.BlockSpec((1, D), const),
            pl.BlockSpec((D, D), const),
            pl.BlockSpec((1, X_DH), const),
            pl.BlockSpec((1, M, D), per_batch),
            pl.BlockSpec((1, M, D), per_batch),
            pl.BlockSpec((D, D), const),
            pl.BlockSpec((1, D), const),
            pl.BlockSpec((D, n_sc), const),
            pl.BlockSpec((2 * PEER_HEADS, PEER_NKEYS, PEER_HALF), lambda i: (0, 0, 0)),
        ],
        out_specs=[pl.BlockSpec((tm, D), tile), pl.BlockSpec((tm, D), tile), pl.BlockSpec((tm, n_sc), tile)],
        out_shape=[
            jax.ShapeDtypeStruct((T, D), F32),
            jax.ShapeDtypeStruct((T, D), BF16),
            jax.ShapeDtypeStruct((T, n_sc), F32),
        ],
        compiler_params=_cparams(("arbitrary",), 56),
    )(x2d, yconv, ynsa, woc, won, xnw, xq, xqw, kmem, vmem, xo, fnw, pwq, keys)


def _peer_topk_kernel(sc_ref, ei_ref, ej_ref, gate_ref, e_acc, s_acc, m_acc):
    tm = sc_ref.shape[0]
    K = PEER_TOPK
    h = pl.program_id(1)
    lane = lax.broadcasted_iota(I32, (tm, LANES), 1)
    lane_f = lane.astype(F32)
    grp = jnp.right_shift(lane, 4)
    pos = jnp.bitwise_and(lane, K - 1)

    def top1(s):
        m = jnp.max(s, axis=-1, keepdims=True)
        idx = jnp.min(jnp.where(s == m, lane_f, float(LANES)), axis=-1, keepdims=True)
        return m, idx, jnp.where(lane_f == idx, -jnp.inf, s)

    s1 = sc_ref[:, 0:LANES]
    s2 = sc_ref[:, LANES:2 * LANES]
    zero = jnp.zeros((tm, LANES), F32)
    c1 = [zero, zero]
    c2 = [zero, zero]
    ce = [zero, zero]
    for k in range(K):
        m1, i1, s1 = top1(s1)
        m2, i2, s2 = top1(s2)
        for half in range(2):
            in_row = grp == (k - half * (LANES // K))
            in_col = pos == k
            c1[half] = jnp.where(in_row, m1, c1[half])
            c2[half] = jnp.where(in_col, m2, c2[half])
            ce[half] = ce[half] + jnp.where(in_row, i1 * float(PEER_NKEYS), 0.0) + jnp.where(in_col, i2, 0.0)
    cand = [c1[0] + c2[0], c1[1] + c2[1]]

    @pl.when(h == 0)
    def _init():
        e_acc[...] = zero
        s_acc[...] = zero
        m_acc[...] = zero

    out_e = e_acc[...]
    out_s = s_acc[...]
    for k in range(K):
        m = jnp.max(jnp.maximum(cand[0], cand[1]), axis=-1, keepdims=True)
        idx = jnp.min(jnp.minimum(jnp.where(cand[0] == m, lane_f, 2.0 * LANES),
                                  jnp.where(cand[1] == m, lane_f + LANES, 2.0 * LANES)),
                      axis=-1, keepdims=True)
        hit0 = lane_f == idx
        hit1 = (lane_f + LANES) == idx
        e = jnp.sum(jnp.where(hit0, ce[0], 0.0) + jnp.where(hit1, ce[1], 0.0), axis=-1, keepdims=True)
        cand = [jnp.where(hit0, -jnp.inf, cand[0]), jnp.where(hit1, -jnp.inf, cand[1])]
        slot = lane == (h * K + k)
        out_e = jnp.where(slot, e, out_e)
        out_s = jnp.where(slot, m, out_s)
        if k == 0:
            m_acc[...] = jnp.where(grp == h, m, m_acc[...])
    e_acc[...] = out_e
    s_acc[...] = out_s

    @pl.when(h == PEER_HEADS - 1)
    def _finish():
        ex = jnp.exp(out_s - m_acc[...])
        den = jnp.zeros((tm, LANES), F32)
        for hh in range(PEER_HEADS):
            in_h = grp == hh
            den = jnp.where(in_h, jnp.sum(jnp.where(in_h, ex, 0.0), axis=-1, keepdims=True), den)
        gate_ref[...] = ex / den
        e_int = out_e.astype(I32)
        ei_ref[...] = jnp.right_shift(e_int, 7)
        ej_ref[...] = jnp.bitwise_and(e_int, PEER_NKEYS - 1)


def _peer_topk(sc, tm=128):
    T = sc.shape[0]
    tile = lambda i, h: (i, 0)
    return pl.pallas_call(
        _peer_topk_kernel,
        grid=(T // tm, PEER_HEADS),
        in_specs=[pl.BlockSpec((tm, 2 * LANES), lambda i, h: (i, h))],
        out_specs=[pl.BlockSpec((tm, LANES), tile)] * 3,
        out_shape=[jax.ShapeDtypeStruct((T, LANES), I32), jax.ShapeDtypeStruct((T, LANES), I32),
                   jax.ShapeDtypeStruct((T, LANES), F32)],
        scratch_shapes=[pltpu.VMEM((tm, LANES), F32)] * 3,
        compiler_params=_cparams(("arbitrary", "arbitrary"), 32),
    )(sc)


G_PITCH = PEER_NKEYS + SUBLANES


def _peer_dense_kernel(h_ref, ei_ref, ej_ref, gate_ref, wd_ref, wu_ref, x2_ref, out_ref, g_ref, acc_ref):
    tm = h_ref.shape[0]
    ec = wd_ref.shape[1]
    c = pl.program_id(1)

    @pl.when(c == 0)
    def _scatter():
        acc_ref[...] = jnp.zeros(acc_ref.shape, F32)
        sub = lax.broadcasted_iota(I32, (PEER_NKEYS, LANES), 0)

        def token(t, carry):
            ri = ei_ref[pl.ds(t, 1), :]
            rj = ej_ref[pl.ds(t, 1), :]
            rg = gate_ref[pl.ds(t, 1), :]
            c_t = jnp.where(sub == ri, rg, 0.0).astype(BF16)
            q_t = jnp.where(sub == rj, 1.0, 0.0).astype(BF16)
            g_ref[pl.ds(pl.multiple_of(t * G_PITCH, SUBLANES), PEER_NKEYS), :] = _dot_nt(c_t, q_t)
            return carry

        lax.fori_loop(0, tm, token, 0)

    a = _dot(h_ref[...], wd_ref[...])
    parts = []
    for k in range(ec // LANES):
        i = c * (ec // LANES) + k
        g_i = g_ref[pl.ds(i, tm, stride=G_PITCH), :]
        parts.append((_gelu(a[:, k * LANES:(k + 1) * LANES]) * g_i).astype(BF16))
    z = jnp.concatenate(parts, axis=1)
    acc_ref[...] += _dot(z, wu_ref[...])

    @pl.when(c == pl.num_programs(1) - 1)
    def _finish():
        out_ref[...] = x2_ref[...] + acc_ref[...]


def _peer_dense(h3, ei, ej, gate, wd_t, wu, x2, tm=256, ec=1024):
    T, D = h3.shape
    n_exp = wu.shape[0]
    tile = lambda t, c: (t, 0)
    return pl.pallas_call(
        _peer_dense_kernel,
        grid=(T // tm, n_exp // ec),
        in_specs=[
            pl.BlockSpec((tm, D), tile),
            pl.BlockSpec((tm, LANES), tile),
            pl.BlockSpec((tm, LANES), tile),
            pl.BlockSpec((tm, LANES), tile),
            pl.BlockSpec((D, ec), lambda t, c: (0, c)),
            pl.BlockSpec((ec, D), lambda t, c: (c, 0)),
            pl.BlockSpec((tm, D), tile),
        ],
        out_specs=pl.BlockSpec((tm, D), tile),
        out_shape=jax.ShapeDtypeStruct((T, D), F32),
        scratch_shapes=[pltpu.VMEM((tm * G_PITCH, LANES), F32), pltpu.VMEM((tm, D), F32)],
        compiler_params=_cparams(("arbitrary", "arbitrary"), 56),
    )(h3, ei, ej, gate, wd_t, wu, x2)


def _pad_half(a, g):
    z = jnp.zeros_like(a)
    return jnp.concatenate([a, z] if g == 0 else [z, a], axis=-1)


def kernel(x, mem, mix_norm_w, w_in, conv_w, conv_b, cmp_pe, cmp_w1, cmp_w2, q_norm_w, k_norm_w, w_out,
           xattn_norm_w, mem_norm_w, xq, xk, xv, xo, xq_norm_w, xk_norm_w, ffn_norm_w, peer_wq, peer_keys,
           peer_down, peer_up):
    B, S, D = x.shape
    T = B * S
    l = 0
    G, R, dh = NSA_GROUPS, NSA_REP, NSA_DH

    w = w_in[l]
    o_q = C_CONV
    o_kv = o_q + NSA_HEADS * dh
    o_g = o_kv + C_KV
    wq = w[:, o_q:o_kv].reshape(D, G, R, dh)
    wq_pad = jnp.concatenate([_pad_half(wq[:, g], g).reshape(D, R * LANES) for g in range(G)], axis=1)
    wg = w[:, o_g:].reshape(D, G, R * 3)
    wg_pad = jnp.pad(wg, ((0, 0), (0, 0), (0, LANES - R * 3))).reshape(D, G * LANES)
    w_all = jnp.concatenate([w[:, :o_q], wq_pad, w[:, o_kv:o_g], wg_pad], axis=1).astype(BF16)
    qw = q_norm_w[l] * dh ** -0.5
    qw_pad = jnp.concatenate([jnp.tile(_pad_half(qw, g), R) for g in range(G)]).reshape(1, C_Q)
    kw = jnp.stack([jnp.tile(k_norm_w[l, 1], G), jnp.tile(k_norm_w[l, 2], G)])

    yconv, q, kvc, kvn, gates = _in_proj(
        x.reshape(T, D), mix_norm_w[l].reshape(1, D), w_all, conv_w[l], conv_b[l].reshape(1, CONV_WIDTH),
        qw_pad, kw, S)

    n_rows = S // CMP_STRIDE
    kc = kvc.reshape(B, n_rows, CMP_STRIDE, 2, G, dh).transpose(0, 3, 4, 1, 2, 5).reshape(
        B, 2, G, n_rows, HALF_FEATS)
    pe = cmp_pe[l].reshape(2, 2, HALF_FEATS)
    w1 = jnp.pad(cmp_w1[l], ((0, 0), (0, 0), (0, LANES - dh))).astype(BF16)
    w2 = jnp.pad(cmp_w2[l], ((0, 0), (0, LANES - dh), (0, 0)))
    w2 = jnp.stack([_pad_half(w2, g) for g in range(G)], axis=1).astype(BF16)
    knw = jnp.stack([_pad_half(k_norm_w[l, 0], g) for g in range(G)]).reshape(G, 1, LANES)
    kvcmp = _compress(kc, pe, w1, w2, knw)

    n_sel = S // SEL_BLOCK
    cmp_start = jnp.arange(n_rows) * CMP_STRIDE
    sel_start = jnp.arange(n_sel) * SEL_BLOCK
    selmap_t = ((cmp_start[None, :] < sel_start[:, None] + SEL_BLOCK)
                & (cmp_start[None, :] + CMP_BLOCK > sel_start[:, None])
                & (jnp.arange(n_rows)[None, :] < n_rows - 1)).astype(BF16)
    key_blk = jnp.arange(S) // SEL_BLOCK
    expand = (jnp.arange(n_sel)[:, None] == key_blk[None, :]).astype(BF16)
    expand = expand.reshape(n_sel, S // SEL_CHUNK, SEL_CHUNK).transpose(1, 0, 2)
    ynsa = _nsa(q.reshape(B, S, C_Q), kvn.reshape(B, S, 4 * LANES), kvcmp, gates.reshape(B, S, C_G),
                selmap_t, expand)

    kmem, vmem = _mem_kv(mem, mem_norm_w[l].reshape(1, D), xk[l].astype(BF16), xv[l].astype(BF16),
                         xk_norm_w[l].reshape(1, X_DH))

    wo = w_out[l]
    won = wo[CONV_WIDTH:].reshape(G, R, dh, D)
    won_pad = jnp.concatenate(
        [jnp.concatenate([won[g], jnp.zeros_like(won[g])] if g == 0 else [jnp.zeros_like(won[g]), won[g]],
                         axis=1).reshape(R * LANES, D) for g in range(G)], axis=0).astype(BF16)
    keys = peer_keys[l].reshape(2 * PEER_HEADS, PEER_NKEYS, PEER_HALF).astype(BF16)
    x2, h3, sc = _post_mix(
        x.reshape(T, D), yconv, ynsa.reshape(T, C_Q), wo[:CONV_WIDTH].astype(BF16), won_pad,
        xattn_norm_w[l].reshape(1, D), xq[l].astype(BF16),
        (xq_norm_w[l] * X_DH ** -0.5).reshape(1, X_DH), kmem, vmem, xo[l].astype(BF16),
        ffn_norm_w[l].reshape(1, D), peer_wq[l].astype(BF16), keys, S)

    ei, ej, gate = _peer_topk(sc)
    out = _peer_dense(h3, ei, ej, gate, peer_down[l].T.astype(BF16), peer_up[l].astype(BF16), x2)
    return out.reshape(B, S, D)
```

```python
import functools

import jax
import jax.numpy as jnp
from jax import lax
from jax.experimental import pallas as pl
from jax.experimental.pallas import tpu as pltpu

F32 = jnp.float32
BF16 = jnp.bfloat16
I32 = jnp.int32

EPS = 1e-6
NEG = -1e30
LANES = 128
SUBLANES = 8

D_MODEL = 1024
CONV_WIDTH = 512
NSA_HEADS = 8
NSA_GROUPS = 2
NSA_REP = NSA_HEADS // NSA_GROUPS
NSA_DH = 64
CMP_BLOCK = 32
CMP_STRIDE = 16
SEL_BLOCK = 64
SEL_TOPN = 16
WINDOW = 512
X_HEADS = 4
X_DH = D_MODEL // X_HEADS
PEER_HEADS = 8
PEER_NKEYS = 128
PEER_HALF = 128
PEER_TOPK = 16
PEER_EXPERTS = PEER_NKEYS * PEER_NKEYS

NT_DIMS = (((1,), (1,)), ((), ()))


def _dot(a, b):
    return jnp.dot(a, b, preferred_element_type=F32)


def _dot_nt(a, b):
    return lax.dot_general(a, b, NT_DIMS, preferred_element_type=F32)


def _gelu(x):
    return 0.5 * x * (1.0 + lax.erf(x * (2.0 ** -0.5)))


def _rms(x, w):
    ms = jnp.mean(x * x, axis=-1, keepdims=True)
    return x * lax.rsqrt(ms + EPS) * w


def _cparams(sem, vmem_mb):
    return pltpu.CompilerParams(dimension_semantics=sem, vmem_limit_bytes=vmem_mb * 1024 * 1024)


C_CONV = 3 * CONV_WIDTH
C_Q = NSA_HEADS * LANES
C_KV = 6 * NSA_GROUPS * NSA_DH
C_G = NSA_GROUPS * LANES
C_ALL = C_CONV + C_Q + C_KV + C_G


def _in_proj_kernel(tiles_per_seq, x_ref, nw_ref, w_ref, cw_ref, cb_ref, qw_ref, kw_ref,
                    yconv_ref, q_ref, kvc_ref, kvn_ref, g_ref, carry_ref):
    tm = x_ref.shape[0]
    hn = _rms(x_ref[...], nw_ref[...]).astype(BF16)

    p = _dot(hn, w_ref[:, 0:C_CONV])
    b_g = p[:, 0:CONV_WIDTH]
    u = p[:, CONV_WIDTH:2 * CONV_WIDTH] * p[:, 2 * CONV_WIDTH:3 * CONV_WIDTH]

    @pl.when((pl.program_id(0) % tiles_per_seq) == 0)
    def _sequence_start():
        carry_ref[...] = jnp.zeros(carry_ref.shape, F32)

    prev = carry_ref[...]
    p1 = prev[SUBLANES - 1:SUBLANES, :]
    p2 = prev[SUBLANES - 2:SUBLANES - 1, :]
    row = lax.broadcasted_iota(I32, u.shape, 0)
    u1 = jnp.where(row == 0, p1, pltpu.roll(u, 1, axis=0))
    u2 = jnp.where(row == 0, p2, jnp.where(row == 1, p1, pltpu.roll(u, 2, axis=0)))
    carry_ref[...] = u[tm - SUBLANES:tm, :]
    cw = cw_ref[...]
    y = b_g * (cw[0:1, :] * u2 + cw[1:2, :] * u1 + cw[2:3, :] * u + cb_ref[...])
    yconv_ref[...] = y.astype(BF16)

    pq = _dot(hn, w_ref[:, C_CONV:C_CONV + C_Q])
    for h in range(NSA_HEADS):
        blk = pq[:, h * LANES:(h + 1) * LANES]
        ms = jnp.sum(blk * blk, axis=-1, keepdims=True) * (1.0 / NSA_DH)
        q_ref[:, h * LANES:(h + 1) * LANES] = (
            blk * lax.rsqrt(ms + EPS) * qw_ref[:, h * LANES:(h + 1) * LANES]).astype(BF16)

    pkv = _dot(hn, w_ref[:, C_CONV + C_Q:C_CONV + C_Q + C_KV])
    kvc_ref[...] = pkv[:, 0:2 * LANES]
    lane = lax.broadcasted_iota(I32, (tm, LANES), 1)
    lo = lane < NSA_DH
    for j, normed in enumerate((True, False, True, False)):
        blk = pkv[:, (2 + j) * LANES:(3 + j) * LANES]
        if normed:
            sq = blk * blk
            ms_lo = jnp.sum(jnp.where(lo, sq, 0.0), axis=-1, keepdims=True) * (1.0 / NSA_DH)
            ms_hi = jnp.sum(jnp.where(lo, 0.0, sq), axis=-1, keepdims=True) * (1.0 / NSA_DH)
            scale = jnp.where(lo, lax.rsqrt(ms_lo + EPS), lax.rsqrt(ms_hi + EPS))
            blk = blk * scale * kw_ref[j // 2:j // 2 + 1, :]
        kvn_ref[:, j * LANES:(j + 1) * LANES] = blk.astype(BF16)

    pg = _dot(hn, w_ref[:, C_CONV + C_Q + C_KV:C_ALL])
    g_ref[...] = jax.nn.sigmoid(pg)


def _in_proj(x2d, nw, w_all, cw, cb, qw, kw, seq, tm=256):
    T = x2d.shape[0]
    const = lambda i: (0, 0)
    tile = lambda i: (i, 0)
    return pl.pallas_call(
        functools.partial(_in_proj_kernel, seq // tm),
        grid=(T // tm,),
        in_specs=[
            pl.BlockSpec((tm, D_MODEL), tile),
            pl.BlockSpec((1, D_MODEL), const),
            pl.BlockSpec((D_MODEL, C_ALL), const),
            pl.BlockSpec((3, CONV_WIDTH), const),
            pl.BlockSpec((1, CONV_WIDTH), const),
            pl.BlockSpec((1, C_Q), const),
            pl.BlockSpec((2, LANES), const),
        ],
        out_specs=[
            pl.BlockSpec((tm, CONV_WIDTH), tile),
            pl.BlockSpec((tm, C_Q), tile),
            pl.BlockSpec((tm, 2 * LANES), tile),
            pl.BlockSpec((tm, 4 * LANES), tile),
            pl.BlockSpec((tm, C_G), tile),
        ],
        out_shape=[
            jax.ShapeDtypeStruct((T, CONV_WIDTH), BF16),
            jax.ShapeDtypeStruct((T, C_Q), BF16),
            jax.ShapeDtypeStruct((T, 2 * LANES), F32),
            jax.ShapeDtypeStruct((T, 4 * LANES), BF16),
            jax.ShapeDtypeStruct((T, C_G), F32),
        ],
        scratch_shapes=[pltpu.VMEM((SUBLANES, CONV_WIDTH), F32)],
        compiler_params=_cparams(("arbitrary",), 48),
    )(x2d, nw, w_all, cw, cb, qw, kw)


HALF_FEATS = CMP_STRIDE * NSA_DH


def _compress_kernel(kc_ref, pe_ref, w1_ref, w2_ref, knw_ref, out_ref):
    n_rows = kc_ref.shape[3]
    row = lax.broadcasted_iota(I32, (n_rows, LANES), 0)
    for kv in range(2):
        for g in range(NSA_GROUPS):
            kr = kc_ref[0, kv, g]
            a = (kr + pe_ref[kv, 0:1, :]).astype(BF16)
            b = (kr + pe_ref[kv, 1:2, :]).astype(BF16)
            ha = _dot(a, w1_ref[kv, 0:HALF_FEATS, :])
            hb = _dot(b, w1_ref[kv, HALF_FEATS:2 * HALF_FEATS, :])
            hid = _gelu(ha + pltpu.roll(hb, n_rows - 1, axis=0))
            out = _dot(hid.astype(BF16), w2_ref[kv, g])
            if kv == 0:
                ms = jnp.sum(out * out, axis=-1, keepdims=True) * (1.0 / NSA_DH)
                out = out * lax.rsqrt(ms + EPS) * knw_ref[g]
            out_ref[0, kv, g] = jnp.where(row < n_rows - 1, out, 0.0).astype(BF16)


def _compress(kc, pe, w1, w2, knw):
    B, n_rows = kc.shape[0], kc.shape[3]
    return pl.pallas_call(
        _compress_kernel,
        grid=(B,),
        in_specs=[
            pl.BlockSpec((1, 2, NSA_GROUPS, n_rows, HALF_FEATS), lambda b: (b, 0, 0, 0, 0)),
            pl.BlockSpec((2, 2, HALF_FEATS), lambda b: (0, 0, 0)),
            pl.BlockSpec((2, 2 * HALF_FEATS, LANES), lambda b: (0, 0, 0)),
            pl.BlockSpec((2, NSA_GROUPS, LANES, LANES), lambda b: (0, 0, 0, 0)),
            pl.BlockSpec((NSA_GROUPS, 1, LANES), lambda b: (0, 0, 0)),
        ],
        out_specs=pl.BlockSpec((1, 2, NSA_GROUPS, n_rows, LANES), lambda b: (b, 0, 0, 0, 0)),
        out_shape=jax.ShapeDtypeStruct((B, 2, NSA_GROUPS, n_rows, LANES), BF16),
        compiler_params=_cparams(("arbitrary",), 32),
    )(kc, pe, w1, w2, knw)


SEL_CHUNK = 256
WIN_CHUNK = 128


def _nsa_kernel(q_ref, kvn_ref, cmp_ref, g_ref, selmap_ref, expand_ref, y_ref, m_ref, l_ref, acc_ref):
    tq = q_ref.shape[1]
    rows = NSA_REP * tq
    i = pl.program_id(2)
    t0 = i * tq
    qb = q_ref[0]
    qs = jnp.concatenate([qb[:, r * LANES:(r + 1) * LANES] for r in range(NSA_REP)], axis=0)
    qpos = t0 + lax.broadcasted_iota(I32, (tq, 1), 0)

    n_cmp = cmp_ref.shape[3]
    s = _dot_nt(qs, cmp_ref[0, 0, 0])
    cmp_last = lax.broadcasted_iota(I32, (tq, n_cmp), 1) * CMP_STRIDE + (CMP_BLOCK - 1)
    vis_c = cmp_last <= qpos
    p_parts = []
    for r in range(NSA_REP):
        sr = jnp.where(vis_c, s[r * tq:(r + 1) * tq], NEG)
        m = jnp.max(sr, axis=-1, keepdims=True)
        m = jnp.where(m > 0.5 * NEG, m, 0.0)
        e = jnp.where(vis_c, jnp.exp(sr - m), 0.0)
        d = jnp.sum(e, axis=-1, keepdims=True)
        p_parts.append(e / jnp.where(d > 0.0, d, 1.0))
    p_c = jnp.concatenate(p_parts, axis=0)
    o_c = _dot(p_c.astype(BF16), cmp_ref[0, 1, 0])

    psum = p_parts[0] + p_parts[1] + p_parts[2] + p_parts[3]
    p_hi = psum.astype(BF16)
    p_lo = (psum - p_hi.astype(F32)).astype(BF16)
    imp = _dot_nt(selmap_ref[...], p_hi) + _dot_nt(selmap_ref[...], p_lo)
    n_sel = imp.shape[0]
    blk = lax.broadcasted_iota(I32, (n_sel, tq), 0)
    cur = jnp.right_shift(t0 + lax.broadcasted_iota(I32, (n_sel, tq), 1), 6)
    forced = (blk == 0) | (blk == cur) | (blk == cur - 1)
    imp = jnp.where(blk > cur, -jnp.inf, jnp.where(forced, jnp.inf, imp))
    rank = jnp.zeros((n_sel, tq), I32)
    for jp in range(n_sel):
        other = imp[jp:jp + 1, :]
        beats = (other > imp) | ((other == imp) & (blk > jp))
        rank = rank + beats.astype(I32)
    sel_t = (rank < min(SEL_TOPN, n_sel)).astype(F32)
    sel = sel_t.T.astype(BF16)

    def attend(k_lane, v_lane, c_lo, c_hi, chunk, visible):
        m_ref[...] = jnp.full(m_ref.shape, NEG, F32)
        l_ref[...] = jnp.zeros(l_ref.shape, F32)
        acc_ref[...] = jnp.zeros(acc_ref.shape, F32)

        def body(c, carry):
            start = pl.multiple_of(c * chunk, chunk)
            k = kvn_ref[0, pl.ds(start, chunk), k_lane:k_lane + LANES]
            v = kvn_ref[0, pl.ds(start, chunk), v_lane:v_lane + LANES]
            sc = _dot_nt(qs, k)
            kpos = start + lax.broadcasted_iota(I32, (tq, chunk), 1)
            vis = visible(c, kpos)
            sc = jnp.concatenate(
                [jnp.where(vis, sc[r * tq:(r + 1) * tq], NEG) for r in range(NSA_REP)], axis=0)
            m_old = m_ref[...]
            m_new = jnp.maximum(m_old, jnp.max(sc, axis=-1, keepdims=True))
            alpha = jnp.exp(m_old - m_new)
            p = jnp.exp(sc - m_new)
            l_ref[...] = alpha * l_ref[...] + jnp.sum(p, axis=-1, keepdims=True)
            acc_ref[...] = alpha * acc_ref[...] + _dot(p.astype(BF16), v)
            m_ref[...] = m_new
            return carry

        lax.fori_loop(c_lo, c_hi, body, 0)
        return acc_ref[...] / l_ref[...]

    def vis_sel(c, kpos):
        chosen = _dot(sel, expand_ref[c])
        return (chosen > 0.5) & (kpos <= qpos)

    o_s = attend(0, LANES, 0, (t0 + tq + SEL_CHUNK - 1) // SEL_CHUNK, SEL_CHUNK, vis_sel)

    def vis_win(c, kpos):
        diff = qpos - kpos
        return (diff >= 0) & (diff < WINDOW)

    w_lo = jnp.maximum(t0 - WINDOW, 0) // WIN_CHUNK
    o_w = attend(2 * LANES, 3 * LANES, w_lo, (t0 + tq) // WIN_CHUNK, WIN_CHUNK, vis_win)

    g = g_ref[0]
    for r in range(NSA_REP):
        sl = slice(r * tq, (r + 1) * tq)
        y = (o_c[sl] * g[:, 3 * r:3 * r + 1] + o_s[sl] * g[:, 3 * r + 1:3 * r + 2]
             + o_w[sl] * g[:, 3 * r + 2:3 * r + 3])
        y_ref[0, :, r * LANES:(r + 1) * LANES] = y.astype(BF16)


def _nsa(q, kvn, kvcmp, gates, selmap_t, expand, tq=128):
    B, S = q.shape[0], q.shape[1]
    n_sel = S // SEL_BLOCK
    rows = NSA_REP * tq
    return pl.pallas_call(
        _nsa_kernel,
        grid=(B, NSA_GROUPS, S // tq),
        in_specs=[
            pl.BlockSpec((1, tq, NSA_REP * LANES), lambda b, g, i: (b, i, g)),
            pl.BlockSpec((1, S, 4 * LANES), lambda b, g, i: (b, 0, 0)),
            pl.BlockSpec((1, 2, 1, kvcmp.shape[3], LANES), lambda b, g, i: (b, 0, g, 0, 0)),
            pl.BlockSpec((1, tq, LANES), lambda b, g, i: (b, i, g)),
            pl.BlockSpec((n_sel, kvcmp.shape[3]), lambda b, g, i: (0, 0)),
            pl.BlockSpec((S // SEL_CHUNK, n_sel, SEL_CHUNK), lambda b, g, i: (0, 0, 0)),
        ],
        out_specs=pl.BlockSpec((1, tq, NSA_REP * LANES), lambda b, g, i: (b, i, g)),
        out_shape=jax.ShapeDtypeStruct((B, S, C_Q), BF16),
        scratch_shapes=[
            pltpu.VMEM((rows, 1), F32),
            pltpu.VMEM((rows, 1), F32),
            pltpu.VMEM((rows, LANES), F32),
        ],
        compiler_params=_cparams(("arbitrary", "arbitrary", "arbitrary"), 48),
    )(q, kvn, kvcmp, gates, selmap_t, expand)


def _mem_kv_kernel(mem_ref, nw_ref, wk_ref, wv_ref, kw_ref, k_ref, v_ref):
    mn = _rms(mem_ref[0], nw_ref[...]).astype(BF16)
    k = _dot(mn, wk_ref[...])
    for h in range(X_HEADS):
        sl = slice(h * X_DH, (h + 1) * X_DH)
        k_ref[0, :, sl] = _rms(k[:, sl], kw_ref[...]).astype(BF16)
    v_ref[0] = _dot(mn, wv_ref[...]).astype(BF16)


def _mem_kv(mem, nw, wk, wv, kw):
    B, M, D = mem.shape
    blk = pl.BlockSpec((1, M, D), lambda b: (b, 0, 0))
    const = lambda b: (0, 0)
    return pl.pallas_call(
        _mem_kv_kernel,
        grid=(B,),
        in_specs=[blk, pl.BlockSpec((1, D), const), pl.BlockSpec((D, D), const),
                  pl.BlockSpec((D, D), const), pl.BlockSpec((1, X_DH), const)],
        out_specs=[blk, blk],
        out_shape=[jax.ShapeDtypeStruct((B, M, D), BF16)] * 2,
        compiler_params=_cparams(("arbitrary",), 32),
    )(mem, nw, wk, wv, kw)


def _post_mix_kernel(x_ref, yc_ref, yn_ref, woc_ref, won_ref, xnw_ref, xq_ref, xqw_ref, km_ref, vm_ref,
                     xo_ref, fnw_ref, pwq_ref, keys_ref, x2_ref, h3_ref, sc_ref):
    x1 = x_ref[...] + _dot(yc_ref[...], woc_ref[...]) + _dot(yn_ref[...], won_ref[...])

    h2 = _rms(x1, xnw_ref[...]).astype(BF16)
    qx = _dot(h2, xq_ref[...])
    heads = []
    for h in range(X_HEADS):
        sl = slice(h * X_DH, (h + 1) * X_DH)
        qh = _rms(qx[:, sl], xqw_ref[...]).astype(BF16)
        s = _dot_nt(qh, km_ref[0, :, sl])
        e = jnp.exp(s - jnp.max(s, axis=-1, keepdims=True))
        p = e / jnp.sum(e, axis=-1, keepdims=True)
        heads.append(_dot(p.astype(BF16), vm_ref[0, :, sl]))
    o = jnp.concatenate(heads, axis=1).astype(BF16)
    x2 = x1 + _dot(o, xo_ref[...])
    x2_ref[...] = x2

    h3 = _rms(x2, fnw_ref[...]).astype(BF16)
    h3_ref[...] = h3
    qp = _dot(h3, pwq_ref[...]).astype(BF16)
    for c in range(2 * PEER_HEADS):
        sl = slice(c * PEER_HALF, (c + 1) * PEER_HALF)
        for j in range(qp.shape[0] // LANES):
            sc_ref[c, j] = _dot_nt(keys_ref[c], qp[j * LANES:(j + 1) * LANES, sl])


def _post_mix(x2d, yconv, ynsa, woc, won, xnw, xq, xqw, kmem, vmem, xo, fnw, pwq, keys, seq, tm=256):
    T, D = x2d.shape
    M = kmem.shape[1]
    n_sc = 2 * PEER_HEADS * PEER_NKEYS
    tile = lambda i: (i, 0)
    const = lambda i: (0, 0)
    per_batch = lambda i: ((i * tm) // seq, 0, 0)
    return pl.pallas_call(
        _post_mix_kernel,
        grid=(T // tm,),
        in_specs=[
            pl.BlockSpec((tm, D), tile),
            pl.BlockSpec((tm, CONV_WIDTH), tile),
            pl.BlockSpec((tm, C_Q), tile),
            pl.BlockSpec((CONV_WIDTH, D), const),
            pl.BlockSpec((C_Q, D), const),
            pl.BlockSpec((1, D), const),
            pl.BlockSpec((D, D), const),
            pl.BlockSpec((1, X_DH), const),
            pl.BlockSpec((1, M, D), per_batch),
            pl.BlockSpec((1, M, D), per_batch),
            pl.BlockSpec((D, D), const),
            pl.BlockSpec((1, D), const),
            pl.BlockSpec((D, n_sc), const),
            pl.BlockSpec((2 * PEER_HEADS, PEER_NKEYS, PEER_HALF), lambda i: (0, 0, 0)),
        ],
        out_specs=[pl.BlockSpec((tm, D), tile), pl.BlockSpec((tm, D), tile),
                   pl.BlockSpec((2 * PEER_HEADS, tm // LANES, PEER_NKEYS, LANES), lambda i: (0, i, 0, 0))],
        out_shape=[
            jax.ShapeDtypeStruct((T, D), F32),
            jax.ShapeDtypeStruct((T, D), BF16),
            jax.ShapeDtypeStruct((2 * PEER_HEADS, T // LANES, PEER_NKEYS, LANES), F32),
        ],
        compiler_params=_cparams(("arbitrary",), 56),
    )(x2d, yconv, ynsa, woc, won, xnw, xq, xqw, kmem, vmem, xo, fnw, pwq, keys)


STAIR = [(a, b) for a in range(PEER_TOPK) for b in range(PEER_TOPK // (a + 1))]
N_STAIR_VREGS = -(-len(STAIR) // SUBLANES)


def _stair_vregs(axis, k):
    return sorted({r // SUBLANES for r, ab in enumerate(STAIR) if ab[axis] == k})


def _peer_topk_kernel(sc_ref, ta_ref, tb_ref, ei_ref, ej_ref, gate_ref, s_out, e_out):
    n_grp = sc_ref.shape[1]
    K = PEER_TOPK
    NV = N_STAIR_VREGS
    kidx = lax.broadcasted_iota(I32, (PEER_NKEYS, LANES), 0).astype(F32)
    ridx = [(lax.broadcasted_iota(I32, (SUBLANES, LANES), 0) + SUBLANES * j).astype(F32) for j in range(NV)]
    ta = [ta_ref[SUBLANES * j:SUBLANES * (j + 1), :] for j in range(NV)]
    tb = [tb_ref[SUBLANES * j:SUBLANES * (j + 1), :] for j in range(NV)]
    zero = jnp.zeros((SUBLANES, LANES), F32)

    def top1(s):
        m = jnp.max(s, axis=0, keepdims=True)
        idx = jnp.min(jnp.where(s == m, kidx, float(PEER_NKEYS)), axis=0, keepdims=True)
        return m, idx, jnp.where(kidx == idx, -jnp.inf, s)

    def sorted_top(s, table, axis):
        val = [zero] * NV
        key = [zero] * NV
        for k in range(K):
            m, idx, s = top1(s)
            for j in _stair_vregs(axis, k):
                hit = table[j] == float(k)
                val[j] = jnp.where(hit, m, val[j])
                key[j] = jnp.where(hit, idx, key[j])
        return val, key

    def head(h, g):
        v1, k1 = sorted_top(sc_ref[2 * h, g], ta, 0)
        v2, k2 = sorted_top(sc_ref[2 * h + 1, g], tb, 1)
        cand = [jnp.where(ta[j] >= 0.0, v1[j] + v2[j], -jnp.inf) for j in range(NV)]
        ce = [k1[j] * float(PEER_NKEYS) + k2[j] for j in range(NV)]
        for k in range(K):
            mx = cand[0]
            for j in range(1, NV):
                mx = jnp.maximum(mx, cand[j])
            m = jnp.max(mx, axis=0, keepdims=True)
            ix = jnp.where(cand[0] == m, ridx[0], float(NV * SUBLANES))
            for j in range(1, NV):
                ix = jnp.minimum(ix, jnp.where(cand[j] == m, ridx[j], float(NV * SUBLANES)))
            idx = jnp.min(ix, axis=0, keepdims=True)
            hits = [ridx[j] == idx for j in range(NV)]
            es = jnp.where(hits[0], ce[0], 0.0)
            for j in range(1, NV):
                es = es + jnp.where(hits[j], ce[j], 0.0)
            cand = [jnp.where(hits[j], -jnp.inf, cand[j]) for j in range(NV)]
            s_out[pl.ds(h * K + k, 1), :] = m
            e_out[pl.ds(h * K + k, 1), :] = jnp.sum(es, axis=0, keepdims=True)

    def group(g, carry):
        def head_pair(u, c):
            head(2 * u, g)
            head(2 * u + 1, g)
            return c

        lax.fori_loop(0, PEER_HEADS // 2, head_pair, 0)
        s_all = s_out[...]
        gates = []
        for h in range(PEER_HEADS):
            sh = s_all[h * K:(h + 1) * K]
            ex = jnp.exp(sh - sh[0:1])
            gates.append(ex / jnp.sum(ex, axis=0, keepdims=True))
        rows = pl.ds(pl.multiple_of(g * LANES, LANES), LANES)
        gate_ref[rows, :] = jnp.concatenate(gates, axis=0).T
        e_int = e_out[...].T.astype(I32)
        ei_ref[rows, :] = jnp.right_shift(e_int, 7)
        ej_ref[rows, :] = jnp.bitwise_and(e_int, PEER_NKEYS - 1)
        return carry

    lax.fori_loop(0, n_grp, group, 0)


def _peer_topk(sc, n_grp=4):
    T = sc.shape[1] * LANES
    tk = n_grp * LANES
    tile = lambda i: (i, 0)
    rows = N_STAIR_VREGS * SUBLANES
    pad = rows - len(STAIR)
    ta = jnp.asarray([float(a) for a, _ in STAIR] + [-1.0] * pad, F32)
    tb = jnp.asarray([float(b) for _, b in STAIR] + [-1.0] * pad, F32)
    ta = jnp.broadcast_to(ta[:, None], (rows, LANES))
    tb = jnp.broadcast_to(tb[:, None], (rows, LANES))
    return pl.pallas_call(
        _peer_topk_kernel,
        grid=(T // tk,),
        in_specs=[pl.BlockSpec((2 * PEER_HEADS, n_grp, PEER_NKEYS, LANES), lambda i: (0, i, 0, 0)),
                  pl.BlockSpec((rows, LANES), lambda i: (0, 0)),
                  pl.BlockSpec((rows, LANES), lambda i: (0, 0))],
        out_specs=[pl.BlockSpec((tk, LANES), tile)] * 3,
        out_shape=[jax.ShapeDtypeStruct((T, LANES), I32), jax.ShapeDtypeStruct((T, LANES), I32),
                   jax.ShapeDtypeStruct((T, LANES), F32)],
        scratch_shapes=[pltpu.VMEM((PEER_HEADS * PEER_TOPK, LANES), F32)] * 2,
        compiler_params=_cparams(("arbitrary",), 40),
    )(sc, ta, tb)


G_ROWS = PEER_NKEYS // 2
G_PITCH = G_ROWS + SUBLANES
HI_MASK = 0xFFFF0000


def _peer_dense_kernel(h_ref, ei_ref, ej_ref, gate_ref, wd_ref, wu_ref, x2_ref, out_ref, g_ref):
    tm = h_ref.shape[0]
    ec = wd_ref.shape[1]
    ipc = ec // LANES
    half = ipc // 2
    half_bits = half.bit_length() - 1
    c = pl.program_id(1)

    @pl.when(c == 0)
    def _scatter():
        out_ref[...] = x2_ref[...]
        m = lax.broadcasted_iota(I32, (PEER_NKEYS, 2 * LANES), 0)
        r = jnp.bitwise_and(m, G_ROWS - 1)
        i_of_m = (jnp.right_shift(r, half_bits) * ipc + jnp.bitwise_and(r, half - 1)
                  + jnp.right_shift(m, 6) * half)
        n = lax.broadcasted_iota(I32, (2 * LANES, 2 * LANES), 0)

        def pair(u, carry):
            t = 2 * u
            ri = jnp.concatenate([ei_ref[pl.ds(t, 1), :], ei_ref[pl.ds(t + 1, 1), :]], axis=1)
            rj = jnp.concatenate([ej_ref[pl.ds(t, 1), :], ej_ref[pl.ds(t + 1, 1), :] + LANES], axis=1)
            rg = jnp.concatenate([gate_ref[pl.ds(t, 1), :], gate_ref[pl.ds(t + 1, 1), :]], axis=1)
            c_t = jnp.where(i_of_m == ri, rg, 0.0).astype(BF16)
            q_t = jnp.where(n == rj, 1.0, 0.0).astype(BF16)
            g2 = _dot_nt(c_t, q_t)
            for tok in range(2):
                g = g2[:, tok * LANES:(tok + 1) * LANES].astype(BF16).astype(F32)
                bits = lax.bitcast_convert_type(g, jnp.uint32)
                word = jnp.bitwise_or(jnp.right_shift(bits[0:G_ROWS], 16),
                                      jnp.bitwise_and(bits[G_ROWS:2 * G_ROWS], jnp.uint32(HI_MASK)))
                g_ref[pl.ds(pl.multiple_of((t + tok) * G_PITCH, SUBLANES), G_ROWS), :] = word
            return carry

        lax.fori_loop(0, tm // 2, pair, 0, unroll=8)

    a = _dot(h_ref[...], wd_ref[...])
    parts = [None] * ipc
    for k in range(half):
        word = g_ref[pl.ds(c * half + k, tm, stride=G_PITCH), :]
        g_lo = lax.bitcast_convert_type(jnp.left_shift(word, 16), F32)
        g_hi = lax.bitcast_convert_type(jnp.bitwise_and(word, jnp.uint32(HI_MASK)), F32)
        parts[k] = (_gelu(a[:, k * LANES:(k + 1) * LANES]) * g_lo).astype(BF16)
        parts[k + half] = (_gelu(a[:, (k + half) * LANES:(k + half + 1) * LANES]) * g_hi).astype(BF16)
    z = jnp.concatenate(parts, axis=1)
    out_ref[...] += _dot(z, wu_ref[...])


def _peer_dense(h3, ei, ej, gate, wd_t, wu, x2, tm=512, ec=1024):
    T, D = h3.shape
    tm = min(tm, T)
    n_exp = wu.shape[0]
    tile = lambda t, c: (t, 0)
    return pl.pallas_call(
        _peer_dense_kernel,
        grid=(T // tm, n_exp // ec),
        in_specs=[
            pl.BlockSpec((tm, D), tile),
            pl.BlockSpec((tm, LANES), tile),
            pl.BlockSpec((tm, LANES), tile),
            pl.BlockSpec((tm, LANES), tile),
            pl.BlockSpec((D, ec), lambda t, c: (0, c)),
            pl.BlockSpec((ec, D), lambda t, c: (c, 0)),
            pl.BlockSpec((tm, D), tile),
        ],
        out_specs=pl.BlockSpec((tm, D), tile),
        out_shape=jax.ShapeDtypeStruct((T, D), F32),
        scratch_shapes=[pltpu.VMEM((tm * G_PITCH, LANES), jnp.uint32)],
        compiler_params=_cparams(("arbitrary", "arbitrary"), 58),
    )(h3, ei, ej, gate, wd_t, wu, x2)


def _pad_half(a, g):
    z = jnp.zeros_like(a)
    return jnp.concatenate([a, z] if g == 0 else [z, a], axis=-1)


def kernel(x, mem, mix_norm_w, w_in, conv_w, conv_b, cmp_pe, cmp_w1, cmp_w2, q_norm_w, k_norm_w, w_out,
           xattn_norm_w, mem_norm_w, xq, xk, xv, xo, xq_norm_w, xk_norm_w, ffn_norm_w, peer_wq, peer_keys,
           peer_down, peer_up):
    B, S, D = x.shape
    T = B * S
    l = 0
    G, R, dh = NSA_GROUPS, NSA_REP, NSA_DH

    w = w_in[l]
    o_q = C_CONV
    o_kv = o_q + NSA_HEADS * dh
    o_g = o_kv + C_KV
    wq = w[:, o_q:o_kv].reshape(D, G, R, dh)
    wq_pad = jnp.concatenate([_pad_half(wq[:, g], g).reshape(D, R * LANES) for g in range(G)], axis=1)
    wg = w[:, o_g:].reshape(D, G, R * 3)
    wg_pad = jnp.pad(wg, ((0, 0), (0, 0), (0, LANES - R * 3))).reshape(D, G * LANES)
    w_all = jnp.concatenate([w[:, :o_q], wq_pad, w[:, o_kv:o_g], wg_pad], axis=1).astype(BF16)
    qw = q_norm_w[l] * dh ** -0.5
    qw_pad = jnp.concatenate([jnp.tile(_pad_half(qw, g), R) for g in range(G)]).reshape(1, C_Q)
    kw = jnp.stack([jnp.tile(k_norm_w[l, 1], G), jnp.tile(k_norm_w[l, 2], G)])

    yconv, q, kvc, kvn, gates = _in_proj(
        x.reshape(T, D), mix_norm_w[l].reshape(1, D), w_all, conv_w[l], conv_b[l].reshape(1, CONV_WIDTH),
        qw_pad, kw, S)

    n_rows = S // CMP_STRIDE
    kc = kvc.reshape(B, n_rows, CMP_STRIDE, 2, G, dh).transpose(0, 3, 4, 1, 2, 5).reshape(
        B, 2, G, n_rows, HALF_FEATS)
    pe = cmp_pe[l].reshape(2, 2, HALF_FEATS)
    w1 = jnp.pad(cmp_w1[l], ((0, 0), (0, 0), (0, LANES - dh))).astype(BF16)
    w2 = jnp.pad(cmp_w2[l], ((0, 0), (0, LANES - dh), (0, 0)))
    w2 = jnp.stack([_pad_half(w2, g) for g in range(G)], axis=1).astype(BF16)
    knw = jnp.stack([_pad_half(k_norm_w[l, 0], g) for g in range(G)]).reshape(G, 1, LANES)
    kvcmp = _compress(kc, pe, w1, w2, knw)

    n_sel = S // SEL_BLOCK
    cmp_start = jnp.arange(n_rows) * CMP_STRIDE
    sel_start = jnp.arange(n_sel) * SEL_BLOCK
    selmap_t = ((cmp_start[None, :] < sel_start[:, None] + SEL_BLOCK)
                & (cmp_start[None, :] + CMP_BLOCK > sel_start[:, None])
                & (jnp.arange(n_rows)[None, :] < n_rows - 1)).astype(BF16)
    key_blk = jnp.arange(S) // SEL_BLOCK
    expand = (jnp.arange(n_sel)[:, None] == key_blk[None, :]).astype(BF16)
    expand = expand.reshape(n_sel, S // SEL_CHUNK, SEL_CHUNK).transpose(1, 0, 2)
    ynsa = _nsa(q.reshape(B, S, C_Q), kvn.reshape(B, S, 4 * LANES), kvcmp, gates.reshape(B, S, C_G),
                selmap_t, expand)

    kmem, vmem = _mem_kv(mem, mem_norm_w[l].reshape(1, D), xk[l].astype(BF16), xv[l].astype(BF16),
                         xk_norm_w[l].reshape(1, X_DH))

    wo = w_out[l]
    won = wo[CONV_WIDTH:].reshape(G, R, dh, D)
    won_pad = jnp.concatenate(
        [jnp.concatenate([won[g], jnp.zeros_like(won[g])] if g == 0 else [jnp.zeros_like(won[g]), won[g]],
                         axis=1).reshape(R * LANES, D) for g in range(G)], axis=0).astype(BF16)
    keys = peer_keys[l].reshape(2 * PEER_HEADS, PEER_NKEYS, PEER_HALF).astype(BF16)
    x2, h3, sc = _post_mix(
        x.reshape(T, D), yconv, ynsa.reshape(T, C_Q), wo[:CONV_WIDTH].astype(BF16), won_pad,
        xattn_norm_w[l].reshape(1, D), xq[l].astype(BF16),
        (xq_norm_w[l] * X_DH ** -0.5).reshape(1, X_DH), kmem, vmem, xo[l].astype(BF16),
        ffn_norm_w[l].reshape(1, D), peer_wq[l].astype(BF16), keys, S)

    ei, ej, gate = _peer_topk(sc)
    out = _peer_dense(h3, ei, ej, gate, peer_down[l].T.astype(BF16), peer_up[l].astype(BF16), x2)
    return out.reshape(B, S, D)
```

```python
import functools

import jax
import jax.numpy as jnp
from jax import lax
from jax.experimental import pallas as pl
from jax.experimental.pallas import tpu as pltpu

F32 = jnp.float32
BF16 = jnp.bfloat16
I32 = jnp.int32

EPS = 1e-6
NEG = -1e30
LANES = 128
SUBLANES = 8

D_MODEL = 1024
CONV_WIDTH = 512
NSA_HEADS = 8
NSA_GROUPS = 2
NSA_REP = NSA_HEADS // NSA_GROUPS
NSA_DH = 64
CMP_BLOCK = 32
CMP_STRIDE = 16
SEL_BLOCK = 64
SEL_TOPN = 16
WINDOW = 512
X_HEADS = 4
X_DH = D_MODEL // X_HEADS
PEER_HEADS = 8
PEER_NKEYS = 128
PEER_HALF = 128
PEER_TOPK = 16
PEER_EXPERTS = PEER_NKEYS * PEER_NKEYS

NT_DIMS = (((1,), (1,)), ((), ()))


def _dot(a, b):
    return jnp.dot(a, b, preferred_element_type=F32)


def _dot_nt(a, b):
    return lax.dot_general(a, b, NT_DIMS, preferred_element_type=F32)


def _gelu(x):
    return 0.5 * x * (1.0 + lax.erf(x * (2.0 ** -0.5)))


def _rms(x, w):
    ms = jnp.mean(x * x, axis=-1, keepdims=True)
    return x * lax.rsqrt(ms + EPS) * w


def _cparams(sem, vmem_mb):
    return pltpu.CompilerParams(dimension_semantics=sem, vmem_limit_bytes=vmem_mb * 1024 * 1024)


C_CONV = 3 * CONV_WIDTH
C_Q = NSA_HEADS * LANES
C_KV = 6 * NSA_GROUPS * NSA_DH
C_KN = 4 * LANES
C_G = NSA_GROUPS * LANES
C_ALL = C_CONV + C_Q + C_KN + C_G


def _in_proj_kernel(tiles_per_seq, x_ref, nw_ref, w_ref, wvt_ref, cw_ref, cb_ref, qw_ref, kw_ref,
                    yconv_ref, q_ref, kvc_ref, kn_ref, vt_ref, g_ref, carry_ref):
    tm = x_ref.shape[0]
    hn = _rms(x_ref[...], nw_ref[...]).astype(BF16)

    p = _dot(hn, w_ref[:, 0:C_CONV])
    b_g = p[:, 0:CONV_WIDTH]
    u = p[:, CONV_WIDTH:2 * CONV_WIDTH] * p[:, 2 * CONV_WIDTH:3 * CONV_WIDTH]

    @pl.when((pl.program_id(0) % tiles_per_seq) == 0)
    def _sequence_start():
        carry_ref[...] = jnp.zeros(carry_ref.shape, F32)

    prev = carry_ref[...]
    p1 = prev[SUBLANES - 1:SUBLANES, :]
    p2 = prev[SUBLANES - 2:SUBLANES - 1, :]
    row = lax.broadcasted_iota(I32, u.shape, 0)
    u1 = jnp.where(row == 0, p1, pltpu.roll(u, 1, axis=0))
    u2 = jnp.where(row == 0, p2, jnp.where(row == 1, p1, pltpu.roll(u, 2, axis=0)))
    carry_ref[...] = u[tm - SUBLANES:tm, :]
    cw = cw_ref[...]
    y = b_g * (cw[0:1, :] * u2 + cw[1:2, :] * u1 + cw[2:3, :] * u + cb_ref[...])
    yconv_ref[...] = y.astype(BF16)

    pq = _dot(hn, w_ref[:, C_CONV:C_CONV + C_Q])
    for h in range(NSA_HEADS):
        blk = pq[:, h * LANES:(h + 1) * LANES]
        ms = jnp.sum(blk * blk, axis=-1, keepdims=True) * (1.0 / NSA_DH)
        q_ref[:, h * LANES:(h + 1) * LANES] = (
            blk * lax.rsqrt(ms + EPS) * qw_ref[:, h * LANES:(h + 1) * LANES]).astype(BF16)

    pkv = _dot(hn, w_ref[:, C_CONV + C_Q:C_CONV + C_Q + C_KN])
    kvc_ref[...] = pkv[:, 0:2 * LANES]
    lane = lax.broadcasted_iota(I32, (tm, LANES), 1)
    lo = lane < NSA_DH
    for j in range(2):
        blk = pkv[:, (2 + j) * LANES:(3 + j) * LANES]
        sq = blk * blk
        ms_lo = jnp.sum(jnp.where(lo, sq, 0.0), axis=-1, keepdims=True) * (1.0 / NSA_DH)
        ms_hi = jnp.sum(jnp.where(lo, 0.0, sq), axis=-1, keepdims=True) * (1.0 / NSA_DH)
        scale = jnp.where(lo, lax.rsqrt(ms_lo + EPS), lax.rsqrt(ms_hi + EPS))
        kn_ref[:, j * LANES:(j + 1) * LANES] = (blk * scale * kw_ref[j:j + 1, :]).astype(BF16)

    for j in range(tm // LANES):
        vt_ref[j] = _dot_nt(wvt_ref[...], hn[j * LANES:(j + 1) * LANES, :]).astype(BF16)

    pg = _dot(hn, w_ref[:, C_CONV + C_Q + C_KN:C_ALL])
    g_ref[...] = jax.nn.sigmoid(pg)


def _in_proj(x2d, nw, w_all, wvt, cw, cb, qw, kw, seq, tm=256):
    T = x2d.shape[0]
    const = lambda i: (0, 0)
    tile = lambda i: (i, 0)
    return pl.pallas_call(
        functools.partial(_in_proj_kernel, seq // tm),
        grid=(T // tm,),
        in_specs=[
            pl.BlockSpec((tm, D_MODEL), tile),
            pl.BlockSpec((1, D_MODEL), const),
            pl.BlockSpec((D_MODEL, C_ALL), const),
            pl.BlockSpec((2 * LANES, D_MODEL), const),
            pl.BlockSpec((3, CONV_WIDTH), const),
            pl.BlockSpec((1, CONV_WIDTH), const),
            pl.BlockSpec((1, C_Q), const),
            pl.BlockSpec((2, LANES), const),
        ],
        out_specs=[
            pl.BlockSpec((tm, CONV_WIDTH), tile),
            pl.BlockSpec((tm, C_Q), tile),
            pl.BlockSpec((tm, 2 * LANES), tile),
            pl.BlockSpec((tm, 2 * LANES), tile),
            pl.BlockSpec((tm // LANES, 2 * LANES, LANES), lambda i: (i, 0, 0)),
            pl.BlockSpec((tm, C_G), tile),
        ],
        out_shape=[
            jax.ShapeDtypeStruct((T, CONV_WIDTH), BF16),
            jax.ShapeDtypeStruct((T, C_Q), BF16),
            jax.ShapeDtypeStruct((T, 2 * LANES), F32),
            jax.ShapeDtypeStruct((T, 2 * LANES), BF16),
            jax.ShapeDtypeStruct((T // LANES, 2 * LANES, LANES), BF16),
            jax.ShapeDtypeStruct((T, C_G), F32),
        ],
        scratch_shapes=[pltpu.VMEM((SUBLANES, CONV_WIDTH), F32)],
        compiler_params=_cparams(("arbitrary",), 48),
    )(x2d, nw, w_all, wvt, cw, cb, qw, kw)


HALF_FEATS = CMP_STRIDE * NSA_DH


def _compress_kernel(kc_ref, pe_ref, w1_ref, w2_ref, w2t_ref, knw_ref, out_ref):
    n_rows = kc_ref.shape[3]
    row = lax.broadcasted_iota(I32, (n_rows, LANES), 0)
    col = lax.broadcasted_iota(I32, (LANES, n_rows), 1)
    for kv in range(2):
        for g in range(NSA_GROUPS):
            kr = kc_ref[0, kv, g]
            a = (kr + pe_ref[kv, 0:1, :]).astype(BF16)
            b = (kr + pe_ref[kv, 1:2, :]).astype(BF16)
            ha = _dot(a, w1_ref[kv, 0:HALF_FEATS, :])
            hb = _dot(b, w1_ref[kv, HALF_FEATS:2 * HALF_FEATS, :])
            hid = _gelu(ha + pltpu.roll(hb, n_rows - 1, axis=0)).astype(BF16)
            if kv == 0:
                out = _dot(hid, w2_ref[g])
                ms = jnp.sum(out * out, axis=-1, keepdims=True) * (1.0 / NSA_DH)
                out = out * lax.rsqrt(ms + EPS) * knw_ref[g]
                out_ref[0, kv, g] = jnp.where(row < n_rows - 1, out, 0.0).astype(BF16)
            else:
                out = _dot_nt(w2t_ref[g], hid)
                out_ref[0, kv, g] = jnp.where(col < n_rows - 1, out, 0.0).astype(BF16)


def _compress(kc, pe, w1, w2, w2t, knw):
    B, n_rows = kc.shape[0], kc.shape[3]
    return pl.pallas_call(
        _compress_kernel,
        grid=(B,),
        in_specs=[
            pl.BlockSpec((1, 2, NSA_GROUPS, n_rows, HALF_FEATS), lambda b: (b, 0, 0, 0, 0)),
            pl.BlockSpec((2, 2, HALF_FEATS), lambda b: (0, 0, 0)),
            pl.BlockSpec((2, 2 * HALF_FEATS, LANES), lambda b: (0, 0, 0)),
            pl.BlockSpec((NSA_GROUPS, LANES, LANES), lambda b: (0, 0, 0)),
            pl.BlockSpec((NSA_GROUPS, LANES, LANES), lambda b: (0, 0, 0)),
            pl.BlockSpec((NSA_GROUPS, 1, LANES), lambda b: (0, 0, 0)),
        ],
        out_specs=pl.BlockSpec((1, 2, NSA_GROUPS, n_rows, LANES), lambda b: (b, 0, 0, 0, 0)),
        out_shape=jax.ShapeDtypeStruct((B, 2, NSA_GROUPS, n_rows, LANES), BF16),
        compiler_params=_cparams(("arbitrary",), 32),
    )(kc, pe, w1, w2, w2t, knw)


SEL_CHUNK = 512


def _nsa_kernel(q_ref, kn_ref, vt_ref, cmp_ref, g_ref, selmap_ref, y_ref, sel_ref, acc_ref):
    tq = q_ref.shape[1]
    cols = NSA_REP * tq
    i = pl.program_id(2)
    t0 = i * tq
    qb = q_ref[0]
    qs = jnp.concatenate([qb[:, r * LANES:(r + 1) * LANES] for r in range(NSA_REP)], axis=0)
    qpos = t0 + jnp.bitwise_and(lax.broadcasted_iota(I32, (1, cols), 1), tq - 1)

    n_cmp = cmp_ref.shape[3]
    s = _dot_nt(cmp_ref[0, 0, 0], qs)
    cmp_last = lax.broadcasted_iota(I32, (n_cmp, cols), 0) * CMP_STRIDE + (CMP_BLOCK - 1)
    vis_c = cmp_last <= qpos
    s = jnp.where(vis_c, s, NEG)
    m = jnp.max(s, axis=0, keepdims=True)
    m = jnp.where(m > 0.5 * NEG, m, 0.0)
    e = jnp.where(vis_c, jnp.exp(s - m), 0.0)
    d = jnp.sum(e, axis=0, keepdims=True)
    p_c = e / jnp.where(d > 0.0, d, 1.0)
    o_c = _dot(cmp_ref[0, 1, 0], p_c.astype(BF16))

    psum = p_c[:, 0:tq]
    for r in range(1, NSA_REP):
        psum = psum + p_c[:, r * tq:(r + 1) * tq]
    p_hi = psum.astype(BF16)
    p_lo = (psum - p_hi.astype(F32)).astype(BF16)
    imp = _dot(selmap_ref[...], p_hi) + _dot(selmap_ref[...], p_lo)
    n_sel = imp.shape[0]
    blk = lax.broadcasted_iota(I32, (n_sel, tq), 0)
    cur = jnp.right_shift(t0 + lax.broadcasted_iota(I32, (n_sel, tq), 1), 6)
    forced = (blk == 0) | (blk == cur) | (blk == cur - 1)
    imp = jnp.where(blk > cur, -jnp.inf, jnp.where(forced, jnp.inf, imp))
    rank = jnp.zeros((n_sel, tq), I32)
    for jp in range(n_sel):
        other = imp[jp:jp + 1, :]
        beats = (other > imp) | ((other == imp) & (blk > jp))
        rank = rank + beats.astype(I32)
    sel_t = (rank < min(SEL_TOPN, n_sel)).astype(F32)
    sel_ref[...] = jnp.concatenate([sel_t] * NSA_REP, axis=1)

    def attend(k_lane, v_row, c_lo, c_hi, chunk, mask):
        acc_ref[...] = jnp.zeros(acc_ref.shape, F32)
        units = chunk // LANES

        def body(c, carry):
            m_old, l_old = carry
            start = pl.multiple_of(c * chunk, chunk)
            k = kn_ref[0, pl.ds(start, chunk), k_lane:k_lane + LANES]
            vt = jnp.concatenate([vt_ref[c * units + u, v_row:v_row + LANES, :] for u in range(units)], axis=1)
            sc = mask(c, start, _dot_nt(k, qs))
            m_new = jnp.maximum(m_old, jnp.max(sc, axis=0, keepdims=True))
            alpha = jnp.exp(m_old - m_new)
            p = jnp.exp(sc - m_new)
            acc_ref[...] = alpha * acc_ref[...] + _dot(vt, p.astype(BF16))
            return m_new, alpha * l_old + jnp.sum(p, axis=0, keepdims=True)

        init = (jnp.full((1, cols), NEG, F32), jnp.zeros((1, cols), F32))
        _, l = lax.fori_loop(c_lo, c_hi, body, init)
        return acc_ref[...], l

    def mask_sel(c, start, sc):
        blocks = SEL_CHUNK // SEL_BLOCK
        kpos = start + lax.broadcasted_iota(I32, (SEL_BLOCK, cols), 0)
        out = []
        for b in range(blocks):
            chosen = sel_ref[pl.ds(c * blocks + b, 1), :] > 0.5
            vis = chosen & (kpos + b * SEL_BLOCK <= qpos)
            out.append(jnp.where(vis, sc[b * SEL_BLOCK:(b + 1) * SEL_BLOCK], NEG))
        return jnp.concatenate(out, axis=0)

    o_s, l_s = attend(0, 0, 0, (t0 + tq + SEL_CHUNK - 1) // SEL_CHUNK, SEL_CHUNK, mask_sel)

    n_win = WINDOW + tq
    w_start = pl.multiple_of(jnp.maximum(t0 - WINDOW, 0), LANES)
    w_unit = w_start // LANES
    k_w = kn_ref[0, pl.ds(w_start, n_win), LANES:2 * LANES]
    vt_w = jnp.concatenate([vt_ref[w_unit + u, LANES:2 * LANES, :] for u in range(n_win // LANES)], axis=1)
    diff = qpos - (w_start + lax.broadcasted_iota(I32, (n_win, cols), 0))
    s_w = jnp.where((diff >= 0) & (diff < WINDOW), _dot_nt(k_w, qs), NEG)
    p_w = jnp.exp(s_w - jnp.max(s_w, axis=0, keepdims=True))
    l_w = jnp.sum(p_w, axis=0, keepdims=True)
    o_w = _dot(vt_w, p_w.astype(BF16))

    g_t = g_ref[0].T
    gate = lambda br: jnp.concatenate([g_t[3 * r + br:3 * r + br + 1, :] for r in range(NSA_REP)], axis=1)
    y_t = o_c * gate(0) + o_s * (gate(1) / l_s) + o_w * (gate(2) / l_w)
    for r in range(NSA_REP):
        y_ref[0, :, r * LANES:(r + 1) * LANES] = y_t[:, r * tq:(r + 1) * tq].T.astype(BF16)


def _nsa(q, kn, vt, kvcmp, gates, selmap, tq=128):
    B, S = q.shape[0], q.shape[1]
    n_sel = S // SEL_BLOCK
    cols = NSA_REP * tq
    units = S // LANES
    return pl.pallas_call(
        _nsa_kernel,
        grid=(B, NSA_GROUPS, S // tq),
        in_specs=[
            pl.BlockSpec((1, tq, NSA_REP * LANES), lambda b, g, i: (b, i, g)),
            pl.BlockSpec((1, S, 2 * LANES), lambda b, g, i: (b, 0, 0)),
            pl.BlockSpec((units, 2 * LANES, LANES), lambda b, g, i: (b, 0, 0)),
            pl.BlockSpec((1, 2, 1, kvcmp.shape[3], LANES), lambda b, g, i: (b, 0, g, 0, 0)),
            pl.BlockSpec((1, tq, LANES), lambda b, g, i: (b, i, g)),
            pl.BlockSpec((n_sel, kvcmp.shape[3]), lambda b, g, i: (0, 0)),
        ],
        out_specs=pl.BlockSpec((1, tq, NSA_REP * LANES), lambda b, g, i: (b, i, g)),
        out_shape=jax.ShapeDtypeStruct((B, S, C_Q), BF16),
        scratch_shapes=[
            pltpu.VMEM((n_sel, cols), F32),
            pltpu.VMEM((LANES, cols), F32),
        ],
        compiler_params=_cparams(("arbitrary", "arbitrary", "arbitrary"), 48),
    )(q, kn, vt, kvcmp, gates, selmap)


def _mem_kv_kernel(mem_ref, nw_ref, wk_ref, wv_ref, kw_ref, k_ref, v_ref):
    mn = _rms(mem_ref[0], nw_ref[...]).astype(BF16)
    k = _dot(mn, wk_ref[...])
    for h in range(X_HEADS):
        sl = slice(h * X_DH, (h + 1) * X_DH)
        k_ref[0, :, sl] = _rms(k[:, sl], kw_ref[...]).astype(BF16)
    v_ref[0] = _dot(mn, wv_ref[...]).astype(BF16)


def _mem_kv(mem, nw, wk, wv, kw):
    B, M, D = mem.shape
    blk = pl.BlockSpec((1, M, D), lambda b: (b, 0, 0))
    const = lambda b: (0, 0)
    return pl.pallas_call(
        _mem_kv_kernel,
        grid=(B,),
        in_specs=[blk, pl.BlockSpec((1, D), const), pl.BlockSpec((D, D), const),
                  pl.BlockSpec((D, D), const), pl.BlockSpec((1, X_DH), const)],
        out_specs=[blk, blk],
        out_shape=[jax.ShapeDtypeStruct((B, M, D), BF16)] * 2,
        compiler_params=_cparams(("arbitrary",), 32),
    )(mem, nw, wk, wv, kw)


def _post_mix_kernel(x_ref, yc_ref, yn_ref, woc_ref, won_ref, xnw_ref, xq_ref, xqw_ref, km_ref, vm_ref,
                     xo_ref, fnw_ref, pwq_ref, keys_ref, x2_ref, h3_ref, sc_ref):
    x1 = x_ref[...] + _dot(yc_ref[...], woc_ref[...]) + _dot(yn_ref[...], won_ref[...])

    h2 = _rms(x1, xnw_ref[...]).astype(BF16)
    qx = _dot(h2, xq_ref[...])
    heads = []
    for h in range(X_HEADS):
        sl = slice(h * X_DH, (h + 1) * X_DH)
        qh = _rms(qx[:, sl], xqw_ref[...]).astype(BF16)
        s = _dot_nt(qh, km_ref[0, :, sl])
        e = jnp.exp(s - jnp.max(s, axis=-1, keepdims=True))
        p = e / jnp.sum(e, axis=-1, keepdims=True)
        heads.append(_dot(p.astype(BF16), vm_ref[0, :, sl]))
    o = jnp.concatenate(heads, axis=1).astype(BF16)
    x2 = x1 + _dot(o, xo_ref[...])
    x2_ref[...] = x2

    h3 = _rms(x2, fnw_ref[...]).astype(BF16)
    h3_ref[...] = h3
    qp = _dot(h3, pwq_ref[...]).astype(BF16)
    for c in range(2 * PEER_HEADS):
        sl = slice(c * PEER_HALF, (c + 1) * PEER_HALF)
        for j in range(qp.shape[0] // LANES):
            sc_ref[c, j] = _dot_nt(keys_ref[c], qp[j * LANES:(j + 1) * LANES, sl])


def _post_mix(x2d, yconv, ynsa, woc, won, xnw, xq, xqw, kmem, vmem, xo, fnw, pwq, keys, seq, tm=256):
    T, D = x2d.shape
    M = kmem.shape[1]
    n_sc = 2 * PEER_HEADS * PEER_NKEYS
    tile = lambda i: (i, 0)
    const = lambda i: (0, 0)
    per_batch = lambda i: ((i * tm) // seq, 0, 0)
    return pl.pallas_call(
        _post_mix_kernel,
        grid=(T // tm,),
        in_specs=[
            pl.BlockSpec((tm, D), tile),
            pl.BlockSpec((tm, CONV_WIDTH), tile),
            pl.BlockSpec((tm, C_Q), tile),
            pl.BlockSpec((CONV_WIDTH, D), const),
            pl.BlockSpec((C_Q, D), const),
            pl.BlockSpec((1, D), const),
            pl.BlockSpec((D, D), const),
            pl.BlockSpec((1, X_DH), const),
            pl.BlockSpec((1, M, D), per_batch),
            pl.BlockSpec((1, M, D), per_batch),
            pl.BlockSpec((D, D), const),
            pl.BlockSpec((1, D), const),
            pl.BlockSpec((D, n_sc), const),
            pl.BlockSpec((2 * PEER_HEADS, PEER_NKEYS, PEER_HALF), lambda i: (0, 0, 0)),
        ],
        out_specs=[pl.BlockSpec((tm, D), tile), pl.BlockSpec((tm, D), tile),
                   pl.BlockSpec((2 * PEER_HEADS, tm // LANES, PEER_NKEYS, LANES), lambda i: (0, i, 0, 0))],
        out_shape=[
            jax.ShapeDtypeStruct((T, D), F32),
            jax.ShapeDtypeStruct((T, D), BF16),
            jax.ShapeDtypeStruct((2 * PEER_HEADS, T // LANES, PEER_NKEYS, LANES), F32),
        ],
        compiler_params=_cparams(("arbitrary",), 56),
    )(x2d, yconv, ynsa, woc, won, xnw, xq, xqw, kmem, vmem, xo, fnw, pwq, keys)


STAIR = [(a, b) for a in range(PEER_TOPK) for b in range(PEER_TOPK // (a + 1))]
N_STAIR_VREGS = -(-len(STAIR) // SUBLANES)


def _stair_vregs(axis, k):
    return sorted({r // SUBLANES for r, ab in enumerate(STAIR) if ab[axis] == k})


def _peer_topk_kernel(sc_ref, ta_ref, tb_ref, ei_ref, ej_ref, gate_ref, s_out, e_out):
    n_grp = sc_ref.shape[1]
    K = PEER_TOPK
    NV = N_STAIR_VREGS
    kidx = lax.broadcasted_iota(I32, (PEER_NKEYS, LANES), 0).astype(F32)
    ridx = [(lax.broadcasted_iota(I32, (SUBLANES, LANES), 0) + SUBLANES * j).astype(F32) for j in range(NV)]
    ta = [ta_ref[SUBLANES * j:SUBLANES * (j + 1), :] for j in range(NV)]
    tb = [tb_ref[SUBLANES * j:SUBLANES * (j + 1), :] for j in range(NV)]
    zero = jnp.zeros((SUBLANES, LANES), F32)

    def top1(s):
        m = jnp.max(s, axis=0, keepdims=True)
        idx = jnp.min(jnp.where(s == m, kidx, float(PEER_NKEYS)), axis=0, keepdims=True)
        return m, idx, jnp.where(kidx == idx, -jnp.inf, s)

    def sorted_top(s, table, axis):
        val = [zero] * NV
        key = [zero] * NV
        for k in range(K):
            m, idx, s = top1(s)
            for j in _stair_vregs(axis, k):
                hit = table[j] == float(k)
                val[j] = jnp.where(hit, m, val[j])
                key[j] = jnp.where(hit, idx, key[j])
        return val, key

    def head(h, g):
        v1, k1 = sorted_top(sc_ref[2 * h, g], ta, 0)
        v2, k2 = sorted_top(sc_ref[2 * h + 1, g], tb, 1)
        cand = [jnp.where(ta[j] >= 0.0, v1[j] + v2[j], -jnp.inf) for j in range(NV)]
        ce = [k1[j] * float(PEER_NKEYS) + k2[j] for j in range(NV)]
        for k in range(K):
            mx = cand[0]
            for j in range(1, NV):
                mx = jnp.maximum(mx, cand[j])
            m = jnp.max(mx, axis=0, keepdims=True)
            ix = jnp.where(cand[0] == m, ridx[0], float(NV * SUBLANES))
            for j in range(1, NV):
                ix = jnp.minimum(ix, jnp.where(cand[j] == m, ridx[j], float(NV * SUBLANES)))
            idx = jnp.min(ix, axis=0, keepdims=True)
            hits = [ridx[j] == idx for j in range(NV)]
            es = jnp.where(hits[0], ce[0], 0.0)
            for j in range(1, NV):
                es = es + jnp.where(hits[j], ce[j], 0.0)
            cand = [jnp.where(hits[j], -jnp.inf, cand[j]) for j in range(NV)]
            s_out[pl.ds(h * K + k, 1), :] = m
            e_out[pl.ds(h * K + k, 1), :] = jnp.sum(es, axis=0, keepdims=True)

    def group(g, carry):
        def head_pair(u, c):
            head(2 * u, g)
            head(2 * u + 1, g)
            return c

        lax.fori_loop(0, PEER_HEADS // 2, head_pair, 0)
        s_all = s_out[...]
        gates = []
        for h in range(PEER_HEADS):
            sh = s_all[h * K:(h + 1) * K]
            ex = jnp.exp(sh - sh[0:1])
            gates.append(ex / jnp.sum(ex, axis=0, keepdims=True))
        rows = pl.ds(pl.multiple_of(g * LANES, LANES), LANES)
        gate_ref[rows, :] = jnp.concatenate(gates, axis=0).T
        e_int = e_out[...].T.astype(I32)
        ei_ref[rows, :] = jnp.right_shift(e_int, 7)
        ej_ref[rows, :] = jnp.bitwise_and(e_int, PEER_NKEYS - 1)
        return carry

    lax.fori_loop(0, n_grp, group, 0)


def _peer_topk(sc, n_grp=4):
    T = sc.shape[1] * LANES
    tk = n_grp * LANES
    tile = lambda i: (i, 0)
    rows = N_STAIR_VREGS * SUBLANES
    pad = rows - len(STAIR)
    ta = jnp.asarray([float(a) for a, _ in STAIR] + [-1.0] * pad, F32)
    tb = jnp.asarray([float(b) for _, b in STAIR] + [-1.0] * pad, F32)
    ta = jnp.broadcast_to(ta[:, None], (rows, LANES))
    tb = jnp.broadcast_to(tb[:, None], (rows, LANES))
    return pl.pallas_call(
        _peer_topk_kernel,
        grid=(T // tk,),
        in_specs=[pl.BlockSpec((2 * PEER_HEADS, n_grp, PEER_NKEYS, LANES), lambda i: (0, i, 0, 0)),
                  pl.BlockSpec((rows, LANES), lambda i: (0, 0)),
                  pl.BlockSpec((rows, LANES), lambda i: (0, 0))],
        out_specs=[pl.BlockSpec((tk, LANES), tile)] * 3,
        out_shape=[jax.ShapeDtypeStruct((T, LANES), I32), jax.ShapeDtypeStruct((T, LANES), I32),
                   jax.ShapeDtypeStruct((T, LANES), F32)],
        scratch_shapes=[pltpu.VMEM((PEER_HEADS * PEER_TOPK, LANES), F32)] * 2,
        compiler_params=_cparams(("arbitrary",), 40),
    )(sc, ta, tb)


G_ROWS = PEER_NKEYS // 2
G_PITCH = G_ROWS + SUBLANES
HI_MASK = 0xFFFF0000


def _peer_dense_kernel(h_ref, ei_ref, ej_ref, gate_ref, wd_ref, wu_ref, x2_ref, out_ref, g_ref):
    tm = h_ref.shape[0]
    ec = wd_ref.shape[1]
    ipc = ec // LANES
    half = ipc // 2
    half_bits = half.bit_length() - 1
    c = pl.program_id(1)

    @pl.when(c == 0)
    def _scatter():
        out_ref[...] = x2_ref[...]
        m = lax.broadcasted_iota(I32, (PEER_NKEYS, 2 * LANES), 0)
        r = jnp.bitwise_and(m, G_ROWS - 1)
        i_of_m = (jnp.right_shift(r, half_bits) * ipc + jnp.bitwise_and(r, half - 1)
                  + jnp.right_shift(m, 6) * half)
        n = lax.broadcasted_iota(I32, (2 * LANES, 2 * LANES), 0)

        def pair(u, carry):
            t = 2 * u
            ri = jnp.concatenate([ei_ref[pl.ds(t, 1), :], ei_ref[pl.ds(t + 1, 1), :]], axis=1)
            rj = jnp.concatenate([ej_ref[pl.ds(t, 1), :], ej_ref[pl.ds(t + 1, 1), :] + LANES], axis=1)
            rg = jnp.concatenate([gate_ref[pl.ds(t, 1), :], gate_ref[pl.ds(t + 1, 1), :]], axis=1)
            c_t = jnp.where(i_of_m == ri, rg, 0.0).astype(BF16)
            q_t = jnp.where(n == rj, 1.0, 0.0).astype(BF16)
            g2 = _dot_nt(c_t, q_t)
            for tok in range(2):
                g = g2[:, tok * LANES:(tok + 1) * LANES].astype(BF16).astype(F32)
                bits = lax.bitcast_convert_type(g, jnp.uint32)
                word = jnp.bitwise_or(jnp.right_shift(bits[0:G_ROWS], 16),
                                      jnp.bitwise_and(bits[G_ROWS:2 * G_ROWS], jnp.uint32(HI_MASK)))
                g_ref[pl.ds(pl.multiple_of((t + tok) * G_PITCH, SUBLANES), G_ROWS), :] = word
            return carry

        lax.fori_loop(0, tm // 2, pair, 0, unroll=8)

    a = _dot(h_ref[...], wd_ref[...])
    parts = [None] * ipc
    for k in range(half):
        word = g_ref[pl.ds(c * half + k, tm, stride=G_PITCH), :]
        g_lo = lax.bitcast_convert_type(jnp.left_shift(word, 16), F32)
        g_hi = lax.bitcast_convert_type(jnp.bitwise_and(word, jnp.uint32(HI_MASK)), F32)
        parts[k] = (_gelu(a[:, k * LANES:(k + 1) * LANES]) * g_lo).astype(BF16)
        parts[k + half] = (_gelu(a[:, (k + half) * LANES:(k + half + 1) * LANES]) * g_hi).astype(BF16)
    z = jnp.concatenate(parts, axis=1)
    out_ref[...] += _dot(z, wu_ref[...])


def _peer_dense(h3, ei, ej, gate, wd_t, wu, x2, tm=512, ec=1024):
    T, D = h3.shape
    tm = min(tm, T)
    n_exp = wu.shape[0]
    tile = lambda t, c: (t, 0)
    return pl.pallas_call(
        _peer_dense_kernel,
        grid=(T // tm, n_exp // ec),
        in_specs=[
            pl.BlockSpec((tm, D), tile),
            pl.BlockSpec((tm, LANES), tile),
            pl.BlockSpec((tm, LANES), tile),
            pl.BlockSpec((tm, LANES), tile),
            pl.BlockSpec((D, ec), lambda t, c: (0, c)),
            pl.BlockSpec((ec, D), lambda t, c: (c, 0)),
            pl.BlockSpec((tm, D), tile),
        ],
        out_specs=pl.BlockSpec((tm, D), tile),
        out_shape=jax.ShapeDtypeStruct((T, D), F32),
        scratch_shapes=[pltpu.VMEM((tm * G_PITCH, LANES), jnp.uint32)],
        compiler_params=_cparams(("arbitrary", "arbitrary"), 58),
    )(h3, ei, ej, gate, wd_t, wu, x2)


def _pad_half(a, g):
    z = jnp.zeros_like(a)
    return jnp.concatenate([a, z] if g == 0 else [z, a], axis=-1)


def kernel(x, mem, mix_norm_w, w_in, conv_w, conv_b, cmp_pe, cmp_w1, cmp_w2, q_norm_w, k_norm_w, w_out,
           xattn_norm_w, mem_norm_w, xq, xk, xv, xo, xq_norm_w, xk_norm_w, ffn_norm_w, peer_wq, peer_keys,
           peer_down, peer_up):
    B, S, D = x.shape
    T = B * S
    l = 0
    G, R, dh = NSA_GROUPS, NSA_REP, NSA_DH

    w = w_in[l]
    o_q = C_CONV
    o_kv = o_q + NSA_HEADS * dh
    o_g = o_kv + C_KV
    wq = w[:, o_q:o_kv].reshape(D, G, R, dh)
    wq_pad = jnp.concatenate([_pad_half(wq[:, g], g).reshape(D, R * LANES) for g in range(G)], axis=1)
    wg = w[:, o_g:].reshape(D, G, R * 3)
    wg_pad = jnp.pad(wg, ((0, 0), (0, 0), (0, LANES - R * 3))).reshape(D, G * LANES)
    wkv = w[:, o_kv:o_g].reshape(D, 6, LANES)
    wkn = wkv[:, jnp.array([0, 1, 2, 4])].reshape(D, C_KN)
    wvt = wkv[:, jnp.array([3, 5])].reshape(D, 2 * LANES).T.astype(BF16)
    w_all = jnp.concatenate([w[:, :o_q], wq_pad, wkn, wg_pad], axis=1).astype(BF16)
    qw = q_norm_w[l] * dh ** -0.5
    qw_pad = jnp.concatenate([jnp.tile(_pad_half(qw, g), R) for g in range(G)]).reshape(1, C_Q)
    kw = jnp.stack([jnp.tile(k_norm_w[l, 1], G), jnp.tile(k_norm_w[l, 2], G)])

    yconv, q, kvc, kn, vt, gates = _in_proj(
        x.reshape(T, D), mix_norm_w[l].reshape(1, D), w_all, wvt, conv_w[l], conv_b[l].reshape(1, CONV_WIDTH),
        qw_pad, kw, S)

    n_rows = S // CMP_STRIDE
    kc = kvc.reshape(B, n_rows, CMP_STRIDE, 2, G, dh).transpose(0, 3, 4, 1, 2, 5).reshape(
        B, 2, G, n_rows, HALF_FEATS)
    pe = cmp_pe[l].reshape(2, 2, HALF_FEATS)
    w1 = jnp.pad(cmp_w1[l], ((0, 0), (0, 0), (0, LANES - dh))).astype(BF16)
    w2 = jnp.pad(cmp_w2[l], ((0, 0), (0, LANES - dh), (0, 0)))
    w2 = jnp.stack([_pad_half(w2, g) for g in range(G)], axis=1).astype(BF16)
    knw = jnp.stack([_pad_half(k_norm_w[l, 0], g) for g in range(G)]).reshape(G, 1, LANES)
    kvcmp = _compress(kc, pe, w1, w2[0], jnp.swapaxes(w2[1], -1, -2), knw)

    n_sel = S // SEL_BLOCK
    cmp_start = jnp.arange(n_rows) * CMP_STRIDE
    sel_start = jnp.arange(n_sel) * SEL_BLOCK
    selmap_t = ((cmp_start[None, :] < sel_start[:, None] + SEL_BLOCK)
                & (cmp_start[None, :] + CMP_BLOCK > sel_start[:, None])
                & (jnp.arange(n_rows)[None, :] < n_rows - 1)).astype(BF16)
    ynsa = _nsa(q.reshape(B, S, C_Q), kn.reshape(B, S, 2 * LANES), vt, kvcmp, gates.reshape(B, S, C_G), selmap_t)

    kmem, vmem = _mem_kv(mem, mem_norm_w[l].reshape(1, D), xk[l].astype(BF16), xv[l].astype(BF16),
                         xk_norm_w[l].reshape(1, X_DH))

    wo = w_out[l]
    won = wo[CONV_WIDTH:].reshape(G, R, dh, D)
    won_pad = jnp.concatenate(
        [jnp.concatenate([won[g], jnp.zeros_like(won[g])] if g == 0 else [jnp.zeros_like(won[g]), won[g]],
                         axis=1).reshape(R * LANES, D) for g in range(G)], axis=0).astype(BF16)
    keys = peer_keys[l].reshape(2 * PEER_HEADS, PEER_NKEYS, PEER_HALF).astype(BF16)
    x2, h3, sc = _post_mix(
        x.reshape(T, D), yconv, ynsa.reshape(T, C_Q), wo[:CONV_WIDTH].astype(BF16), won_pad,
        xattn_norm_w[l].reshape(1, D), xq[l].astype(BF16),
        (xq_norm_w[l] * X_DH ** -0.5).reshape(1, X_DH), kmem, vmem, xo[l].astype(BF16),
        ffn_norm_w[l].reshape(1, D), peer_wq[l].astype(BF16), keys, S)

    ei, ej, gate = _peer_topk(sc)
    out = _peer_dense(h3, ei, ej, gate, peer_down[l].T.astype(BF16), peer_up[l].astype(BF16), x2)
    return out.reshape(B, S, D)
```

```python
import functools

import jax
import jax.numpy as jnp
from jax import lax
from jax.experimental import pallas as pl
from jax.experimental.pallas import tpu as pltpu

F32 = jnp.float32
BF16 = jnp.bfloat16
I32 = jnp.int32

EPS = 1e-6
NEG = -1e30
LANES = 128
SUBLANES = 8

D_MODEL = 1024
CONV_WIDTH = 512
NSA_HEADS = 8
NSA_GROUPS = 2
NSA_REP = NSA_HEADS // NSA_GROUPS
NSA_DH = 64
CMP_BLOCK = 32
CMP_STRIDE = 16
SEL_BLOCK = 64
SEL_TOPN = 16
WINDOW = 512
X_HEADS = 4
X_DH = D_MODEL // X_HEADS
PEER_HEADS = 8
PEER_NKEYS = 128
PEER_HALF = 128
PEER_TOPK = 16
PEER_EXPERTS = PEER_NKEYS * PEER_NKEYS

NT_DIMS = (((1,), (1,)), ((), ()))


def _dot(a, b):
    return jnp.dot(a, b, preferred_element_type=F32)


def _dot_nt(a, b):
    return lax.dot_general(a, b, NT_DIMS, preferred_element_type=F32)


def _gelu(x):
    return 0.5 * x * (1.0 + lax.erf(x * (2.0 ** -0.5)))


def _rms(x, w):
    ms = jnp.mean(x * x, axis=-1, keepdims=True)
    return x * lax.rsqrt(ms + EPS) * w


def _cparams(sem, vmem_mb):
    return pltpu.CompilerParams(dimension_semantics=sem, vmem_limit_bytes=vmem_mb * 1024 * 1024)


C_CONV = 3 * CONV_WIDTH
C_Q = NSA_HEADS * LANES
C_KV = 6 * NSA_GROUPS * NSA_DH
C_KN = 4 * LANES
C_G = NSA_GROUPS * LANES
C_ALL = C_CONV + C_Q + C_KN + C_G


def _in_proj_kernel(tiles_per_seq, x_ref, nw_ref, w_ref, wvt_ref, cw_ref, cb_ref, qw_ref, kw_ref,
                    yconv_ref, q_ref, kvc_ref, kn_ref, vt_ref, g_ref, carry_ref):
    tm = x_ref.shape[0]
    hn = _rms(x_ref[...], nw_ref[...]).astype(BF16)

    p = _dot(hn, w_ref[:, 0:C_CONV])
    b_g = p[:, 0:CONV_WIDTH]
    u = p[:, CONV_WIDTH:2 * CONV_WIDTH] * p[:, 2 * CONV_WIDTH:3 * CONV_WIDTH]

    @pl.when((pl.program_id(0) % tiles_per_seq) == 0)
    def _sequence_start():
        carry_ref[...] = jnp.zeros(carry_ref.shape, F32)

    prev = carry_ref[...]
    p1 = prev[SUBLANES - 1:SUBLANES, :]
    p2 = prev[SUBLANES - 2:SUBLANES - 1, :]
    row = lax.broadcasted_iota(I32, u.shape, 0)
    u1 = jnp.where(row == 0, p1, pltpu.roll(u, 1, axis=0))
    u2 = jnp.where(row == 0, p2, jnp.where(row == 1, p1, pltpu.roll(u, 2, axis=0)))
    carry_ref[...] = u[tm - SUBLANES:tm, :]
    cw = cw_ref[...]
    y = b_g * (cw[0:1, :] * u2 + cw[1:2, :] * u1 + cw[2:3, :] * u + cb_ref[...])
    yconv_ref[...] = y.astype(BF16)

    pq = _dot(hn, w_ref[:, C_CONV:C_CONV + C_Q])
    for h in range(NSA_HEADS):
        blk = pq[:, h * LANES:(h + 1) * LANES]
        ms = jnp.sum(blk * blk, axis=-1, keepdims=True) * (1.0 / NSA_DH)
        q_ref[:, h * LANES:(h + 1) * LANES] = (
            blk * lax.rsqrt(ms + EPS) * qw_ref[:, h * LANES:(h + 1) * LANES]).astype(BF16)

    pkv = _dot(hn, w_ref[:, C_CONV + C_Q:C_CONV + C_Q + C_KN])
    kvc_ref[...] = pkv[:, 0:2 * LANES]
    lane = lax.broadcasted_iota(I32, (tm, LANES), 1)
    lo = lane < NSA_DH
    for j in range(2):
        blk = pkv[:, (2 + j) * LANES:(3 + j) * LANES]
        sq = blk * blk
        ms_lo = jnp.sum(jnp.where(lo, sq, 0.0), axis=-1, keepdims=True) * (1.0 / NSA_DH)
        ms_hi = jnp.sum(jnp.where(lo, 0.0, sq), axis=-1, keepdims=True) * (1.0 / NSA_DH)
        scale = jnp.where(lo, lax.rsqrt(ms_lo + EPS), lax.rsqrt(ms_hi + EPS))
        kn_ref[:, j * LANES:(j + 1) * LANES] = (blk * scale * kw_ref[j:j + 1, :]).astype(BF16)

    for j in range(tm // LANES):
        vt_ref[j] = _dot_nt(wvt_ref[...], hn[j * LANES:(j + 1) * LANES, :]).astype(BF16)

    pg = _dot(hn, w_ref[:, C_CONV + C_Q + C_KN:C_ALL])
    g_ref[...] = jax.nn.sigmoid(pg)


def _in_proj(x2d, nw, w_all, wvt, cw, cb, qw, kw, seq, tm=256):
    T = x2d.shape[0]
    const = lambda i: (0, 0)
    tile = lambda i: (i, 0)
    return pl.pallas_call(
        functools.partial(_in_proj_kernel, seq // tm),
        grid=(T // tm,),
        in_specs=[
            pl.BlockSpec((tm, D_MODEL), tile),
            pl.BlockSpec((1, D_MODEL), const),
            pl.BlockSpec((D_MODEL, C_ALL), const),
            pl.BlockSpec((2 * LANES, D_MODEL), const),
            pl.BlockSpec((3, CONV_WIDTH), const),
            pl.BlockSpec((1, CONV_WIDTH), const),
            pl.BlockSpec((1, C_Q), const),
            pl.BlockSpec((2, LANES), const),
        ],
        out_specs=[
            pl.BlockSpec((tm, CONV_WIDTH), tile),
            pl.BlockSpec((tm, C_Q), tile),
            pl.BlockSpec((tm, 2 * LANES), tile),
            pl.BlockSpec((tm, 2 * LANES), tile),
            pl.BlockSpec((tm // LANES, 2 * LANES, LANES), lambda i: (i, 0, 0)),
            pl.BlockSpec((tm, C_G), tile),
        ],
        out_shape=[
            jax.ShapeDtypeStruct((T, CONV_WIDTH), BF16),
            jax.ShapeDtypeStruct((T, C_Q), BF16),
            jax.ShapeDtypeStruct((T, 2 * LANES), F32),
            jax.ShapeDtypeStruct((T, 2 * LANES), BF16),
            jax.ShapeDtypeStruct((T // LANES, 2 * LANES, LANES), BF16),
            jax.ShapeDtypeStruct((T, C_G), F32),
        ],
        scratch_shapes=[pltpu.VMEM((SUBLANES, CONV_WIDTH), F32)],
        compiler_params=_cparams(("arbitrary",), 48),
    )(x2d, nw, w_all, wvt, cw, cb, qw, kw)


HALF_FEATS = CMP_STRIDE * NSA_DH


def _compress_kernel(kc_ref, pe_ref, w1_ref, w2_ref, w2t_ref, knw_ref, out_ref):
    n_rows = kc_ref.shape[3]
    row = lax.broadcasted_iota(I32, (n_rows, LANES), 0)
    col = lax.broadcasted_iota(I32, (LANES, n_rows), 1)
    for kv in range(2):
        for g in range(NSA_GROUPS):
            kr = kc_ref[0, kv, g]
            a = (kr + pe_ref[kv, 0:1, :]).astype(BF16)
            b = (kr + pe_ref[kv, 1:2, :]).astype(BF16)
            ha = _dot(a, w1_ref[kv, 0:HALF_FEATS, :])
            hb = _dot(b, w1_ref[kv, HALF_FEATS:2 * HALF_FEATS, :])
            hid = _gelu(ha + pltpu.roll(hb, n_rows - 1, axis=0)).astype(BF16)
            if kv == 0:
                out = _dot(hid, w2_ref[g])
                ms = jnp.sum(out * out, axis=-1, keepdims=True) * (1.0 / NSA_DH)
                out = out * lax.rsqrt(ms + EPS) * knw_ref[g]
                out_ref[0, kv, g] = jnp.where(row < n_rows - 1, out, 0.0).astype(BF16)
            else:
                out = _dot_nt(w2t_ref[g], hid)
                out_ref[0, kv, g] = jnp.where(col < n_rows - 1, out, 0.0).astype(BF16)


def _compress(kc, pe, w1, w2, w2t, knw):
    B, n_rows = kc.shape[0], kc.shape[3]
    return pl.pallas_call(
        _compress_kernel,
        grid=(B,),
        in_specs=[
            pl.BlockSpec((1, 2, NSA_GROUPS, n_rows, HALF_FEATS), lambda b: (b, 0, 0, 0, 0)),
            pl.BlockSpec((2, 2, HALF_FEATS), lambda b: (0, 0, 0)),
            pl.BlockSpec((2, 2 * HALF_FEATS, LANES), lambda b: (0, 0, 0)),
            pl.BlockSpec((NSA_GROUPS, LANES, LANES), lambda b: (0, 0, 0)),
            pl.BlockSpec((NSA_GROUPS, LANES, LANES), lambda b: (0, 0, 0)),
            pl.BlockSpec((NSA_GROUPS, 1, LANES), lambda b: (0, 0, 0)),
        ],
        out_specs=pl.BlockSpec((1, 2, NSA_GROUPS, n_rows, LANES), lambda b: (b, 0, 0, 0, 0)),
        out_shape=jax.ShapeDtypeStruct((B, 2, NSA_GROUPS, n_rows, LANES), BF16),
        compiler_params=_cparams(("arbitrary",), 32),
    )(kc, pe, w1, w2, w2t, knw)


SEL_CHUNK = 512


def _nsa_kernel(q_ref, kn_ref, vt_ref, cmp_ref, g_ref, selmap_ref, y_ref, sel_ref, acc_ref):
    tq = q_ref.shape[1]
    cols = NSA_HEADS * tq
    i = pl.program_id(1)
    t0 = i * tq
    qb = q_ref[0]
    qs = jnp.concatenate([qb[:, h * LANES:(h + 1) * LANES] for h in range(NSA_HEADS)], axis=0)
    qpos = t0 + jnp.bitwise_and(lax.broadcasted_iota(I32, (1, cols), 1), tq - 1)
    both = lambda kv: (cmp_ref[0, kv, 0].astype(F32) + cmp_ref[0, kv, 1].astype(F32)).astype(BF16)

    n_cmp = cmp_ref.shape[3]
    s = _dot_nt(both(0), qs)
    cmp_last = lax.broadcasted_iota(I32, (n_cmp, cols), 0) * CMP_STRIDE + (CMP_BLOCK - 1)
    vis_c = cmp_last <= qpos
    s = jnp.where(vis_c, s, NEG)
    m = jnp.max(s, axis=0, keepdims=True)
    m = jnp.where(m > 0.5 * NEG, m, 0.0)
    e = jnp.where(vis_c, jnp.exp(s - m), 0.0)
    d = jnp.sum(e, axis=0, keepdims=True)
    p_c = e / jnp.where(d > 0.0, d, 1.0)
    o_c = _dot(both(1), p_c.astype(BF16))

    n_sel = selmap_ref.shape[0]
    blk = lax.broadcasted_iota(I32, (n_sel, tq), 0)
    cur = jnp.right_shift(t0 + lax.broadcasted_iota(I32, (n_sel, tq), 1), 6)
    forced = (blk == 0) | (blk == cur) | (blk == cur - 1)
    chosen = []
    for g in range(NSA_GROUPS):
        psum = p_c[:, g * NSA_REP * tq:(g * NSA_REP + 1) * tq]
        for r in range(1, NSA_REP):
            psum = psum + p_c[:, (g * NSA_REP + r) * tq:(g * NSA_REP + r + 1) * tq]
        p_hi = psum.astype(BF16)
        p_lo = (psum - p_hi.astype(F32)).astype(BF16)
        imp = _dot(selmap_ref[...], p_hi) + _dot(selmap_ref[...], p_lo)
        imp = jnp.where(blk > cur, -jnp.inf, jnp.where(forced, jnp.inf, imp))
        rank = jnp.zeros((n_sel, tq), I32)
        for jp in range(n_sel):
            other = imp[jp:jp + 1, :]
            beats = (other > imp) | ((other == imp) & (blk > jp))
            rank = rank + beats.astype(I32)
        chosen += [rank < min(SEL_TOPN, n_sel)] * NSA_REP
    sel_ref[...] = jnp.where(jnp.concatenate(chosen, axis=1), qpos, -1)

    acc_ref[...] = jnp.zeros(acc_ref.shape, F32)
    units = SEL_CHUNK // LANES
    blocks = SEL_CHUNK // SEL_BLOCK
    koff = lax.broadcasted_iota(I32, (SEL_BLOCK, cols), 0)

    def sel_chunk(c, carry):
        m_old, l_old = carry
        start = pl.multiple_of(c * SEL_CHUNK, SEL_CHUNK)
        k = kn_ref[0, pl.ds(start, SEL_CHUNK), 0:LANES]
        vt = jnp.concatenate([vt_ref[c * units + u, 0:LANES, :] for u in range(units)], axis=1)
        sc = _dot_nt(k, qs)
        parts = []
        for b in range(blocks):
            limit = sel_ref[pl.ds(c * blocks + b, 1), :] - (start + b * SEL_BLOCK)
            parts.append(jnp.where(koff <= limit, sc[b * SEL_BLOCK:(b + 1) * SEL_BLOCK], NEG))
        sc = jnp.concatenate(parts, axis=0)
        m_new = jnp.maximum(m_old, jnp.max(sc, axis=0, keepdims=True))
        alpha = jnp.exp(m_old - m_new)
        p = jnp.exp(sc - m_new)
        acc_ref[...] = alpha * acc_ref[...] + _dot(vt, p.astype(BF16))
        return m_new, alpha * l_old + jnp.sum(p, axis=0, keepdims=True)

    init = (jnp.full((1, cols), NEG, F32), jnp.zeros((1, cols), F32))
    _, l_s = lax.fori_loop(0, (t0 + tq + SEL_CHUNK - 1) // SEL_CHUNK, sel_chunk, init)
    o_s = acc_ref[...]

    n_win = WINDOW + tq
    w_start = pl.multiple_of(jnp.maximum(t0 - WINDOW, 0), LANES)
    w_unit = w_start // LANES
    k_w = kn_ref[0, pl.ds(w_start, n_win), LANES:2 * LANES]
    vt_w = jnp.concatenate([vt_ref[w_unit + u, LANES:2 * LANES, :] for u in range(n_win // LANES)], axis=1)
    diff = (qpos - w_start) - lax.broadcasted_iota(I32, (n_win, cols), 0)
    in_window = lax.bitcast_convert_type(diff, jnp.uint32) < jnp.uint32(WINDOW)
    s_w = jnp.where(in_window, _dot_nt(k_w, qs), NEG)
    p_w = jnp.exp(s_w - jnp.max(s_w, axis=0, keepdims=True))
    l_w = jnp.sum(p_w, axis=0, keepdims=True)
    o_w = _dot(vt_w, p_w.astype(BF16))

    g_t = [g_ref[0, :, g * LANES:(g + 1) * LANES].T for g in range(NSA_GROUPS)]
    gate = lambda br: jnp.concatenate(
        [g_t[h // NSA_REP][3 * (h % NSA_REP) + br:3 * (h % NSA_REP) + br + 1, :] for h in range(NSA_HEADS)], axis=1)
    y_t = o_c * gate(0) + o_s * (gate(1) / l_s) + o_w * (gate(2) / l_w)
    for h in range(NSA_HEADS):
        y_ref[0, :, h * LANES:(h + 1) * LANES] = y_t[:, h * tq:(h + 1) * tq].T.astype(BF16)


def _nsa(q, kn, vt, kvcmp, gates, selmap, tq=128):
    B, S = q.shape[0], q.shape[1]
    n_sel = S // SEL_BLOCK
    cols = NSA_HEADS * tq
    units = S // LANES
    n_cmp = kvcmp.shape[3]
    return pl.pallas_call(
        _nsa_kernel,
        grid=(B, S // tq),
        in_specs=[
            pl.BlockSpec((1, tq, C_Q), lambda b, i: (b, i, 0)),
            pl.BlockSpec((1, S, 2 * LANES), lambda b, i: (b, 0, 0)),
            pl.BlockSpec((units, 2 * LANES, LANES), lambda b, i: (b, 0, 0)),
            pl.BlockSpec((1, 2, NSA_GROUPS, n_cmp, LANES), lambda b, i: (b, 0, 0, 0, 0)),
            pl.BlockSpec((1, tq, C_G), lambda b, i: (b, i, 0)),
            pl.BlockSpec((n_sel, n_cmp), lambda b, i: (0, 0)),
        ],
        out_specs=pl.BlockSpec((1, tq, C_Q), lambda b, i: (b, i, 0)),
        out_shape=jax.ShapeDtypeStruct((B, S, C_Q), BF16),
        scratch_shapes=[
            pltpu.VMEM((n_sel, cols), I32),
            pltpu.VMEM((LANES, cols), F32),
        ],
        compiler_params=_cparams(("arbitrary", "arbitrary"), 56),
    )(q, kn, vt, kvcmp, gates, selmap)


def _mem_kv_kernel(mem_ref, nw_ref, wk_ref, wv_ref, kw_ref, k_ref, v_ref):
    mn = _rms(mem_ref[0], nw_ref[...]).astype(BF16)
    k = _dot(mn, wk_ref[...])
    for h in range(X_HEADS):
        sl = slice(h * X_DH, (h + 1) * X_DH)
        k_ref[0, :, sl] = _rms(k[:, sl], kw_ref[...]).astype(BF16)
    v_ref[0] = _dot(mn, wv_ref[...]).astype(BF16)


def _mem_kv(mem, nw, wk, wv, kw):
    B, M, D = mem.shape
    blk = pl.BlockSpec((1, M, D), lambda b: (b, 0, 0))
    const = lambda b: (0, 0)
    return pl.pallas_call(
        _mem_kv_kernel,
        grid=(B,),
        in_specs=[blk, pl.BlockSpec((1, D), const), pl.BlockSpec((D, D), const),
                  pl.BlockSpec((D, D), const), pl.BlockSpec((1, X_DH), const)],
        out_specs=[blk, blk],
        out_shape=[jax.ShapeDtypeStruct((B, M, D), BF16)] * 2,
        compiler_params=_cparams(("arbitrary",), 32),
    )(mem, nw, wk, wv, kw)


def _post_mix_kernel(x_ref, yc_ref, yn_ref, woc_ref, won_ref, xnw_ref, xq_ref, xqw_ref, km_ref, vm_ref,
                     xo_ref, fnw_ref, pwq_ref, keys_ref, x2_ref, h3_ref, sc_ref):
    x1 = x_ref[...] + _dot(yc_ref[...], woc_ref[...]) + _dot(yn_ref[...], won_ref[...])

    h2 = _rms(x1, xnw_ref[...]).astype(BF16)
    qx = _dot(h2, xq_ref[...])
    heads = []
    for h in range(X_HEADS):
        sl = slice(h * X_DH, (h + 1) * X_DH)
        qh = _rms(qx[:, sl], xqw_ref[...]).astype(BF16)
        s = _dot_nt(qh, km_ref[0, :, sl])
        e = jnp.exp(s - jnp.max(s, axis=-1, keepdims=True))
        p = e / jnp.sum(e, axis=-1, keepdims=True)
        heads.append(_dot(p.astype(BF16), vm_ref[0, :, sl]))
    o = jnp.concatenate(heads, axis=1).astype(BF16)
    x2 = x1 + _dot(o, xo_ref[...])
    x2_ref[...] = x2

    h3 = _rms(x2, fnw_ref[...]).astype(BF16)
    h3_ref[...] = h3
    qp = _dot(h3, pwq_ref[...]).astype(BF16)
    for c in range(2 * PEER_HEADS):
        sl = slice(c * PEER_HALF, (c + 1) * PEER_HALF)
        for j in range(qp.shape[0] // LANES):
            sc_ref[c, j] = _dot_nt(keys_ref[c], qp[j * LANES:(j + 1) * LANES, sl])


def _post_mix(x2d, yconv, ynsa, woc, won, xnw, xq, xqw, kmem, vmem, xo, fnw, pwq, keys, seq, tm=256):
    T, D = x2d.shape
    M = kmem.shape[1]
    n_sc = 2 * PEER_HEADS * PEER_NKEYS
    tile = lambda i: (i, 0)
    const = lambda i: (0, 0)
    per_batch = lambda i: ((i * tm) // seq, 0, 0)
    return pl.pallas_call(
        _post_mix_kernel,
        grid=(T // tm,),
        in_specs=[
            pl.BlockSpec((tm, D), tile),
            pl.BlockSpec((tm, CONV_WIDTH), tile),
            pl.BlockSpec((tm, C_Q), tile),
            pl.BlockSpec((CONV_WIDTH, D), const),
            pl.BlockSpec((C_Q, D), const),
            pl.BlockSpec((1, D), const),
            pl.BlockSpec((D, D), const),
            pl.BlockSpec((1, X_DH), const),
            pl.BlockSpec((1, M, D), per_batch),
            pl.BlockSpec((1, M, D), per_batch),
            pl.BlockSpec((D, D), const),
            pl.BlockSpec((1, D), const),
            pl.BlockSpec((D, n_sc), const),
            pl.BlockSpec((2 * PEER_HEADS, PEER_NKEYS, PEER_HALF), lambda i: (0, 0, 0)),
        ],
        out_specs=[pl.BlockSpec((tm, D), tile), pl.BlockSpec((tm, D), tile),
                   pl.BlockSpec((2 * PEER_HEADS, tm // LANES, PEER_NKEYS, LANES), lambda i: (0, i, 0, 0))],
        out_shape=[
            jax.ShapeDtypeStruct((T, D), F32),
            jax.ShapeDtypeStruct((T, D), BF16),
            jax.ShapeDtypeStruct((2 * PEER_HEADS, T // LANES, PEER_NKEYS, LANES), F32),
        ],
        compiler_params=_cparams(("arbitrary",), 56),
    )(x2d, yconv, ynsa, woc, won, xnw, xq, xqw, kmem, vmem, xo, fnw, pwq, keys)


STAIR = [(a, b) for a in range(PEER_TOPK) for b in range(PEER_TOPK // (a + 1))]
N_STAIR_VREGS = -(-len(STAIR) // SUBLANES)


def _stair_vregs(axis, k):
    return sorted({r // SUBLANES for r, ab in enumerate(STAIR) if ab[axis] == k})


def _peer_topk_kernel(sc_ref, ta_ref, tb_ref, ei_ref, ej_ref, gate_ref, s_out, e_out):
    n_grp = sc_ref.shape[1]
    K = PEER_TOPK
    NV = N_STAIR_VREGS
    kidx = lax.broadcasted_iota(I32, (PEER_NKEYS, LANES), 0).astype(F32)
    ridx = [(lax.broadcasted_iota(I32, (SUBLANES, LANES), 0) + SUBLANES * j).astype(F32) for j in range(NV)]
    ta = [ta_ref[SUBLANES * j:SUBLANES * (j + 1), :] for j in range(NV)]
    tb = [tb_ref[SUBLANES * j:SUBLANES * (j + 1), :] for j in range(NV)]
    zero = jnp.zeros((SUBLANES, LANES), F32)

    def top1(s):
        m = jnp.max(s, axis=0, keepdims=True)
        idx = jnp.min(jnp.where(s == m, kidx, float(PEER_NKEYS)), axis=0, keepdims=True)
        return m, idx, jnp.where(kidx == idx, -jnp.inf, s)

    def sorted_top(s, table, axis):
        val = [zero] * NV
        key = [zero] * NV
        for k in range(K):
            m, idx, s = top1(s)
            for j in _stair_vregs(axis, k):
                hit = table[j] == float(k)
                val[j] = jnp.where(hit, m, val[j])
                key[j] = jnp.where(hit, idx, key[j])
        return val, key

    def head(h, g):
        v1, k1 = sorted_top(sc_ref[2 * h, g], ta, 0)
        v2, k2 = sorted_top(sc_ref[2 * h + 1, g], tb, 1)
        cand = [jnp.where(ta[j] >= 0.0, v1[j] + v2[j], -jnp.inf) for j in range(NV)]
        ce = [k1[j] * float(PEER_NKEYS) + k2[j] for j in range(NV)]
        for k in range(K):
            mx = cand[0]
            for j in range(1, NV):
                mx = jnp.maximum(mx, cand[j])
            m = jnp.max(mx, axis=0, keepdims=True)
            ix = jnp.where(cand[0] == m, ridx[0], float(NV * SUBLANES))
            for j in range(1, NV):
                ix = jnp.minimum(ix, jnp.where(cand[j] == m, ridx[j], float(NV * SUBLANES)))
            idx = jnp.min(ix, axis=0, keepdims=True)
            hits = [ridx[j] == idx for j in range(NV)]
            es = jnp.where(hits[0], ce[0], 0.0)
            for j in range(1, NV):
                es = es + jnp.where(hits[j], ce[j], 0.0)
            cand = [jnp.where(hits[j], -jnp.inf, cand[j]) for j in range(NV)]
            s_out[pl.ds(h * K + k, 1), :] = m
            e_out[pl.ds(h * K + k, 1), :] = jnp.sum(es, axis=0, keepdims=True)

    def group(g, carry):
        def head_pair(u, c):
            head(2 * u, g)
            head(2 * u + 1, g)
            return c

        lax.fori_loop(0, PEER_HEADS // 2, head_pair, 0)
        s_all = s_out[...]
        gates = []
        for h in range(PEER_HEADS):
            sh = s_all[h * K:(h + 1) * K]
            ex = jnp.exp(sh - sh[0:1])
            gates.append(ex / jnp.sum(ex, axis=0, keepdims=True))
        rows = pl.ds(pl.multiple_of(g * LANES, LANES), LANES)
        gate_ref[rows, :] = jnp.concatenate(gates, axis=0).T
        e_int = e_out[...].T.astype(I32)
        ei_ref[rows, :] = jnp.right_shift(e_int, 7)
        ej_ref[rows, :] = jnp.bitwise_and(e_int, PEER_NKEYS - 1)
        return carry

    lax.fori_loop(0, n_grp, group, 0)


def _peer_topk(sc, n_grp=4):
    T = sc.shape[1] * LANES
    tk = n_grp * LANES
    tile = lambda i: (i, 0)
    rows = N_STAIR_VREGS * SUBLANES
    pad = rows - len(STAIR)
    ta = jnp.asarray([float(a) for a, _ in STAIR] + [-1.0] * pad, F32)
    tb = jnp.asarray([float(b) for _, b in STAIR] + [-1.0] * pad, F32)
    ta = jnp.broadcast_to(ta[:, None], (rows, LANES))
    tb = jnp.broadcast_to(tb[:, None], (rows, LANES))
    return pl.pallas_call(
        _peer_topk_kernel,
        grid=(T // tk,),
        in_specs=[pl.BlockSpec((2 * PEER_HEADS, n_grp, PEER_NKEYS, LANES), lambda i: (0, i, 0, 0)),
                  pl.BlockSpec((rows, LANES), lambda i: (0, 0)),
                  pl.BlockSpec((rows, LANES), lambda i: (0, 0))],
        out_specs=[pl.BlockSpec((tk, LANES), tile)] * 3,
        out_shape=[jax.ShapeDtypeStruct((T, LANES), I32), jax.ShapeDtypeStruct((T, LANES), I32),
                   jax.ShapeDtypeStruct((T, LANES), F32)],
        scratch_shapes=[pltpu.VMEM((PEER_HEADS * PEER_TOPK, LANES), F32)] * 2,
        compiler_params=_cparams(("arbitrary",), 40),
    )(sc, ta, tb)


G_ROWS = PEER_NKEYS // 2
G_PITCH = G_ROWS + SUBLANES
HI_MASK = 0xFFFF0000


def _peer_dense_kernel(h_ref, ei_ref, ej_ref, gate_ref, wd_ref, wu_ref, x2_ref, out_ref, g_ref):
    tm = h_ref.shape[0]
    ec = wd_ref.shape[1]
    ipc = ec // LANES
    half = ipc // 2
    half_bits = half.bit_length() - 1
    c = pl.program_id(1)

    @pl.when(c == 0)
    def _scatter():
        out_ref[...] = x2_ref[...]
        m = lax.broadcasted_iota(I32, (PEER_NKEYS, 2 * LANES), 0)
        r = jnp.bitwise_and(m, G_ROWS - 1)
        i_of_m = (jnp.right_shift(r, half_bits) * ipc + jnp.bitwise_and(r, half - 1)
                  + jnp.right_shift(m, 6) * half)
        as_bf16 = lambda v: v.astype(F32).astype(BF16)
        i_of_m = as_bf16(i_of_m)
        n = as_bf16(lax.broadcasted_iota(I32, (LANES, LANES), 0))
        one = jnp.ones((LANES, LANES), BF16)
        zero = jnp.zeros((LANES, LANES), BF16)

        def pair(u, carry):
            t = 2 * u
            ri = as_bf16(jnp.concatenate([ei_ref[pl.ds(t, 1), :], ei_ref[pl.ds(t + 1, 1), :]], axis=1))
            rg = jnp.concatenate([gate_ref[pl.ds(t, 1), :], gate_ref[pl.ds(t + 1, 1), :]], axis=1).astype(BF16)
            c_t = jnp.where(i_of_m == ri, rg, jnp.zeros_like(rg))
            q_a = jnp.where(n == as_bf16(ej_ref[pl.ds(t, 1), :]), one, zero)
            q_b = jnp.where(n == as_bf16(ej_ref[pl.ds(t + 1, 1), :]), one, zero)
            q_t = jnp.concatenate([jnp.concatenate([q_a, zero], axis=1),
                                   jnp.concatenate([zero, q_b], axis=1)], axis=0)
            g2 = _dot_nt(c_t, q_t)
            for tok in range(2):
                g = g2[:, tok * LANES:(tok + 1) * LANES].astype(BF16).astype(F32)
                bits = lax.bitcast_convert_type(g, jnp.uint32)
                word = jnp.bitwise_or(jnp.right_shift(bits[0:G_ROWS], 16),
                                      jnp.bitwise_and(bits[G_ROWS:2 * G_ROWS], jnp.uint32(HI_MASK)))
                g_ref[pl.ds(pl.multiple_of((t + tok) * G_PITCH, SUBLANES), G_ROWS), :] = word
            return carry

        lax.fori_loop(0, tm // 2, pair, 0, unroll=16)

    a = _dot(h_ref[...], wd_ref[...])
    parts = [None] * ipc
    for k in range(half):
        word = g_ref[pl.ds(c * half + k, tm, stride=G_PITCH), :]
        g_lo = lax.bitcast_convert_type(jnp.left_shift(word, 16), F32)
        g_hi = lax.bitcast_convert_type(jnp.bitwise_and(word, jnp.uint32(HI_MASK)), F32)
        parts[k] = (_gelu(a[:, k * LANES:(k + 1) * LANES]) * g_lo).astype(BF16)
        parts[k + half] = (_gelu(a[:, (k + half) * LANES:(k + half + 1) * LANES]) * g_hi).astype(BF16)
    z = jnp.concatenate(parts, axis=1)
    out_ref[...] += _dot(z, wu_ref[...])


def _peer_dense(h3, ei, ej, gate, wd_t, wu, x2, tm=512, ec=1024):
    T, D = h3.shape
    tm = min(tm, T)
    n_exp = wu.shape[0]
    tile = lambda t, c: (t, 0)
    return pl.pallas_call(
        _peer_dense_kernel,
        grid=(T // tm, n_exp // ec),
        in_specs=[
            pl.BlockSpec((tm, D), tile),
            pl.BlockSpec((tm, LANES), tile),
            pl.BlockSpec((tm, LANES), tile),
            pl.BlockSpec((tm, LANES), tile),
            pl.BlockSpec((D, ec), lambda t, c: (0, c)),
            pl.BlockSpec((ec, D), lambda t, c: (c, 0)),
            pl.BlockSpec((tm, D), tile),
        ],
        out_specs=pl.BlockSpec((tm, D), tile),
        out_shape=jax.ShapeDtypeStruct((T, D), F32),
        scratch_shapes=[pltpu.VMEM((tm * G_PITCH, LANES), jnp.uint32)],
        compiler_params=_cparams(("arbitrary", "arbitrary"), 58),
    )(h3, ei, ej, gate, wd_t, wu, x2)


def _pad_half(a, g):
    z = jnp.zeros_like(a)
    return jnp.concatenate([a, z] if g == 0 else [z, a], axis=-1)


def kernel(x, mem, mix_norm_w, w_in, conv_w, conv_b, cmp_pe, cmp_w1, cmp_w2, q_norm_w, k_norm_w, w_out,
           xattn_norm_w, mem_norm_w, xq, xk, xv, xo, xq_norm_w, xk_norm_w, ffn_norm_w, peer_wq, peer_keys,
           peer_down, peer_up):
    B, S, D = x.shape
    T = B * S
    l = 0
    G, R, dh = NSA_GROUPS, NSA_REP, NSA_DH

    w = w_in[l]
    o_q = C_CONV
    o_kv = o_q + NSA_HEADS * dh
    o_g = o_kv + C_KV
    wq = w[:, o_q:o_kv].reshape(D, G, R, dh)
    wq_pad = jnp.concatenate([_pad_half(wq[:, g], g).reshape(D, R * LANES) for g in range(G)], axis=1)
    wg = w[:, o_g:].reshape(D, G, R * 3)
    wg_pad = jnp.pad(wg, ((0, 0), (0, 0), (0, LANES - R * 3))).reshape(D, G * LANES)
    wkv = w[:, o_kv:o_g].reshape(D, 6, LANES)
    wkn = wkv[:, jnp.array([0, 1, 2, 4])].reshape(D, C_KN)
    wvt = wkv[:, jnp.array([3, 5])].reshape(D, 2 * LANES).T.astype(BF16)
    w_all = jnp.concatenate([w[:, :o_q], wq_pad, wkn, wg_pad], axis=1).astype(BF16)
    qw = q_norm_w[l] * dh ** -0.5
    qw_pad = jnp.concatenate([jnp.tile(_pad_half(qw, g), R) for g in range(G)]).reshape(1, C_Q)
    kw = jnp.stack([jnp.tile(k_norm_w[l, 1], G), jnp.tile(k_norm_w[l, 2], G)])

    yconv, q, kvc, kn, vt, gates = _in_proj(
        x.reshape(T, D), mix_norm_w[l].reshape(1, D), w_all, wvt, conv_w[l], conv_b[l].reshape(1, CONV_WIDTH),
        qw_pad, kw, S)

    n_rows = S // CMP_STRIDE
    kc = kvc.reshape(B, n_rows, CMP_STRIDE, 2, G, dh).transpose(0, 3, 4, 1, 2, 5).reshape(
        B, 2, G, n_rows, HALF_FEATS)
    pe = cmp_pe[l].reshape(2, 2, HALF_FEATS)
    w1 = jnp.pad(cmp_w1[l], ((0, 0), (0, 0), (0, LANES - dh))).astype(BF16)
    w2 = jnp.pad(cmp_w2[l], ((0, 0), (0, LANES - dh), (0, 0)))
    w2 = jnp.stack([_pad_half(w2, g) for g in range(G)], axis=1).astype(BF16)
    knw = jnp.stack([_pad_half(k_norm_w[l, 0], g) for g in range(G)]).reshape(G, 1, LANES)
    kvcmp = _compress(kc, pe, w1, w2[0], jnp.swapaxes(w2[1], -1, -2), knw)

    n_sel = S // SEL_BLOCK
    cmp_start = jnp.arange(n_rows) * CMP_STRIDE
    sel_start = jnp.arange(n_sel) * SEL_BLOCK
    selmap_t = ((cmp_start[None, :] < sel_start[:, None] + SEL_BLOCK)
                & (cmp_start[None, :] + CMP_BLOCK > sel_start[:, None])
                & (jnp.arange(n_rows)[None, :] < n_rows - 1)).astype(BF16)
    ynsa = _nsa(q.reshape(B, S, C_Q), kn.reshape(B, S, 2 * LANES), vt, kvcmp, gates.reshape(B, S, C_G), selmap_t)

    kmem, vmem = _mem_kv(mem, mem_norm_w[l].reshape(1, D), xk[l].astype(BF16), xv[l].astype(BF16),
                         xk_norm_w[l].reshape(1, X_DH))

    wo = w_out[l]
    won = wo[CONV_WIDTH:].reshape(G, R, dh, D)
    won_pad = jnp.concatenate(
        [jnp.concatenate([won[g], jnp.zeros_like(won[g])] if g == 0 else [jnp.zeros_like(won[g]), won[g]],
                         axis=1).reshape(R * LANES, D) for g in range(G)], axis=0).astype(BF16)
    keys = peer_keys[l].reshape(2 * PEER_HEADS, PEER_NKEYS, PEER_HALF).astype(BF16)
    x2, h3, sc = _post_mix(
        x.reshape(T, D), yconv, ynsa.reshape(T, C_Q), wo[:CONV_WIDTH].astype(BF16), won_pad,
        xattn_norm_w[l].reshape(1, D), xq[l].astype(BF16),
        (xq_norm_w[l] * X_DH ** -0.5).reshape(1, X_DH), kmem, vmem, xo[l].astype(BF16),
        ffn_norm_w[l].reshape(1, D), peer_wq[l].astype(BF16), keys, S)

    ei, ej, gate = _peer_topk(sc)
    out = _peer_dense(h3, ei, ej, gate, peer_down[l].T.astype(BF16), peer_up[l].astype(BF16), x2)
    return out.reshape(B, S, D)
```

```python
import functools

import jax
import jax.numpy as jnp
from jax import lax
from jax.experimental import pallas as pl
from jax.experimental.pallas import tpu as pltpu

F32 = jnp.float32
BF16 = jnp.bfloat16
I32 = jnp.int32

EPS = 1e-6
NEG = -1e30
LANES = 128
SUBLANES = 8

D_MODEL = 1024
CONV_WIDTH = 512
NSA_HEADS = 8
NSA_GROUPS = 2
NSA_REP = NSA_HEADS // NSA_GROUPS
NSA_DH = 64
CMP_BLOCK = 32
CMP_STRIDE = 16
SEL_BLOCK = 64
SEL_TOPN = 16
WINDOW = 512
X_HEADS = 4
X_DH = D_MODEL // X_HEADS
PEER_HEADS = 8
PEER_NKEYS = 128
PEER_HALF = 128
PEER_TOPK = 16
PEER_EXPERTS = PEER_NKEYS * PEER_NKEYS

NT_DIMS = (((1,), (1,)), ((), ()))


def _dot(a, b):
    return jnp.dot(a, b, preferred_element_type=F32)


def _dot_nt(a, b):
    return lax.dot_general(a, b, NT_DIMS, preferred_element_type=F32)


def _gelu(x):
    return 0.5 * x * (1.0 + lax.erf(x * (2.0 ** -0.5)))


def _rms(x, w):
    ms = jnp.mean(x * x, axis=-1, keepdims=True)
    return x * lax.rsqrt(ms + EPS) * w


def _cparams(sem, vmem_mb):
    return pltpu.CompilerParams(dimension_semantics=sem, vmem_limit_bytes=vmem_mb * 1024 * 1024)


C_CONV = 3 * CONV_WIDTH
C_Q = NSA_HEADS * LANES
C_KV = 6 * NSA_GROUPS * NSA_DH
C_KN = 4 * LANES
C_G = NSA_GROUPS * LANES
C_ALL = C_CONV + C_Q + C_KN + C_G


def _in_proj_kernel(tiles_per_seq, x_ref, nw_ref, w_ref, wvt_ref, cw_ref, cb_ref, qw_ref, kw_ref,
                    yconv_ref, q_ref, kvc_ref, kn_ref, vt_ref, g_ref, carry_ref):
    tm = x_ref.shape[0]
    hn = _rms(x_ref[...], nw_ref[...]).astype(BF16)

    p = _dot(hn, w_ref[:, 0:C_CONV])
    b_g = p[:, 0:CONV_WIDTH]
    u = p[:, CONV_WIDTH:2 * CONV_WIDTH] * p[:, 2 * CONV_WIDTH:3 * CONV_WIDTH]

    @pl.when((pl.program_id(0) % tiles_per_seq) == 0)
    def _sequence_start():
        carry_ref[...] = jnp.zeros(carry_ref.shape, F32)

    prev = carry_ref[...]
    p1 = prev[SUBLANES - 1:SUBLANES, :]
    p2 = prev[SUBLANES - 2:SUBLANES - 1, :]
    row = lax.broadcasted_iota(I32, u.shape, 0)
    u1 = jnp.where(row == 0, p1, pltpu.roll(u, 1, axis=0))
    u2 = jnp.where(row == 0, p2, jnp.where(row == 1, p1, pltpu.roll(u, 2, axis=0)))
    carry_ref[...] = u[tm - SUBLANES:tm, :]
    cw = cw_ref[...]
    y = b_g * (cw[0:1, :] * u2 + cw[1:2, :] * u1 + cw[2:3, :] * u + cb_ref[...])
    yconv_ref[...] = y.astype(BF16)

    pq = _dot(hn, w_ref[:, C_CONV:C_CONV + C_Q])
    for h in range(NSA_HEADS):
        blk = pq[:, h * LANES:(h + 1) * LANES]
        ms = jnp.sum(blk * blk, axis=-1, keepdims=True) * (1.0 / NSA_DH)
        q_ref[:, h * LANES:(h + 1) * LANES] = (
            blk * lax.rsqrt(ms + EPS) * qw_ref[:, h * LANES:(h + 1) * LANES]).astype(BF16)

    pkv = _dot(hn, w_ref[:, C_CONV + C_Q:C_CONV + C_Q + C_KN])
    kvc_ref[...] = pkv[:, 0:2 * LANES]
    lane = lax.broadcasted_iota(I32, (tm, LANES), 1)
    lo = lane < NSA_DH
    for j in range(2):
        blk = pkv[:, (2 + j) * LANES:(3 + j) * LANES]
        sq = blk * blk
        ms_lo = jnp.sum(jnp.where(lo, sq, 0.0), axis=-1, keepdims=True) * (1.0 / NSA_DH)
        ms_hi = jnp.sum(jnp.where(lo, 0.0, sq), axis=-1, keepdims=True) * (1.0 / NSA_DH)
        scale = jnp.where(lo, lax.rsqrt(ms_lo + EPS), lax.rsqrt(ms_hi + EPS))
        kn_ref[:, j * LANES:(j + 1) * LANES] = (blk * scale * kw_ref[j:j + 1, :]).astype(BF16)

    for j in range(tm // LANES):
        vt_ref[j] = _dot_nt(wvt_ref[...], hn[j * LANES:(j + 1) * LANES, :]).astype(BF16)

    pg = _dot(hn, w_ref[:, C_CONV + C_Q + C_KN:C_ALL])
    g_ref[...] = jax.nn.sigmoid(pg)


def _in_proj(x2d, nw, w_all, wvt, cw, cb, qw, kw, seq, tm=256):
    T = x2d.shape[0]
    const = lambda i: (0, 0)
    tile = lambda i: (i, 0)
    return pl.pallas_call(
        functools.partial(_in_proj_kernel, seq // tm),
        grid=(T // tm,),
        in_specs=[
            pl.BlockSpec((tm, D_MODEL), tile),
            pl.BlockSpec((1, D_MODEL), const),
            pl.BlockSpec((D_MODEL, C_ALL), const),
            pl.BlockSpec((2 * LANES, D_MODEL), const),
            pl.BlockSpec((3, CONV_WIDTH), const),
            pl.BlockSpec((1, CONV_WIDTH), const),
            pl.BlockSpec((1, C_Q), const),
            pl.BlockSpec((2, LANES), const),
        ],
        out_specs=[
            pl.BlockSpec((tm, CONV_WIDTH), tile),
            pl.BlockSpec((tm, C_Q), tile),
            pl.BlockSpec((tm, 2 * LANES), tile),
            pl.BlockSpec((tm, 2 * LANES), tile),
            pl.BlockSpec((tm // LANES, 2 * LANES, LANES), lambda i: (i, 0, 0)),
            pl.BlockSpec((tm, C_G), tile),
        ],
        out_shape=[
            jax.ShapeDtypeStruct((T, CONV_WIDTH), BF16),
            jax.ShapeDtypeStruct((T, C_Q), BF16),
            jax.ShapeDtypeStruct((T, 2 * LANES), F32),
            jax.ShapeDtypeStruct((T, 2 * LANES), BF16),
            jax.ShapeDtypeStruct((T // LANES, 2 * LANES, LANES), BF16),
            jax.ShapeDtypeStruct((T, C_G), F32),
        ],
        scratch_shapes=[pltpu.VMEM((SUBLANES, CONV_WIDTH), F32)],
        compiler_params=_cparams(("arbitrary",), 48),
    )(x2d, nw, w_all, wvt, cw, cb, qw, kw)


HALF_FEATS = CMP_STRIDE * NSA_DH


def _compress_kernel(kc_ref, pe_ref, w1_ref, w2_ref, w2t_ref, knw_ref, out_ref):
    n_rows = kc_ref.shape[3]
    row = lax.broadcasted_iota(I32, (n_rows, LANES), 0)
    col = lax.broadcasted_iota(I32, (LANES, n_rows), 1)
    for kv in range(2):
        for g in range(NSA_GROUPS):
            kr = kc_ref[0, kv, g]
            a = (kr + pe_ref[kv, 0:1, :]).astype(BF16)
            b = (kr + pe_ref[kv, 1:2, :]).astype(BF16)
            ha = _dot(a, w1_ref[kv, 0:HALF_FEATS, :])
            hb = _dot(b, w1_ref[kv, HALF_FEATS:2 * HALF_FEATS, :])
            hid = _gelu(ha + pltpu.roll(hb, n_rows - 1, axis=0)).astype(BF16)
            if kv == 0:
                out = _dot(hid, w2_ref[g])
                ms = jnp.sum(out * out, axis=-1, keepdims=True) * (1.0 / NSA_DH)
                out = out * lax.rsqrt(ms + EPS) * knw_ref[g]
                out_ref[0, kv, g] = jnp.where(row < n_rows - 1, out, 0.0).astype(BF16)
            else:
                out = _dot_nt(w2t_ref[g], hid)
                out_ref[0, kv, g] = jnp.where(col < n_rows - 1, out, 0.0).astype(BF16)


def _compress(kc, pe, w1, w2, w2t, knw):
    B, n_rows = kc.shape[0], kc.shape[3]
    return pl.pallas_call(
        _compress_kernel,
        grid=(B,),
        in_specs=[
            pl.BlockSpec((1, 2, NSA_GROUPS, n_rows, HALF_FEATS), lambda b: (b, 0, 0, 0, 0)),
            pl.BlockSpec((2, 2, HALF_FEATS), lambda b: (0, 0, 0)),
            pl.BlockSpec((2, 2 * HALF_FEATS, LANES), lambda b: (0, 0, 0)),
            pl.BlockSpec((NSA_GROUPS, LANES, LANES), lambda b: (0, 0, 0)),
            pl.BlockSpec((NSA_GROUPS, LANES, LANES), lambda b: (0, 0, 0)),
            pl.BlockSpec((NSA_GROUPS, 1, LANES), lambda b: (0, 0, 0)),
        ],
        out_specs=pl.BlockSpec((1, 2, NSA_GROUPS, n_rows, LANES), lambda b: (b, 0, 0, 0, 0)),
        out_shape=jax.ShapeDtypeStruct((B, 2, NSA_GROUPS, n_rows, LANES), BF16),
        compiler_params=_cparams(("arbitrary",), 32),
    )(kc, pe, w1, w2, w2t, knw)


SEL_CHUNK = 512


def _nsa_kernel(q_ref, kn_ref, vt_ref, cmp_ref, g_ref, selmap_ref, y_ref, sel_ref, acc_ref):
    tq = q_ref.shape[1]
    cols = NSA_HEADS * tq
    i = pl.program_id(1)
    t0 = i * tq
    qb = q_ref[0]
    qs = jnp.concatenate([qb[:, h * LANES:(h + 1) * LANES] for h in range(NSA_HEADS)], axis=0)
    qpos = t0 + jnp.bitwise_and(lax.broadcasted_iota(I32, (1, cols), 1), tq - 1)
    both = lambda kv: (cmp_ref[0, kv, 0].astype(F32) + cmp_ref[0, kv, 1].astype(F32)).astype(BF16)

    n_cmp = cmp_ref.shape[3]
    s = _dot_nt(both(0), qs)
    cmp_last = lax.broadcasted_iota(I32, (n_cmp, cols), 0) * CMP_STRIDE + (CMP_BLOCK - 1)
    vis_c = cmp_last <= qpos
    s = jnp.where(vis_c, s, NEG)
    m = jnp.max(s, axis=0, keepdims=True)
    m = jnp.where(m > 0.5 * NEG, m, 0.0)
    e = jnp.where(vis_c, jnp.exp(s - m), 0.0)
    d = jnp.sum(e, axis=0, keepdims=True)
    p_c = e / jnp.where(d > 0.0, d, 1.0)
    o_c = _dot(both(1), p_c.astype(BF16))

    n_sel = selmap_ref.shape[0]
    blk = lax.broadcasted_iota(I32, (n_sel, tq), 0)
    cur = jnp.right_shift(t0 + lax.broadcasted_iota(I32, (n_sel, tq), 1), 6)
    forced = (blk == 0) | (blk == cur) | (blk == cur - 1)
    chosen = []
    for g in range(NSA_GROUPS):
        psum = p_c[:, g * NSA_REP * tq:(g * NSA_REP + 1) * tq]
        for r in range(1, NSA_REP):
            psum = psum + p_c[:, (g * NSA_REP + r) * tq:(g * NSA_REP + r + 1) * tq]
        p_hi = psum.astype(BF16)
        p_lo = (psum - p_hi.astype(F32)).astype(BF16)
        imp = _dot(selmap_ref[...], p_hi) + _dot(selmap_ref[...], p_lo)
        imp = jnp.where(blk > cur, -jnp.inf, jnp.where(forced, jnp.inf, imp))
        rank = jnp.zeros((n_sel, tq), I32)
        for jp in range(n_sel):
            other = imp[jp:jp + 1, :]
            beats = (other > imp) | ((other == imp) & (blk > jp))
            rank = rank + beats.astype(I32)
        chosen += [rank < min(SEL_TOPN, n_sel)] * NSA_REP
    sel_ref[...] = jnp.where(jnp.concatenate(chosen, axis=1), qpos, -1)

    acc_ref[...] = jnp.zeros(acc_ref.shape, F32)
    units = SEL_CHUNK // LANES
    blocks = SEL_CHUNK // SEL_BLOCK
    koff = lax.broadcasted_iota(I32, (SEL_BLOCK, cols), 0)

    def sel_chunk(c, carry):
        m_old, l_old = carry
        start = pl.multiple_of(c * SEL_CHUNK, SEL_CHUNK)
        k = kn_ref[0, pl.ds(start, SEL_CHUNK), 0:LANES]
        vt = jnp.concatenate([vt_ref[c * units + u, 0:LANES, :] for u in range(units)], axis=1)
        sc = _dot_nt(k, qs)
        parts = []
        for b in range(blocks):
            limit = sel_ref[pl.ds(c * blocks + b, 1), :] - (start + b * SEL_BLOCK)
            parts.append(jnp.where(koff <= limit, sc[b * SEL_BLOCK:(b + 1) * SEL_BLOCK], NEG))
        sc = jnp.concatenate(parts, axis=0)
        m_new = jnp.maximum(m_old, jnp.max(sc, axis=0, keepdims=True))
        alpha = jnp.exp(m_old - m_new)
        p = jnp.exp(sc - m_new)
        acc_ref[...] = alpha * acc_ref[...] + _dot(vt, p.astype(BF16))
        return m_new, alpha * l_old + jnp.sum(p, axis=0, keepdims=True)

    init = (jnp.full((1, cols), NEG, F32), jnp.zeros((1, cols), F32))
    _, l_s = lax.fori_loop(0, (t0 + tq + SEL_CHUNK - 1) // SEL_CHUNK, sel_chunk, init)
    o_s = acc_ref[...]

    n_win = WINDOW + tq
    w_start = pl.multiple_of(jnp.maximum(t0 - WINDOW, 0), LANES)
    w_unit = w_start // LANES
    k_w = kn_ref[0, pl.ds(w_start, n_win), LANES:2 * LANES]
    vt_w = jnp.concatenate([vt_ref[w_unit + u, LANES:2 * LANES, :] for u in range(n_win // LANES)], axis=1)
    diff = (qpos - w_start) - lax.broadcasted_iota(I32, (n_win, cols), 0)
    in_window = lax.bitcast_convert_type(diff, jnp.uint32) < jnp.uint32(WINDOW)
    s_w = jnp.where(in_window, _dot_nt(k_w, qs), NEG)
    p_w = jnp.exp(s_w - jnp.max(s_w, axis=0, keepdims=True))
    l_w = jnp.sum(p_w, axis=0, keepdims=True)
    o_w = _dot(vt_w, p_w.astype(BF16))

    g_t = [g_ref[0, :, g * LANES:(g + 1) * LANES].T for g in range(NSA_GROUPS)]
    gate = lambda br: jnp.concatenate(
        [g_t[h // NSA_REP][3 * (h % NSA_REP) + br:3 * (h % NSA_REP) + br + 1, :] for h in range(NSA_HEADS)], axis=1)
    y_t = o_c * gate(0) + o_s * (gate(1) / l_s) + o_w * (gate(2) / l_w)
    for h in range(NSA_HEADS):
        y_ref[0, :, h * LANES:(h + 1) * LANES] = y_t[:, h * tq:(h + 1) * tq].T.astype(BF16)


def _nsa(q, kn, vt, kvcmp, gates, selmap, tq=128):
    B, S = q.shape[0], q.shape[1]
    n_sel = S // SEL_BLOCK
    cols = NSA_HEADS * tq
    units = S // LANES
    n_cmp = kvcmp.shape[3]
    return pl.pallas_call(
        _nsa_kernel,
        grid=(B, S // tq),
        in_specs=[
            pl.BlockSpec((1, tq, C_Q), lambda b, i: (b, i, 0)),
            pl.BlockSpec((1, S, 2 * LANES), lambda b, i: (b, 0, 0)),
            pl.BlockSpec((units, 2 * LANES, LANES), lambda b, i: (b, 0, 0)),
            pl.BlockSpec((1, 2, NSA_GROUPS, n_cmp, LANES), lambda b, i: (b, 0, 0, 0, 0)),
            pl.BlockSpec((1, tq, C_G), lambda b, i: (b, i, 0)),
            pl.BlockSpec((n_sel, n_cmp), lambda b, i: (0, 0)),
        ],
        out_specs=pl.BlockSpec((1, tq, C_Q), lambda b, i: (b, i, 0)),
        out_shape=jax.ShapeDtypeStruct((B, S, C_Q), BF16),
        scratch_shapes=[
            pltpu.VMEM((n_sel, cols), I32),
            pltpu.VMEM((LANES, cols), F32),
        ],
        compiler_params=_cparams(("arbitrary", "arbitrary"), 56),
    )(q, kn, vt, kvcmp, gates, selmap)


def _mem_kv_kernel(mem_ref, nw_ref, wk_ref, wv_ref, kw_ref, k_ref, v_ref):
    mn = _rms(mem_ref[0], nw_ref[...]).astype(BF16)
    k = _dot(mn, wk_ref[...])
    for h in range(X_HEADS):
        sl = slice(h * X_DH, (h + 1) * X_DH)
        k_ref[0, :, sl] = _rms(k[:, sl], kw_ref[...]).astype(BF16)
    v_ref[0] = _dot(mn, wv_ref[...]).astype(BF16)


def _mem_kv(mem, nw, wk, wv, kw):
    B, M, D = mem.shape
    blk = pl.BlockSpec((1, M, D), lambda b: (b, 0, 0))
    const = lambda b: (0, 0)
    return pl.pallas_call(
        _mem_kv_kernel,
        grid=(B,),
        in_specs=[blk, pl.BlockSpec((1, D), const), pl.BlockSpec((D, D), const),
                  pl.BlockSpec((D, D), const), pl.BlockSpec((1, X_DH), const)],
        out_specs=[blk, blk],
        out_shape=[jax.ShapeDtypeStruct((B, M, D), BF16)] * 2,
        compiler_params=_cparams(("arbitrary",), 32),
    )(mem, nw, wk, wv, kw)


def _post_mix_kernel(x_ref, yc_ref, yn_ref, woc_ref, won_ref, xnw_ref, xq_ref, xqw_ref, km_ref, vm_ref,
                     xo_ref, fnw_ref, pwq_ref, keys_ref, x2_ref, h3_ref, sc_ref):
    x1 = x_ref[...] + _dot(yc_ref[...], woc_ref[...]) + _dot(yn_ref[...], won_ref[...])

    h2 = _rms(x1, xnw_ref[...]).astype(BF16)
    qx = _dot(h2, xq_ref[...])
    heads = []
    for h in range(X_HEADS):
        sl = slice(h * X_DH, (h + 1) * X_DH)
        qh = _rms(qx[:, sl], xqw_ref[...]).astype(BF16)
        s = _dot_nt(qh, km_ref[0, :, sl])
        e = jnp.exp(s - jnp.max(s, axis=-1, keepdims=True))
        p = e / jnp.sum(e, axis=-1, keepdims=True)
        heads.append(_dot(p.astype(BF16), vm_ref[0, :, sl]))
    o = jnp.concatenate(heads, axis=1).astype(BF16)
    x2 = x1 + _dot(o, xo_ref[...])
    x2_ref[...] = x2

    h3 = _rms(x2, fnw_ref[...]).astype(BF16)
    h3_ref[...] = h3
    qp = _dot(h3, pwq_ref[...]).astype(BF16)
    for c in range(2 * PEER_HEADS):
        sl = slice(c * PEER_HALF, (c + 1) * PEER_HALF)
        for j in range(qp.shape[0] // LANES):
            sc_ref[c, j] = _dot_nt(keys_ref[c], qp[j * LANES:(j + 1) * LANES, sl])


def _post_mix(x2d, yconv, ynsa, woc, won, xnw, xq, xqw, kmem, vmem, xo, fnw, pwq, keys, seq, tm=256):
    T, D = x2d.shape
    M = kmem.shape[1]
    n_sc = 2 * PEER_HEADS * PEER_NKEYS
    tile = lambda i: (i, 0)
    const = lambda i: (0, 0)
    per_batch = lambda i: ((i * tm) // seq, 0, 0)
    return pl.pallas_call(
        _post_mix_kernel,
        grid=(T // tm,),
        in_specs=[
            pl.BlockSpec((tm, D), tile),
            pl.BlockSpec((tm, CONV_WIDTH), tile),
            pl.BlockSpec((tm, C_Q), tile),
            pl.BlockSpec((CONV_WIDTH, D), const),
            pl.BlockSpec((C_Q, D), const),
            pl.BlockSpec((1, D), const),
            pl.BlockSpec((D, D), const),
            pl.BlockSpec((1, X_DH), const),
            pl.BlockSpec((1, M, D), per_batch),
            pl.BlockSpec((1, M, D), per_batch),
            pl.BlockSpec((D, D), const),
            pl.BlockSpec((1, D), const),
            pl.BlockSpec((D, n_sc), const),
            pl.BlockSpec((2 * PEER_HEADS, PEER_NKEYS, PEER_HALF), lambda i: (0, 0, 0)),
        ],
        out_specs=[pl.BlockSpec((tm, D), tile), pl.BlockSpec((tm, D), tile),
                   pl.BlockSpec((2 * PEER_HEADS, tm // LANES, PEER_NKEYS, LANES), lambda i: (0, i, 0, 0))],
        out_shape=[
            jax.ShapeDtypeStruct((T, D), F32),
            jax.ShapeDtypeStruct((T, D), BF16),
            jax.ShapeDtypeStruct((2 * PEER_HEADS, T // LANES, PEER_NKEYS, LANES), F32),
        ],
        compiler_params=_cparams(("arbitrary",), 56),
    )(x2d, yconv, ynsa, woc, won, xnw, xq, xqw, kmem, vmem, xo, fnw, pwq, keys)


STAIR = [(a, b) for a in range(PEER_TOPK) for b in range(PEER_TOPK // (a + 1))]
N_STAIR_VREGS = -(-len(STAIR) // SUBLANES)


def _stair_vregs(axis, k):
    return sorted({r // SUBLANES for r, ab in enumerate(STAIR) if ab[axis] == k})


def _stair_tables():
    rows = N_STAIR_VREGS * SUBLANES
    pad = rows - len(STAIR)
    ta = jnp.asarray([float(a) for a, _ in STAIR] + [-1.0] * pad, F32)
    tb = jnp.asarray([float(b) for _, b in STAIR] + [-1.0] * pad, F32)
    return jnp.broadcast_to(ta[:, None], (rows, LANES)), jnp.broadcast_to(tb[:, None], (rows, LANES))


def _make_topk(sc_ref, ta_ref, tb_ref, s_out, e_out):
    K = PEER_TOPK
    NV = N_STAIR_VREGS
    kidx = lax.broadcasted_iota(I32, (PEER_NKEYS, LANES), 0).astype(F32)
    ridx = [(lax.broadcasted_iota(I32, (SUBLANES, LANES), 0) + SUBLANES * j).astype(F32) for j in range(NV)]
    ta = [ta_ref[SUBLANES * j:SUBLANES * (j + 1), :] for j in range(NV)]
    tb = [tb_ref[SUBLANES * j:SUBLANES * (j + 1), :] for j in range(NV)]
    zero = jnp.zeros((SUBLANES, LANES), F32)

    def top1(s):
        m = jnp.max(s, axis=0, keepdims=True)
        idx = jnp.min(jnp.where(s == m, kidx, float(PEER_NKEYS)), axis=0, keepdims=True)
        return m, idx, jnp.where(kidx == idx, -jnp.inf, s)

    def sorted_top(s, table, axis):
        val = [zero] * NV
        key = [zero] * NV
        for k in range(K):
            m, idx, s = top1(s)
            for j in _stair_vregs(axis, k):
                hit = table[j] == float(k)
                val[j] = jnp.where(hit, m, val[j])
                key[j] = jnp.where(hit, idx, key[j])
        return val, key

    def head(h, g):
        v1, k1 = sorted_top(sc_ref[2 * h, g], ta, 0)
        v2, k2 = sorted_top(sc_ref[2 * h + 1, g], tb, 1)
        cand = [jnp.where(ta[j] >= 0.0, v1[j] + v2[j], -jnp.inf) for j in range(NV)]
        ce = [k1[j] * float(PEER_NKEYS) + k2[j] for j in range(NV)]
        for k in range(K):
            mx = cand[0]
            for j in range(1, NV):
                mx = jnp.maximum(mx, cand[j])
            m = jnp.max(mx, axis=0, keepdims=True)
            ix = jnp.where(cand[0] == m, ridx[0], float(NV * SUBLANES))
            for j in range(1, NV):
                ix = jnp.minimum(ix, jnp.where(cand[j] == m, ridx[j], float(NV * SUBLANES)))
            idx = jnp.min(ix, axis=0, keepdims=True)
            hits = [ridx[j] == idx for j in range(NV)]
            es = jnp.where(hits[0], ce[0], 0.0)
            for j in range(1, NV):
                es = es + jnp.where(hits[j], ce[j], 0.0)
            cand = [jnp.where(hits[j], -jnp.inf, cand[j]) for j in range(NV)]
            s_out[pl.ds(h * K + k, 1), :] = m
            e_out[pl.ds(h * K + k, 1), :] = jnp.sum(es, axis=0, keepdims=True)

    def finish(rows, ei_dst, ej_dst, gate_dst):
        s_all = s_out[...]
        gates = []
        for h in range(PEER_HEADS):
            sh = s_all[h * K:(h + 1) * K]
            ex = jnp.exp(sh - sh[0:1])
            gates.append(ex / jnp.sum(ex, axis=0, keepdims=True))
        gate_dst[rows, :] = jnp.concatenate(gates, axis=0).T
        e_int = e_out[...].T.astype(I32)
        ei_dst[rows, :] = jnp.right_shift(e_int, 7)
        ej_dst[rows, :] = jnp.bitwise_and(e_int, PEER_NKEYS - 1)

    return head, finish


G_ROWS = PEER_NKEYS // 2
G_PITCH = G_ROWS + SUBLANES
HI_MASK = 0xFFFF0000


def _peer_kernel(sc_ref, ta_ref, tb_ref, h_ref, wd_ref, wu_ref, x2_ref, out_ref,
                 g_ref, ei_ref, ej_ref, gate_ref, s_out, e_out):
    tm = h_ref.shape[0]
    ec = wd_ref.shape[1]
    ipc = ec // LANES
    half = ipc // 2
    half_bits = half.bit_length() - 1
    t = pl.program_id(0)
    c = pl.program_id(1)
    n_chunks = pl.num_programs(1)
    n_grp = tm // LANES
    steps_per_grp = PEER_EXPERTS // ec // n_grp
    heads_per_step = PEER_HEADS // steps_per_grp
    rd = (t + 1) % 2
    wr = t % 2
    ei_rd, ej_rd, gate_rd = ei_ref.at[rd], ej_ref.at[rd], gate_ref.at[rd]

    @pl.when((t == 0) & (c == 0))
    def _first_step():
        ei_ref[...] = jnp.zeros(ei_ref.shape, I32)
        ej_ref[...] = jnp.zeros(ej_ref.shape, I32)
        gate_ref[...] = jnp.zeros(gate_ref.shape, F32)

    @pl.when(c == 0)
    def _scatter():
        out_ref[...] = x2_ref[...]
        m = lax.broadcasted_iota(I32, (PEER_NKEYS, 2 * LANES), 0)
        r = jnp.bitwise_and(m, G_ROWS - 1)
        i_of_m = (jnp.right_shift(r, half_bits) * ipc + jnp.bitwise_and(r, half - 1)
                  + jnp.right_shift(m, 6) * half)
        as_bf16 = lambda v: v.astype(F32).astype(BF16)
        i_of_m = as_bf16(i_of_m)
        n = as_bf16(lax.broadcasted_iota(I32, (LANES, LANES), 0))
        one = jnp.ones((LANES, LANES), BF16)
        zero = jnp.zeros((LANES, LANES), BF16)

        def pair(u, carry):
            ta_, tb_ = pl.ds(2 * u, 1), pl.ds(2 * u + 1, 1)
            ri = as_bf16(jnp.concatenate([ei_rd[ta_, :], ei_rd[tb_, :]], axis=1))
            rg = jnp.concatenate([gate_rd[ta_, :], gate_rd[tb_, :]], axis=1).astype(BF16)
            c_t = jnp.where(i_of_m == ri, rg, jnp.zeros_like(rg))
            q_a = jnp.where(n == as_bf16(ej_rd[ta_, :]), one, zero)
            q_b = jnp.where(n == as_bf16(ej_rd[tb_, :]), one, zero)
            q_t = jnp.concatenate([jnp.concatenate([q_a, zero], axis=1),
                                   jnp.concatenate([zero, q_b], axis=1)], axis=0)
            g2 = _dot_nt(c_t, q_t)
            for tok in range(2):
                g = g2[:, tok * LANES:(tok + 1) * LANES].astype(BF16).astype(F32)
                bits = lax.bitcast_convert_type(g, jnp.uint32)
                word = jnp.bitwise_or(jnp.right_shift(bits[0:G_ROWS], 16),
                                      jnp.bitwise_and(bits[G_ROWS:2 * G_ROWS], jnp.uint32(HI_MASK)))
                g_ref[pl.ds(pl.multiple_of((2 * u + tok) * G_PITCH, SUBLANES), G_ROWS), :] = word
            return carry

        lax.fori_loop(0, tm // 2, pair, 0, unroll=16)

    head, finish = _make_topk(sc_ref, ta_ref, tb_ref, s_out, e_out)
    grp = c // steps_per_grp
    h0 = (c % steps_per_grp) * heads_per_step
    for hh in range(heads_per_step):
        head(h0 + hh, grp)

    a = _dot(h_ref[...], wd_ref[...])
    parts = [None] * ipc
    for k in range(half):
        word = g_ref[pl.ds(c * half + k, tm, stride=G_PITCH), :]
        g_lo = lax.bitcast_convert_type(jnp.left_shift(word, 16), F32)
        g_hi = lax.bitcast_convert_type(jnp.bitwise_and(word, jnp.uint32(HI_MASK)), F32)
        parts[k] = (_gelu(a[:, k * LANES:(k + 1) * LANES]) * g_lo).astype(BF16)
        parts[k + half] = (_gelu(a[:, (k + half) * LANES:(k + half + 1) * LANES]) * g_hi).astype(BF16)
    z = jnp.concatenate(parts, axis=1)
    out_ref[...] += _dot(z, wu_ref[...])

    @pl.when(c % steps_per_grp == steps_per_grp - 1)
    def _group_done():
        rows = pl.ds(pl.multiple_of(grp * LANES, LANES), LANES)
        finish(rows, ei_ref.at[wr], ej_ref.at[wr], gate_ref.at[wr])


def _peer(sc, h3, wd_t, wu, x2, tm=512, ec=1024):
    T, D = h3.shape
    tm = min(tm, T)
    n_tiles = T // tm
    n_grp = tm // LANES
    n_chunks = wu.shape[0] // ec
    assert n_chunks % n_grp == 0 and PEER_HEADS % (n_chunks // n_grp) == 0
    ta, tb = _stair_tables()
    prev = lambda t, c: (jnp.maximum(t - 1, 0), 0)
    return pl.pallas_call(
        _peer_kernel,
        grid=(n_tiles + 1, n_chunks),
        in_specs=[
            pl.BlockSpec((2 * PEER_HEADS, n_grp, PEER_NKEYS, LANES),
                         lambda t, c: (0, jnp.minimum(t, n_tiles - 1), 0, 0)),
            pl.BlockSpec(ta.shape, lambda t, c: (0, 0)),
            pl.BlockSpec(tb.shape, lambda t, c: (0, 0)),
            pl.BlockSpec((tm, D), prev),
            pl.BlockSpec((D, ec), lambda t, c: (0, c)),
            pl.BlockSpec((ec, D), lambda t, c: (c, 0)),
            pl.BlockSpec((tm, D), prev),
        ],
        out_specs=pl.BlockSpec((tm, D), prev),
        out_shape=jax.ShapeDtypeStruct((T, D), F32),
        scratch_shapes=[
            pltpu.VMEM((tm * G_PITCH, LANES), jnp.uint32),
            pltpu.VMEM((2, tm, LANES), I32),
            pltpu.VMEM((2, tm, LANES), I32),
            pltpu.VMEM((2, tm, LANES), F32),
            pltpu.VMEM((PEER_HEADS * PEER_TOPK, LANES), F32),
            pltpu.VMEM((PEER_HEADS * PEER_TOPK, LANES), F32),
        ],
        compiler_params=_cparams(("arbitrary", "arbitrary"), 58),
    )(sc, ta, tb, h3, wd_t, wu, x2)


def _pad_half(a, g):
    z = jnp.zeros_like(a)
    return jnp.concatenate([a, z] if g == 0 else [z, a], axis=-1)


def kernel(x, mem, mix_norm_w, w_in, conv_w, conv_b, cmp_pe, cmp_w1, cmp_w2, q_norm_w, k_norm_w, w_out,
           xattn_norm_w, mem_norm_w, xq, xk, xv, xo, xq_norm_w, xk_norm_w, ffn_norm_w, peer_wq, peer_keys,
           peer_down, peer_up):
    B, S, D = x.shape
    T = B * S
    l = 0
    G, R, dh = NSA_GROUPS, NSA_REP, NSA_DH

    w = w_in[l]
    o_q = C_CONV
    o_kv = o_q + NSA_HEADS * dh
    o_g = o_kv + C_KV
    wq = w[:, o_q:o_kv].reshape(D, G, R, dh)
    wq_pad = jnp.concatenate([_pad_half(wq[:, g], g).reshape(D, R * LANES) for g in range(G)], axis=1)
    wg = w[:, o_g:].reshape(D, G, R * 3)
    wg_pad = jnp.pad(wg, ((0, 0), (0, 0), (0, LANES - R * 3))).reshape(D, G * LANES)
    wkv = w[:, o_kv:o_g].reshape(D, 6, LANES)
    wkn = wkv[:, jnp.array([0, 1, 2, 4])].reshape(D, C_KN)
    wvt = wkv[:, jnp.array([3, 5])].reshape(D, 2 * LANES).T.astype(BF16)
    w_all = jnp.concatenate([w[:, :o_q], wq_pad, wkn, wg_pad], axis=1).astype(BF16)
    qw = q_norm_w[l] * dh ** -0.5
    qw_pad = jnp.concatenate([jnp.tile(_pad_half(qw, g), R) for g in range(G)]).reshape(1, C_Q)
    kw = jnp.stack([jnp.tile(k_norm_w[l, 1], G), jnp.tile(k_norm_w[l, 2], G)])

    yconv, q, kvc, kn, vt, gates = _in_proj(
        x.reshape(T, D), mix_norm_w[l].reshape(1, D), w_all, wvt, conv_w[l], conv_b[l].reshape(1, CONV_WIDTH),
        qw_pad, kw, S)

    n_rows = S // CMP_STRIDE
    kc = kvc.reshape(B, n_rows, CMP_STRIDE, 2, G, dh).transpose(0, 3, 4, 1, 2, 5).reshape(
        B, 2, G, n_rows, HALF_FEATS)
    pe = cmp_pe[l].reshape(2, 2, HALF_FEATS)
    w1 = jnp.pad(cmp_w1[l], ((0, 0), (0, 0), (0, LANES - dh))).astype(BF16)
    w2 = jnp.pad(cmp_w2[l], ((0, 0), (0, LANES - dh), (0, 0)))
    w2 = jnp.stack([_pad_half(w2, g) for g in range(G)], axis=1).astype(BF16)
    knw = jnp.stack([_pad_half(k_norm_w[l, 0], g) for g in range(G)]).reshape(G, 1, LANES)
    kvcmp = _compress(kc, pe, w1, w2[0], jnp.swapaxes(w2[1], -1, -2), knw)

    n_sel = S // SEL_BLOCK
    cmp_start = jnp.arange(n_rows) * CMP_STRIDE
    sel_start = jnp.arange(n_sel) * SEL_BLOCK
    selmap_t = ((cmp_start[None, :] < sel_start[:, None] + SEL_BLOCK)
                & (cmp_start[None, :] + CMP_BLOCK > sel_start[:, None])
                & (jnp.arange(n_rows)[None, :] < n_rows - 1)).astype(BF16)
    ynsa = _nsa(q.reshape(B, S, C_Q), kn.reshape(B, S, 2 * LANES), vt, kvcmp, gates.reshape(B, S, C_G), selmap_t)

    kmem, vmem = _mem_kv(mem, mem_norm_w[l].reshape(1, D), xk[l].astype(BF16), xv[l].astype(BF16),
                         xk_norm_w[l].reshape(1, X_DH))

    wo = w_out[l]
    won = wo[CONV_WIDTH:].reshape(G, R, dh, D)
    won_pad = jnp.concatenate(
        [jnp.concatenate([won[g], jnp.zeros_like(won[g])] if g == 0 else [jnp.zeros_like(won[g]), won[g]],
                         axis=1).reshape(R * LANES, D) for g in range(G)], axis=0).astype(BF16)
    keys = peer_keys[l].reshape(2 * PEER_HEADS, PEER_NKEYS, PEER_HALF).astype(BF16)
    x2, h3, sc = _post_mix(
        x.reshape(T, D), yconv, ynsa.reshape(T, C_Q), wo[:CONV_WIDTH].astype(BF16), won_pad,
        xattn_norm_w[l].reshape(1, D), xq[l].astype(BF16),
        (xq_norm_w[l] * X_DH ** -0.5).reshape(1, X_DH), kmem, vmem, xo[l].astype(BF16),
        ffn_norm_w[l].reshape(1, D), peer_wq[l].astype(BF16), keys, S)

    out = _peer(sc, h3, peer_down[l].T.astype(BF16), peer_up[l].astype(BF16), x2)
    return out.reshape(B, S, D)
```

```python
import functools

import jax
import jax.numpy as jnp
from jax import lax
from jax.experimental import pallas as pl
from jax.experimental.pallas import tpu as pltpu

F32 = jnp.float32
BF16 = jnp.bfloat16
I32 = jnp.int32

EPS = 1e-6
NEG = -1e30
LANES = 128
SUBLANES = 8

D_MODEL = 1024
CONV_WIDTH = 512
NSA_HEADS = 8
NSA_GROUPS = 2
NSA_REP = NSA_HEADS // NSA_GROUPS
NSA_DH = 64
CMP_BLOCK = 32
CMP_STRIDE = 16
SEL_BLOCK = 64
SEL_TOPN = 16
WINDOW = 512
X_HEADS = 4
X_DH = D_MODEL // X_HEADS
PEER_HEADS = 8
PEER_NKEYS = 128
PEER_HALF = 128
PEER_TOPK = 16
PEER_EXPERTS = PEER_NKEYS * PEER_NKEYS

NT_DIMS = (((1,), (1,)), ((), ()))


def _dot(a, b):
    return jnp.dot(a, b, preferred_element_type=F32)


def _dot_nt(a, b):
    return lax.dot_general(a, b, NT_DIMS, preferred_element_type=F32)


def _gelu2(x):
    return x * (1.0 + lax.erf(x * (2.0 ** -0.5)))


def _gelu(x):
    return 0.5 * _gelu2(x)


def _rms(x, w):
    ms = jnp.mean(x * x, axis=-1, keepdims=True)
    return x * lax.rsqrt(ms + EPS) * w


def _cparams(sem, vmem_mb):
    return pltpu.CompilerParams(dimension_semantics=sem, vmem_limit_bytes=vmem_mb * 1024 * 1024)


C_CONV = 3 * CONV_WIDTH
C_Q = NSA_HEADS * LANES
C_KV = 6 * NSA_GROUPS * NSA_DH
C_KN = 4 * LANES
C_G = NSA_GROUPS * LANES
C_ALL = C_CONV + C_Q + C_KN + C_G


def _in_proj_kernel(tiles_per_seq, x_ref, nw_ref, w_ref, wvt_ref, cw_ref, cb_ref, qw_ref, kw_ref,
                    yconv_ref, q_ref, kvc_ref, kn_ref, vt_ref, g_ref, carry_ref):
    tm = x_ref.shape[0]
    hn = _rms(x_ref[...], nw_ref[...]).astype(BF16)

    p = _dot(hn, w_ref[:, 0:C_CONV])
    b_g = p[:, 0:CONV_WIDTH]
    u = p[:, CONV_WIDTH:2 * CONV_WIDTH] * p[:, 2 * CONV_WIDTH:3 * CONV_WIDTH]

    @pl.when((pl.program_id(0) % tiles_per_seq) == 0)
    def _sequence_start():
        carry_ref[...] = jnp.zeros(carry_ref.shape, F32)

    prev = carry_ref[...]
    p1 = prev[SUBLANES - 1:SUBLANES, :]
    p2 = prev[SUBLANES - 2:SUBLANES - 1, :]
    row = lax.broadcasted_iota(I32, u.shape, 0)
    u1 = jnp.where(row == 0, p1, pltpu.roll(u, 1, axis=0))
    u2 = jnp.where(row == 0, p2, jnp.where(row == 1, p1, pltpu.roll(u, 2, axis=0)))
    carry_ref[...] = u[tm - SUBLANES:tm, :]
    cw = cw_ref[...]
    y = b_g * (cw[0:1, :] * u2 + cw[1:2, :] * u1 + cw[2:3, :] * u + cb_ref[...])
    yconv_ref[...] = y.astype(BF16)

    pq = _dot(hn, w_ref[:, C_CONV:C_CONV + C_Q])
    for h in range(NSA_HEADS):
        blk = pq[:, h * LANES:(h + 1) * LANES]
        ms = jnp.sum(blk * blk, axis=-1, keepdims=True) * (1.0 / NSA_DH)
        q_ref[:, h * LANES:(h + 1) * LANES] = (
            blk * lax.rsqrt(ms + EPS) * qw_ref[:, h * LANES:(h + 1) * LANES]).astype(BF16)

    pkv = _dot(hn, w_ref[:, C_CONV + C_Q:C_CONV + C_Q + C_KN])
    kvc_ref[...] = pkv[:, 0:2 * LANES]
    lane = lax.broadcasted_iota(I32, (tm, LANES), 1)
    lo = lane < NSA_DH
    for j in range(2):
        blk = pkv[:, (2 + j) * LANES:(3 + j) * LANES]
        sq = blk * blk
        ms_lo = jnp.sum(jnp.where(lo, sq, 0.0), axis=-1, keepdims=True) * (1.0 / NSA_DH)
        ms_hi = jnp.sum(jnp.where(lo, 0.0, sq), axis=-1, keepdims=True) * (1.0 / NSA_DH)
        scale = jnp.where(lo, lax.rsqrt(ms_lo + EPS), lax.rsqrt(ms_hi + EPS))
        kn_ref[:, j * LANES:(j + 1) * LANES] = (blk * scale * kw_ref[j:j + 1, :]).astype(BF16)

    for j in range(tm // LANES):
        vt_ref[j] = _dot_nt(wvt_ref[...], hn[j * LANES:(j + 1) * LANES, :]).astype(BF16)

    pg = _dot(hn, w_ref[:, C_CONV + C_Q + C_KN:C_ALL])
    g_ref[...] = jax.nn.sigmoid(pg)


def _in_proj(x2d, nw, w_all, wvt, cw, cb, qw, kw, seq, tm=256):
    T = x2d.shape[0]
    const = lambda i: (0, 0)
    tile = lambda i: (i, 0)
    return pl.pallas_call(
        functools.partial(_in_proj_kernel, seq // tm),
        grid=(T // tm,),
        in_specs=[
            pl.BlockSpec((tm, D_MODEL), tile),
            pl.BlockSpec((1, D_MODEL), const),
            pl.BlockSpec((D_MODEL, C_ALL), const),
            pl.BlockSpec((2 * LANES, D_MODEL), const),
            pl.BlockSpec((3, CONV_WIDTH), const),
            pl.BlockSpec((1, CONV_WIDTH), const),
            pl.BlockSpec((1, C_Q), const),
            pl.BlockSpec((2, LANES), const),
        ],
        out_specs=[
            pl.BlockSpec((tm, CONV_WIDTH), tile),
            pl.BlockSpec((tm, C_Q), tile),
            pl.BlockSpec((tm, 2 * LANES), tile),
            pl.BlockSpec((tm, 2 * LANES), tile),
            pl.BlockSpec((tm // LANES, 2 * LANES, LANES), lambda i: (i, 0, 0)),
            pl.BlockSpec((tm, C_G), tile),
        ],
        out_shape=[
            jax.ShapeDtypeStruct((T, CONV_WIDTH), BF16),
            jax.ShapeDtypeStruct((T, C_Q), BF16),
            jax.ShapeDtypeStruct((T, 2 * LANES), F32),
            jax.ShapeDtypeStruct((T, 2 * LANES), BF16),
            jax.ShapeDtypeStruct((T // LANES, 2 * LANES, LANES), BF16),
            jax.ShapeDtypeStruct((T, C_G), F32),
        ],
        scratch_shapes=[pltpu.VMEM((SUBLANES, CONV_WIDTH), F32)],
        compiler_params=_cparams(("arbitrary",), 48),
    )(x2d, nw, w_all, wvt, cw, cb, qw, kw)


HALF_FEATS = CMP_STRIDE * NSA_DH


def _compress_kernel(kc_ref, pe_ref, w1_ref, w2_ref, w2t_ref, knw_ref, out_ref):
    n_rows = kc_ref.shape[3]
    row = lax.broadcasted_iota(I32, (n_rows, LANES), 0)
    col = lax.broadcasted_iota(I32, (LANES, n_rows), 1)
    for kv in range(2):
        for g in range(NSA_GROUPS):
            kr = kc_ref[0, kv, g]
            a = (kr + pe_ref[kv, 0:1, :]).astype(BF16)
            b = (kr + pe_ref[kv, 1:2, :]).astype(BF16)
            ha = _dot(a, w1_ref[kv, 0:HALF_FEATS, :])
            hb = _dot(b, w1_ref[kv, HALF_FEATS:2 * HALF_FEATS, :])
            hid = _gelu(ha + pltpu.roll(hb, n_rows - 1, axis=0)).astype(BF16)
            if kv == 0:
                out = _dot(hid, w2_ref[g])
                ms = jnp.sum(out * out, axis=-1, keepdims=True) * (1.0 / NSA_DH)
                out = out * lax.rsqrt(ms + EPS) * knw_ref[g]
                out_ref[0, kv, g] = jnp.where(row < n_rows - 1, out, 0.0).astype(BF16)
            else:
                out = _dot_nt(w2t_ref[g], hid)
                out_ref[0, kv, g] = jnp.where(col < n_rows - 1, out, 0.0).astype(BF16)


def _compress(kc, pe, w1, w2, w2t, knw):
    B, n_rows = kc.shape[0], kc.shape[3]
    return pl.pallas_call(
        _compress_kernel,
        grid=(B,),
        in_specs=[
            pl.BlockSpec((1, 2, NSA_GROUPS, n_rows, HALF_FEATS), lambda b: (b, 0, 0, 0, 0)),
            pl.BlockSpec((2, 2, HALF_FEATS), lambda b: (0, 0, 0)),
            pl.BlockSpec((2, 2 * HALF_FEATS, LANES), lambda b: (0, 0, 0)),
            pl.BlockSpec((NSA_GROUPS, LANES, LANES), lambda b: (0, 0, 0)),
            pl.BlockSpec((NSA_GROUPS, LANES, LANES), lambda b: (0, 0, 0)),
            pl.BlockSpec((NSA_GROUPS, 1, LANES), lambda b: (0, 0, 0)),
        ],
        out_specs=pl.BlockSpec((1, 2, NSA_GROUPS, n_rows, LANES), lambda b: (b, 0, 0, 0, 0)),
        out_shape=jax.ShapeDtypeStruct((B, 2, NSA_GROUPS, n_rows, LANES), BF16),
        compiler_params=_cparams(("arbitrary",), 32),
    )(kc, pe, w1, w2, w2t, knw)


SEL_CHUNK = 512


def _nsa_kernel(q_ref, kn_ref, vt_ref, cmp_ref, g_ref, selmap_ref, y_ref, sel_ref, acc_ref):
    tq = q_ref.shape[1]
    cols = NSA_HEADS * tq
    i = pl.program_id(1)
    t0 = i * tq
    qb = q_ref[0]
    qs = jnp.concatenate([qb[:, h * LANES:(h + 1) * LANES] for h in range(NSA_HEADS)], axis=0)
    qpos = t0 + jnp.bitwise_and(lax.broadcasted_iota(I32, (1, cols), 1), tq - 1)
    both = lambda kv: (cmp_ref[0, kv, 0].astype(F32) + cmp_ref[0, kv, 1].astype(F32)).astype(BF16)

    n_cmp = cmp_ref.shape[3]
    s = _dot_nt(both(0), qs)
    cmp_last = lax.broadcasted_iota(I32, (n_cmp, cols), 0) * CMP_STRIDE + (CMP_BLOCK - 1)
    vis_c = cmp_last <= qpos
    s = jnp.where(vis_c, s, NEG)
    m = jnp.max(s, axis=0, keepdims=True)
    m = jnp.where(m > 0.5 * NEG, m, 0.0)
    e = jnp.where(vis_c, jnp.exp(s - m), 0.0)
    d = jnp.sum(e, axis=0, keepdims=True)
    p_c = e / jnp.where(d > 0.0, d, 1.0)
    o_c = _dot(both(1), p_c.astype(BF16))

    n_sel = selmap_ref.shape[0]
    blk = lax.broadcasted_iota(I32, (n_sel, tq), 0)
    cur = jnp.right_shift(t0 + lax.broadcasted_iota(I32, (n_sel, tq), 1), 6)
    forced = (blk == 0) | (blk == cur) | (blk == cur - 1)
    chosen = []
    for g in range(NSA_GROUPS):
        psum = p_c[:, g * NSA_REP * tq:(g * NSA_REP + 1) * tq]
        for r in range(1, NSA_REP):
            psum = psum + p_c[:, (g * NSA_REP + r) * tq:(g * NSA_REP + r + 1) * tq]
        p_hi = psum.astype(BF16)
        p_lo = (psum - p_hi.astype(F32)).astype(BF16)
        imp = _dot(selmap_ref[...], p_hi) + _dot(selmap_ref[...], p_lo)
        imp = jnp.where(blk > cur, -jnp.inf, jnp.where(forced, jnp.inf, imp))
        rank = jnp.zeros((n_sel, tq), I32)
        for jp in range(n_sel):
            other = imp[jp:jp + 1, :]
            beats = (other > imp) | ((other == imp) & (blk > jp))
            rank = rank + beats.astype(I32)
        chosen += [rank < min(SEL_TOPN, n_sel)] * NSA_REP
    sel_ref[...] = jnp.where(jnp.concatenate(chosen, axis=1), qpos, -1)

    acc_ref[...] = jnp.zeros(acc_ref.shape, F32)
    units = SEL_CHUNK // LANES
    blocks = SEL_CHUNK // SEL_BLOCK
    koff = lax.broadcasted_iota(I32, (SEL_BLOCK, cols), 0)

    def sel_chunk(c, carry):
        m_old, l_old = carry
        start = pl.multiple_of(c * SEL_CHUNK, SEL_CHUNK)
        k = kn_ref[0, pl.ds(start, SEL_CHUNK), 0:LANES]
        vt = jnp.concatenate([vt_ref[c * units + u, 0:LANES, :] for u in range(units)], axis=1)
        sc = _dot_nt(k, qs)
        parts = []
        for b in range(blocks):
            limit = sel_ref[pl.ds(c * blocks + b, 1), :] - (start + b * SEL_BLOCK)
            parts.append(jnp.where(koff <= limit, sc[b * SEL_BLOCK:(b + 1) * SEL_BLOCK], NEG))
        sc = jnp.concatenate(parts, axis=0)
        m_new = jnp.maximum(m_old, jnp.max(sc, axis=0, keepdims=True))
        alpha = jnp.exp(m_old - m_new)
        p = jnp.exp(sc - m_new)
        acc_ref[...] = alpha * acc_ref[...] + _dot(vt, p.astype(BF16))
        return m_new, alpha * l_old + jnp.sum(p, axis=0, keepdims=True)

    init = (jnp.full((1, cols), NEG, F32), jnp.zeros((1, cols), F32))
    _, l_s = lax.fori_loop(0, (t0 + tq + SEL_CHUNK - 1) // SEL_CHUNK, sel_chunk, init)
    o_s = acc_ref[...]

    n_win = WINDOW + tq
    w_start = pl.multiple_of(jnp.maximum(t0 - WINDOW, 0), LANES)
    w_unit = w_start // LANES
    k_w = kn_ref[0, pl.ds(w_start, n_win), LANES:2 * LANES]
    vt_w = jnp.concatenate([vt_ref[w_unit + u, LANES:2 * LANES, :] for u in range(n_win // LANES)], axis=1)
    diff = (qpos - w_start) - lax.broadcasted_iota(I32, (n_win, cols), 0)
    in_window = lax.bitcast_convert_type(diff, jnp.uint32) < jnp.uint32(WINDOW)
    s_w = jnp.where(in_window, _dot_nt(k_w, qs), NEG)
    p_w = jnp.exp(s_w - jnp.max(s_w, axis=0, keepdims=True))
    l_w = jnp.sum(p_w, axis=0, keepdims=True)
    o_w = _dot(vt_w, p_w.astype(BF16))

    g_t = [g_ref[0, :, g * LANES:(g + 1) * LANES].T for g in range(NSA_GROUPS)]
    gate = lambda br: jnp.concatenate(
        [g_t[h // NSA_REP][3 * (h % NSA_REP) + br:3 * (h % NSA_REP) + br + 1, :] for h in range(NSA_HEADS)], axis=1)
    y_t = o_c * gate(0) + o_s * (gate(1) / l_s) + o_w * (gate(2) / l_w)
    for h in range(NSA_HEADS):
        y_ref[0, :, h * LANES:(h + 1) * LANES] = y_t[:, h * tq:(h + 1) * tq].T.astype(BF16)


def _nsa(q, kn, vt, kvcmp, gates, selmap, tq=128):
    B, S = q.shape[0], q.shape[1]
    n_sel = S // SEL_BLOCK
    cols = NSA_HEADS * tq
    units = S // LANES
    n_cmp = kvcmp.shape[3]
    return pl.pallas_call(
        _nsa_kernel,
        grid=(B, S // tq),
        in_specs=[
            pl.BlockSpec((1, tq, C_Q), lambda b, i: (b, i, 0)),
            pl.BlockSpec((1, S, 2 * LANES), lambda b, i: (b, 0, 0)),
            pl.BlockSpec((units, 2 * LANES, LANES), lambda b, i: (b, 0, 0)),
            pl.BlockSpec((1, 2, NSA_GROUPS, n_cmp, LANES), lambda b, i: (b, 0, 0, 0, 0)),
            pl.BlockSpec((1, tq, C_G), lambda b, i: (b, i, 0)),
            pl.BlockSpec((n_sel, n_cmp), lambda b, i: (0, 0)),
        ],
        out_specs=pl.BlockSpec((1, tq, C_Q), lambda b, i: (b, i, 0)),
        out_shape=jax.ShapeDtypeStruct((B, S, C_Q), BF16),
        scratch_shapes=[
            pltpu.VMEM((n_sel, cols), I32),
            pltpu.VMEM((LANES, cols), F32),
        ],
        compiler_params=_cparams(("arbitrary", "arbitrary"), 56),
    )(q, kn, vt, kvcmp, gates, selmap)


def _mem_kv_kernel(mem_ref, nw_ref, wk_ref, wv_ref, kw_ref, k_ref, v_ref):
    mn = _rms(mem_ref[0], nw_ref[...]).astype(BF16)
    k = _dot(mn, wk_ref[...])
    for h in range(X_HEADS):
        sl = slice(h * X_DH, (h + 1) * X_DH)
        k_ref[0, :, sl] = _rms(k[:, sl], kw_ref[...]).astype(BF16)
    v_ref[0] = _dot(mn, wv_ref[...]).astype(BF16)


def _mem_kv(mem, nw, wk, wv, kw):
    B, M, D = mem.shape
    blk = pl.BlockSpec((1, M, D), lambda b: (b, 0, 0))
    const = lambda b: (0, 0)
    return pl.pallas_call(
        _mem_kv_kernel,
        grid=(B,),
        in_specs=[blk, pl.BlockSpec((1, D), const), pl.BlockSpec((D, D), const),
                  pl.BlockSpec((D, D), const), pl.BlockSpec((1, X_DH), const)],
        out_specs=[blk, blk],
        out_shape=[jax.ShapeDtypeStruct((B, M, D), BF16)] * 2,
        compiler_params=_cparams(("arbitrary",), 32),
    )(mem, nw, wk, wv, kw)


def _post_mix_kernel(x_ref, yc_ref, yn_ref, woc_ref, won_ref, xnw_ref, xq_ref, xqw_ref, km_ref, vm_ref,
                     xo_ref, fnw_ref, pwq_ref, keys_ref, x2_ref, h3_ref, sc_ref):
    x1 = x_ref[...] + _dot(yc_ref[...], woc_ref[...]) + _dot(yn_ref[...], won_ref[...])

    h2 = _rms(x1, xnw_ref[...]).astype(BF16)
    qx = _dot(h2, xq_ref[...])
    heads = []
    for h in range(X_HEADS):
        sl = slice(h * X_DH, (h + 1) * X_DH)
        qh = _rms(qx[:, sl], xqw_ref[...]).astype(BF16)
        s = _dot_nt(qh, km_ref[0, :, sl])
        e = jnp.exp(s - jnp.max(s, axis=-1, keepdims=True))
        p = e / jnp.sum(e, axis=-1, keepdims=True)
        heads.append(_dot(p.astype(BF16), vm_ref[0, :, sl]))
    o = jnp.concatenate(heads, axis=1).astype(BF16)
    x2 = x1 + _dot(o, xo_ref[...])
    x2_ref[...] = x2

    h3 = _rms(x2, fnw_ref[...]).astype(BF16)
    h3_ref[...] = h3
    qp = _dot(h3, pwq_ref[...]).astype(BF16)
    for c in range(2 * PEER_HEADS):
        sl = slice(c * PEER_HALF, (c + 1) * PEER_HALF)
        for j in range(qp.shape[0] // LANES):
            sc_ref[c, j] = _dot_nt(keys_ref[c], qp[j * LANES:(j + 1) * LANES, sl])


def _post_mix(x2d, yconv, ynsa, woc, won, xnw, xq, xqw, kmem, vmem, xo, fnw, pwq, keys, seq, tm=256):
    T, D = x2d.shape
    M = kmem.shape[1]
    n_sc = 2 * PEER_HEADS * PEER_NKEYS
    tile = lambda i: (i, 0)
    const = lambda i: (0, 0)
    per_batch = lambda i: ((i * tm) // seq, 0, 0)
    return pl.pallas_call(
        _post_mix_kernel,
        grid=(T // tm,),
        in_specs=[
            pl.BlockSpec((tm, D), tile),
            pl.BlockSpec((tm, CONV_WIDTH), tile),
            pl.BlockSpec((tm, C_Q), tile),
            pl.BlockSpec((CONV_WIDTH, D), const),
            pl.BlockSpec((C_Q, D), const),
            pl.BlockSpec((1, D), const),
            pl.BlockSpec((D, D), const),
            pl.BlockSpec((1, X_DH), const),
            pl.BlockSpec((1, M, D), per_batch),
            pl.BlockSpec((1, M, D), per_batch),
            pl.BlockSpec((D, D), const),
            pl.BlockSpec((1, D), const),
            pl.BlockSpec((D, n_sc), const),
            pl.BlockSpec((2 * PEER_HEADS, PEER_NKEYS, PEER_HALF), lambda i: (0, 0, 0)),
        ],
        out_specs=[pl.BlockSpec((tm, D), tile), pl.BlockSpec((tm, D), tile),
                   pl.BlockSpec((2 * PEER_HEADS, tm // LANES, PEER_NKEYS, LANES), lambda i: (0, i, 0, 0))],
        out_shape=[
            jax.ShapeDtypeStruct((T, D), F32),
            jax.ShapeDtypeStruct((T, D), BF16),
            jax.ShapeDtypeStruct((2 * PEER_HEADS, T // LANES, PEER_NKEYS, LANES), F32),
        ],
        compiler_params=_cparams(("arbitrary",), 56),
    )(x2d, yconv, ynsa, woc, won, xnw, xq, xqw, kmem, vmem, xo, fnw, pwq, keys)


STAIR = [(a, b) for a in range(PEER_TOPK) for b in range(PEER_TOPK // (a + 1))]
N_STAIR_VREGS = -(-len(STAIR) // SUBLANES)


def _stair_vregs(axis, k):
    return sorted({r // SUBLANES for r, ab in enumerate(STAIR) if ab[axis] == k})


def _stair_tables():
    rows = N_STAIR_VREGS * SUBLANES
    pad = rows - len(STAIR)
    ta = jnp.asarray([float(a) for a, _ in STAIR] + [-1.0] * pad, F32)
    tb = jnp.asarray([float(b) for _, b in STAIR] + [-1.0] * pad, F32)
    return jnp.broadcast_to(ta[:, None], (rows, LANES)), jnp.broadcast_to(tb[:, None], (rows, LANES))


def _make_topk(sc_ref, ta_ref, tb_ref, s_out, e_out):
    K = PEER_TOPK
    NV = N_STAIR_VREGS
    kidx = lax.broadcasted_iota(I32, (PEER_NKEYS, LANES), 0).astype(F32)
    ridx = [(lax.broadcasted_iota(I32, (SUBLANES, LANES), 0) + SUBLANES * j).astype(F32) for j in range(NV)]
    ta = [ta_ref[SUBLANES * j:SUBLANES * (j + 1), :] for j in range(NV)]
    tb = [tb_ref[SUBLANES * j:SUBLANES * (j + 1), :] for j in range(NV)]
    zero = jnp.zeros((SUBLANES, LANES), F32)

    def top1(s):
        m = jnp.max(s, axis=0, keepdims=True)
        idx = jnp.min(jnp.where(s == m, kidx, float(PEER_NKEYS)), axis=0, keepdims=True)
        return m, idx, jnp.where(kidx == idx, -jnp.inf, s)

    def sorted_top(s, table, axis):
        val = [zero] * NV
        key = [zero] * NV
        for k in range(K):
            m, idx, s = top1(s)
            for j in _stair_vregs(axis, k):
                hit = table[j] == float(k)
                val[j] = jnp.where(hit, m, val[j])
                key[j] = jnp.where(hit, idx, key[j])
        return val, key

    def head(h, g):
        v1, k1 = sorted_top(sc_ref[2 * h, g], ta, 0)
        v2, k2 = sorted_top(sc_ref[2 * h + 1, g], tb, 1)
        cand = [jnp.where(ta[j] >= 0.0, v1[j] + v2[j], -jnp.inf) for j in range(NV)]
        ce = [k1[j] * float(PEER_NKEYS) + k2[j] for j in range(NV)]
        for k in range(K):
            mx = cand[0]
            for j in range(1, NV):
                mx = jnp.maximum(mx, cand[j])
            m = jnp.max(mx, axis=0, keepdims=True)
            ix = jnp.where(cand[0] == m, ridx[0], float(NV * SUBLANES))
            for j in range(1, NV):
                ix = jnp.minimum(ix, jnp.where(cand[j] == m, ridx[j], float(NV * SUBLANES)))
            idx = jnp.min(ix, axis=0, keepdims=True)
            hits = [ridx[j] == idx for j in range(NV)]
            es = jnp.where(hits[0], ce[0], 0.0)
            for j in range(1, NV):
                es = es + jnp.where(hits[j], ce[j], 0.0)
            cand = [jnp.where(hits[j], -jnp.inf, cand[j]) for j in range(NV)]
            s_out[pl.ds(h * K + k, 1), :] = m
            e_out[pl.ds(h * K + k, 1), :] = jnp.sum(es, axis=0, keepdims=True)

    def finish(rows, ei_dst, ej_dst, gate_dst):
        s_all = s_out[...]
        gates = []
        for h in range(PEER_HEADS):
            sh = s_all[h * K:(h + 1) * K]
            ex = jnp.exp(sh - sh[0:1])
            gates.append(ex / jnp.sum(ex, axis=0, keepdims=True))
        gate_dst[rows, :] = jnp.concatenate(gates, axis=0).T
        e_int = e_out[...].T.astype(I32)
        ei_dst[rows, :] = jnp.right_shift(e_int, 7)
        ej_dst[rows, :] = jnp.bitwise_and(e_int, PEER_NKEYS - 1)

    return head, finish


G_ROWS = PEER_NKEYS // 2
G_PITCH = G_ROWS + SUBLANES
HI_MASK = 0xFFFF0000


def _peer_kernel(sc_ref, ta_ref, tb_ref, h_ref, wd_ref, wu_ref, x2_ref, out_ref,
                 g_ref, ei_ref, ej_ref, gate_ref, s_out, e_out, a_ref):
    tm = h_ref.shape[0]
    ec = wd_ref.shape[1]
    ipc = ec // LANES
    half = ipc // 2
    half_bits = half.bit_length() - 1
    t = pl.program_id(0)
    c = pl.program_id(1)
    n_chunks = pl.num_programs(1)
    n_grp = tm // LANES
    steps_per_grp = PEER_EXPERTS // ec // n_grp
    heads_per_step = PEER_HEADS // steps_per_grp
    rd = (t + 1) % 2
    wr = t % 2
    ei_rd, ej_rd, gate_rd = ei_ref.at[rd], ej_ref.at[rd], gate_ref.at[rd]

    @pl.when((t == 0) & (c == 0))
    def _first_step():
        ei_ref[...] = jnp.zeros(ei_ref.shape, I32)
        ej_ref[...] = jnp.zeros(ej_ref.shape, I32)
        gate_ref[...] = jnp.zeros(gate_ref.shape, F32)
        a_ref[...] = jnp.zeros(a_ref.shape, F32)

    @pl.when(c == 0)
    def _scatter():
        out_ref[...] = x2_ref[...]
        m = lax.broadcasted_iota(I32, (PEER_NKEYS, 2 * LANES), 0)
        r = jnp.bitwise_and(m, G_ROWS - 1)
        i_of_m = (jnp.right_shift(r, half_bits) * ipc + jnp.bitwise_and(r, half - 1)
                  + jnp.right_shift(m, 6) * half)
        as_bf16 = lambda v: v.astype(F32).astype(BF16)
        i_of_m = as_bf16(i_of_m)
        n = as_bf16(lax.broadcasted_iota(I32, (LANES, LANES), 0))
        one = jnp.ones((LANES, LANES), BF16)
        zero = jnp.zeros((LANES, LANES), BF16)

        def pair(u, carry):
            ta_, tb_ = pl.ds(2 * u, 1), pl.ds(2 * u + 1, 1)
            ri = as_bf16(jnp.concatenate([ei_rd[ta_, :], ei_rd[tb_, :]], axis=1))
            rg = (0.5 * jnp.concatenate([gate_rd[ta_, :], gate_rd[tb_, :]], axis=1)).astype(BF16)
            c_t = jnp.where(i_of_m == ri, rg, jnp.zeros_like(rg))
            q_a = jnp.where(n == as_bf16(ej_rd[ta_, :]), one, zero)
            q_b = jnp.where(n == as_bf16(ej_rd[tb_, :]), one, zero)
            q_t = jnp.concatenate([jnp.concatenate([q_a, zero], axis=1),
                                   jnp.concatenate([zero, q_b], axis=1)], axis=0)
            g2 = _dot_nt(c_t, q_t)
            for tok in range(2):
                g = g2[:, tok * LANES:(tok + 1) * LANES].astype(BF16).astype(F32)
                bits = lax.bitcast_convert_type(g, jnp.uint32)
                word = jnp.bitwise_or(jnp.right_shift(bits[0:G_ROWS], 16),
                                      jnp.bitwise_and(bits[G_ROWS:2 * G_ROWS], jnp.uint32(HI_MASK)))
                g_ref[pl.ds(pl.multiple_of((2 * u + tok) * G_PITCH, SUBLANES), G_ROWS), :] = word
            return carry

        lax.fori_loop(0, tm // 2, pair, 0, unroll=16)

    head, finish = _make_topk(sc_ref, ta_ref, tb_ref, s_out, e_out)
    grp = c // steps_per_grp
    h0 = (c % steps_per_grp) * heads_per_step
    for hh in range(heads_per_step):
        head(h0 + hh, grp)

    cur = (t * n_chunks + c) % 2
    a_ref[1 - cur] = _dot(h_ref[...], wd_ref[...])
    a = a_ref[cur]
    parts = [None] * ipc
    for k in range(half):
        word = g_ref[pl.ds(c * half + k, tm, stride=G_PITCH), :]
        g_lo = lax.bitcast_convert_type(jnp.left_shift(word, 16), F32)
        g_hi = lax.bitcast_convert_type(jnp.bitwise_and(word, jnp.uint32(HI_MASK)), F32)
        parts[k] = (_gelu2(a[:, k * LANES:(k + 1) * LANES]) * g_lo).astype(BF16)
        parts[k + half] = (_gelu2(a[:, (k + half) * LANES:(k + half + 1) * LANES]) * g_hi).astype(BF16)
    z = jnp.concatenate(parts, axis=1)
    out_ref[...] += _dot(z, wu_ref[...])

    @pl.when(c % steps_per_grp == steps_per_grp - 1)
    def _group_done():
        rows = pl.ds(pl.multiple_of(grp * LANES, LANES), LANES)
        finish(rows, ei_ref.at[wr], ej_ref.at[wr], gate_ref.at[wr])


def _peer(sc, h3, wd_t, wu, x2, tm=512, ec=1024):
    T, D = h3.shape
    tm = min(tm, T)
    n_tiles = T // tm
    n_grp = tm // LANES
    n_chunks = wu.shape[0] // ec
    assert n_chunks % n_grp == 0 and PEER_HEADS % (n_chunks // n_grp) == 0
    ta, tb = _stair_tables()
    prev = lambda t, c: (jnp.maximum(t - 1, 0), 0)
    ahead_tile = lambda t, c: (jnp.clip(t - 1 + (c + 1) // n_chunks, 0, n_tiles - 1), 0)
    ahead_chunk = lambda t, c: (0, (c + 1) % n_chunks)
    return pl.pallas_call(
        _peer_kernel,
        grid=(n_tiles + 1, n_chunks),
        in_specs=[
            pl.BlockSpec((2 * PEER_HEADS, n_grp, PEER_NKEYS, LANES),
                         lambda t, c: (0, jnp.minimum(t, n_tiles - 1), 0, 0)),
            pl.BlockSpec(ta.shape, lambda t, c: (0, 0)),
            pl.BlockSpec(tb.shape, lambda t, c: (0, 0)),
            pl.BlockSpec((tm, D), ahead_tile),
            pl.BlockSpec((D, ec), ahead_chunk),
            pl.BlockSpec((ec, D), lambda t, c: (c, 0)),
            pl.BlockSpec((tm, D), prev),
        ],
        out_specs=pl.BlockSpec((tm, D), prev),
        out_shape=jax.ShapeDtypeStruct((T, D), F32),
        scratch_shapes=[
            pltpu.VMEM((tm * G_PITCH, LANES), jnp.uint32),
            pltpu.VMEM((2, tm, LANES), I32),
            pltpu.VMEM((2, tm, LANES), I32),
            pltpu.VMEM((2, tm, LANES), F32),
            pltpu.VMEM((PEER_HEADS * PEER_TOPK, LANES), F32),
            pltpu.VMEM((PEER_HEADS * PEER_TOPK, LANES), F32),
            pltpu.VMEM((2, tm, ec), F32),
        ],
        compiler_params=_cparams(("arbitrary", "arbitrary"), 60),
    )(sc, ta, tb, h3, wd_t, wu, x2)


def _pad_half(a, g):
    z = jnp.zeros_like(a)
    return jnp.concatenate([a, z] if g == 0 else [z, a], axis=-1)


def kernel(x, mem, mix_norm_w, w_in, conv_w, conv_b, cmp_pe, cmp_w1, cmp_w2, q_norm_w, k_norm_w, w_out,
           xattn_norm_w, mem_norm_w, xq, xk, xv, xo, xq_norm_w, xk_norm_w, ffn_norm_w, peer_wq, peer_keys,
           peer_down, peer_up):
    B, S, D = x.shape
    T = B * S
    l = 0
    G, R, dh = NSA_GROUPS, NSA_REP, NSA_DH

    w = w_in[l]
    o_q = C_CONV
    o_kv = o_q + NSA_HEADS * dh
    o_g = o_kv + C_KV
    wq = w[:, o_q:o_kv].reshape(D, G, R, dh)
    wq_pad = jnp.concatenate([_pad_half(wq[:, g], g).reshape(D, R * LANES) for g in range(G)], axis=1)
    wg = w[:, o_g:].reshape(D, G, R * 3)
    wg_pad = jnp.pad(wg, ((0, 0), (0, 0), (0, LANES - R * 3))).reshape(D, G * LANES)
    wkv = w[:, o_kv:o_g].reshape(D, 6, LANES)
    wkn = wkv[:, jnp.array([0, 1, 2, 4])].reshape(D, C_KN)
    wvt = wkv[:, jnp.array([3, 5])].reshape(D, 2 * LANES).T.astype(BF16)
    w_all = jnp.concatenate([w[:, :o_q], wq_pad, wkn, wg_pad], axis=1).astype(BF16)
    qw = q_norm_w[l] * dh ** -0.5
    qw_pad = jnp.concatenate([jnp.tile(_pad_half(qw, g), R) for g in range(G)]).reshape(1, C_Q)
    kw = jnp.stack([jnp.tile(k_norm_w[l, 1], G), jnp.tile(k_norm_w[l, 2], G)])

    yconv, q, kvc, kn, vt, gates = _in_proj(
        x.reshape(T, D), mix_norm_w[l].reshape(1, D), w_all, wvt, conv_w[l], conv_b[l].reshape(1, CONV_WIDTH),
        qw_pad, kw, S)

    n_rows = S // CMP_STRIDE
    kc = kvc.reshape(B, n_rows, CMP_STRIDE, 2, G, dh).transpose(0, 3, 4, 1, 2, 5).reshape(
        B, 2, G, n_rows, HALF_FEATS)
    pe = cmp_pe[l].reshape(2, 2, HALF_FEATS)
    w1 = jnp.pad(cmp_w1[l], ((0, 0), (0, 0), (0, LANES - dh))).astype(BF16)
    w2 = jnp.pad(cmp_w2[l], ((0, 0), (0, LANES - dh), (0, 0)))
    w2 = jnp.stack([_pad_half(w2, g) for g in range(G)], axis=1).astype(BF16)
    knw = jnp.stack([_pad_half(k_norm_w[l, 0], g) for g in range(G)]).reshape(G, 1, LANES)
    kvcmp = _compress(kc, pe, w1, w2[0], jnp.swapaxes(w2[1], -1, -2), knw)

    n_sel = S // SEL_BLOCK
    cmp_start = jnp.arange(n_rows) * CMP_STRIDE
    sel_start = jnp.arange(n_sel) * SEL_BLOCK
    selmap_t = ((cmp_start[None, :] < sel_start[:, None] + SEL_BLOCK)
                & (cmp_start[None, :] + CMP_BLOCK > sel_start[:, None])
                & (jnp.arange(n_rows)[None, :] < n_rows - 1)).astype(BF16)
    ynsa = _nsa(q.reshape(B, S, C_Q), kn.reshape(B, S, 2 * LANES), vt, kvcmp, gates.reshape(B, S, C_G), selmap_t)

    kmem, vmem = _mem_kv(mem, mem_norm_w[l].reshape(1, D), xk[l].astype(BF16), xv[l].astype(BF16),
                         xk_norm_w[l].reshape(1, X_DH))

    wo = w_out[l]
    won = wo[CONV_WIDTH:].reshape(G, R, dh, D)
    won_pad = jnp.concatenate(
        [jnp.concatenate([won[g], jnp.zeros_like(won[g])] if g == 0 else [jnp.zeros_like(won[g]), won[g]],
                         axis=1).reshape(R * LANES, D) for g in range(G)], axis=0).astype(BF16)
    keys = peer_keys[l].reshape(2 * PEER_HEADS, PEER_NKEYS, PEER_HALF).astype(BF16)
    x2, h3, sc = _post_mix(
        x.reshape(T, D), yconv, ynsa.reshape(T, C_Q), wo[:CONV_WIDTH].astype(BF16), won_pad,
        xattn_norm_w[l].reshape(1, D), xq[l].astype(BF16),
        (xq_norm_w[l] * X_DH ** -0.5).reshape(1, X_DH), kmem, vmem, xo[l].astype(BF16),
        ffn_norm_w[l].reshape(1, D), peer_wq[l].astype(BF16), keys, S)

    out = _peer(sc, h3, peer_down[l].T.astype(BF16), peer_up[l].astype(BF16), x2)
    return out.reshape(B, S, D)
```

```python
import functools

import jax
import jax.numpy as jnp
from jax import lax
from jax.experimental import pallas as pl
from jax.experimental.pallas import tpu as pltpu

F32 = jnp.float32
BF16 = jnp.bfloat16
I32 = jnp.int32

EPS = 1e-6
NEG = -1e30
LANES = 128
SUBLANES = 8

D_MODEL = 1024
CONV_WIDTH = 512
NSA_HEADS = 8
NSA_GROUPS = 2
NSA_REP = NSA_HEADS // NSA_GROUPS
NSA_DH = 64
CMP_BLOCK = 32
CMP_STRIDE = 16
SEL_BLOCK = 64
SEL_TOPN = 16
WINDOW = 512
X_HEADS = 4
X_DH = D_MODEL // X_HEADS
PEER_HEADS = 8
PEER_NKEYS = 128
PEER_HALF = 128
PEER_TOPK = 16
PEER_EXPERTS = PEER_NKEYS * PEER_NKEYS

NT_DIMS = (((1,), (1,)), ((), ()))


def _dot(a, b):
    return jnp.dot(a, b, preferred_element_type=F32)


def _dot_nt(a, b):
    return lax.dot_general(a, b, NT_DIMS, preferred_element_type=F32)


def _gelu2(x):
    return x * (1.0 + lax.erf(x * (2.0 ** -0.5)))


def _gelu(x):
    return 0.5 * _gelu2(x)


def _rms(x, w):
    ms = jnp.mean(x * x, axis=-1, keepdims=True)
    return x * lax.rsqrt(ms + EPS) * w


def _cparams(sem, vmem_mb):
    return pltpu.CompilerParams(dimension_semantics=sem, vmem_limit_bytes=vmem_mb * 1024 * 1024)


C_CONV = 3 * CONV_WIDTH
C_Q = NSA_HEADS * LANES
C_KV = 6 * NSA_GROUPS * NSA_DH
C_KN = 4 * LANES
C_G = NSA_GROUPS * LANES
C_ALL = C_CONV + C_Q + C_KN + C_G


def _in_proj_kernel(tiles_per_seq, x_ref, nw_ref, w_ref, wvt_ref, cw_ref, cb_ref, qw_ref, kw_ref,
                    yconv_ref, q_ref, kvc_ref, kn_ref, vt_ref, g_ref, carry_ref):
    tm = x_ref.shape[0]
    hn = _rms(x_ref[...], nw_ref[...]).astype(BF16)

    p = _dot(hn, w_ref[:, 0:C_CONV])
    b_g = p[:, 0:CONV_WIDTH]
    u = p[:, CONV_WIDTH:2 * CONV_WIDTH] * p[:, 2 * CONV_WIDTH:3 * CONV_WIDTH]

    @pl.when((pl.program_id(0) % tiles_per_seq) == 0)
    def _sequence_start():
        carry_ref[...] = jnp.zeros(carry_ref.shape, F32)

    prev = carry_ref[...]
    p1 = prev[SUBLANES - 1:SUBLANES, :]
    p2 = prev[SUBLANES - 2:SUBLANES - 1, :]
    row = lax.broadcasted_iota(I32, u.shape, 0)
    u1 = jnp.where(row == 0, p1, pltpu.roll(u, 1, axis=0))
    u2 = jnp.where(row == 0, p2, jnp.where(row == 1, p1, pltpu.roll(u, 2, axis=0)))
    carry_ref[...] = u[tm - SUBLANES:tm, :]
    cw = cw_ref[...]
    y = b_g * (cw[0:1, :] * u2 + cw[1:2, :] * u1 + cw[2:3, :] * u + cb_ref[...])
    yconv_ref[...] = y.astype(BF16)

    pq = _dot(hn, w_ref[:, C_CONV:C_CONV + C_Q])
    for h in range(NSA_HEADS):
        blk = pq[:, h * LANES:(h + 1) * LANES]
        ms = jnp.sum(blk * blk, axis=-1, keepdims=True) * (1.0 / NSA_DH)
        q_ref[:, h * LANES:(h + 1) * LANES] = (
            blk * lax.rsqrt(ms + EPS) * qw_ref[:, h * LANES:(h + 1) * LANES]).astype(BF16)

    pkv = _dot(hn, w_ref[:, C_CONV + C_Q:C_CONV + C_Q + C_KN])
    kvc_ref[0] = pkv[:, 0:LANES]
    kvc_ref[1] = pkv[:, LANES:2 * LANES]
    lane = lax.broadcasted_iota(I32, (tm, LANES), 1)
    lo = lane < NSA_DH
    for j in range(2):
        blk = pkv[:, (2 + j) * LANES:(3 + j) * LANES]
        sq = blk * blk
        ms_lo = jnp.sum(jnp.where(lo, sq, 0.0), axis=-1, keepdims=True) * (1.0 / NSA_DH)
        ms_hi = jnp.sum(jnp.where(lo, 0.0, sq), axis=-1, keepdims=True) * (1.0 / NSA_DH)
        scale = jnp.where(lo, lax.rsqrt(ms_lo + EPS), lax.rsqrt(ms_hi + EPS))
        kn_ref[:, j * LANES:(j + 1) * LANES] = (blk * scale * kw_ref[j:j + 1, :]).astype(BF16)

    for j in range(tm // LANES):
        vt_ref[j] = _dot_nt(wvt_ref[...], hn[j * LANES:(j + 1) * LANES, :]).astype(BF16)

    pg = _dot(hn, w_ref[:, C_CONV + C_Q + C_KN:C_ALL])
    g_ref[...] = jax.nn.sigmoid(pg)


def _in_proj(x2d, nw, w_all, wvt, cw, cb, qw, kw, seq, tm=256):
    T = x2d.shape[0]
    const = lambda i: (0, 0)
    tile = lambda i: (i, 0)
    return pl.pallas_call(
        functools.partial(_in_proj_kernel, seq // tm),
        grid=(T // tm,),
        in_specs=[
            pl.BlockSpec((tm, D_MODEL), tile),
            pl.BlockSpec((1, D_MODEL), const),
            pl.BlockSpec((D_MODEL, C_ALL), const),
            pl.BlockSpec((2 * LANES, D_MODEL), const),
            pl.BlockSpec((3, CONV_WIDTH), const),
            pl.BlockSpec((1, CONV_WIDTH), const),
            pl.BlockSpec((1, C_Q), const),
            pl.BlockSpec((2, LANES), const),
        ],
        out_specs=[
            pl.BlockSpec((tm, CONV_WIDTH), tile),
            pl.BlockSpec((tm, C_Q), tile),
            pl.BlockSpec((2, tm, LANES), lambda i: (0, i, 0)),
            pl.BlockSpec((tm, 2 * LANES), tile),
            pl.BlockSpec((tm // LANES, 2 * LANES, LANES), lambda i: (i, 0, 0)),
            pl.BlockSpec((tm, C_G), tile),
        ],
        out_shape=[
            jax.ShapeDtypeStruct((T, CONV_WIDTH), BF16),
            jax.ShapeDtypeStruct((T, C_Q), BF16),
            jax.ShapeDtypeStruct((2, T, LANES), F32),
            jax.ShapeDtypeStruct((T, 2 * LANES), BF16),
            jax.ShapeDtypeStruct((T // LANES, 2 * LANES, LANES), BF16),
            jax.ShapeDtypeStruct((T, C_G), F32),
        ],
        scratch_shapes=[pltpu.VMEM((SUBLANES, CONV_WIDTH), F32)],
        compiler_params=_cparams(("arbitrary",), 48),
    )(x2d, nw, w_all, wvt, cw, cb, qw, kw)


def _compress_kernel(kvc_ref, pe_ref, w1_ref, w2k_ref, w2vt_ref, knw_ref, out_ref):
    n_rows = out_ref.shape[3]
    acc_a = jnp.zeros((n_rows, 2 * LANES), F32)
    acc_b = jnp.zeros((n_rows, 2 * LANES), F32)
    for l in range(CMP_STRIDE):
        rows = pl.ds(l, n_rows, stride=CMP_STRIDE)
        x = jnp.concatenate([kvc_ref[0, rows, :], kvc_ref[1, rows, :]], axis=1)
        acc_a = acc_a + _dot((x + pe_ref[l:l + 1, :]).astype(BF16), w1_ref[l])
        acc_b = acc_b + _dot((x + pe_ref[CMP_STRIDE + l:CMP_STRIDE + l + 1, :]).astype(BF16),
                             w1_ref[CMP_STRIDE + l])
    hid = _gelu(acc_a + pltpu.roll(acc_b, n_rows - 1, axis=0)).astype(BF16)
    row = lax.broadcasted_iota(I32, (n_rows, LANES), 0)
    col = lax.broadcasted_iota(I32, (LANES, n_rows), 1)
    keys = _dot(hid, w2k_ref[...])
    vals = _dot_nt(w2vt_ref[...], hid)
    for g in range(NSA_GROUPS):
        out = keys[:, g * LANES:(g + 1) * LANES]
        ms = jnp.sum(out * out, axis=-1, keepdims=True) * (1.0 / NSA_DH)
        out = out * lax.rsqrt(ms + EPS) * knw_ref[g]
        out_ref[0, 0, g] = jnp.where(row < n_rows - 1, out, 0.0).astype(BF16)
        out_ref[0, 1, g] = jnp.where(col < n_rows - 1, vals[g * LANES:(g + 1) * LANES, :], 0.0).astype(BF16)


def _compress(kvc, pe, w1, w2k, w2vt, knw, S):
    B = kvc.shape[1] // S
    n_rows = S // CMP_STRIDE
    return pl.pallas_call(
        _compress_kernel,
        grid=(B,),
        in_specs=[
            pl.BlockSpec((2, S, LANES), lambda b: (0, b, 0)),
            pl.BlockSpec((CMP_BLOCK, 2 * LANES), lambda b: (0, 0)),
            pl.BlockSpec((CMP_BLOCK, 2 * LANES, 2 * LANES), lambda b: (0, 0, 0)),
            pl.BlockSpec((2 * LANES, 2 * LANES), lambda b: (0, 0)),
            pl.BlockSpec((2 * LANES, 2 * LANES), lambda b: (0, 0)),
            pl.BlockSpec((NSA_GROUPS, 1, LANES), lambda b: (0, 0, 0)),
        ],
        out_specs=pl.BlockSpec((1, 2, NSA_GROUPS, n_rows, LANES), lambda b: (b, 0, 0, 0, 0)),
        out_shape=jax.ShapeDtypeStruct((B, 2, NSA_GROUPS, n_rows, LANES), BF16),
        compiler_params=_cparams(("arbitrary",), 32),
    )(kvc, pe, w1, w2k, w2vt, knw)


SEL_CHUNK = 512


def _nsa_kernel(q_ref, kn_ref, vt_ref, cmp_ref, g_ref, selmap_ref, y_ref, sel_ref, acc_ref):
    tq = q_ref.shape[1]
    cols = NSA_HEADS * tq
    i = pl.program_id(1)
    t0 = i * tq
    qb = q_ref[0]
    qs = jnp.concatenate([qb[:, h * LANES:(h + 1) * LANES] for h in range(NSA_HEADS)], axis=0)
    qpos = t0 + jnp.bitwise_and(lax.broadcasted_iota(I32, (1, cols), 1), tq - 1)
    both = lambda kv: (cmp_ref[0, kv, 0].astype(F32) + cmp_ref[0, kv, 1].astype(F32)).astype(BF16)

    n_cmp = cmp_ref.shape[3]
    s = _dot_nt(both(0), qs)
    cmp_last = lax.broadcasted_iota(I32, (n_cmp, cols), 0) * CMP_STRIDE + (CMP_BLOCK - 1)
    vis_c = cmp_last <= qpos
    s = jnp.where(vis_c, s, NEG)
    m = jnp.max(s, axis=0, keepdims=True)
    m = jnp.where(m > 0.5 * NEG, m, 0.0)
    e = jnp.where(vis_c, jnp.exp(s - m), 0.0)
    d = jnp.sum(e, axis=0, keepdims=True)
    p_c = e / jnp.where(d > 0.0, d, 1.0)
    o_c = _dot(both(1), p_c.astype(BF16))

    n_sel = selmap_ref.shape[0]
    blk = lax.broadcasted_iota(I32, (n_sel, tq), 0)
    cur = jnp.right_shift(t0 + lax.broadcasted_iota(I32, (n_sel, tq), 1), 6)
    forced = (blk == 0) | (blk == cur) | (blk == cur - 1)
    chosen = []
    for g in range(NSA_GROUPS):
        psum = p_c[:, g * NSA_REP * tq:(g * NSA_REP + 1) * tq]
        for r in range(1, NSA_REP):
            psum = psum + p_c[:, (g * NSA_REP + r) * tq:(g * NSA_REP + r + 1) * tq]
        p_hi = psum.astype(BF16)
        p_lo = (psum - p_hi.astype(F32)).astype(BF16)
        imp = _dot(selmap_ref[...], p_hi) + _dot(selmap_ref[...], p_lo)
        imp = jnp.where(blk > cur, -jnp.inf, jnp.where(forced, jnp.inf, imp))
        rank = jnp.zeros((n_sel, tq), I32)
        for jp in range(n_sel):
            other = imp[jp:jp + 1, :]
            beats = (other > imp) | ((other == imp) & (blk > jp))
            rank = rank + beats.astype(I32)
        chosen += [rank < min(SEL_TOPN, n_sel)] * NSA_REP
    sel_ref[...] = jnp.where(jnp.concatenate(chosen, axis=1), qpos, -1)

    acc_ref[...] = jnp.zeros(acc_ref.shape, F32)
    units = SEL_CHUNK // LANES
    blocks = SEL_CHUNK // SEL_BLOCK
    koff = lax.broadcasted_iota(I32, (SEL_BLOCK, cols), 0)

    def sel_chunk(c, carry):
        m_old, l_old = carry
        start = pl.multiple_of(c * SEL_CHUNK, SEL_CHUNK)
        k = kn_ref[0, pl.ds(start, SEL_CHUNK), 0:LANES]
        vt = jnp.concatenate([vt_ref[c * units + u, 0:LANES, :] for u in range(units)], axis=1)
        sc = _dot_nt(k, qs)
        parts = []
        for b in range(blocks):
            limit = sel_ref[pl.ds(c * blocks + b, 1), :] - (start + b * SEL_BLOCK)
            parts.append(jnp.where(koff <= limit, sc[b * SEL_BLOCK:(b + 1) * SEL_BLOCK], NEG))
        sc = jnp.concatenate(parts, axis=0)
        m_new = jnp.maximum(m_old, jnp.max(sc, axis=0, keepdims=True))
        alpha = jnp.exp(m_old - m_new)
        p = jnp.exp(sc - m_new)
        acc_ref[...] = alpha * acc_ref[...] + _dot(vt, p.astype(BF16))
        return m_new, alpha * l_old + jnp.sum(p, axis=0, keepdims=True)

    init = (jnp.full((1, cols), NEG, F32), jnp.zeros((1, cols), F32))
    _, l_s = lax.fori_loop(0, (t0 + tq + SEL_CHUNK - 1) // SEL_CHUNK, sel_chunk, init)
    o_s = acc_ref[...]

    n_win = WINDOW + tq
    w_start = pl.multiple_of(jnp.maximum(t0 - WINDOW, 0), LANES)
    w_unit = w_start // LANES
    k_w = kn_ref[0, pl.ds(w_start, n_win), LANES:2 * LANES]
    vt_w = jnp.concatenate([vt_ref[w_unit + u, LANES:2 * LANES, :] for u in range(n_win // LANES)], axis=1)
    diff = (qpos - w_start) - lax.broadcasted_iota(I32, (n_win, cols), 0)
    in_window = lax.bitcast_convert_type(diff, jnp.uint32) < jnp.uint32(WINDOW)
    s_w = jnp.where(in_window, _dot_nt(k_w, qs), NEG)
    p_w = jnp.exp(s_w - jnp.max(s_w, axis=0, keepdims=True))
    l_w = jnp.sum(p_w, axis=0, keepdims=True)
    o_w = _dot(vt_w, p_w.astype(BF16))

    g_t = [g_ref[0, :, g * LANES:(g + 1) * LANES].T for g in range(NSA_GROUPS)]
    gate = lambda br: jnp.concatenate(
        [g_t[h // NSA_REP][3 * (h % NSA_REP) + br:3 * (h % NSA_REP) + br + 1, :] for h in range(NSA_HEADS)], axis=1)
    y_t = o_c * gate(0) + o_s * (gate(1) / l_s) + o_w * (gate(2) / l_w)
    for h in range(NSA_HEADS):
        y_ref[0, :, h * LANES:(h + 1) * LANES] = y_t[:, h * tq:(h + 1) * tq].T.astype(BF16)


def _nsa(q, kn, vt, kvcmp, gates, selmap, tq=128):
    B, S = q.shape[0], q.shape[1]
    n_sel = S // SEL_BLOCK
    cols = NSA_HEADS * tq
    units = S // LANES
    n_cmp = kvcmp.shape[3]
    return pl.pallas_call(
        _nsa_kernel,
        grid=(B, S // tq),
        in_specs=[
            pl.BlockSpec((1, tq, C_Q), lambda b, i: (b, i, 0)),
            pl.BlockSpec((1, S, 2 * LANES), lambda b, i: (b, 0, 0)),
            pl.BlockSpec((units, 2 * LANES, LANES), lambda b, i: (b, 0, 0)),
            pl.BlockSpec((1, 2, NSA_GROUPS, n_cmp, LANES), lambda b, i: (b, 0, 0, 0, 0)),
            pl.BlockSpec((1, tq, C_G), lambda b, i: (b, i, 0)),
            pl.BlockSpec((n_sel, n_cmp), lambda b, i: (0, 0)),
        ],
        out_specs=pl.BlockSpec((1, tq, C_Q), lambda b, i: (b, i, 0)),
        out_shape=jax.ShapeDtypeStruct((B, S, C_Q), BF16),
        scratch_shapes=[
            pltpu.VMEM((n_sel, cols), I32),
            pltpu.VMEM((LANES, cols), F32),
        ],
        compiler_params=_cparams(("arbitrary", "arbitrary"), 56),
    )(q, kn, vt, kvcmp, gates, selmap)


def _mem_kv_kernel(mem_ref, nw_ref, wk_ref, wv_ref, kw_ref, k_ref, v_ref):
    mn = _rms(mem_ref[0], nw_ref[...]).astype(BF16)
    k = _dot(mn, wk_ref[...])
    for h in range(X_HEADS):
        sl = slice(h * X_DH, (h + 1) * X_DH)
        k_ref[0, :, sl] = _rms(k[:, sl], kw_ref[...]).astype(BF16)
    v_ref[0] = _dot(mn, wv_ref[...]).astype(BF16)


def _mem_kv(mem, nw, wk, wv, kw):
    B, M, D = mem.shape
    blk = pl.BlockSpec((1, M, D), lambda b: (b, 0, 0))
    const = lambda b: (0, 0)
    return pl.pallas_call(
        _mem_kv_kernel,
        grid=(B,),
        in_specs=[blk, pl.BlockSpec((1, D), const), pl.BlockSpec((D, D), const),
                  pl.BlockSpec((D, D), const), pl.BlockSpec((1, X_DH), const)],
        out_specs=[blk, blk],
        out_shape=[jax.ShapeDtypeStruct((B, M, D), BF16)] * 2,
        compiler_params=_cparams(("arbitrary",), 32),
    )(mem, nw, wk, wv, kw)


def _post_mix_kernel(x_ref, yc_ref, yn_ref, woc_ref, won_ref, xnw_ref, xq_ref, xqw_ref, km_ref, vm_ref,
                     xo_ref, fnw_ref, pwq_ref, keys_ref, x2_ref, h3_ref, sc_ref):
    x1 = x_ref[...] + _dot(yc_ref[...], woc_ref[...]) + _dot(yn_ref[...], won_ref[...])

    h2 = _rms(x1, xnw_ref[...]).astype(BF16)
    qx = _dot(h2, xq_ref[...])
    heads = []
    for h in range(X_HEADS):
        sl = slice(h * X_DH, (h + 1) * X_DH)
        qh = _rms(qx[:, sl], xqw_ref[...]).astype(BF16)
        s = _dot_nt(qh, km_ref[0, :, sl])
        e = jnp.exp(s - jnp.max(s, axis=-1, keepdims=True))
        p = e / jnp.sum(e, axis=-1, keepdims=True)
        heads.append(_dot(p.astype(BF16), vm_ref[0, :, sl]))
    o = jnp.concatenate(heads, axis=1).astype(BF16)
    x2 = x1 + _dot(o, xo_ref[...])
    x2_ref[...] = x2

    h3 = _rms(x2, fnw_ref[...]).astype(BF16)
    h3_ref[...] = h3
    qp = _dot(h3, pwq_ref[...]).astype(BF16)
    for c in range(2 * PEER_HEADS):
        sl = slice(c * PEER_HALF, (c + 1) * PEER_HALF)
        for j in range(qp.shape[0] // LANES):
            sc_ref[c, j] = _dot_nt(keys_ref[c], qp[j * LANES:(j + 1) * LANES, sl])


def _post_mix(x2d, yconv, ynsa, woc, won, xnw, xq, xqw, kmem, vmem, xo, fnw, pwq, keys, seq, tm=256):
    T, D = x2d.shape
    M = kmem.shape[1]
    n_sc = 2 * PEER_HEADS * PEER_NKEYS
    tile = lambda i: (i, 0)
    const = lambda i: (0, 0)
    per_batch = lambda i: ((i * tm) // seq, 0, 0)
    return pl.pallas_call(
        _post_mix_kernel,
        grid=(T // tm,),
        in_specs=[
            pl.BlockSpec((tm, D), tile),
            pl.BlockSpec((tm, CONV_WIDTH), tile),
            pl.BlockSpec((tm, C_Q), tile),
            pl.BlockSpec((CONV_WIDTH, D), const),
            pl.BlockSpec((C_Q, D), const),
            pl.BlockSpec((1, D), const),
            pl.BlockSpec((D, D), const),
            pl.BlockSpec((1, X_DH), const),
            pl.BlockSpec((1, M, D), per_batch),
            pl.BlockSpec((1, M, D), per_batch),
            pl.BlockSpec((D, D), const),
            pl.BlockSpec((1, D), const),
            pl.BlockSpec((D, n_sc), const),
            pl.BlockSpec((2 * PEER_HEADS, PEER_NKEYS, PEER_HALF), lambda i: (0, 0, 0)),
        ],
        out_specs=[pl.BlockSpec((tm, D), tile), pl.BlockSpec((tm, D), tile),
                   pl.BlockSpec((2 * PEER_HEADS, tm // LANES, PEER_NKEYS, LANES), lambda i: (0, i, 0, 0))],
        out_shape=[
            jax.ShapeDtypeStruct((T, D), F32),
            jax.ShapeDtypeStruct((T, D), BF16),
            jax.ShapeDtypeStruct((2 * PEER_HEADS, T // LANES, PEER_NKEYS, LANES), F32),
        ],
        compiler_params=_cparams(("arbitrary",), 56),
    )(x2d, yconv, ynsa, woc, won, xnw, xq, xqw, kmem, vmem, xo, fnw, pwq, keys)


STAIR = [(a, b) for a in range(PEER_TOPK) for b in range(PEER_TOPK // (a + 1))]
N_STAIR_VREGS = -(-len(STAIR) // SUBLANES)


def _stair_vregs(axis, k):
    return sorted({r // SUBLANES for r, ab in enumerate(STAIR) if ab[axis] == k})


def _stair_tables():
    rows = N_STAIR_VREGS * SUBLANES
    pad = rows - len(STAIR)
    ta = jnp.asarray([float(a) for a, _ in STAIR] + [-1.0] * pad, F32)
    tb = jnp.asarray([float(b) for _, b in STAIR] + [-1.0] * pad, F32)
    return jnp.broadcast_to(ta[:, None], (rows, LANES)), jnp.broadcast_to(tb[:, None], (rows, LANES))


def _make_topk(sc_ref, ta_ref, tb_ref, s_out, e_out):
    K = PEER_TOPK
    NV = N_STAIR_VREGS
    kidx = lax.broadcasted_iota(I32, (PEER_NKEYS, LANES), 0).astype(F32)
    ridx = [(lax.broadcasted_iota(I32, (SUBLANES, LANES), 0) + SUBLANES * j).astype(F32) for j in range(NV)]
    ta = [ta_ref[SUBLANES * j:SUBLANES * (j + 1), :] for j in range(NV)]
    tb = [tb_ref[SUBLANES * j:SUBLANES * (j + 1), :] for j in range(NV)]
    zero = jnp.zeros((SUBLANES, LANES), F32)

    def top1(s):
        m = jnp.max(s, axis=0, keepdims=True)
        idx = jnp.min(jnp.where(s == m, kidx, float(PEER_NKEYS)), axis=0, keepdims=True)
        return m, idx, jnp.where(kidx == idx, -jnp.inf, s)

    def sorted_top(s, table, axis):
        val = [zero] * NV
        key = [zero] * NV
        for k in range(K):
            m, idx, s = top1(s)
            for j in _stair_vregs(axis, k):
                hit = table[j] == float(k)
                val[j] = jnp.where(hit, m, val[j])
                key[j] = jnp.where(hit, idx, key[j])
        return val, key

    def head(h, g):
        v1, k1 = sorted_top(sc_ref[2 * h, g], ta, 0)
        v2, k2 = sorted_top(sc_ref[2 * h + 1, g], tb, 1)
        cand = [jnp.where(ta[j] >= 0.0, v1[j] + v2[j], -jnp.inf) for j in range(NV)]
        ce = [k1[j] * float(PEER_NKEYS) + k2[j] for j in range(NV)]
        sub = lax.broadcasted_iota(I32, (SUBLANES, LANES), 0)
        best_s = [zero] * (K // SUBLANES)
        best_e = [zero] * (K // SUBLANES)
        for k in range(K):
            mx = cand[0]
            for j in range(1, NV):
                mx = jnp.maximum(mx, cand[j])
            m = jnp.max(mx, axis=0, keepdims=True)
            ix = jnp.where(cand[0] == m, ridx[0], float(NV * SUBLANES))
            for j in range(1, NV):
                ix = jnp.minimum(ix, jnp.where(cand[j] == m, ridx[j], float(NV * SUBLANES)))
            idx = jnp.min(ix, axis=0, keepdims=True)
            hits = [ridx[j] == idx for j in range(NV)]
            es = jnp.where(hits[0], ce[0], 0.0)
            for j in range(1, NV):
                es = es + jnp.where(hits[j], ce[j], 0.0)
            cand = [jnp.where(hits[j], -jnp.inf, cand[j]) for j in range(NV)]
            here = sub == (k % SUBLANES)
            best_s[k // SUBLANES] = jnp.where(here, m, best_s[k // SUBLANES])
            best_e[k // SUBLANES] = jnp.where(here, jnp.sum(es, axis=0, keepdims=True), best_e[k // SUBLANES])
        return jnp.concatenate(best_s, axis=0), jnp.concatenate(best_e, axis=0)

    def store(h, best):
        rows = pl.ds(pl.multiple_of(h * K, K), K)
        s_out[rows, :] = best[0]
        e_out[rows, :] = best[1]

    def finish(rows, ei_dst, ej_dst, gate_dst):
        s_all = s_out[...]
        gates = []
        for h in range(PEER_HEADS):
            sh = s_all[h * K:(h + 1) * K]
            ex = jnp.exp(sh - sh[0:1])
            gates.append(ex / jnp.sum(ex, axis=0, keepdims=True))
        gate_dst[rows, :] = jnp.concatenate(gates, axis=0).T
        e_int = e_out[...].T.astype(I32)
        ei_dst[rows, :] = jnp.right_shift(e_int, 7)
        ej_dst[rows, :] = jnp.bitwise_and(e_int, PEER_NKEYS - 1)

    return head, store, finish


G_ROWS = PEER_NKEYS // 2
G_PITCH = G_ROWS + SUBLANES
HI_MASK = 0xFFFF0000


def _peer_kernel(sc_ref, ta_ref, tb_ref, h_ref, wd_ref, wu_ref, x2_ref, out_ref,
                 g_ref, ei_ref, ej_ref, gate_ref, s_out, e_out, a_ref):
    tm = h_ref.shape[0]
    ec = wd_ref.shape[1]
    ipc = ec // LANES
    half = ipc // 2
    half_bits = half.bit_length() - 1
    t = pl.program_id(0)
    c = pl.program_id(1)
    n_chunks = pl.num_programs(1)
    n_grp = tm // LANES
    steps_per_grp = PEER_EXPERTS // ec // n_grp
    heads_per_step = PEER_HEADS // steps_per_grp
    rd = (t + 1) % 2
    wr = t % 2
    ei_rd, ej_rd, gate_rd = ei_ref.at[rd], ej_ref.at[rd], gate_ref.at[rd]

    @pl.when((t == 0) & (c == 0))
    def _first_step():
        ei_ref[...] = jnp.zeros(ei_ref.shape, I32)
        ej_ref[...] = jnp.zeros(ej_ref.shape, I32)
        gate_ref[...] = jnp.zeros(gate_ref.shape, F32)
        a_ref[...] = jnp.zeros(a_ref.shape, F32)

    @pl.when(c == 0)
    def _scatter():
        out_ref[...] = x2_ref[...]
        m = lax.broadcasted_iota(I32, (PEER_NKEYS, 2 * LANES), 0)
        r = jnp.bitwise_and(m, G_ROWS - 1)
        i_of_m = (jnp.right_shift(r, half_bits) * ipc + jnp.bitwise_and(r, half - 1)
                  + jnp.right_shift(m, 6) * half)
        as_bf16 = lambda v: v.astype(F32).astype(BF16)
        i_of_m = as_bf16(i_of_m)
        n = as_bf16(lax.broadcasted_iota(I32, (LANES, LANES), 0))
        one = jnp.ones((LANES, LANES), BF16)
        zero = jnp.zeros((LANES, LANES), BF16)

        def pair(u, carry):
            ta_, tb_ = pl.ds(2 * u, 1), pl.ds(2 * u + 1, 1)
            ri = as_bf16(jnp.concatenate([ei_rd[ta_, :], ei_rd[tb_, :]], axis=1))
            rg = (0.5 * jnp.concatenate([gate_rd[ta_, :], gate_rd[tb_, :]], axis=1)).astype(BF16)
            c_t = jnp.where(i_of_m == ri, rg, jnp.zeros_like(rg))
            q_a = jnp.where(n == as_bf16(ej_rd[ta_, :]), one, zero)
            q_b = jnp.where(n == as_bf16(ej_rd[tb_, :]), one, zero)
            q_t = jnp.concatenate([jnp.concatenate([q_a, zero], axis=1),
                                   jnp.concatenate([zero, q_b], axis=1)], axis=0)
            g2 = _dot_nt(c_t, q_t)
            for tok in range(2):
                g = g2[:, tok * LANES:(tok + 1) * LANES].astype(BF16).astype(F32)
                bits = lax.bitcast_convert_type(g, jnp.uint32)
                word = jnp.bitwise_or(jnp.right_shift(bits[0:G_ROWS], 16),
                                      jnp.bitwise_and(bits[G_ROWS:2 * G_ROWS], jnp.uint32(HI_MASK)))
                g_ref[pl.ds(pl.multiple_of((2 * u + tok) * G_PITCH, SUBLANES), G_ROWS), :] = word
            return carry

        lax.fori_loop(0, tm // 2, pair, 0, unroll=16)

    head, store, finish = _make_topk(sc_ref, ta_ref, tb_ref, s_out, e_out)
    grp = c // steps_per_grp
    h0 = (c % steps_per_grp) * heads_per_step
    best = [head(h0 + hh, grp) for hh in range(heads_per_step)]

    cur = (t * n_chunks + c) % 2
    a_ref[1 - cur] = _dot(h_ref[...], wd_ref[...])
    a = a_ref[cur]
    parts = [None] * ipc
    for k in range(half):
        word = g_ref[pl.ds(c * half + k, tm, stride=G_PITCH), :]
        g_lo = lax.bitcast_convert_type(jnp.left_shift(word, 16), F32)
        g_hi = lax.bitcast_convert_type(jnp.bitwise_and(word, jnp.uint32(HI_MASK)), F32)
        parts[k] = (_gelu2(a[:, k * LANES:(k + 1) * LANES]) * g_lo).astype(BF16)
        parts[k + half] = (_gelu2(a[:, (k + half) * LANES:(k + half + 1) * LANES]) * g_hi).astype(BF16)
    z = jnp.concatenate(parts, axis=1)
    out_ref[...] += _dot(z, wu_ref[...])
    for hh in range(heads_per_step):
        store(h0 + hh, best[hh])

    @pl.when(c % steps_per_grp == steps_per_grp - 1)
    def _group_done():
        rows = pl.ds(pl.multiple_of(grp * LANES, LANES), LANES)
        finish(rows, ei_ref.at[wr], ej_ref.at[wr], gate_ref.at[wr])


def _peer(sc, h3, wd_t, wu, x2, tm=512, ec=1024):
    T, D = h3.shape
    tm = min(tm, T)
    n_tiles = T // tm
    n_grp = tm // LANES
    n_chunks = wu.shape[0] // ec
    assert n_chunks % n_grp == 0 and PEER_HEADS % (n_chunks // n_grp) == 0
    ta, tb = _stair_tables()
    prev = lambda t, c: (jnp.maximum(t - 1, 0), 0)
    ahead_tile = lambda t, c: (jnp.clip(t - 1 + (c + 1) // n_chunks, 0, n_tiles - 1), 0)
    ahead_chunk = lambda t, c: (0, (c + 1) % n_chunks)
    return pl.pallas_call(
        _peer_kernel,
        grid=(n_tiles + 1, n_chunks),
        in_specs=[
            pl.BlockSpec((2 * PEER_HEADS, n_grp, PEER_NKEYS, LANES),
                         lambda t, c: (0, jnp.minimum(t, n_tiles - 1), 0, 0)),
            pl.BlockSpec(ta.shape, lambda t, c: (0, 0)),
            pl.BlockSpec(tb.shape, lambda t, c: (0, 0)),
            pl.BlockSpec((tm, D), ahead_tile),
            pl.BlockSpec((D, ec), ahead_chunk),
            pl.BlockSpec((ec, D), lambda t, c: (c, 0)),
            pl.BlockSpec((tm, D), prev),
        ],
        out_specs=pl.BlockSpec((tm, D), prev),
        out_shape=jax.ShapeDtypeStruct((T, D), F32),
        scratch_shapes=[
            pltpu.VMEM((tm * G_PITCH, LANES), jnp.uint32),
            pltpu.VMEM((2, tm, LANES), I32),
            pltpu.VMEM((2, tm, LANES), I32),
            pltpu.VMEM((2, tm, LANES), F32),
            pltpu.VMEM((PEER_HEADS * PEER_TOPK, LANES), F32),
            pltpu.VMEM((PEER_HEADS * PEER_TOPK, LANES), F32),
            pltpu.VMEM((2, tm, ec), F32),
        ],
        compiler_params=_cparams(("arbitrary", "arbitrary"), 60),
    )(sc, ta, tb, h3, wd_t, wu, x2)


def _pad_half(a, g):
    z = jnp.zeros_like(a)
    return jnp.concatenate([a, z] if g == 0 else [z, a], axis=-1)


def kernel(x, mem, mix_norm_w, w_in, conv_w, conv_b, cmp_pe, cmp_w1, cmp_w2, q_norm_w, k_norm_w, w_out,
           xattn_norm_w, mem_norm_w, xq, xk, xv, xo, xq_norm_w, xk_norm_w, ffn_norm_w, peer_wq, peer_keys,
           peer_down, peer_up):
    B, S, D = x.shape
    T = B * S
    l = 0
    G, R, dh = NSA_GROUPS, NSA_REP, NSA_DH

    w = w_in[l]
    o_q = C_CONV
    o_kv = o_q + NSA_HEADS * dh
    o_g = o_kv + C_KV
    wq = w[:, o_q:o_kv].reshape(D, G, R, dh)
    wq_pad = jnp.concatenate([_pad_half(wq[:, g], g).reshape(D, R * LANES) for g in range(G)], axis=1)
    wg = w[:, o_g:].reshape(D, G, R * 3)
    wg_pad = jnp.pad(wg, ((0, 0), (0, 0), (0, LANES - R * 3))).reshape(D, G * LANES)
    wkv = w[:, o_kv:o_g].reshape(D, 6, LANES)
    wkn = wkv[:, jnp.array([0, 1, 2, 4])].reshape(D, C_KN)
    wvt = wkv[:, jnp.array([3, 5])].reshape(D, 2 * LANES).T.astype(BF16)
    w_all = jnp.concatenate([w[:, :o_q], wq_pad, wkn, wg_pad], axis=1).astype(BF16)
    qw = q_norm_w[l] * dh ** -0.5
    qw_pad = jnp.concatenate([jnp.tile(_pad_half(qw, g), R) for g in range(G)]).reshape(1, C_Q)
    kw = jnp.stack([jnp.tile(k_norm_w[l, 1], G), jnp.tile(k_norm_w[l, 2], G)])

    yconv, q, kvc, kn, vt, gates = _in_proj(
        x.reshape(T, D), mix_norm_w[l].reshape(1, D), w_all, wvt, conv_w[l], conv_b[l].reshape(1, CONV_WIDTH),
        qw_pad, kw, S)

    n_rows = S // CMP_STRIDE
    kvg = jnp.array([0, 0, 1, 1])
    pe = jnp.concatenate([cmp_pe[l, kv] for kv in (0, 0, 1, 1)], axis=-1)
    w1 = cmp_w1[l].reshape(2, CMP_BLOCK, dh, dh)[kvg]
    w1 = jnp.einsum('qp,qjdh->jqdph', jnp.eye(4, dtype=F32), w1).reshape(CMP_BLOCK, 4 * dh, 4 * dh).astype(BF16)
    w2k = jnp.zeros((4 * dh, G * LANES), F32)
    w2vt = jnp.zeros((G * LANES, 4 * dh), F32)
    for g in range(G):
        o = g * LANES + g * dh
        w2k = w2k.at[g * dh:(g + 1) * dh, o:o + dh].set(cmp_w2[l, 0])
        w2vt = w2vt.at[o:o + dh, (G + g) * dh:(G + g + 1) * dh].set(cmp_w2[l, 1].T)
    knw = jnp.stack([_pad_half(k_norm_w[l, 0], g) for g in range(G)]).reshape(G, 1, LANES)
    kvcmp = _compress(kvc, pe, w1, w2k.astype(BF16), w2vt.astype(BF16), knw, S)

    n_sel = S // SEL_BLOCK
    cmp_start = jnp.arange(n_rows) * CMP_STRIDE
    sel_start = jnp.arange(n_sel) * SEL_BLOCK
    selmap_t = ((cmp_start[None, :] < sel_start[:, None] + SEL_BLOCK)
                & (cmp_start[None, :] + CMP_BLOCK > sel_start[:, None])
                & (jnp.arange(n_rows)[None, :] < n_rows - 1)).astype(BF16)
    ynsa = _nsa(q.reshape(B, S, C_Q), kn.reshape(B, S, 2 * LANES), vt, kvcmp, gates.reshape(B, S, C_G), selmap_t)

    kmem, vmem = _mem_kv(mem, mem_norm_w[l].reshape(1, D), xk[l].astype(BF16), xv[l].astype(BF16),
                         xk_norm_w[l].reshape(1, X_DH))

    wo = w_out[l]
    won = wo[CONV_WIDTH:].reshape(G, R, dh, D)
    won_pad = jnp.concatenate(
        [jnp.concatenate([won[g], jnp.zeros_like(won[g])] if g == 0 else [jnp.zeros_like(won[g]), won[g]],
                         axis=1).reshape(R * LANES, D) for g in range(G)], axis=0).astype(BF16)
    keys = peer_keys[l].reshape(2 * PEER_HEADS, PEER_NKEYS, PEER_HALF).astype(BF16)
    x2, h3, sc = _post_mix(
        x.reshape(T, D), yconv, ynsa.reshape(T, C_Q), wo[:CONV_WIDTH].astype(BF16), won_pad,
        xattn_norm_w[l].reshape(1, D), xq[l].astype(BF16),
        (xq_norm_w[l] * X_DH ** -0.5).reshape(1, X_DH), kmem, vmem, xo[l].astype(BF16),
        ffn_norm_w[l].reshape(1, D), peer_wq[l].astype(BF16), keys, S)

    out = _peer(sc, h3, peer_down[l].T.astype(BF16), peer_up[l].astype(BF16), x2)
    return out.reshape(B, S, D)
```

```python
import functools

import jax
import jax.numpy as jnp
from jax import lax
from jax.experimental import pallas as pl
from jax.experimental.pallas import tpu as pltpu

F32 = jnp.float32
BF16 = jnp.bfloat16
I32 = jnp.int32

EPS = 1e-6
NEG = -1e30
LANES = 128
SUBLANES = 8

D_MODEL = 1024
CONV_WIDTH = 512
NSA_HEADS = 8
NSA_GROUPS = 2
NSA_REP = NSA_HEADS // NSA_GROUPS
NSA_DH = 64
CMP_BLOCK = 32
CMP_STRIDE = 16
SEL_BLOCK = 64
SEL_TOPN = 16
WINDOW = 512
X_HEADS = 4
X_DH = D_MODEL // X_HEADS
PEER_HEADS = 8
PEER_NKEYS = 128
PEER_HALF = 128
PEER_TOPK = 16
PEER_EXPERTS = PEER_NKEYS * PEER_NKEYS

NT_DIMS = (((1,), (1,)), ((), ()))


def _dot(a, b):
    return jnp.dot(a, b, preferred_element_type=F32)


def _dot_nt(a, b):
    return lax.dot_general(a, b, NT_DIMS, preferred_element_type=F32)


def _gelu2(x):
    return x * (1.0 + lax.erf(x * (2.0 ** -0.5)))


def _gelu(x):
    return 0.5 * _gelu2(x)


def _rms(x, w):
    ms = jnp.mean(x * x, axis=-1, keepdims=True)
    return x * lax.rsqrt(ms + EPS) * w


def _cparams(sem, vmem_mb):
    return pltpu.CompilerParams(dimension_semantics=sem, vmem_limit_bytes=vmem_mb * 1024 * 1024)


C_CONV = 3 * CONV_WIDTH
C_Q = NSA_HEADS * LANES
C_KV = 6 * NSA_GROUPS * NSA_DH
C_KN = 4 * LANES
C_G = NSA_GROUPS * LANES
C_ALL = C_CONV + C_Q + C_KN + C_G


def _in_proj_kernel(tiles_per_seq, x_ref, nw_ref, w_ref, wvt_ref, cw_ref, cb_ref, qw_ref, kw_ref,
                    yconv_ref, q_ref, kvc_ref, kn_ref, vt_ref, g_ref, carry_ref):
    tm = x_ref.shape[0]
    hn = _rms(x_ref[...], nw_ref[...]).astype(BF16)

    p = _dot(hn, w_ref[:, 0:C_CONV])
    b_g = p[:, 0:CONV_WIDTH]
    u = p[:, CONV_WIDTH:2 * CONV_WIDTH] * p[:, 2 * CONV_WIDTH:3 * CONV_WIDTH]

    @pl.when((pl.program_id(0) % tiles_per_seq) == 0)
    def _sequence_start():
        carry_ref[...] = jnp.zeros(carry_ref.shape, F32)

    prev = carry_ref[...]
    p1 = prev[SUBLANES - 1:SUBLANES, :]
    p2 = prev[SUBLANES - 2:SUBLANES - 1, :]
    row = lax.broadcasted_iota(I32, u.shape, 0)
    u1 = jnp.where(row == 0, p1, pltpu.roll(u, 1, axis=0))
    u2 = jnp.where(row == 0, p2, jnp.where(row == 1, p1, pltpu.roll(u, 2, axis=0)))
    carry_ref[...] = u[tm - SUBLANES:tm, :]
    cw = cw_ref[...]
    y = b_g * (cw[0:1, :] * u2 + cw[1:2, :] * u1 + cw[2:3, :] * u + cb_ref[...])
    yconv_ref[...] = y.astype(BF16)

    pq = _dot(hn, w_ref[:, C_CONV:C_CONV + C_Q])
    for h in range(NSA_HEADS):
        blk = pq[:, h * LANES:(h + 1) * LANES]
        ms = jnp.sum(blk * blk, axis=-1, keepdims=True) * (1.0 / NSA_DH)
        q_ref[:, h * LANES:(h + 1) * LANES] = (
            blk * lax.rsqrt(ms + EPS) * qw_ref[:, h * LANES:(h + 1) * LANES]).astype(BF16)

    pkv = _dot(hn, w_ref[:, C_CONV + C_Q:C_CONV + C_Q + C_KN])
    kvc_ref[0] = pkv[:, 0:LANES]
    kvc_ref[1] = pkv[:, LANES:2 * LANES]
    lane = lax.broadcasted_iota(I32, (tm, LANES), 1)
    lo = lane < NSA_DH
    for j in range(2):
        blk = pkv[:, (2 + j) * LANES:(3 + j) * LANES]
        sq = blk * blk
        ms_lo = jnp.sum(jnp.where(lo, sq, 0.0), axis=-1, keepdims=True) * (1.0 / NSA_DH)
        ms_hi = jnp.sum(jnp.where(lo, 0.0, sq), axis=-1, keepdims=True) * (1.0 / NSA_DH)
        scale = jnp.where(lo, lax.rsqrt(ms_lo + EPS), lax.rsqrt(ms_hi + EPS))
        kn_ref[:, j * LANES:(j + 1) * LANES] = (blk * scale * kw_ref[j:j + 1, :]).astype(BF16)

    for j in range(tm // LANES):
        vt_ref[j] = _dot_nt(wvt_ref[...], hn[j * LANES:(j + 1) * LANES, :]).astype(BF16)

    pg = _dot(hn, w_ref[:, C_CONV + C_Q + C_KN:C_ALL])
    g_ref[...] = jax.nn.sigmoid(pg)


def _in_proj(x2d, nw, w_all, wvt, cw, cb, qw, kw, seq, tm=256):
    T = x2d.shape[0]
    const = lambda i: (0, 0)
    tile = lambda i: (i, 0)
    return pl.pallas_call(
        functools.partial(_in_proj_kernel, seq // tm),
        grid=(T // tm,),
        in_specs=[
            pl.BlockSpec((tm, D_MODEL), tile),
            pl.BlockSpec((1, D_MODEL), const),
            pl.BlockSpec((D_MODEL, C_ALL), const),
            pl.BlockSpec((2 * LANES, D_MODEL), const),
            pl.BlockSpec((3, CONV_WIDTH), const),
            pl.BlockSpec((1, CONV_WIDTH), const),
            pl.BlockSpec((1, C_Q), const),
            pl.BlockSpec((2, LANES), const),
        ],
        out_specs=[
            pl.BlockSpec((tm, CONV_WIDTH), tile),
            pl.BlockSpec((tm, C_Q), tile),
            pl.BlockSpec((2, tm, LANES), lambda i: (0, i, 0)),
            pl.BlockSpec((tm, 2 * LANES), tile),
            pl.BlockSpec((tm // LANES, 2 * LANES, LANES), lambda i: (i, 0, 0)),
            pl.BlockSpec((tm, C_G), tile),
        ],
        out_shape=[
            jax.ShapeDtypeStruct((T, CONV_WIDTH), BF16),
            jax.ShapeDtypeStruct((T, C_Q), BF16),
            jax.ShapeDtypeStruct((2, T, LANES), F32),
            jax.ShapeDtypeStruct((T, 2 * LANES), BF16),
            jax.ShapeDtypeStruct((T // LANES, 2 * LANES, LANES), BF16),
            jax.ShapeDtypeStruct((T, C_G), F32),
        ],
        scratch_shapes=[pltpu.VMEM((SUBLANES, CONV_WIDTH), F32)],
        compiler_params=_cparams(("arbitrary",), 48),
    )(x2d, nw, w_all, wvt, cw, cb, qw, kw)


def _compress_kernel(kvc_ref, pe_ref, w1_ref, w2k_ref, w2vt_ref, knw_ref, out_ref):
    n_rows = out_ref.shape[3]
    acc_a = jnp.zeros((n_rows, 2 * LANES), F32)
    acc_b = jnp.zeros((n_rows, 2 * LANES), F32)
    for l in range(CMP_STRIDE):
        rows = pl.ds(l, n_rows, stride=CMP_STRIDE)
        x = jnp.concatenate([kvc_ref[0, rows, :], kvc_ref[1, rows, :]], axis=1)
        acc_a = acc_a + _dot((x + pe_ref[l:l + 1, :]).astype(BF16), w1_ref[l])
        acc_b = acc_b + _dot((x + pe_ref[CMP_STRIDE + l:CMP_STRIDE + l + 1, :]).astype(BF16),
                             w1_ref[CMP_STRIDE + l])
    hid = _gelu(acc_a + pltpu.roll(acc_b, n_rows - 1, axis=0)).astype(BF16)
    row = lax.broadcasted_iota(I32, (n_rows, LANES), 0)
    col = lax.broadcasted_iota(I32, (LANES, n_rows), 1)
    keys = _dot(hid, w2k_ref[...])
    vals = _dot_nt(w2vt_ref[...], hid)
    for g in range(NSA_GROUPS):
        out = keys[:, g * LANES:(g + 1) * LANES]
        ms = jnp.sum(out * out, axis=-1, keepdims=True) * (1.0 / NSA_DH)
        out = out * lax.rsqrt(ms + EPS) * knw_ref[g]
        out_ref[0, 0, g] = jnp.where(row < n_rows - 1, out, 0.0).astype(BF16)
        out_ref[0, 1, g] = jnp.where(col < n_rows - 1, vals[g * LANES:(g + 1) * LANES, :], 0.0).astype(BF16)


def _compress(kvc, pe, w1, w2k, w2vt, knw, S):
    B = kvc.shape[1] // S
    n_rows = S // CMP_STRIDE
    return pl.pallas_call(
        _compress_kernel,
        grid=(B,),
        in_specs=[
            pl.BlockSpec((2, S, LANES), lambda b: (0, b, 0)),
            pl.BlockSpec((CMP_BLOCK, 2 * LANES), lambda b: (0, 0)),
            pl.BlockSpec((CMP_BLOCK, 2 * LANES, 2 * LANES), lambda b: (0, 0, 0)),
            pl.BlockSpec((2 * LANES, 2 * LANES), lambda b: (0, 0)),
            pl.BlockSpec((2 * LANES, 2 * LANES), lambda b: (0, 0)),
            pl.BlockSpec((NSA_GROUPS, 1, LANES), lambda b: (0, 0, 0)),
        ],
        out_specs=pl.BlockSpec((1, 2, NSA_GROUPS, n_rows, LANES), lambda b: (b, 0, 0, 0, 0)),
        out_shape=jax.ShapeDtypeStruct((B, 2, NSA_GROUPS, n_rows, LANES), BF16),
        compiler_params=_cparams(("arbitrary",), 32),
    )(kvc, pe, w1, w2k, w2vt, knw)


SEL_CHUNK = 512


def _nsa_kernel(q_ref, kn_ref, vt_ref, cmp_ref, g_ref, selmap_ref, y_ref, sel_ref, acc_ref):
    tq = q_ref.shape[1]
    cols = NSA_HEADS * tq
    i = pl.program_id(1)
    t0 = i * tq
    qb = q_ref[0]
    qs = jnp.concatenate([qb[:, h * LANES:(h + 1) * LANES] for h in range(NSA_HEADS)], axis=0)
    qpos = t0 + jnp.bitwise_and(lax.broadcasted_iota(I32, (1, cols), 1), tq - 1)
    both = lambda kv: (cmp_ref[0, kv, 0].astype(F32) + cmp_ref[0, kv, 1].astype(F32)).astype(BF16)

    n_cmp = cmp_ref.shape[3]
    s = _dot_nt(both(0), qs)
    cmp_last = lax.broadcasted_iota(I32, (n_cmp, cols), 0) * CMP_STRIDE + (CMP_BLOCK - 1)
    vis_c = cmp_last <= qpos
    s = jnp.where(vis_c, s, NEG)
    m = jnp.max(s, axis=0, keepdims=True)
    m = jnp.where(m > 0.5 * NEG, m, 0.0)
    e = jnp.where(vis_c, jnp.exp(s - m), 0.0)
    d = jnp.sum(e, axis=0, keepdims=True)
    p_c = e / jnp.where(d > 0.0, d, 1.0)
    o_c = _dot(both(1), p_c.astype(BF16))

    n_sel = selmap_ref.shape[0]
    blk = lax.broadcasted_iota(I32, (n_sel, tq), 0)
    cur = jnp.right_shift(t0 + lax.broadcasted_iota(I32, (n_sel, tq), 1), 6)
    forced = (blk == 0) | (blk == cur) | (blk == cur - 1)
    chosen = []
    for g in range(NSA_GROUPS):
        psum = p_c[:, g * NSA_REP * tq:(g * NSA_REP + 1) * tq]
        for r in range(1, NSA_REP):
            psum = psum + p_c[:, (g * NSA_REP + r) * tq:(g * NSA_REP + r + 1) * tq]
        p_hi = psum.astype(BF16)
        p_lo = (psum - p_hi.astype(F32)).astype(BF16)
        imp = _dot(selmap_ref[...], p_hi) + _dot(selmap_ref[...], p_lo)
        imp = jnp.where(blk > cur, -jnp.inf, jnp.where(forced, jnp.inf, imp))
        rank = jnp.zeros((n_sel, tq), I32)
        for jp in range(n_sel):
            other = imp[jp:jp + 1, :]
            beats = (other > imp) | ((other == imp) & (blk > jp))
            rank = rank + beats.astype(I32)
        chosen += [rank < min(SEL_TOPN, n_sel)] * NSA_REP
    sel_ref[...] = jnp.where(jnp.concatenate(chosen, axis=1), qpos, -1)

    acc_ref[...] = jnp.zeros(acc_ref.shape, F32)
    units = SEL_CHUNK // LANES
    blocks = SEL_CHUNK // SEL_BLOCK
    koff = lax.broadcasted_iota(I32, (SEL_BLOCK, cols), 0)

    def sel_chunk(c, carry):
        m_old, l_old = carry
        start = pl.multiple_of(c * SEL_CHUNK, SEL_CHUNK)
        k = kn_ref[0, pl.ds(start, SEL_CHUNK), 0:LANES]
        vt = jnp.concatenate([vt_ref[c * units + u, 0:LANES, :] for u in range(units)], axis=1)
        sc = _dot_nt(k, qs)
        parts = []
        for b in range(blocks):
            limit = sel_ref[pl.ds(c * blocks + b, 1), :] - (start + b * SEL_BLOCK)
            parts.append(jnp.where(koff <= limit, sc[b * SEL_BLOCK:(b + 1) * SEL_BLOCK], NEG))
        sc = jnp.concatenate(parts, axis=0)
        m_new = jnp.maximum(m_old, jnp.max(sc, axis=0, keepdims=True))
        alpha = jnp.exp(m_old - m_new)
        p = jnp.exp(sc - m_new)
        acc_ref[...] = alpha * acc_ref[...] + _dot(vt, p.astype(BF16))
        return m_new, alpha * l_old + jnp.sum(p, axis=0, keepdims=True)

    init = (jnp.full((1, cols), NEG, F32), jnp.zeros((1, cols), F32))
    _, l_s = lax.fori_loop(0, (t0 + tq + SEL_CHUNK - 1) // SEL_CHUNK, sel_chunk, init)
    o_s = acc_ref[...]

    n_win = WINDOW + tq
    w_start = pl.multiple_of(jnp.maximum(t0 - WINDOW, 0), LANES)
    w_unit = w_start // LANES
    k_w = kn_ref[0, pl.ds(w_start, n_win), LANES:2 * LANES]
    vt_w = jnp.concatenate([vt_ref[w_unit + u, LANES:2 * LANES, :] for u in range(n_win // LANES)], axis=1)
    diff = (qpos - w_start) - lax.broadcasted_iota(I32, (n_win, cols), 0)
    in_window = lax.bitcast_convert_type(diff, jnp.uint32) < jnp.uint32(WINDOW)
    s_w = jnp.where(in_window, _dot_nt(k_w, qs), NEG)
    p_w = jnp.exp(s_w - jnp.max(s_w, axis=0, keepdims=True))
    l_w = jnp.sum(p_w, axis=0, keepdims=True)
    o_w = _dot(vt_w, p_w.astype(BF16))

    g_t = [g_ref[0, :, g * LANES:(g + 1) * LANES].T for g in range(NSA_GROUPS)]
    gate = lambda br: jnp.concatenate(
        [g_t[h // NSA_REP][3 * (h % NSA_REP) + br:3 * (h % NSA_REP) + br + 1, :] for h in range(NSA_HEADS)], axis=1)
    y_t = o_c * gate(0) + o_s * (gate(1) / l_s) + o_w * (gate(2) / l_w)
    for h in range(NSA_HEADS):
        y_ref[0, :, h * LANES:(h + 1) * LANES] = y_t[:, h * tq:(h + 1) * tq].T.astype(BF16)


def _nsa(q, kn, vt, kvcmp, gates, selmap, tq=128):
    B, S = q.shape[0], q.shape[1]
    n_sel = S // SEL_BLOCK
    cols = NSA_HEADS * tq
    units = S // LANES
    n_cmp = kvcmp.shape[3]
    return pl.pallas_call(
        _nsa_kernel,
        grid=(B, S // tq),
        in_specs=[
            pl.BlockSpec((1, tq, C_Q), lambda b, i: (b, i, 0)),
            pl.BlockSpec((1, S, 2 * LANES), lambda b, i: (b, 0, 0)),
            pl.BlockSpec((units, 2 * LANES, LANES), lambda b, i: (b, 0, 0)),
            pl.BlockSpec((1, 2, NSA_GROUPS, n_cmp, LANES), lambda b, i: (b, 0, 0, 0, 0)),
            pl.BlockSpec((1, tq, C_G), lambda b, i: (b, i, 0)),
            pl.BlockSpec((n_sel, n_cmp), lambda b, i: (0, 0)),
        ],
        out_specs=pl.BlockSpec((1, tq, C_Q), lambda b, i: (b, i, 0)),
        out_shape=jax.ShapeDtypeStruct((B, S, C_Q), BF16),
        scratch_shapes=[
            pltpu.VMEM((n_sel, cols), I32),
            pltpu.VMEM((LANES, cols), F32),
        ],
        compiler_params=_cparams(("arbitrary", "arbitrary"), 56),
    )(q, kn, vt, kvcmp, gates, selmap)


def _mem_kv_kernel(mem_ref, nw_ref, wk_ref, wv_ref, kw_ref, k_ref, v_ref):
    mn = _rms(mem_ref[0], nw_ref[...]).astype(BF16)
    k = _dot(mn, wk_ref[...])
    for h in range(X_HEADS):
        sl = slice(h * X_DH, (h + 1) * X_DH)
        k_ref[0, :, sl] = _rms(k[:, sl], kw_ref[...]).astype(BF16)
    v_ref[0] = _dot(mn, wv_ref[...]).astype(BF16)


def _mem_kv(mem, nw, wk, wv, kw):
    B, M, D = mem.shape
    blk = pl.BlockSpec((1, M, D), lambda b: (b, 0, 0))
    const = lambda b: (0, 0)
    return pl.pallas_call(
        _mem_kv_kernel,
        grid=(B,),
        in_specs=[blk, pl.BlockSpec((1, D), const), pl.BlockSpec((D, D), const),
                  pl.BlockSpec((D, D), const), pl.BlockSpec((1, X_DH), const)],
        out_specs=[blk, blk],
        out_shape=[jax.ShapeDtypeStruct((B, M, D), BF16)] * 2,
        compiler_params=_cparams(("arbitrary",), 32),
    )(mem, nw, wk, wv, kw)


def _post_mix_kernel(x_ref, yc_ref, yn_ref, woc_ref, won_ref, xnw_ref, xq_ref, xqw_ref, km_ref, vm_ref,
                     xo_ref, fnw_ref, pwq_ref, keys_ref, x2_ref, h3_ref, sc_ref):
    x1 = x_ref[...] + _dot(yc_ref[...], woc_ref[...]) + _dot(yn_ref[...], won_ref[...])

    h2 = _rms(x1, xnw_ref[...]).astype(BF16)
    qx = _dot(h2, xq_ref[...])
    heads = []
    for h in range(X_HEADS):
        sl = slice(h * X_DH, (h + 1) * X_DH)
        qh = _rms(qx[:, sl], xqw_ref[...]).astype(BF16)
        s = _dot_nt(qh, km_ref[0, :, sl])
        e = jnp.exp(s - jnp.max(s, axis=-1, keepdims=True))
        p = e / jnp.sum(e, axis=-1, keepdims=True)
        heads.append(_dot(p.astype(BF16), vm_ref[0, :, sl]))
    o = jnp.concatenate(heads, axis=1).astype(BF16)
    x2 = x1 + _dot(o, xo_ref[...])
    x2_ref[...] = x2

    h3 = _rms(x2, fnw_ref[...]).astype(BF16)
    h3_ref[...] = h3
    qp = _dot(h3, pwq_ref[...]).astype(BF16)
    for c in range(2 * PEER_HEADS):
        sl = slice(c * PEER_HALF, (c + 1) * PEER_HALF)
        for j in range(qp.shape[0] // LANES):
            sc_ref[c, j] = _dot_nt(keys_ref[c], qp[j * LANES:(j + 1) * LANES, sl])


def _post_mix(x2d, yconv, ynsa, woc, won, xnw, xq, xqw, kmem, vmem, xo, fnw, pwq, keys, seq, tm=256):
    T, D = x2d.shape
    M = kmem.shape[1]
    n_sc = 2 * PEER_HEADS * PEER_NKEYS
    tile = lambda i: (i, 0)
    const = lambda i: (0, 0)
    per_batch = lambda i: ((i * tm) // seq, 0, 0)
    return pl.pallas_call(
        _post_mix_kernel,
        grid=(T // tm,),
        in_specs=[
            pl.BlockSpec((tm, D), tile),
            pl.BlockSpec((tm, CONV_WIDTH), tile),
            pl.BlockSpec((tm, C_Q), tile),
            pl.BlockSpec((CONV_WIDTH, D), const),
            pl.BlockSpec((C_Q, D), const),
            pl.BlockSpec((1, D), const),
            pl.BlockSpec((D, D), const),
            pl.BlockSpec((1, X_DH), const),
            pl.BlockSpec((1, M, D), per_batch),
            pl.BlockSpec((1, M, D), per_batch),
            pl.BlockSpec((D, D), const),
            pl.BlockSpec((1, D), const),
            pl.BlockSpec((D, n_sc), const),
            pl.BlockSpec((2 * PEER_HEADS, PEER_NKEYS, PEER_HALF), lambda i: (0, 0, 0)),
        ],
        out_specs=[pl.BlockSpec((tm, D), tile), pl.BlockSpec((tm, D), tile),
                   pl.BlockSpec((2 * PEER_HEADS, tm // LANES, PEER_NKEYS, LANES), lambda i: (0, i, 0, 0))],
        out_shape=[
            jax.ShapeDtypeStruct((T, D), F32),
            jax.ShapeDtypeStruct((T, D), BF16),
            jax.ShapeDtypeStruct((2 * PEER_HEADS, T // LANES, PEER_NKEYS, LANES), F32),
        ],
        compiler_params=_cparams(("arbitrary",), 56),
    )(x2d, yconv, ynsa, woc, won, xnw, xq, xqw, kmem, vmem, xo, fnw, pwq, keys)


STAIR = [(a, b) for a in range(PEER_TOPK) for b in range(PEER_TOPK // (a + 1))]
N_STAIR_VREGS = -(-len(STAIR) // SUBLANES)


def _stair_vregs(axis, k):
    return sorted({r // SUBLANES for r, ab in enumerate(STAIR) if ab[axis] == k})


def _stair_tables():
    rows = N_STAIR_VREGS * SUBLANES
    pad = rows - len(STAIR)
    ta = jnp.asarray([float(a) for a, _ in STAIR] + [-1.0] * pad, F32)
    tb = jnp.asarray([float(b) for _, b in STAIR] + [-1.0] * pad, F32)
    return jnp.broadcast_to(ta[:, None], (rows, LANES)), jnp.broadcast_to(tb[:, None], (rows, LANES))


def _make_topk(sc_ref, ta_ref, tb_ref, s_out, e_out):
    K = PEER_TOPK
    NV = N_STAIR_VREGS
    kidx = lax.broadcasted_iota(I32, (PEER_NKEYS, LANES), 0).astype(F32)
    ridx = [(lax.broadcasted_iota(I32, (SUBLANES, LANES), 0) + SUBLANES * j).astype(F32) for j in range(NV)]
    ta = [ta_ref[SUBLANES * j:SUBLANES * (j + 1), :] for j in range(NV)]
    tb = [tb_ref[SUBLANES * j:SUBLANES * (j + 1), :] for j in range(NV)]
    zero = jnp.zeros((SUBLANES, LANES), F32)

    def top1(s):
        m = jnp.max(s, axis=0, keepdims=True)
        idx = jnp.min(jnp.where(s == m, kidx, float(PEER_NKEYS)), axis=0, keepdims=True)
        return m, idx, jnp.where(kidx == idx, -jnp.inf, s)

    def sorted_top(s, table, axis):
        val = [zero] * NV
        key = [zero] * NV
        for k in range(K):
            m, idx, s = top1(s)
            for j in _stair_vregs(axis, k):
                hit = table[j] == float(k)
                val[j] = jnp.where(hit, m, val[j])
                key[j] = jnp.where(hit, idx, key[j])
        return val, key

    def top_half(h, g, which):
        return sorted_top(sc_ref[2 * h + which, g], (ta, tb)[which], which)

    def select(first, second):
        (v1, k1), (v2, k2) = first, second
        cand = [jnp.where(ta[j] >= 0.0, v1[j] + v2[j], -jnp.inf) for j in range(NV)]
        ce = [k1[j] * float(PEER_NKEYS) + k2[j] for j in range(NV)]
        sub = lax.broadcasted_iota(I32, (SUBLANES, LANES), 0)
        best_s = [zero] * (K // SUBLANES)
        best_e = [zero] * (K // SUBLANES)
        for k in range(K):
            mx = cand[0]
            for j in range(1, NV):
                mx = jnp.maximum(mx, cand[j])
            m = jnp.max(mx, axis=0, keepdims=True)
            ix = jnp.where(cand[0] == m, ridx[0], float(NV * SUBLANES))
            for j in range(1, NV):
                ix = jnp.minimum(ix, jnp.where(cand[j] == m, ridx[j], float(NV * SUBLANES)))
            idx = jnp.min(ix, axis=0, keepdims=True)
            hits = [ridx[j] == idx for j in range(NV)]
            es = jnp.where(hits[0], ce[0], 0.0)
            for j in range(1, NV):
                es = es + jnp.where(hits[j], ce[j], 0.0)
            cand = [jnp.where(hits[j], -jnp.inf, cand[j]) for j in range(NV)]
            here = sub == (k % SUBLANES)
            best_s[k // SUBLANES] = jnp.where(here, m, best_s[k // SUBLANES])
            best_e[k // SUBLANES] = jnp.where(here, jnp.sum(es, axis=0, keepdims=True), best_e[k // SUBLANES])
        return jnp.concatenate(best_s, axis=0), jnp.concatenate(best_e, axis=0)

    def store(h, best):
        rows = pl.ds(pl.multiple_of(h * K, K), K)
        s_out[rows, :] = best[0]
        e_out[rows, :] = best[1]

    def finish(rows, ei_dst, ej_dst, gate_dst):
        s_all = s_out[...]
        gates = []
        for h in range(PEER_HEADS):
            sh = s_all[h * K:(h + 1) * K]
            ex = jnp.exp(sh - sh[0:1])
            gates.append(ex / jnp.sum(ex, axis=0, keepdims=True))
        gate_dst[rows, :] = jnp.concatenate(gates, axis=0).T
        e_int = e_out[...].T.astype(I32)
        ei_dst[rows, :] = jnp.right_shift(e_int, 7)
        ej_dst[rows, :] = jnp.bitwise_and(e_int, PEER_NKEYS - 1)

    return top_half, select, store, finish


G_ROWS = PEER_NKEYS // 2
G_PITCH = G_ROWS + SUBLANES
HI_MASK = 0xFFFF0000


def _peer_kernel(sc_ref, ta_ref, tb_ref, h_ref, wd_ref, wu_ref, x2_ref, out_ref,
                 g_ref, ei_ref, ej_ref, gate_ref, s_out, e_out):
    tm = h_ref.shape[0]
    ec = wd_ref.shape[1]
    ipc = ec // LANES
    half = ipc // 2
    half_bits = half.bit_length() - 1
    t = pl.program_id(0)
    c = pl.program_id(1)
    n_grp = tm // LANES
    steps_per_grp = PEER_EXPERTS // ec // n_grp
    heads_per_step = PEER_HEADS // steps_per_grp
    rd = (t + 1) % 2
    wr = t % 2
    ei_rd, ej_rd, gate_rd = ei_ref.at[rd], ej_ref.at[rd], gate_ref.at[rd]

    @pl.when((t == 0) & (c == 0))
    def _first_step():
        ei_ref[...] = jnp.zeros(ei_ref.shape, I32)
        ej_ref[...] = jnp.zeros(ej_ref.shape, I32)
        gate_ref[...] = jnp.zeros(gate_ref.shape, F32)

    @pl.when(c == 0)
    def _scatter():
        out_ref[...] = x2_ref[...]
        m = lax.broadcasted_iota(I32, (PEER_NKEYS, 2 * LANES), 0)
        r = jnp.bitwise_and(m, G_ROWS - 1)
        i_of_m = (jnp.right_shift(r, half_bits) * ipc + 2 * jnp.bitwise_and(r, half - 1)
                  + jnp.right_shift(m, 6))
        as_bf16 = lambda v: v.astype(F32).astype(BF16)
        i_of_m = as_bf16(i_of_m)
        n = as_bf16(lax.broadcasted_iota(I32, (LANES, LANES), 0))
        one = jnp.ones((LANES, LANES), BF16)
        zero = jnp.zeros((LANES, LANES), BF16)

        def pair(u, carry):
            ta_, tb_ = pl.ds(2 * u, 1), pl.ds(2 * u + 1, 1)
            ri = as_bf16(jnp.concatenate([ei_rd[ta_, :], ei_rd[tb_, :]], axis=1))
            rg = (0.5 * jnp.concatenate([gate_rd[ta_, :], gate_rd[tb_, :]], axis=1)).astype(BF16)
            c_t = jnp.where(i_of_m == ri, rg, jnp.zeros_like(rg))
            q_a = jnp.where(n == as_bf16(ej_rd[ta_, :]), one, zero)
            q_b = jnp.where(n == as_bf16(ej_rd[tb_, :]), one, zero)
            q_t = jnp.concatenate([jnp.concatenate([q_a, zero], axis=1),
                                   jnp.concatenate([zero, q_b], axis=1)], axis=0)
            g2 = _dot_nt(c_t, q_t)
            for tok in range(2):
                g = g2[:, tok * LANES:(tok + 1) * LANES].astype(BF16).astype(F32)
                bits = lax.bitcast_convert_type(g, jnp.uint32)
                word = jnp.bitwise_or(jnp.right_shift(bits[0:G_ROWS], 16),
                                      jnp.bitwise_and(bits[G_ROWS:2 * G_ROWS], jnp.uint32(HI_MASK)))
                g_ref[pl.ds(pl.multiple_of((2 * u + tok) * G_PITCH, SUBLANES), G_ROWS), :] = word
            return carry

        lax.fori_loop(0, tm // 2, pair, 0, unroll=16)

    top_half, select, store, finish = _make_topk(sc_ref, ta_ref, tb_ref, s_out, e_out)
    grp = c // steps_per_grp
    h0 = (c % steps_per_grp) * heads_per_step

    def experts_slice(k):
        a = _dot(h_ref[...], wd_ref[:, 2 * k * LANES:2 * (k + 1) * LANES]).astype(BF16)
        word = g_ref[pl.ds(c * half + k, tm, stride=G_PITCH), :]
        g_lo = lax.bitcast_convert_type(jnp.left_shift(word, 16), F32).astype(BF16)
        g_hi = lax.bitcast_convert_type(jnp.bitwise_and(word, jnp.uint32(HI_MASK)), F32).astype(BF16)
        return jnp.concatenate([_gelu2(a[:, 0:LANES]) * g_lo, _gelu2(a[:, LANES:2 * LANES]) * g_hi], axis=1)

    assert heads_per_step == 2 and half == 4
    h_a, h_b = h0, h0 + 1
    a_first = top_half(h_a, grp, 0)
    z = [experts_slice(0)]
    a_second = top_half(h_a, grp, 1)
    z.append(experts_slice(1))
    b_first = top_half(h_b, grp, 0)
    z.append(experts_slice(2))
    b_second = top_half(h_b, grp, 1)
    z.append(experts_slice(3))
    best_a = select(a_first, a_second)
    out_ref[...] += _dot(jnp.concatenate(z, axis=1), wu_ref[...])
    best_b = select(b_first, b_second)
    store(h_a, best_a)
    store(h_b, best_b)

    @pl.when(c % steps_per_grp == steps_per_grp - 1)
    def _group_done():
        rows = pl.ds(pl.multiple_of(grp * LANES, LANES), LANES)
        finish(rows, ei_ref.at[wr], ej_ref.at[wr], gate_ref.at[wr])


def _peer(sc, h3, wd_t, wu, x2, tm=512, ec=1024):
    T, D = h3.shape
    tm = min(tm, T)
    n_tiles = T // tm
    n_grp = tm // LANES
    n_chunks = wu.shape[0] // ec
    assert n_chunks % n_grp == 0 and PEER_HEADS % (n_chunks // n_grp) == 0
    ta, tb = _stair_tables()
    prev = lambda t, c: (jnp.maximum(t - 1, 0), 0)
    return pl.pallas_call(
        _peer_kernel,
        grid=(n_tiles + 1, n_chunks),
        in_specs=[
            pl.BlockSpec((2 * PEER_HEADS, n_grp, PEER_NKEYS, LANES),
                         lambda t, c: (0, jnp.minimum(t, n_tiles - 1), 0, 0)),
            pl.BlockSpec(ta.shape, lambda t, c: (0, 0)),
            pl.BlockSpec(tb.shape, lambda t, c: (0, 0)),
            pl.BlockSpec((tm, D), prev),
            pl.BlockSpec((D, ec), lambda t, c: (0, c)),
            pl.BlockSpec((ec, D), lambda t, c: (c, 0)),
            pl.BlockSpec((tm, D), prev),
        ],
        out_specs=pl.BlockSpec((tm, D), prev),
        out_shape=jax.ShapeDtypeStruct((T, D), F32),
        scratch_shapes=[
            pltpu.VMEM((tm * G_PITCH, LANES), jnp.uint32),
            pltpu.VMEM((2, tm, LANES), I32),
            pltpu.VMEM((2, tm, LANES), I32),
            pltpu.VMEM((2, tm, LANES), F32),
            pltpu.VMEM((PEER_HEADS * PEER_TOPK, LANES), F32),
            pltpu.VMEM((PEER_HEADS * PEER_TOPK, LANES), F32),
        ],
        compiler_params=_cparams(("arbitrary", "arbitrary"), 60),
    )(sc, ta, tb, h3, wd_t, wu, x2)


def _pad_half(a, g):
    z = jnp.zeros_like(a)
    return jnp.concatenate([a, z] if g == 0 else [z, a], axis=-1)


def kernel(x, mem, mix_norm_w, w_in, conv_w, conv_b, cmp_pe, cmp_w1, cmp_w2, q_norm_w, k_norm_w, w_out,
           xattn_norm_w, mem_norm_w, xq, xk, xv, xo, xq_norm_w, xk_norm_w, ffn_norm_w, peer_wq, peer_keys,
           peer_down, peer_up):
    B, S, D = x.shape
    T = B * S
    l = 0
    G, R, dh = NSA_GROUPS, NSA_REP, NSA_DH

    w = w_in[l]
    o_q = C_CONV
    o_kv = o_q + NSA_HEADS * dh
    o_g = o_kv + C_KV
    wq = w[:, o_q:o_kv].reshape(D, G, R, dh)
    wq_pad = jnp.concatenate([_pad_half(wq[:, g], g).reshape(D, R * LANES) for g in range(G)], axis=1)
    wg = w[:, o_g:].reshape(D, G, R * 3)
    wg_pad = jnp.pad(wg, ((0, 0), (0, 0), (0, LANES - R * 3))).reshape(D, G * LANES)
    wkv = w[:, o_kv:o_g].reshape(D, 6, LANES)
    wkn = wkv[:, jnp.array([0, 1, 2, 4])].reshape(D, C_KN)
    wvt = wkv[:, jnp.array([3, 5])].reshape(D, 2 * LANES).T.astype(BF16)
    w_all = jnp.concatenate([w[:, :o_q], wq_pad, wkn, wg_pad], axis=1).astype(BF16)
    qw = q_norm_w[l] * dh ** -0.5
    qw_pad = jnp.concatenate([jnp.tile(_pad_half(qw, g), R) for g in range(G)]).reshape(1, C_Q)
    kw = jnp.stack([jnp.tile(k_norm_w[l, 1], G), jnp.tile(k_norm_w[l, 2], G)])

    yconv, q, kvc, kn, vt, gates = _in_proj(
        x.reshape(T, D), mix_norm_w[l].reshape(1, D), w_all, wvt, conv_w[l], conv_b[l].reshape(1, CONV_WIDTH),
        qw_pad, kw, S)

    n_rows = S // CMP_STRIDE
    kvg = jnp.array([0, 0, 1, 1])
    pe = jnp.concatenate([cmp_pe[l, kv] for kv in (0, 0, 1, 1)], axis=-1)
    w1 = cmp_w1[l].reshape(2, CMP_BLOCK, dh, dh)[kvg]
    w1 = jnp.einsum('qp,qjdh->jqdph', jnp.eye(4, dtype=F32), w1).reshape(CMP_BLOCK, 4 * dh, 4 * dh).astype(BF16)
    w2k = jnp.zeros((4 * dh, G * LANES), F32)
    w2vt = jnp.zeros((G * LANES, 4 * dh), F32)
    for g in range(G):
        o = g * LANES + g * dh
        w2k = w2k.at[g * dh:(g + 1) * dh, o:o + dh].set(cmp_w2[l, 0])
        w2vt = w2vt.at[o:o + dh, (G + g) * dh:(G + g + 1) * dh].set(cmp_w2[l, 1].T)
    knw = jnp.stack([_pad_half(k_norm_w[l, 0], g) for g in range(G)]).reshape(G, 1, LANES)
    kvcmp = _compress(kvc, pe, w1, w2k.astype(BF16), w2vt.astype(BF16), knw, S)

    n_sel = S // SEL_BLOCK
    cmp_start = jnp.arange(n_rows) * CMP_STRIDE
    sel_start = jnp.arange(n_sel) * SEL_BLOCK
    selmap_t = ((cmp_start[None, :] < sel_start[:, None] + SEL_BLOCK)
                & (cmp_start[None, :] + CMP_BLOCK > sel_start[:, None])
                & (jnp.arange(n_rows)[None, :] < n_rows - 1)).astype(BF16)
    ynsa = _nsa(q.reshape(B, S, C_Q), kn.reshape(B, S, 2 * LANES), vt, kvcmp, gates.reshape(B, S, C_G), selmap_t)

    kmem, vmem = _mem_kv(mem, mem_norm_w[l].reshape(1, D), xk[l].astype(BF16), xv[l].astype(BF16),
                         xk_norm_w[l].reshape(1, X_DH))

    wo = w_out[l]
    won = wo[CONV_WIDTH:].reshape(G, R, dh, D)
    won_pad = jnp.concatenate(
        [jnp.concatenate([won[g], jnp.zeros_like(won[g])] if g == 0 else [jnp.zeros_like(won[g]), won[g]],
                         axis=1).reshape(R * LANES, D) for g in range(G)], axis=0).astype(BF16)
    keys = peer_keys[l].reshape(2 * PEER_HEADS, PEER_NKEYS, PEER_HALF).astype(BF16)
    x2, h3, sc = _post_mix(
        x.reshape(T, D), yconv, ynsa.reshape(T, C_Q), wo[:CONV_WIDTH].astype(BF16), won_pad,
        xattn_norm_w[l].reshape(1, D), xq[l].astype(BF16),
        (xq_norm_w[l] * X_DH ** -0.5).reshape(1, X_DH), kmem, vmem, xo[l].astype(BF16),
        ffn_norm_w[l].reshape(1, D), peer_wq[l].astype(BF16), keys, S)

    out = _peer(sc, h3, peer_down[l].T.astype(BF16), peer_up[l].astype(BF16), x2)
    return out.reshape(B, S, D)
```

```python
import functools

import jax
import jax.numpy as jnp
from jax import lax
from jax.experimental import pallas as pl
from jax.experimental.pallas import tpu as pltpu

F32 = jnp.float32
BF16 = jnp.bfloat16
I32 = jnp.int32

EPS = 1e-6
NEG = -1e30
LANES = 128
SUBLANES = 8

D_MODEL = 1024
CONV_WIDTH = 512
NSA_HEADS = 8
NSA_GROUPS = 2
NSA_REP = NSA_HEADS // NSA_GROUPS
NSA_DH = 64
CMP_BLOCK = 32
CMP_STRIDE = 16
SEL_BLOCK = 64
SEL_TOPN = 16
WINDOW = 512
X_HEADS = 4
X_DH = D_MODEL // X_HEADS
PEER_HEADS = 8
PEER_NKEYS = 128
PEER_HALF = 128
PEER_TOPK = 16
PEER_EXPERTS = PEER_NKEYS * PEER_NKEYS

NT_DIMS = (((1,), (1,)), ((), ()))


def _dot(a, b):
    return jnp.dot(a, b, preferred_element_type=F32)


def _dot_nt(a, b):
    return lax.dot_general(a, b, NT_DIMS, preferred_element_type=F32)


def _gelu2(x):
    return x * (1.0 + lax.erf(x * (2.0 ** -0.5)))


def _gelu(x):
    return 0.5 * _gelu2(x)


def _rms(x, w):
    ms = jnp.mean(x * x, axis=-1, keepdims=True)
    return x * lax.rsqrt(ms + EPS) * w


def _cparams(sem, vmem_mb):
    return pltpu.CompilerParams(dimension_semantics=sem, vmem_limit_bytes=vmem_mb * 1024 * 1024)


C_CONV = 3 * CONV_WIDTH
C_Q = NSA_HEADS * LANES
C_KV = 6 * NSA_GROUPS * NSA_DH
C_KN = 4 * LANES
C_G = NSA_GROUPS * LANES
C_ALL = C_CONV + C_Q + C_KN + C_G


def _in_proj_kernel(tiles_per_seq, x_ref, nw_ref, w_ref, wvt_ref, cw_ref, cb_ref, qw_ref, kw_ref,
                    yconv_ref, q_ref, kvc_ref, kn_ref, vt_ref, g_ref, carry_ref):
    tm = x_ref.shape[0]
    hn = _rms(x_ref[...], nw_ref[...]).astype(BF16)

    p = _dot(hn, w_ref[:, 0:C_CONV])
    b_g = p[:, 0:CONV_WIDTH]
    u = p[:, CONV_WIDTH:2 * CONV_WIDTH] * p[:, 2 * CONV_WIDTH:3 * CONV_WIDTH]

    @pl.when((pl.program_id(0) % tiles_per_seq) == 0)
    def _sequence_start():
        carry_ref[...] = jnp.zeros(carry_ref.shape, F32)

    prev = carry_ref[...]
    p1 = prev[SUBLANES - 1:SUBLANES, :]
    p2 = prev[SUBLANES - 2:SUBLANES - 1, :]
    row = lax.broadcasted_iota(I32, u.shape, 0)
    u1 = jnp.where(row == 0, p1, pltpu.roll(u, 1, axis=0))
    u2 = jnp.where(row == 0, p2, jnp.where(row == 1, p1, pltpu.roll(u, 2, axis=0)))
    carry_ref[...] = u[tm - SUBLANES:tm, :]
    cw = cw_ref[...]
    y = b_g * (cw[0:1, :] * u2 + cw[1:2, :] * u1 + cw[2:3, :] * u + cb_ref[...])
    yconv_ref[...] = y.astype(BF16)

    pq = _dot(hn, w_ref[:, C_CONV:C_CONV + C_Q])
    for h in range(NSA_HEADS):
        blk = pq[:, h * LANES:(h + 1) * LANES]
        ms = jnp.sum(blk * blk, axis=-1, keepdims=True) * (1.0 / NSA_DH)
        q_ref[:, h * LANES:(h + 1) * LANES] = (
            blk * lax.rsqrt(ms + EPS) * qw_ref[:, h * LANES:(h + 1) * LANES]).astype(BF16)

    pkv = _dot(hn, w_ref[:, C_CONV + C_Q:C_CONV + C_Q + C_KN])
    kvc_ref[0] = pkv[:, 0:LANES]
    kvc_ref[1] = pkv[:, LANES:2 * LANES]
    lane = lax.broadcasted_iota(I32, (tm, LANES), 1)
    lo = lane < NSA_DH
    for j in range(2):
        blk = pkv[:, (2 + j) * LANES:(3 + j) * LANES]
        sq = blk * blk
        ms_lo = jnp.sum(jnp.where(lo, sq, 0.0), axis=-1, keepdims=True) * (1.0 / NSA_DH)
        ms_hi = jnp.sum(jnp.where(lo, 0.0, sq), axis=-1, keepdims=True) * (1.0 / NSA_DH)
        scale = jnp.where(lo, lax.rsqrt(ms_lo + EPS), lax.rsqrt(ms_hi + EPS))
        kn_ref[:, j * LANES:(j + 1) * LANES] = (blk * scale * kw_ref[j:j + 1, :]).astype(BF16)

    for j in range(tm // LANES):
        vt_ref[j] = _dot_nt(wvt_ref[...], hn[j * LANES:(j + 1) * LANES, :]).astype(BF16)

    pg = _dot(hn, w_ref[:, C_CONV + C_Q + C_KN:C_ALL])
    g_ref[...] = jax.nn.sigmoid(pg)


def _in_proj(x2d, nw, w_all, wvt, cw, cb, qw, kw, seq, tm=256):
    T = x2d.shape[0]
    const = lambda i: (0, 0)
    tile = lambda i: (i, 0)
    return pl.pallas_call(
        functools.partial(_in_proj_kernel, seq // tm),
        grid=(T // tm,),
        in_specs=[
            pl.BlockSpec((tm, D_MODEL), tile),
            pl.BlockSpec((1, D_MODEL), const),
            pl.BlockSpec((D_MODEL, C_ALL), const),
            pl.BlockSpec((2 * LANES, D_MODEL), const),
            pl.BlockSpec((3, CONV_WIDTH), const),
            pl.BlockSpec((1, CONV_WIDTH), const),
            pl.BlockSpec((1, C_Q), const),
            pl.BlockSpec((2, LANES), const),
        ],
        out_specs=[
            pl.BlockSpec((tm, CONV_WIDTH), tile),
            pl.BlockSpec((tm, C_Q), tile),
            pl.BlockSpec((2, tm, LANES), lambda i: (0, i, 0)),
            pl.BlockSpec((tm, 2 * LANES), tile),
            pl.BlockSpec((tm // LANES, 2 * LANES, LANES), lambda i: (i, 0, 0)),
            pl.BlockSpec((tm, C_G), tile),
        ],
        out_shape=[
            jax.ShapeDtypeStruct((T, CONV_WIDTH), BF16),
            jax.ShapeDtypeStruct((T, C_Q), BF16),
            jax.ShapeDtypeStruct((2, T, LANES), F32),
            jax.ShapeDtypeStruct((T, 2 * LANES), BF16),
            jax.ShapeDtypeStruct((T // LANES, 2 * LANES, LANES), BF16),
            jax.ShapeDtypeStruct((T, C_G), F32),
        ],
        scratch_shapes=[pltpu.VMEM((SUBLANES, CONV_WIDTH), F32)],
        compiler_params=_cparams(("arbitrary",), 48),
    )(x2d, nw, w_all, wvt, cw, cb, qw, kw)


def _compress_kernel(kvc_ref, pe_ref, w1_ref, w2k_ref, w2vt_ref, knw_ref, out_ref):
    n_rows = out_ref.shape[3]
    acc_a = jnp.zeros((n_rows, 2 * LANES), F32)
    acc_b = jnp.zeros((n_rows, 2 * LANES), F32)
    for l in range(CMP_STRIDE):
        rows = pl.ds(l, n_rows, stride=CMP_STRIDE)
        x = jnp.concatenate([kvc_ref[0, rows, :], kvc_ref[1, rows, :]], axis=1)
        acc_a = acc_a + _dot((x + pe_ref[l:l + 1, :]).astype(BF16), w1_ref[l])
        acc_b = acc_b + _dot((x + pe_ref[CMP_STRIDE + l:CMP_STRIDE + l + 1, :]).astype(BF16),
                             w1_ref[CMP_STRIDE + l])
    hid = _gelu(acc_a + pltpu.roll(acc_b, n_rows - 1, axis=0)).astype(BF16)
    row = lax.broadcasted_iota(I32, (n_rows, LANES), 0)
    col = lax.broadcasted_iota(I32, (LANES, n_rows), 1)
    keys = _dot(hid, w2k_ref[...])
    vals = _dot_nt(w2vt_ref[...], hid)
    for g in range(NSA_GROUPS):
        out = keys[:, g * LANES:(g + 1) * LANES]
        ms = jnp.sum(out * out, axis=-1, keepdims=True) * (1.0 / NSA_DH)
        out = out * lax.rsqrt(ms + EPS) * knw_ref[g]
        out_ref[0, 0, g] = jnp.where(row < n_rows - 1, out, 0.0).astype(BF16)
        out_ref[0, 1, g] = jnp.where(col < n_rows - 1, vals[g * LANES:(g + 1) * LANES, :], 0.0).astype(BF16)


def _compress(kvc, pe, w1, w2k, w2vt, knw, S):
    B = kvc.shape[1] // S
    n_rows = S // CMP_STRIDE
    return pl.pallas_call(
        _compress_kernel,
        grid=(B,),
        in_specs=[
            pl.BlockSpec((2, S, LANES), lambda b: (0, b, 0)),
            pl.BlockSpec((CMP_BLOCK, 2 * LANES), lambda b: (0, 0)),
            pl.BlockSpec((CMP_BLOCK, 2 * LANES, 2 * LANES), lambda b: (0, 0, 0)),
            pl.BlockSpec((2 * LANES, 2 * LANES), lambda b: (0, 0)),
            pl.BlockSpec((2 * LANES, 2 * LANES), lambda b: (0, 0)),
            pl.BlockSpec((NSA_GROUPS, 1, LANES), lambda b: (0, 0, 0)),
        ],
        out_specs=pl.BlockSpec((1, 2, NSA_GROUPS, n_rows, LANES), lambda b: (b, 0, 0, 0, 0)),
        out_shape=jax.ShapeDtypeStruct((B, 2, NSA_GROUPS, n_rows, LANES), BF16),
        compiler_params=_cparams(("arbitrary",), 32),
    )(kvc, pe, w1, w2k, w2vt, knw)


SEL_CHUNK = 512


def _nsa_kernel(q_ref, kn_ref, vt_ref, cmp_ref, g_ref, selmap_ref, y_ref, sel_ref, acc_ref):
    tq = q_ref.shape[1]
    cols = NSA_HEADS * tq
    i = pl.program_id(1)
    t0 = i * tq
    qb = q_ref[0]
    qs = jnp.concatenate([qb[:, h * LANES:(h + 1) * LANES] for h in range(NSA_HEADS)], axis=0)
    qpos = t0 + jnp.bitwise_and(lax.broadcasted_iota(I32, (1, cols), 1), tq - 1)
    both = lambda kv: (cmp_ref[0, kv, 0].astype(F32) + cmp_ref[0, kv, 1].astype(F32)).astype(BF16)

    n_cmp = cmp_ref.shape[3]
    s = _dot_nt(both(0), qs)
    cmp_last = lax.broadcasted_iota(I32, (n_cmp, cols), 0) * CMP_STRIDE + (CMP_BLOCK - 1)
    vis_c = cmp_last <= qpos
    s = jnp.where(vis_c, s, NEG)
    m = jnp.max(s, axis=0, keepdims=True)
    m = jnp.where(m > 0.5 * NEG, m, 0.0)
    e = jnp.where(vis_c, jnp.exp(s - m), 0.0)
    d = jnp.sum(e, axis=0, keepdims=True)
    p_c = e / jnp.where(d > 0.0, d, 1.0)
    o_c = _dot(both(1), p_c.astype(BF16))

    n_sel = selmap_ref.shape[0]
    blk = lax.broadcasted_iota(I32, (n_sel, tq), 0)
    cur = jnp.right_shift(t0 + lax.broadcasted_iota(I32, (n_sel, tq), 1), 6)
    forced = (blk == 0) | (blk == cur) | (blk == cur - 1)
    chosen = []
    for g in range(NSA_GROUPS):
        psum = p_c[:, g * NSA_REP * tq:(g * NSA_REP + 1) * tq]
        for r in range(1, NSA_REP):
            psum = psum + p_c[:, (g * NSA_REP + r) * tq:(g * NSA_REP + r + 1) * tq]
        p_hi = psum.astype(BF16)
        p_lo = (psum - p_hi.astype(F32)).astype(BF16)
        imp = _dot(selmap_ref[...], p_hi) + _dot(selmap_ref[...], p_lo)
        imp = jnp.where(blk > cur, -jnp.inf, jnp.where(forced, jnp.inf, imp))
        rank = jnp.zeros((n_sel, tq), I32)
        for jp in range(n_sel):
            other = imp[jp:jp + 1, :]
            beats = (other > imp) | ((other == imp) & (blk > jp))
            rank = rank + beats.astype(I32)
        chosen += [rank < min(SEL_TOPN, n_sel)] * NSA_REP
    sel_ref[...] = jnp.where(jnp.concatenate(chosen, axis=1), qpos, -1)

    acc_ref[...] = jnp.zeros(acc_ref.shape, F32)
    units = SEL_CHUNK // LANES
    blocks = SEL_CHUNK // SEL_BLOCK
    koff = lax.broadcasted_iota(I32, (SEL_BLOCK, cols), 0)

    def sel_chunk(c, carry):
        m_old, l_old = carry
        start = pl.multiple_of(c * SEL_CHUNK, SEL_CHUNK)
        k = kn_ref[0, pl.ds(start, SEL_CHUNK), 0:LANES]
        vt = jnp.concatenate([vt_ref[c * units + u, 0:LANES, :] for u in range(units)], axis=1)
        sc = _dot_nt(k, qs)
        parts = []
        for b in range(blocks):
            limit = sel_ref[pl.ds(c * blocks + b, 1), :] - (start + b * SEL_BLOCK)
            parts.append(jnp.where(koff <= limit, sc[b * SEL_BLOCK:(b + 1) * SEL_BLOCK], NEG))
        sc = jnp.concatenate(parts, axis=0)
        m_new = jnp.maximum(m_old, jnp.max(sc, axis=0, keepdims=True))
        alpha = jnp.exp(m_old - m_new)
        p = jnp.exp(sc - m_new)
        acc_ref[...] = alpha * acc_ref[...] + _dot(vt, p.astype(BF16))
        return m_new, alpha * l_old + jnp.sum(p, axis=0, keepdims=True)

    init = (jnp.full((1, cols), NEG, F32), jnp.zeros((1, cols), F32))
    _, l_s = lax.fori_loop(0, (t0 + tq + SEL_CHUNK - 1) // SEL_CHUNK, sel_chunk, init)
    o_s = acc_ref[...]

    n_win = WINDOW + tq
    w_start = pl.multiple_of(jnp.maximum(t0 - WINDOW, 0), LANES)
    w_unit = w_start // LANES
    k_w = kn_ref[0, pl.ds(w_start, n_win), LANES:2 * LANES]
    vt_w = jnp.concatenate([vt_ref[w_unit + u, LANES:2 * LANES, :] for u in range(n_win // LANES)], axis=1)
    diff = (qpos - w_start) - lax.broadcasted_iota(I32, (n_win, cols), 0)
    in_window = lax.bitcast_convert_type(diff, jnp.uint32) < jnp.uint32(WINDOW)
    s_w = jnp.where(in_window, _dot_nt(k_w, qs), NEG)
    p_w = jnp.exp(s_w - jnp.max(s_w, axis=0, keepdims=True))
    l_w = jnp.sum(p_w, axis=0, keepdims=True)
    o_w = _dot(vt_w, p_w.astype(BF16))

    g_t = [g_ref[0, :, g * LANES:(g + 1) * LANES].T for g in range(NSA_GROUPS)]
    gate = lambda br: jnp.concatenate(
        [g_t[h // NSA_REP][3 * (h % NSA_REP) + br:3 * (h % NSA_REP) + br + 1, :] for h in range(NSA_HEADS)], axis=1)
    y_t = o_c * gate(0) + o_s * (gate(1) / l_s) + o_w * (gate(2) / l_w)
    for h in range(NSA_HEADS):
        y_ref[0, :, h * LANES:(h + 1) * LANES] = y_t[:, h * tq:(h + 1) * tq].T.astype(BF16)


def _nsa(q, kn, vt, kvcmp, gates, selmap, tq=128):
    B, S = q.shape[0], q.shape[1]
    n_sel = S // SEL_BLOCK
    cols = NSA_HEADS * tq
    units = S // LANES
    n_cmp = kvcmp.shape[3]
    return pl.pallas_call(
        _nsa_kernel,
        grid=(B, S // tq),
        in_specs=[
            pl.BlockSpec((1, tq, C_Q), lambda b, i: (b, i, 0)),
            pl.BlockSpec((1, S, 2 * LANES), lambda b, i: (b, 0, 0)),
            pl.BlockSpec((units, 2 * LANES, LANES), lambda b, i: (b, 0, 0)),
            pl.BlockSpec((1, 2, NSA_GROUPS, n_cmp, LANES), lambda b, i: (b, 0, 0, 0, 0)),
            pl.BlockSpec((1, tq, C_G), lambda b, i: (b, i, 0)),
            pl.BlockSpec((n_sel, n_cmp), lambda b, i: (0, 0)),
        ],
        out_specs=pl.BlockSpec((1, tq, C_Q), lambda b, i: (b, i, 0)),
        out_shape=jax.ShapeDtypeStruct((B, S, C_Q), BF16),
        scratch_shapes=[
            pltpu.VMEM((n_sel, cols), I32),
            pltpu.VMEM((LANES, cols), F32),
        ],
        compiler_params=_cparams(("arbitrary", "arbitrary"), 56),
    )(q, kn, vt, kvcmp, gates, selmap)


def _mem_kv_kernel(mem_ref, nw_ref, wk_ref, wv_ref, kw_ref, k_ref, v_ref):
    mn = _rms(mem_ref[0], nw_ref[...]).astype(BF16)
    k = _dot(mn, wk_ref[...])
    for h in range(X_HEADS):
        sl = slice(h * X_DH, (h + 1) * X_DH)
        k_ref[0, :, sl] = _rms(k[:, sl], kw_ref[...]).astype(BF16)
    v_ref[0] = _dot(mn, wv_ref[...]).astype(BF16)


def _mem_kv(mem, nw, wk, wv, kw):
    B, M, D = mem.shape
    blk = pl.BlockSpec((1, M, D), lambda b: (b, 0, 0))
    const = lambda b: (0, 0)
    return pl.pallas_call(
        _mem_kv_kernel,
        grid=(B,),
        in_specs=[blk, pl.BlockSpec((1, D), const), pl.BlockSpec((D, D), const),
                  pl.BlockSpec((D, D), const), pl.BlockSpec((1, X_DH), const)],
        out_specs=[blk, blk],
        out_shape=[jax.ShapeDtypeStruct((B, M, D), BF16)] * 2,
        compiler_params=_cparams(("arbitrary",), 32),
    )(mem, nw, wk, wv, kw)


def _post_mix_kernel(x_ref, yc_ref, yn_ref, woc_ref, won_ref, xnw_ref, xq_ref, xqw_ref, km_ref, vm_ref,
                     xo_ref, fnw_ref, pwq_ref, keys_ref, x2_ref, h3_ref, sc_ref):
    tm = x_ref.shape[0]
    n_streams = 2
    rows_per = tm // n_streams

    def stream(i):
        rs = slice(i * rows_per, (i + 1) * rows_per)
        x1 = x_ref[rs, :] + _dot(yc_ref[rs, :], woc_ref[...]) + _dot(yn_ref[rs, :], won_ref[...])
        yield
        h2 = _rms(x1, xnw_ref[...]).astype(BF16)
        qx = _dot(h2, xq_ref[...])
        yield
        heads = []
        for h in range(X_HEADS):
            sl = slice(h * X_DH, (h + 1) * X_DH)
            qh = _rms(qx[:, sl], xqw_ref[...]).astype(BF16)
            s = _dot_nt(qh, km_ref[0, :, sl])
            e = jnp.exp(s - jnp.max(s, axis=-1, keepdims=True))
            p = e / jnp.sum(e, axis=-1, keepdims=True)
            heads.append(_dot(p.astype(BF16), vm_ref[0, :, sl]))
        yield
        o = jnp.concatenate(heads, axis=1).astype(BF16)
        x2 = x1 + _dot(o, xo_ref[...])
        x2_ref[rs, :] = x2
        yield
        h3 = _rms(x2, fnw_ref[...]).astype(BF16)
        h3_ref[rs, :] = h3
        qp = _dot(h3, pwq_ref[...]).astype(BF16)
        for c in range(2 * PEER_HEADS):
            sl = slice(c * PEER_HALF, (c + 1) * PEER_HALF)
            for j in range(rows_per // LANES):
                sc_ref[c, i * (rows_per // LANES) + j] = _dot_nt(keys_ref[c], qp[j * LANES:(j + 1) * LANES, sl])
        yield

    first, second = stream(0), stream(1)
    next(first)
    for _ in range(4):
        next(first)
        next(second)
    next(second)


def _post_mix(x2d, yconv, ynsa, woc, won, xnw, xq, xqw, kmem, vmem, xo, fnw, pwq, keys, seq, tm=512):
    T, D = x2d.shape
    M = kmem.shape[1]
    n_sc = 2 * PEER_HEADS * PEER_NKEYS
    tile = lambda i: (i, 0)
    const = lambda i: (0, 0)
    per_batch = lambda i: ((i * tm) // seq, 0, 0)
    return pl.pallas_call(
        _post_mix_kernel,
        grid=(T // tm,),
        in_specs=[
            pl.BlockSpec((tm, D), tile),
            pl.BlockSpec((tm, CONV_WIDTH), tile),
            pl.BlockSpec((tm, C_Q), tile),
            pl.BlockSpec((CONV_WIDTH, D), const),
            pl.BlockSpec((C_Q, D), const),
            pl.BlockSpec((1, D), const),
            pl.BlockSpec((D, D), const),
            pl.BlockSpec((1, X_DH), const),
            pl.BlockSpec((1, M, D), per_batch),
            pl.BlockSpec((1, M, D), per_batch),
            pl.BlockSpec((D, D), const),
            pl.BlockSpec((1, D), const),
            pl.BlockSpec((D, n_sc), const),
            pl.BlockSpec((2 * PEER_HEADS, PEER_NKEYS, PEER_HALF), lambda i: (0, 0, 0)),
        ],
        out_specs=[pl.BlockSpec((tm, D), tile), pl.BlockSpec((tm, D), tile),
                   pl.BlockSpec((2 * PEER_HEADS, tm // LANES, PEER_NKEYS, LANES), lambda i: (0, i, 0, 0))],
        out_shape=[
            jax.ShapeDtypeStruct((T, D), F32),
            jax.ShapeDtypeStruct((T, D), BF16),
            jax.ShapeDtypeStruct((2 * PEER_HEADS, T // LANES, PEER_NKEYS, LANES), F32),
        ],
        compiler_params=_cparams(("arbitrary",), 56),
    )(x2d, yconv, ynsa, woc, won, xnw, xq, xqw, kmem, vmem, xo, fnw, pwq, keys)


STAIR = [(a, b) for a in range(PEER_TOPK) for b in range(PEER_TOPK // (a + 1))]
N_STAIR_VREGS = -(-len(STAIR) // SUBLANES)


def _stair_vregs(axis, k):
    return sorted({r // SUBLANES for r, ab in enumerate(STAIR) if ab[axis] == k})


def _stair_tables():
    rows = N_STAIR_VREGS * SUBLANES
    pad = rows - len(STAIR)
    ta = jnp.asarray([float(a) for a, _ in STAIR] + [-1.0] * pad, F32)
    tb = jnp.asarray([float(b) for _, b in STAIR] + [-1.0] * pad, F32)
    return jnp.broadcast_to(ta[:, None], (rows, LANES)), jnp.broadcast_to(tb[:, None], (rows, LANES))


def _make_topk(sc_ref, ta_ref, tb_ref, s_out, e_out):
    K = PEER_TOPK
    NV = N_STAIR_VREGS
    kidx = lax.broadcasted_iota(I32, (PEER_NKEYS, LANES), 0).astype(F32)
    ridx = [(lax.broadcasted_iota(I32, (SUBLANES, LANES), 0) + SUBLANES * j).astype(F32) for j in range(NV)]
    ta = [ta_ref[SUBLANES * j:SUBLANES * (j + 1), :] for j in range(NV)]
    tb = [tb_ref[SUBLANES * j:SUBLANES * (j + 1), :] for j in range(NV)]
    zero = jnp.zeros((SUBLANES, LANES), F32)

    def top1(s):
        m = jnp.max(s, axis=0, keepdims=True)
        idx = jnp.min(jnp.where(s == m, kidx, float(PEER_NKEYS)), axis=0, keepdims=True)
        return m, idx, jnp.where(kidx == idx, -jnp.inf, s)

    def sorted_top(s, table, axis):
        val = [zero] * NV
        key = [zero] * NV
        for k in range(K):
            m, idx, s = top1(s)
            for j in _stair_vregs(axis, k):
                hit = table[j] == float(k)
                val[j] = jnp.where(hit, m, val[j])
                key[j] = jnp.where(hit, idx, key[j])
        return val, key

    def top_half(h, g, which):
        return sorted_top(sc_ref[2 * h + which, g], (ta, tb)[which], which)

    def select(first, second):
        (v1, k1), (v2, k2) = first, second
        cand = [jnp.where(ta[j] >= 0.0, v1[j] + v2[j], -jnp.inf) for j in range(NV)]
        ce = [k1[j] * float(PEER_NKEYS) + k2[j] for j in range(NV)]
        sub = lax.broadcasted_iota(I32, (SUBLANES, LANES), 0)
        best_s = [zero] * (K // SUBLANES)
        best_e = [zero] * (K // SUBLANES)
        for k in range(K):
            mx = cand[0]
            for j in range(1, NV):
                mx = jnp.maximum(mx, cand[j])
            m = jnp.max(mx, axis=0, keepdims=True)
            ix = jnp.where(cand[0] == m, ridx[0], float(NV * SUBLANES))
            for j in range(1, NV):
                ix = jnp.minimum(ix, jnp.where(cand[j] == m, ridx[j], float(NV * SUBLANES)))
            idx = jnp.min(ix, axis=0, keepdims=True)
            hits = [ridx[j] == idx for j in range(NV)]
            es = jnp.where(hits[0], ce[0], 0.0)
            for j in range(1, NV):
                es = es + jnp.where(hits[j], ce[j], 0.0)
            cand = [jnp.where(hits[j], -jnp.inf, cand[j]) for j in range(NV)]
            here = sub == (k % SUBLANES)
            best_s[k // SUBLANES] = jnp.where(here, m, best_s[k // SUBLANES])
            best_e[k // SUBLANES] = jnp.where(here, jnp.sum(es, axis=0, keepdims=True), best_e[k // SUBLANES])
        return jnp.concatenate(best_s, axis=0), jnp.concatenate(best_e, axis=0)

    def store(h, best):
        rows = pl.ds(pl.multiple_of(h * K, K), K)
        s_out[rows, :] = best[0]
        e_out[rows, :] = best[1]

    def finish(rows, ei_dst, ej_dst, gate_dst):
        s_all = s_out[...]
        gates = []
        for h in range(PEER_HEADS):
            sh = s_all[h * K:(h + 1) * K]
            ex = jnp.exp(sh - sh[0:1])
            gates.append(ex / jnp.sum(ex, axis=0, keepdims=True))
        gate_dst[rows, :] = jnp.concatenate(gates, axis=0).T
        e_int = e_out[...].T.astype(I32)
        ei_dst[rows, :] = jnp.right_shift(e_int, 7)
        ej_dst[rows, :] = jnp.bitwise_and(e_int, PEER_NKEYS - 1)

    return top_half, select, store, finish


G_ROWS = PEER_NKEYS // 2
G_PITCH = G_ROWS + SUBLANES
HI_MASK = 0xFFFF0000


def _peer_kernel(sc_ref, ta_ref, tb_ref, h_ref, wd_ref, wu_ref, x2_ref, out_ref,
                 g_ref, ei_ref, ej_ref, gate_ref, s_out, e_out):
    tm = h_ref.shape[0]
    ec = wd_ref.shape[1]
    ipc = ec // LANES
    half = ipc // 2
    half_bits = half.bit_length() - 1
    t = pl.program_id(0)
    c = pl.program_id(1)
    n_grp = tm // LANES
    steps_per_grp = PEER_EXPERTS // ec // n_grp
    heads_per_step = PEER_HEADS // steps_per_grp
    rd = (t + 1) % 2
    wr = t % 2
    ei_rd, ej_rd, gate_rd = ei_ref.at[rd], ej_ref.at[rd], gate_ref.at[rd]

    @pl.when((t == 0) & (c == 0))
    def _first_step():
        ei_ref[...] = jnp.zeros(ei_ref.shape, I32)
        ej_ref[...] = jnp.zeros(ej_ref.shape, I32)
        gate_ref[...] = jnp.zeros(gate_ref.shape, F32)

    @pl.when(c == 0)
    def _scatter():
        out_ref[...] = x2_ref[...]
        m = lax.broadcasted_iota(I32, (PEER_NKEYS, 2 * LANES), 0)
        r = jnp.bitwise_and(m, G_ROWS - 1)
        i_of_m = (jnp.right_shift(r, half_bits) * ipc + 2 * jnp.bitwise_and(r, half - 1)
                  + jnp.right_shift(m, 6))
        as_bf16 = lambda v: v.astype(F32).astype(BF16)
        i_of_m = as_bf16(i_of_m)
        n = as_bf16(lax.broadcasted_iota(I32, (LANES, LANES), 0))
        one = jnp.ones((LANES, LANES), BF16)
        zero = jnp.zeros((LANES, LANES), BF16)

        def pair(u, carry):
            ta_, tb_ = pl.ds(2 * u, 1), pl.ds(2 * u + 1, 1)
            ri = as_bf16(jnp.concatenate([ei_rd[ta_, :], ei_rd[tb_, :]], axis=1))
            rg = (0.5 * jnp.concatenate([gate_rd[ta_, :], gate_rd[tb_, :]], axis=1)).astype(BF16)
            c_t = jnp.where(i_of_m == ri, rg, jnp.zeros_like(rg))
            q_a = jnp.where(n == as_bf16(ej_rd[ta_, :]), one, zero)
            q_b = jnp.where(n == as_bf16(ej_rd[tb_, :]), one, zero)
            q_t = jnp.concatenate([jnp.concatenate([q_a, zero], axis=1),
                                   jnp.concatenate([zero, q_b], axis=1)], axis=0)
            g2 = _dot_nt(c_t, q_t)
            for tok in range(2):
                g = g2[:, tok * LANES:(tok + 1) * LANES].astype(BF16).astype(F32)
                bits = lax.bitcast_convert_type(g, jnp.uint32)
                word = jnp.bitwise_or(jnp.right_shift(bits[0:G_ROWS], 16),
                                      jnp.bitwise_and(bits[G_ROWS:2 * G_ROWS], jnp.uint32(HI_MASK)))
                g_ref[pl.ds(pl.multiple_of((2 * u + tok) * G_PITCH, SUBLANES), G_ROWS), :] = word
            return carry

        lax.fori_loop(0, tm // 2, pair, 0, unroll=16)

    top_half, select, store, finish = _make_topk(sc_ref, ta_ref, tb_ref, s_out, e_out)
    grp = c // steps_per_grp
    h0 = (c % steps_per_grp) * heads_per_step

    def experts_slice(k):
        a = _dot(h_ref[...], wd_ref[:, 2 * k * LANES:2 * (k + 1) * LANES]).astype(BF16)
        word = g_ref[pl.ds(c * half + k, tm, stride=G_PITCH), :]
        g_lo = lax.bitcast_convert_type(jnp.left_shift(word, 16), F32).astype(BF16)
        g_hi = lax.bitcast_convert_type(jnp.bitwise_and(word, jnp.uint32(HI_MASK)), F32).astype(BF16)
        return jnp.concatenate([_gelu2(a[:, 0:LANES]) * g_lo, _gelu2(a[:, LANES:2 * LANES]) * g_hi], axis=1)

    assert heads_per_step == 2 and half == 4
    h_a, h_b = h0, h0 + 1
    a_first = top_half(h_a, grp, 0)
    z = [experts_slice(0)]
    a_second = top_half(h_a, grp, 1)
    z.append(experts_slice(1))
    b_first = top_half(h_b, grp, 0)
    z.append(experts_slice(2))
    b_second = top_half(h_b, grp, 1)
    z.append(experts_slice(3))
    best_a = select(a_first, a_second)
    out_ref[...] += _dot(jnp.concatenate(z, axis=1), wu_ref[...])
    best_b = select(b_first, b_second)
    store(h_a, best_a)
    store(h_b, best_b)

    @pl.when(c % steps_per_grp == steps_per_grp - 1)
    def _group_done():
        rows = pl.ds(pl.multiple_of(grp * LANES, LANES), LANES)
        finish(rows, ei_ref.at[wr], ej_ref.at[wr], gate_ref.at[wr])


def _peer(sc, h3, wd_t, wu, x2, tm=512, ec=1024):
    T, D = h3.shape
    tm = min(tm, T)
    n_tiles = T // tm
    n_grp = tm // LANES
    n_chunks = wu.shape[0] // ec
    assert n_chunks % n_grp == 0 and PEER_HEADS % (n_chunks // n_grp) == 0
    ta, tb = _stair_tables()
    prev = lambda t, c: (jnp.maximum(t - 1, 0), 0)
    return pl.pallas_call(
        _peer_kernel,
        grid=(n_tiles + 1, n_chunks),
        in_specs=[
            pl.BlockSpec((2 * PEER_HEADS, n_grp, PEER_NKEYS, LANES),
                         lambda t, c: (0, jnp.minimum(t, n_tiles - 1), 0, 0)),
            pl.BlockSpec(ta.shape, lambda t, c: (0, 0)),
            pl.BlockSpec(tb.shape, lambda t, c: (0, 0)),
            pl.BlockSpec((tm, D), prev),
            pl.BlockSpec((D, ec), lambda t, c: (0, c)),
            pl.BlockSpec((ec, D), lambda t, c: (c, 0)),
            pl.BlockSpec((tm, D), prev),
        ],
        out_specs=pl.BlockSpec((tm, D), prev),
        out_shape=jax.ShapeDtypeStruct((T, D), F32),
        scratch_shapes=[
            pltpu.VMEM((tm * G_PITCH, LANES), jnp.uint32),
            pltpu.VMEM((2, tm, LANES), I32),
            pltpu.VMEM((2, tm, LANES), I32),
            pltpu.VMEM((2, tm, LANES), F32),
            pltpu.VMEM((PEER_HEADS * PEER_TOPK, LANES), F32),
            pltpu.VMEM((PEER_HEADS * PEER_TOPK, LANES), F32),
        ],
        compiler_params=_cparams(("arbitrary", "arbitrary"), 60),
    )(sc, ta, tb, h3, wd_t, wu, x2)


def _pad_half(a, g):
    z = jnp.zeros_like(a)
    return jnp.concatenate([a, z] if g == 0 else [z, a], axis=-1)


def kernel(x, mem, mix_norm_w, w_in, conv_w, conv_b, cmp_pe, cmp_w1, cmp_w2, q_norm_w, k_norm_w, w_out,
           xattn_norm_w, mem_norm_w, xq, xk, xv, xo, xq_norm_w, xk_norm_w, ffn_norm_w, peer_wq, peer_keys,
           peer_down, peer_up):
    B, S, D = x.shape
    T = B * S
    l = 0
    G, R, dh = NSA_GROUPS, NSA_REP, NSA_DH

    w = w_in[l]
    o_q = C_CONV
    o_kv = o_q + NSA_HEADS * dh
    o_g = o_kv + C_KV
    wq = w[:, o_q:o_kv].reshape(D, G, R, dh)
    wq_pad = jnp.concatenate([_pad_half(wq[:, g], g).reshape(D, R * LANES) for g in range(G)], axis=1)
    wg = w[:, o_g:].reshape(D, G, R * 3)
    wg_pad = jnp.pad(wg, ((0, 0), (0, 0), (0, LANES - R * 3))).reshape(D, G * LANES)
    wkv = w[:, o_kv:o_g].reshape(D, 6, LANES)
    wkn = wkv[:, jnp.array([0, 1, 2, 4])].reshape(D, C_KN)
    wvt = wkv[:, jnp.array([3, 5])].reshape(D, 2 * LANES).T.astype(BF16)
    w_all = jnp.concatenate([w[:, :o_q], wq_pad, wkn, wg_pad], axis=1).astype(BF16)
    qw = q_norm_w[l] * dh ** -0.5
    qw_pad = jnp.concatenate([jnp.tile(_pad_half(qw, g), R) for g in range(G)]).reshape(1, C_Q)
    kw = jnp.stack([jnp.tile(k_norm_w[l, 1], G), jnp.tile(k_norm_w[l, 2], G)])

    yconv, q, kvc, kn, vt, gates = _in_proj(
        x.reshape(T, D), mix_norm_w[l].reshape(1, D), w_all, wvt, conv_w[l], conv_b[l].reshape(1, CONV_WIDTH),
        qw_pad, kw, S)

    n_rows = S // CMP_STRIDE
    kvg = jnp.array([0, 0, 1, 1])
    pe = jnp.concatenate([cmp_pe[l, kv] for kv in (0, 0, 1, 1)], axis=-1)
    w1 = cmp_w1[l].reshape(2, CMP_BLOCK, dh, dh)[kvg]
    w1 = jnp.einsum('qp,qjdh->jqdph', jnp.eye(4, dtype=F32), w1).reshape(CMP_BLOCK, 4 * dh, 4 * dh).astype(BF16)
    w2k = jnp.zeros((4 * dh, G * LANES), F32)
    w2vt = jnp.zeros((G * LANES, 4 * dh), F32)
    for g in range(G):
        o = g * LANES + g * dh
        w2k = w2k.at[g * dh:(g + 1) * dh, o:o + dh].set(cmp_w2[l, 0])
        w2vt = w2vt.at[o:o + dh, (G + g) * dh:(G + g + 1) * dh].set(cmp_w2[l, 1].T)
    knw = jnp.stack([_pad_half(k_norm_w[l, 0], g) for g in range(G)]).reshape(G, 1, LANES)
    kvcmp = _compress(kvc, pe, w1, w2k.astype(BF16), w2vt.astype(BF16), knw, S)

    n_sel = S // SEL_BLOCK
    cmp_start = jnp.arange(n_rows) * CMP_STRIDE
    sel_start = jnp.arange(n_sel) * SEL_BLOCK
    selmap_t = ((cmp_start[None, :] < sel_start[:, None] + SEL_BLOCK)
                & (cmp_start[None, :] + CMP_BLOCK > sel_start[:, None])
                & (jnp.arange(n_rows)[None, :] < n_rows - 1)).astype(BF16)
    ynsa = _nsa(q.reshape(B, S, C_Q), kn.reshape(B, S, 2 * LANES), vt, kvcmp, gates.reshape(B, S, C_G), selmap_t)

    kmem, vmem = _mem_kv(mem, mem_norm_w[l].reshape(1, D), xk[l].astype(BF16), xv[l].astype(BF16),
                         xk_norm_w[l].reshape(1, X_DH))

    wo = w_out[l]
    won = wo[CONV_WIDTH:].reshape(G, R, dh, D)
    won_pad = jnp.concatenate(
        [jnp.concatenate([won[g], jnp.zeros_like(won[g])] if g == 0 else [jnp.zeros_like(won[g]), won[g]],
                         axis=1).reshape(R * LANES, D) for g in range(G)], axis=0).astype(BF16)
    keys = peer_keys[l].reshape(2 * PEER_HEADS, PEER_NKEYS, PEER_HALF).astype(BF16)
    x2, h3, sc = _post_mix(
        x.reshape(T, D), yconv, ynsa.reshape(T, C_Q), wo[:CONV_WIDTH].astype(BF16), won_pad,
        xattn_norm_w[l].reshape(1, D), xq[l].astype(BF16),
        (xq_norm_w[l] * X_DH ** -0.5).reshape(1, X_DH), kmem, vmem, xo[l].astype(BF16),
        ffn_norm_w[l].reshape(1, D), peer_wq[l].astype(BF16), keys, S)

    out = _peer(sc, h3, peer_down[l].T.astype(BF16), peer_up[l].astype(BF16), x2)
    return out.reshape(B, S, D)
```

```python
import functools

import jax
import jax.numpy as jnp
from jax import lax
from jax.experimental import pallas as pl
from jax.experimental.pallas import tpu as pltpu

F32 = jnp.float32
BF16 = jnp.bfloat16
I32 = jnp.int32

EPS = 1e-6
NEG = -1e30
LANES = 128
SUBLANES = 8

D_MODEL = 1024
CONV_WIDTH = 512
NSA_HEADS = 8
NSA_GROUPS = 2
NSA_REP = NSA_HEADS // NSA_GROUPS
NSA_DH = 64
CMP_BLOCK = 32
CMP_STRIDE = 16
SEL_BLOCK = 64
SEL_TOPN = 16
WINDOW = 512
X_HEADS = 4
X_DH = D_MODEL // X_HEADS
PEER_HEADS = 8
PEER_NKEYS = 128
PEER_HALF = 128
PEER_TOPK = 16
PEER_EXPERTS = PEER_NKEYS * PEER_NKEYS

NT_DIMS = (((1,), (1,)), ((), ()))


def _dot(a, b):
    return jnp.dot(a, b, preferred_element_type=F32)


def _dot_nt(a, b):
    return lax.dot_general(a, b, NT_DIMS, preferred_element_type=F32)


def _gelu2(x):
    return x * (1.0 + lax.erf(x * (2.0 ** -0.5)))


def _gelu(x):
    return 0.5 * _gelu2(x)


def _rms(x, w):
    ms = jnp.mean(x * x, axis=-1, keepdims=True)
    return x * lax.rsqrt(ms + EPS) * w


def _cparams(sem, vmem_mb):
    return pltpu.CompilerParams(dimension_semantics=sem, vmem_limit_bytes=vmem_mb * 1024 * 1024)


C_CONV = 3 * CONV_WIDTH
C_Q = NSA_HEADS * LANES
C_KV = 6 * NSA_GROUPS * NSA_DH
C_KN = 4 * LANES
C_G = NSA_GROUPS * LANES
C_ALL = C_CONV + C_Q + C_KN + C_G


def _in_proj_kernel(tiles_per_seq, x_ref, nw_ref, w_ref, wvt_ref, cw_ref, cb_ref, qw_ref, kw_ref,
                    yconv_ref, q_ref, kvc_ref, kn_ref, vt_ref, g_ref, carry_ref):
    tm = x_ref.shape[0]
    hn = _rms(x_ref[...], nw_ref[...]).astype(BF16)

    p = _dot(hn, w_ref[:, 0:C_CONV])
    b_g = p[:, 0:CONV_WIDTH]
    u = p[:, CONV_WIDTH:2 * CONV_WIDTH] * p[:, 2 * CONV_WIDTH:3 * CONV_WIDTH]

    @pl.when((pl.program_id(0) % tiles_per_seq) == 0)
    def _sequence_start():
        carry_ref[...] = jnp.zeros(carry_ref.shape, F32)

    prev = carry_ref[...]
    p1 = prev[SUBLANES - 1:SUBLANES, :]
    p2 = prev[SUBLANES - 2:SUBLANES - 1, :]
    row = lax.broadcasted_iota(I32, u.shape, 0)
    u1 = jnp.where(row == 0, p1, pltpu.roll(u, 1, axis=0))
    u2 = jnp.where(row == 0, p2, jnp.where(row == 1, p1, pltpu.roll(u, 2, axis=0)))
    carry_ref[...] = u[tm - SUBLANES:tm, :]
    cw = cw_ref[...]
    y = b_g * (cw[0:1, :] * u2 + cw[1:2, :] * u1 + cw[2:3, :] * u + cb_ref[...])
    yconv_ref[...] = y.astype(BF16)

    pq = _dot(hn, w_ref[:, C_CONV:C_CONV + C_Q])
    for h in range(NSA_HEADS):
        blk = pq[:, h * LANES:(h + 1) * LANES]
        ms = jnp.sum(blk * blk, axis=-1, keepdims=True) * (1.0 / NSA_DH)
        q_ref[:, h * LANES:(h + 1) * LANES] = (
            blk * lax.rsqrt(ms + EPS) * qw_ref[:, h * LANES:(h + 1) * LANES]).astype(BF16)

    pkv = _dot(hn, w_ref[:, C_CONV + C_Q:C_CONV + C_Q + C_KN])
    kvc_ref[0] = pkv[:, 0:LANES]
    kvc_ref[1] = pkv[:, LANES:2 * LANES]
    lane = lax.broadcasted_iota(I32, (tm, LANES), 1)
    lo = lane < NSA_DH
    for j in range(2):
        blk = pkv[:, (2 + j) * LANES:(3 + j) * LANES]
        sq = blk * blk
        ms_lo = jnp.sum(jnp.where(lo, sq, 0.0), axis=-1, keepdims=True) * (1.0 / NSA_DH)
        ms_hi = jnp.sum(jnp.where(lo, 0.0, sq), axis=-1, keepdims=True) * (1.0 / NSA_DH)
        scale = jnp.where(lo, lax.rsqrt(ms_lo + EPS), lax.rsqrt(ms_hi + EPS))
        kn_ref[:, j * LANES:(j + 1) * LANES] = (blk * scale * kw_ref[j:j + 1, :]).astype(BF16)

    for j in range(tm // LANES):
        vt_ref[j] = _dot_nt(wvt_ref[...], hn[j * LANES:(j + 1) * LANES, :]).astype(BF16)

    pg = _dot(hn, w_ref[:, C_CONV + C_Q + C_KN:C_ALL])
    g_ref[...] = jax.nn.sigmoid(pg)


def _in_proj(x2d, nw, w_all, wvt, cw, cb, qw, kw, seq, tm=256):
    T = x2d.shape[0]
    const = lambda i: (0, 0)
    tile = lambda i: (i, 0)
    return pl.pallas_call(
        functools.partial(_in_proj_kernel, seq // tm),
        grid=(T // tm,),
        in_specs=[
            pl.BlockSpec((tm, D_MODEL), tile),
            pl.BlockSpec((1, D_MODEL), const),
            pl.BlockSpec((D_MODEL, C_ALL), const),
            pl.BlockSpec((2 * LANES, D_MODEL), const),
            pl.BlockSpec((3, CONV_WIDTH), const),
            pl.BlockSpec((1, CONV_WIDTH), const),
            pl.BlockSpec((1, C_Q), const),
            pl.BlockSpec((2, LANES), const),
        ],
        out_specs=[
            pl.BlockSpec((tm, CONV_WIDTH), tile),
            pl.BlockSpec((tm, C_Q), tile),
            pl.BlockSpec((2, tm, LANES), lambda i: (0, i, 0)),
            pl.BlockSpec((tm, 2 * LANES), tile),
            pl.BlockSpec((tm // LANES, 2 * LANES, LANES), lambda i: (i, 0, 0)),
            pl.BlockSpec((tm, C_G), tile),
        ],
        out_shape=[
            jax.ShapeDtypeStruct((T, CONV_WIDTH), BF16),
            jax.ShapeDtypeStruct((T, C_Q), BF16),
            jax.ShapeDtypeStruct((2, T, LANES), F32),
            jax.ShapeDtypeStruct((T, 2 * LANES), BF16),
            jax.ShapeDtypeStruct((T // LANES, 2 * LANES, LANES), BF16),
            jax.ShapeDtypeStruct((T, C_G), F32),
        ],
        scratch_shapes=[pltpu.VMEM((SUBLANES, CONV_WIDTH), F32)],
        compiler_params=_cparams(("arbitrary",), 48),
    )(x2d, nw, w_all, wvt, cw, cb, qw, kw)


def _compress_kernel(kvc_ref, pe_ref, w1_ref, w2k_ref, w2vt_ref, knw_ref, out_ref):
    n_rows = out_ref.shape[3]
    acc_a = jnp.zeros((n_rows, 2 * LANES), F32)
    acc_b = jnp.zeros((n_rows, 2 * LANES), F32)
    for l in range(CMP_STRIDE):
        rows = pl.ds(l, n_rows, stride=CMP_STRIDE)
        x = jnp.concatenate([kvc_ref[0, rows, :], kvc_ref[1, rows, :]], axis=1)
        acc_a = acc_a + _dot((x + pe_ref[l:l + 1, :]).astype(BF16), w1_ref[l])
        acc_b = acc_b + _dot((x + pe_ref[CMP_STRIDE + l:CMP_STRIDE + l + 1, :]).astype(BF16),
                             w1_ref[CMP_STRIDE + l])
    hid = _gelu(acc_a + pltpu.roll(acc_b, n_rows - 1, axis=0)).astype(BF16)
    row = lax.broadcasted_iota(I32, (n_rows, LANES), 0)
    col = lax.broadcasted_iota(I32, (LANES, n_rows), 1)
    keys = _dot(hid, w2k_ref[...])
    vals = _dot_nt(w2vt_ref[...], hid)
    for g in range(NSA_GROUPS):
        out = keys[:, g * LANES:(g + 1) * LANES]
        ms = jnp.sum(out * out, axis=-1, keepdims=True) * (1.0 / NSA_DH)
        out = out * lax.rsqrt(ms + EPS) * knw_ref[g]
        out_ref[0, 0, g] = jnp.where(row < n_rows - 1, out, 0.0).astype(BF16)
        out_ref[0, 1, g] = jnp.where(col < n_rows - 1, vals[g * LANES:(g + 1) * LANES, :], 0.0).astype(BF16)


def _compress(kvc, pe, w1, w2k, w2vt, knw, S):
    B = kvc.shape[1] // S
    n_rows = S // CMP_STRIDE
    return pl.pallas_call(
        _compress_kernel,
        grid=(B,),
        in_specs=[
            pl.BlockSpec((2, S, LANES), lambda b: (0, b, 0)),
            pl.BlockSpec((CMP_BLOCK, 2 * LANES), lambda b: (0, 0)),
            pl.BlockSpec((CMP_BLOCK, 2 * LANES, 2 * LANES), lambda b: (0, 0, 0)),
            pl.BlockSpec((2 * LANES, 2 * LANES), lambda b: (0, 0)),
            pl.BlockSpec((2 * LANES, 2 * LANES), lambda b: (0, 0)),
            pl.BlockSpec((NSA_GROUPS, 1, LANES), lambda b: (0, 0, 0)),
        ],
        out_specs=pl.BlockSpec((1, 2, NSA_GROUPS, n_rows, LANES), lambda b: (b, 0, 0, 0, 0)),
        out_shape=jax.ShapeDtypeStruct((B, 2, NSA_GROUPS, n_rows, LANES), BF16),
        compiler_params=_cparams(("arbitrary",), 32),
    )(kvc, pe, w1, w2k, w2vt, knw)


SEL_CHUNK = 512


def _nsa_kernel(q_ref, kn_ref, vt_ref, cmp_ref, g_ref, selmap_ref, y_ref, sel_ref, acc_ref):
    tq = q_ref.shape[1]
    cols = NSA_HEADS * tq
    i = pl.program_id(1)
    t0 = i * tq
    qb = q_ref[0]
    qs = jnp.concatenate([qb[:, h * LANES:(h + 1) * LANES] for h in range(NSA_HEADS)], axis=0)
    qpos = t0 + jnp.bitwise_and(lax.broadcasted_iota(I32, (1, cols), 1), tq - 1)
    both = lambda kv: (cmp_ref[0, kv, 0].astype(F32) + cmp_ref[0, kv, 1].astype(F32)).astype(BF16)

    n_cmp = cmp_ref.shape[3]
    s = _dot_nt(both(0), qs)
    cmp_last = lax.broadcasted_iota(I32, (n_cmp, cols), 0) * CMP_STRIDE + (CMP_BLOCK - 1)
    vis_c = cmp_last <= qpos
    s = jnp.where(vis_c, s, NEG)
    m = jnp.max(s, axis=0, keepdims=True)
    m = jnp.where(m > 0.5 * NEG, m, 0.0)
    e = jnp.where(vis_c, jnp.exp(s - m), 0.0)
    d = jnp.sum(e, axis=0, keepdims=True)
    p_c = e / jnp.where(d > 0.0, d, 1.0)
    o_c = _dot(both(1), p_c.astype(BF16))

    n_sel = selmap_ref.shape[0]
    blk = lax.broadcasted_iota(I32, (n_sel, tq), 0)
    cur = jnp.right_shift(t0 + lax.broadcasted_iota(I32, (n_sel, tq), 1), 6)
    forced = (blk == 0) | (blk == cur) | (blk == cur - 1)
    chosen = []
    for g in range(NSA_GROUPS):
        psum = p_c[:, g * NSA_REP * tq:(g * NSA_REP + 1) * tq]
        for r in range(1, NSA_REP):
            psum = psum + p_c[:, (g * NSA_REP + r) * tq:(g * NSA_REP + r + 1) * tq]
        p_hi = psum.astype(BF16)
        p_lo = (psum - p_hi.astype(F32)).astype(BF16)
        imp = _dot(selmap_ref[...], p_hi) + _dot(selmap_ref[...], p_lo)
        imp = jnp.where(blk > cur, -jnp.inf, jnp.where(forced, jnp.inf, imp))
        rank = jnp.zeros((n_sel, tq), I32)
        for jp in range(n_sel):
            other = imp[jp:jp + 1, :]
            beats = (other > imp) | ((other == imp) & (blk > jp))
            rank = rank + beats.astype(I32)
        chosen += [rank < min(SEL_TOPN, n_sel)] * NSA_REP
    sel_ref[...] = jnp.where(jnp.concatenate(chosen, axis=1), qpos, -1)

    acc_ref[...] = jnp.zeros(acc_ref.shape, F32)
    units = SEL_CHUNK // LANES
    blocks = SEL_CHUNK // SEL_BLOCK
    koff = lax.broadcasted_iota(I32, (SEL_BLOCK, cols), 0)

    def sel_chunk(c, carry):
        m_old, l_old = carry
        start = pl.multiple_of(c * SEL_CHUNK, SEL_CHUNK)
        k = kn_ref[0, pl.ds(start, SEL_CHUNK), 0:LANES]
        vt = jnp.concatenate([vt_ref[c * units + u, 0:LANES, :] for u in range(units)], axis=1)
        sc = _dot_nt(k, qs)
        parts = []
        for b in range(blocks):
            limit = sel_ref[pl.ds(c * blocks + b, 1), :] - (start + b * SEL_BLOCK)
            parts.append(jnp.where(koff <= limit, sc[b * SEL_BLOCK:(b + 1) * SEL_BLOCK], NEG))
        sc = jnp.concatenate(parts, axis=0)
        m_new = jnp.maximum(m_old, jnp.max(sc, axis=0, keepdims=True))
        alpha = jnp.exp(m_old - m_new)
        p = jnp.exp(sc - m_new)
        acc_ref[...] = alpha * acc_ref[...] + _dot(vt, p.astype(BF16))
        return m_new, alpha * l_old + jnp.sum(p, axis=0, keepdims=True)

    init = (jnp.full((1, cols), NEG, F32), jnp.zeros((1, cols), F32))
    _, l_s = lax.fori_loop(0, (t0 + tq + SEL_CHUNK - 1) // SEL_CHUNK, sel_chunk, init)
    o_s = acc_ref[...]

    n_win = WINDOW + tq
    w_start = pl.multiple_of(jnp.maximum(t0 - WINDOW, 0), LANES)
    w_unit = w_start // LANES
    k_w = kn_ref[0, pl.ds(w_start, n_win), LANES:2 * LANES]
    vt_w = jnp.concatenate([vt_ref[w_unit + u, LANES:2 * LANES, :] for u in range(n_win // LANES)], axis=1)
    diff = (qpos - w_start) - lax.broadcasted_iota(I32, (n_win, cols), 0)
    in_window = lax.bitcast_convert_type(diff, jnp.uint32) < jnp.uint32(WINDOW)
    s_w = jnp.where(in_window, _dot_nt(k_w, qs), NEG)
    p_w = jnp.exp(s_w - jnp.max(s_w, axis=0, keepdims=True))
    l_w = jnp.sum(p_w, axis=0, keepdims=True)
    o_w = _dot(vt_w, p_w.astype(BF16))

    g_t = [g_ref[0, :, g * LANES:(g + 1) * LANES].T for g in range(NSA_GROUPS)]
    gate = lambda br: jnp.concatenate(
        [g_t[h // NSA_REP][3 * (h % NSA_REP) + br:3 * (h % NSA_REP) + br + 1, :] for h in range(NSA_HEADS)], axis=1)
    y_t = o_c * gate(0) + o_s * (gate(1) / l_s) + o_w * (gate(2) / l_w)
    for h in range(NSA_HEADS):
        y_ref[0, :, h * LANES:(h + 1) * LANES] = y_t[:, h * tq:(h + 1) * tq].T.astype(BF16)


def _nsa(q, kn, vt, kvcmp, gates, selmap, tq=128):
    B, S = q.shape[0], q.shape[1]
    n_sel = S // SEL_BLOCK
    cols = NSA_HEADS * tq
    units = S // LANES
    n_cmp = kvcmp.shape[3]
    return pl.pallas_call(
        _nsa_kernel,
        grid=(B, S // tq),
        in_specs=[
            pl.BlockSpec((1, tq, C_Q), lambda b, i: (b, i, 0)),
            pl.BlockSpec((1, S, 2 * LANES), lambda b, i: (b, 0, 0)),
            pl.BlockSpec((units, 2 * LANES, LANES), lambda b, i: (b, 0, 0)),
            pl.BlockSpec((1, 2, NSA_GROUPS, n_cmp, LANES), lambda b, i: (b, 0, 0, 0, 0)),
            pl.BlockSpec((1, tq, C_G), lambda b, i: (b, i, 0)),
            pl.BlockSpec((n_sel, n_cmp), lambda b, i: (0, 0)),
        ],
        out_specs=pl.BlockSpec((1, tq, C_Q), lambda b, i: (b, i, 0)),
        out_shape=jax.ShapeDtypeStruct((B, S, C_Q), BF16),
        scratch_shapes=[
            pltpu.VMEM((n_sel, cols), I32),
            pltpu.VMEM((LANES, cols), F32),
        ],
        compiler_params=_cparams(("arbitrary", "arbitrary"), 56),
    )(q, kn, vt, kvcmp, gates, selmap)


def _mem_kv_kernel(mem_ref, nw_ref, wk_ref, wv_ref, kw_ref, k_ref, v_ref):
    mn = _rms(mem_ref[0], nw_ref[...]).astype(BF16)
    k = _dot(mn, wk_ref[...])
    for h in range(X_HEADS):
        sl = slice(h * X_DH, (h + 1) * X_DH)
        k_ref[0, :, sl] = _rms(k[:, sl], kw_ref[...]).astype(BF16)
    v_ref[0] = _dot(mn, wv_ref[...]).astype(BF16)


def _mem_kv(mem, nw, wk, wv, kw):
    B, M, D = mem.shape
    blk = pl.BlockSpec((1, M, D), lambda b: (b, 0, 0))
    const = lambda b: (0, 0)
    return pl.pallas_call(
        _mem_kv_kernel,
        grid=(B,),
        in_specs=[blk, pl.BlockSpec((1, D), const), pl.BlockSpec((D, D), const),
                  pl.BlockSpec((D, D), const), pl.BlockSpec((1, X_DH), const)],
        out_specs=[blk, blk],
        out_shape=[jax.ShapeDtypeStruct((B, M, D), BF16)] * 2,
        compiler_params=_cparams(("arbitrary",), 32),
    )(mem, nw, wk, wv, kw)


def _post_mix_kernel(x_ref, yc_ref, yn_ref, woc_ref, won_ref, xnw_ref, xq_ref, xqw_ref, km_ref, vm_ref,
                     xo_ref, fnw_ref, pwq_ref, keys_ref, x2_ref, h3_ref, sc_ref):
    tm = x_ref.shape[0]
    n_streams = 2
    rows_per = tm // n_streams

    def stream(i):
        rs = slice(i * rows_per, (i + 1) * rows_per)
        x1 = x_ref[rs, :] + _dot(yc_ref[rs, :], woc_ref[...]) + _dot(yn_ref[rs, :], won_ref[...])
        yield
        h2 = _rms(x1, xnw_ref[...]).astype(BF16)
        qx = _dot(h2, xq_ref[...])
        yield
        heads = []
        for h in range(X_HEADS):
            sl = slice(h * X_DH, (h + 1) * X_DH)
            qh = _rms(qx[:, sl], xqw_ref[...]).astype(BF16)
            s = _dot_nt(qh, km_ref[0, :, sl])
            e = jnp.exp(s - jnp.max(s, axis=-1, keepdims=True))
            p = e / jnp.sum(e, axis=-1, keepdims=True)
            heads.append(_dot(p.astype(BF16), vm_ref[0, :, sl]))
        yield
        o = jnp.concatenate(heads, axis=1).astype(BF16)
        x2 = x1 + _dot(o, xo_ref[...])
        x2_ref[rs, :] = x2
        yield
        h3 = _rms(x2, fnw_ref[...]).astype(BF16)
        h3_ref[rs, :] = h3
        qp = _dot(h3, pwq_ref[...]).astype(BF16)
        for c in range(2 * PEER_HEADS):
            sl = slice(c * PEER_HALF, (c + 1) * PEER_HALF)
            for j in range(rows_per // LANES):
                sc_ref[c, i * (rows_per // LANES) + j] = _dot_nt(keys_ref[c], qp[j * LANES:(j + 1) * LANES, sl])
        yield

    first, second = stream(0), stream(1)
    next(first)
    for _ in range(4):
        next(first)
        next(second)
    next(second)


def _post_mix(x2d, yconv, ynsa, woc, won, xnw, xq, xqw, kmem, vmem, xo, fnw, pwq, keys, seq, tm=512):
    T, D = x2d.shape
    M = kmem.shape[1]
    n_sc = 2 * PEER_HEADS * PEER_NKEYS
    tile = lambda i: (i, 0)
    const = lambda i: (0, 0)
    per_batch = lambda i: ((i * tm) // seq, 0, 0)
    return pl.pallas_call(
        _post_mix_kernel,
        grid=(T // tm,),
        in_specs=[
            pl.BlockSpec((tm, D), tile),
            pl.BlockSpec((tm, CONV_WIDTH), tile),
            pl.BlockSpec((tm, C_Q), tile),
            pl.BlockSpec((CONV_WIDTH, D), const),
            pl.BlockSpec((C_Q, D), const),
            pl.BlockSpec((1, D), const),
            pl.BlockSpec((D, D), const),
            pl.BlockSpec((1, X_DH), const),
            pl.BlockSpec((1, M, D), per_batch),
            pl.BlockSpec((1, M, D), per_batch),
            pl.BlockSpec((D, D), const),
            pl.BlockSpec((1, D), const),
            pl.BlockSpec((D, n_sc), const),
            pl.BlockSpec((2 * PEER_HEADS, PEER_NKEYS, PEER_HALF), lambda i: (0, 0, 0)),
        ],
        out_specs=[pl.BlockSpec((tm, D), tile), pl.BlockSpec((tm, D), tile),
                   pl.BlockSpec((2 * PEER_HEADS, tm // LANES, PEER_NKEYS, LANES), lambda i: (0, i, 0, 0))],
        out_shape=[
            jax.ShapeDtypeStruct((T, D), F32),
            jax.ShapeDtypeStruct((T, D), BF16),
            jax.ShapeDtypeStruct((2 * PEER_HEADS, T // LANES, PEER_NKEYS, LANES), F32),
        ],
        compiler_params=_cparams(("arbitrary",), 56),
    )(x2d, yconv, ynsa, woc, won, xnw, xq, xqw, kmem, vmem, xo, fnw, pwq, keys)


STAIR = [(a, b) for a in range(PEER_TOPK) for b in range(PEER_TOPK // (a + 1))]
N_STAIR_VREGS = -(-len(STAIR) // SUBLANES)


def _stair_vregs(axis, k):
    return sorted({r // SUBLANES for r, ab in enumerate(STAIR) if ab[axis] == k})


def _stair_tables():
    rows = N_STAIR_VREGS * SUBLANES
    pad = rows - len(STAIR)
    ta = jnp.asarray([float(a) for a, _ in STAIR] + [-1.0] * pad, F32)
    tb = jnp.asarray([float(b) for _, b in STAIR] + [-1.0] * pad, F32)
    return jnp.broadcast_to(ta[:, None], (rows, LANES)), jnp.broadcast_to(tb[:, None], (rows, LANES))


def _make_topk(sc_ref, ta_ref, tb_ref, s_out, e_out):
    K = PEER_TOPK
    NV = N_STAIR_VREGS
    kidx = lax.broadcasted_iota(I32, (PEER_NKEYS, LANES), 0).astype(F32)
    ridx = [(lax.broadcasted_iota(I32, (SUBLANES, LANES), 0) + SUBLANES * j).astype(F32) for j in range(NV)]
    ta = [ta_ref[SUBLANES * j:SUBLANES * (j + 1), :] for j in range(NV)]
    tb = [tb_ref[SUBLANES * j:SUBLANES * (j + 1), :] for j in range(NV)]
    zero = jnp.zeros((SUBLANES, LANES), F32)

    def top1(s):
        m = jnp.max(s, axis=0, keepdims=True)
        idx = jnp.min(jnp.where(s == m, kidx, float(PEER_NKEYS)), axis=0, keepdims=True)
        return m, idx, jnp.where(kidx == idx, -jnp.inf, s)

    def sorted_top(s, table, axis):
        val = [zero] * NV
        key = [zero] * NV
        for k in range(K):
            m, idx, s = top1(s)
            for j in _stair_vregs(axis, k):
                hit = table[j] == float(k)
                val[j] = jnp.where(hit, m, val[j])
                key[j] = jnp.where(hit, idx, key[j])
        return val, key

    def top_half(h, g, which):
        return sorted_top(sc_ref[2 * h + which, g], (ta, tb)[which], which)

    def select(first, second):
        (v1, k1), (v2, k2) = first, second
        cand = [jnp.where(ta[j] >= 0.0, v1[j] + v2[j], -jnp.inf) for j in range(NV)]
        ce = [k1[j] * float(PEER_NKEYS) + k2[j] for j in range(NV)]
        sub = lax.broadcasted_iota(I32, (SUBLANES, LANES), 0)
        best_s = [zero] * (K // SUBLANES)
        best_e = [zero] * (K // SUBLANES)
        for k in range(K):
            mx = cand[0]
            for j in range(1, NV):
                mx = jnp.maximum(mx, cand[j])
            m = jnp.max(mx, axis=0, keepdims=True)
            ix = jnp.where(cand[0] == m, ridx[0], float(NV * SUBLANES))
            for j in range(1, NV):
                ix = jnp.minimum(ix, jnp.where(cand[j] == m, ridx[j], float(NV * SUBLANES)))
            idx = jnp.min(ix, axis=0, keepdims=True)
            hits = [ridx[j] == idx for j in range(NV)]
            es = jnp.where(hits[0], ce[0], 0.0)
            for j in range(1, NV):
                es = es + jnp.where(hits[j], ce[j], 0.0)
            cand = [jnp.where(hits[j], -jnp.inf, cand[j]) for j in range(NV)]
            here = sub == (k % SUBLANES)
            best_s[k // SUBLANES] = jnp.where(here, m, best_s[k // SUBLANES])
            best_e[k // SUBLANES] = jnp.where(here, jnp.sum(es, axis=0, keepdims=True), best_e[k // SUBLANES])
        return jnp.concatenate(best_s, axis=0), jnp.concatenate(best_e, axis=0)

    def store(h, best):
        rows = pl.ds(pl.multiple_of(h * K, K), K)
        s_out[rows, :] = best[0]
        e_out[rows, :] = best[1]

    def finish(rows, ei_dst, ej_dst, gate_dst):
        s_all = s_out[...]
        gates = []
        for h in range(PEER_HEADS):
            sh = s_all[h * K:(h + 1) * K]
            ex = jnp.exp(sh - sh[0:1])
            gates.append(ex / jnp.sum(ex, axis=0, keepdims=True))
        gate_dst[rows, :] = jnp.concatenate(gates, axis=0).T
        e_int = e_out[...].T.astype(I32)
        ei_dst[rows, :] = jnp.right_shift(e_int, 7)
        ej_dst[rows, :] = jnp.bitwise_and(e_int, PEER_NKEYS - 1)

    return top_half, select, store, finish


G_ROWS = PEER_NKEYS // 2
G_PITCH = G_ROWS + SUBLANES
HI_MASK = 0xFFFF0000


def _peer_kernel(sc_ref, ta_ref, tb_ref, h_ref, wd_ref, wu_ref, x2_ref, out_ref,
                 g_ref, ei_ref, ej_ref, gate_ref, s_out, e_out):
    tm = h_ref.shape[0]
    ec = wd_ref.shape[1]
    ipc = ec // LANES
    half = ipc // 2
    half_bits = half.bit_length() - 1
    t = pl.program_id(0)
    c = pl.program_id(1)
    n_grp = tm // LANES
    steps_per_grp = PEER_EXPERTS // ec // n_grp
    heads_per_step = PEER_HEADS // steps_per_grp
    rd = (t + 1) % 2
    wr = t % 2
    ei_rd, ej_rd, gate_rd = ei_ref.at[rd], ej_ref.at[rd], gate_ref.at[rd]

    @pl.when((t == 0) & (c == 0))
    def _first_step():
        ei_ref[...] = jnp.zeros(ei_ref.shape, I32)
        ej_ref[...] = jnp.zeros(ej_ref.shape, I32)
        gate_ref[...] = jnp.zeros(gate_ref.shape, F32)

    @pl.when(c == 0)
    def _scatter():
        out_ref[...] = x2_ref[...]
        m = lax.broadcasted_iota(I32, (PEER_NKEYS, 2 * LANES), 0)
        r = jnp.bitwise_and(m, G_ROWS - 1)
        i_of_m = (jnp.right_shift(r, half_bits) * ipc + 2 * jnp.bitwise_and(r, half - 1)
                  + jnp.right_shift(m, 6))
        as_bf16 = lambda v: v.astype(F32).astype(BF16)
        i_of_m = as_bf16(i_of_m)
        n = as_bf16(lax.broadcasted_iota(I32, (LANES, LANES), 0))
        one = jnp.ones((LANES, LANES), BF16)
        zero = jnp.zeros((LANES, LANES), BF16)

        def pair(u, carry):
            ta_, tb_ = pl.ds(2 * u, 1), pl.ds(2 * u + 1, 1)
            ri = as_bf16(jnp.concatenate([ei_rd[ta_, :], ei_rd[tb_, :]], axis=1))
            rg = (0.5 * jnp.concatenate([gate_rd[ta_, :], gate_rd[tb_, :]], axis=1)).astype(BF16)
            c_t = jnp.where(i_of_m == ri, rg, jnp.zeros_like(rg))
            q_a = jnp.where(n == as_bf16(ej_rd[ta_, :]), one, zero)
            q_b = jnp.where(n == as_bf16(ej_rd[tb_, :]), one, zero)
            q_t = jnp.concatenate([jnp.concatenate([q_a, zero], axis=1),
                                   jnp.concatenate([zero, q_b], axis=1)], axis=0)
            g2 = _dot_nt(c_t, q_t)
            for tok in range(2):
                g = g2[:, tok * LANES:(tok + 1) * LANES].astype(BF16).astype(F32)
                bits = lax.bitcast_convert_type(g, jnp.uint32)
                word = jnp.bitwise_or(jnp.right_shift(bits[0:G_ROWS], 16),
                                      jnp.bitwise_and(bits[G_ROWS:2 * G_ROWS], jnp.uint32(HI_MASK)))
                g_ref[pl.ds(pl.multiple_of((2 * u + tok) * G_PITCH, SUBLANES), G_ROWS), :] = word
            return carry

        lax.fori_loop(0, tm // 2, pair, 0, unroll=16)

    top_half, select, store, finish = _make_topk(sc_ref, ta_ref, tb_ref, s_out, e_out)
    grp = c // steps_per_grp
    h0 = (c % steps_per_grp) * heads_per_step

    def experts_slice(k):
        a = _dot(h_ref[...], wd_ref[:, 2 * k * LANES:2 * (k + 1) * LANES]).astype(BF16)
        word = g_ref[pl.ds(c * half + k, tm, stride=G_PITCH), :]
        g_lo = lax.bitcast_convert_type(jnp.left_shift(word, 16), F32).astype(BF16)
        g_hi = lax.bitcast_convert_type(jnp.bitwise_and(word, jnp.uint32(HI_MASK)), F32).astype(BF16)
        return jnp.concatenate([_gelu2(a[:, 0:LANES]) * g_lo, _gelu2(a[:, LANES:2 * LANES]) * g_hi], axis=1)

    assert half == 2 * heads_per_step
    halves, z = [], []
    for k in range(half):
        halves.append(top_half(h0 + k // 2, grp, k % 2))
        z.append(experts_slice(k))
    best = [select(halves[2 * hh], halves[2 * hh + 1]) for hh in range(heads_per_step - 1)]
    out_ref[...] += _dot(jnp.concatenate(z, axis=1), wu_ref[...])
    best.append(select(halves[-2], halves[-1]))
    for hh in range(heads_per_step):
        store(h0 + hh, best[hh])

    @pl.when(c % steps_per_grp == steps_per_grp - 1)
    def _group_done():
        rows = pl.ds(pl.multiple_of(grp * LANES, LANES), LANES)
        finish(rows, ei_ref.at[wr], ej_ref.at[wr], gate_ref.at[wr])


def _peer(sc, h3, wd_t, wu, x2, tm=512, ec=2048):
    T, D = h3.shape
    tm = min(tm, T)
    n_tiles = T // tm
    n_grp = tm // LANES
    n_chunks = wu.shape[0] // ec
    assert n_chunks % n_grp == 0 and PEER_HEADS % (n_chunks // n_grp) == 0
    ta, tb = _stair_tables()
    prev = lambda t, c: (jnp.maximum(t - 1, 0), 0)
    return pl.pallas_call(
        _peer_kernel,
        grid=(n_tiles + 1, n_chunks),
        in_specs=[
            pl.BlockSpec((2 * PEER_HEADS, n_grp, PEER_NKEYS, LANES),
                         lambda t, c: (0, jnp.minimum(t, n_tiles - 1), 0, 0)),
            pl.BlockSpec(ta.shape, lambda t, c: (0, 0)),
            pl.BlockSpec(tb.shape, lambda t, c: (0, 0)),
            pl.BlockSpec((tm, D), prev),
            pl.BlockSpec((D, ec), lambda t, c: (0, c)),
            pl.BlockSpec((ec, D), lambda t, c: (c, 0)),
            pl.BlockSpec((tm, D), prev),
        ],
        out_specs=pl.BlockSpec((tm, D), prev),
        out_shape=jax.ShapeDtypeStruct((T, D), F32),
        scratch_shapes=[
            pltpu.VMEM((tm * G_PITCH, LANES), jnp.uint32),
            pltpu.VMEM((2, tm, LANES), I32),
            pltpu.VMEM((2, tm, LANES), I32),
            pltpu.VMEM((2, tm, LANES), F32),
            pltpu.VMEM((PEER_HEADS * PEER_TOPK, LANES), F32),
            pltpu.VMEM((PEER_HEADS * PEER_TOPK, LANES), F32),
        ],
        compiler_params=_cparams(("arbitrary", "arbitrary"), 60),
    )(sc, ta, tb, h3, wd_t, wu, x2)


def _pad_half(a, g):
    z = jnp.zeros_like(a)
    return jnp.concatenate([a, z] if g == 0 else [z, a], axis=-1)


def kernel(x, mem, mix_norm_w, w_in, conv_w, conv_b, cmp_pe, cmp_w1, cmp_w2, q_norm_w, k_norm_w, w_out,
           xattn_norm_w, mem_norm_w, xq, xk, xv, xo, xq_norm_w, xk_norm_w, ffn_norm_w, peer_wq, peer_keys,
           peer_down, peer_up):
    B, S, D = x.shape
    T = B * S
    l = 0
    G, R, dh = NSA_GROUPS, NSA_REP, NSA_DH

    w = w_in[l]
    o_q = C_CONV
    o_kv = o_q + NSA_HEADS * dh
    o_g = o_kv + C_KV
    wq = w[:, o_q:o_kv].reshape(D, G, R, dh)
    wq_pad = jnp.concatenate([_pad_half(wq[:, g], g).reshape(D, R * LANES) for g in range(G)], axis=1)
    wg = w[:, o_g:].reshape(D, G, R * 3)
    wg_pad = jnp.pad(wg, ((0, 0), (0, 0), (0, LANES - R * 3))).reshape(D, G * LANES)
    wkv = w[:, o_kv:o_g].reshape(D, 6, LANES)
    wkn = wkv[:, jnp.array([0, 1, 2, 4])].reshape(D, C_KN)
    wvt = wkv[:, jnp.array([3, 5])].reshape(D, 2 * LANES).T.astype(BF16)
    w_all = jnp.concatenate([w[:, :o_q], wq_pad, wkn, wg_pad], axis=1).astype(BF16)
    qw = q_norm_w[l] * dh ** -0.5
    qw_pad = jnp.concatenate([jnp.tile(_pad_half(qw, g), R) for g in range(G)]).reshape(1, C_Q)
    kw = jnp.stack([jnp.tile(k_norm_w[l, 1], G), jnp.tile(k_norm_w[l, 2], G)])

    yconv, q, kvc, kn, vt, gates = _in_proj(
        x.reshape(T, D), mix_norm_w[l].reshape(1, D), w_all, wvt, conv_w[l], conv_b[l].reshape(1, CONV_WIDTH),
        qw_pad, kw, S)

    n_rows = S // CMP_STRIDE
    kvg = jnp.array([0, 0, 1, 1])
    pe = jnp.concatenate([cmp_pe[l, kv] for kv in (0, 0, 1, 1)], axis=-1)
    w1 = cmp_w1[l].reshape(2, CMP_BLOCK, dh, dh)[kvg]
    w1 = jnp.einsum('qp,qjdh->jqdph', jnp.eye(4, dtype=F32), w1).reshape(CMP_BLOCK, 4 * dh, 4 * dh).astype(BF16)
    w2k = jnp.zeros((4 * dh, G * LANES), F32)
    w2vt = jnp.zeros((G * LANES, 4 * dh), F32)
    for g in range(G):
        o = g * LANES + g * dh
        w2k = w2k.at[g * dh:(g + 1) * dh, o:o + dh].set(cmp_w2[l, 0])
        w2vt = w2vt.at[o:o + dh, (G + g) * dh:(G + g + 1) * dh].set(cmp_w2[l, 1].T)
    knw = jnp.stack([_pad_half(k_norm_w[l, 0], g) for g in range(G)]).reshape(G, 1, LANES)
    kvcmp = _compress(kvc, pe, w1, w2k.astype(BF16), w2vt.astype(BF16), knw, S)

    n_sel = S // SEL_BLOCK
    cmp_start = jnp.arange(n_rows) * CMP_STRIDE
    sel_start = jnp.arange(n_sel) * SEL_BLOCK
    selmap_t = ((cmp_start[None, :] < sel_start[:, None] + SEL_BLOCK)
                & (cmp_start[None, :] + CMP_BLOCK > sel_start[:, None])
                & (jnp.arange(n_rows)[None, :] < n_rows - 1)).astype(BF16)
    ynsa = _nsa(q.reshape(B, S, C_Q), kn.reshape(B, S, 2 * LANES), vt, kvcmp, gates.reshape(B, S, C_G), selmap_t)

    kmem, vmem = _mem_kv(mem, mem_norm_w[l].reshape(1, D), xk[l].astype(BF16), xv[l].astype(BF16),
                         xk_norm_w[l].reshape(1, X_DH))

    wo = w_out[l]
    won = wo[CONV_WIDTH:].reshape(G, R, dh, D)
    won_pad = jnp.concatenate(
        [jnp.concatenate([won[g], jnp.zeros_like(won[g])] if g == 0 else [jnp.zeros_like(won[g]), won[g]],
                         axis=1).reshape(R * LANES, D) for g in range(G)], axis=0).astype(BF16)
    keys = peer_keys[l].reshape(2 * PEER_HEADS, PEER_NKEYS, PEER_HALF).astype(BF16)
    x2, h3, sc = _post_mix(
        x.reshape(T, D), yconv, ynsa.reshape(T, C_Q), wo[:CONV_WIDTH].astype(BF16), won_pad,
        xattn_norm_w[l].reshape(1, D), xq[l].astype(BF16),
        (xq_norm_w[l] * X_DH ** -0.5).reshape(1, X_DH), kmem, vmem, xo[l].astype(BF16),
        ffn_norm_w[l].reshape(1, D), peer_wq[l].astype(BF16), keys, S)

    out = _peer(sc, h3, peer_down[l].T.astype(BF16), peer_up[l].astype(BF16), x2)
    return out.reshape(B, S, D)
```

```python
import functools

import jax
import jax.numpy as jnp
from jax import lax
from jax.experimental import pallas as pl
from jax.experimental.pallas import tpu as pltpu

F32 = jnp.float32
BF16 = jnp.bfloat16
I32 = jnp.int32

EPS = 1e-6
NEG = -1e30
LANES = 128
SUBLANES = 8

D_MODEL = 1024
CONV_WIDTH = 512
NSA_HEADS = 8
NSA_GROUPS = 2
NSA_REP = NSA_HEADS // NSA_GROUPS
NSA_DH = 64
CMP_BLOCK = 32
CMP_STRIDE = 16
SEL_BLOCK = 64
SEL_TOPN = 16
WINDOW = 512
X_HEADS = 4
X_DH = D_MODEL // X_HEADS
PEER_HEADS = 8
PEER_NKEYS = 128
PEER_HALF = 128
PEER_TOPK = 16
PEER_EXPERTS = PEER_NKEYS * PEER_NKEYS

NT_DIMS = (((1,), (1,)), ((), ()))


def _dot(a, b):
    return jnp.dot(a, b, preferred_element_type=F32)


def _dot_nt(a, b):
    return lax.dot_general(a, b, NT_DIMS, preferred_element_type=F32)


def _gelu2(x):
    return x * (1.0 + lax.erf(x * (2.0 ** -0.5)))


def _gelu(x):
    return 0.5 * _gelu2(x)


def _rms(x, w):
    ms = jnp.mean(x * x, axis=-1, keepdims=True)
    return x * lax.rsqrt(ms + EPS) * w


def _cparams(sem, vmem_mb):
    return pltpu.CompilerParams(dimension_semantics=sem, vmem_limit_bytes=vmem_mb * 1024 * 1024)


C_CONV = 3 * CONV_WIDTH
C_Q = NSA_HEADS * LANES
C_KV = 6 * NSA_GROUPS * NSA_DH
C_KN = 4 * LANES
C_G = NSA_GROUPS * LANES
C_ALL = C_CONV + C_Q + C_KN + C_G


def _in_proj_kernel(tiles_per_seq, x_ref, nw_ref, w_ref, wvt_ref, cw_ref, cb_ref, qw_ref, kw_ref,
                    yconv_ref, q_ref, kvc_ref, kn_ref, vt_ref, g_ref, carry_ref):
    tm = x_ref.shape[0]
    hn = _rms(x_ref[...], nw_ref[...]).astype(BF16)

    p = _dot(hn, w_ref[:, 0:C_CONV])
    b_g = p[:, 0:CONV_WIDTH]
    u = p[:, CONV_WIDTH:2 * CONV_WIDTH] * p[:, 2 * CONV_WIDTH:3 * CONV_WIDTH]

    @pl.when((pl.program_id(0) % tiles_per_seq) == 0)
    def _sequence_start():
        carry_ref[...] = jnp.zeros(carry_ref.shape, F32)

    prev = carry_ref[...]
    p1 = prev[SUBLANES - 1:SUBLANES, :]
    p2 = prev[SUBLANES - 2:SUBLANES - 1, :]
    row = lax.broadcasted_iota(I32, u.shape, 0)
    u1 = jnp.where(row == 0, p1, pltpu.roll(u, 1, axis=0))
    u2 = jnp.where(row == 0, p2, jnp.where(row == 1, p1, pltpu.roll(u, 2, axis=0)))
    carry_ref[...] = u[tm - SUBLANES:tm, :]
    cw = cw_ref[...]
    y = b_g * (cw[0:1, :] * u2 + cw[1:2, :] * u1 + cw[2:3, :] * u + cb_ref[...])
    yconv_ref[...] = y.astype(BF16)

    pq = _dot(hn, w_ref[:, C_CONV:C_CONV + C_Q])
    for h in range(NSA_HEADS):
        blk = pq[:, h * LANES:(h + 1) * LANES]
        ms = jnp.sum(blk * blk, axis=-1, keepdims=True) * (1.0 / NSA_DH)
        q_ref[:, h * LANES:(h + 1) * LANES] = (
            blk * lax.rsqrt(ms + EPS) * qw_ref[:, h * LANES:(h + 1) * LANES]).astype(BF16)

    pkv = _dot(hn, w_ref[:, C_CONV + C_Q:C_CONV + C_Q + C_KN])
    kvc_ref[0] = pkv[:, 0:LANES]
    kvc_ref[1] = pkv[:, LANES:2 * LANES]
    lane = lax.broadcasted_iota(I32, (tm, LANES), 1)
    lo = lane < NSA_DH
    for j in range(2):
        blk = pkv[:, (2 + j) * LANES:(3 + j) * LANES]
        sq = blk * blk
        ms_lo = jnp.sum(jnp.where(lo, sq, 0.0), axis=-1, keepdims=True) * (1.0 / NSA_DH)
        ms_hi = jnp.sum(jnp.where(lo, 0.0, sq), axis=-1, keepdims=True) * (1.0 / NSA_DH)
        scale = jnp.where(lo, lax.rsqrt(ms_lo + EPS), lax.rsqrt(ms_hi + EPS))
        kn_ref[:, j * LANES:(j + 1) * LANES] = (blk * scale * kw_ref[j:j + 1, :]).astype(BF16)

    for j in range(tm // LANES):
        vt_ref[j] = _dot_nt(wvt_ref[...], hn[j * LANES:(j + 1) * LANES, :]).astype(BF16)

    pg = _dot(hn, w_ref[:, C_CONV + C_Q + C_KN:C_ALL])
    g_ref[...] = jax.nn.sigmoid(pg)


def _in_proj(x2d, nw, w_all, wvt, cw, cb, qw, kw, seq, tm=256):
    T = x2d.shape[0]
    const = lambda i: (0, 0)
    tile = lambda i: (i, 0)
    return pl.pallas_call(
        functools.partial(_in_proj_kernel, seq // tm),
        grid=(T // tm,),
        in_specs=[
            pl.BlockSpec((tm, D_MODEL), tile),
            pl.BlockSpec((1, D_MODEL), const),
            pl.BlockSpec((D_MODEL, C_ALL), const),
            pl.BlockSpec((2 * LANES, D_MODEL), const),
            pl.BlockSpec((3, CONV_WIDTH), const),
            pl.BlockSpec((1, CONV_WIDTH), const),
            pl.BlockSpec((1, C_Q), const),
            pl.BlockSpec((2, LANES), const),
        ],
        out_specs=[
            pl.BlockSpec((tm, CONV_WIDTH), tile),
            pl.BlockSpec((tm, C_Q), tile),
            pl.BlockSpec((2, tm, LANES), lambda i: (0, i, 0)),
            pl.BlockSpec((tm, 2 * LANES), tile),
            pl.BlockSpec((tm // LANES, 2 * LANES, LANES), lambda i: (i, 0, 0)),
            pl.BlockSpec((tm, C_G), tile),
        ],
        out_shape=[
            jax.ShapeDtypeStruct((T, CONV_WIDTH), BF16),
            jax.ShapeDtypeStruct((T, C_Q), BF16),
            jax.ShapeDtypeStruct((2, T, LANES), F32),
            jax.ShapeDtypeStruct((T, 2 * LANES), BF16),
            jax.ShapeDtypeStruct((T // LANES, 2 * LANES, LANES), BF16),
            jax.ShapeDtypeStruct((T, C_G), F32),
        ],
        scratch_shapes=[pltpu.VMEM((SUBLANES, CONV_WIDTH), F32)],
        compiler_params=_cparams(("arbitrary",), 48),
    )(x2d, nw, w_all, wvt, cw, cb, qw, kw)


def _compress_kernel(kvc_ref, pe_ref, w1_ref, w2k_ref, w2vt_ref, knw_ref, out_ref):
    n_rows = out_ref.shape[3]
    acc_a = jnp.zeros((n_rows, 2 * LANES), F32)
    acc_b = jnp.zeros((n_rows, 2 * LANES), F32)
    for l in range(CMP_STRIDE):
        rows = pl.ds(l, n_rows, stride=CMP_STRIDE)
        x = jnp.concatenate([kvc_ref[0, rows, :], kvc_ref[1, rows, :]], axis=1)
        acc_a = acc_a + _dot((x + pe_ref[l:l + 1, :]).astype(BF16), w1_ref[l])
        acc_b = acc_b + _dot((x + pe_ref[CMP_STRIDE + l:CMP_STRIDE + l + 1, :]).astype(BF16),
                             w1_ref[CMP_STRIDE + l])
    hid = _gelu(acc_a + pltpu.roll(acc_b, n_rows - 1, axis=0)).astype(BF16)
    row = lax.broadcasted_iota(I32, (n_rows, LANES), 0)
    col = lax.broadcasted_iota(I32, (LANES, n_rows), 1)
    keys = _dot(hid, w2k_ref[...])
    vals = _dot_nt(w2vt_ref[...], hid)
    for g in range(NSA_GROUPS):
        out = keys[:, g * LANES:(g + 1) * LANES]
        ms = jnp.sum(out * out, axis=-1, keepdims=True) * (1.0 / NSA_DH)
        out = out * lax.rsqrt(ms + EPS) * knw_ref[g]
        out_ref[0, 0, g] = jnp.where(row < n_rows - 1, out, 0.0).astype(BF16)
        out_ref[0, 1, g] = jnp.where(col < n_rows - 1, vals[g * LANES:(g + 1) * LANES, :], 0.0).astype(BF16)


def _compress(kvc, pe, w1, w2k, w2vt, knw, S):
    B = kvc.shape[1] // S
    n_rows = S // CMP_STRIDE
    return pl.pallas_call(
        _compress_kernel,
        grid=(B,),
        in_specs=[
            pl.BlockSpec((2, S, LANES), lambda b: (0, b, 0)),
            pl.BlockSpec((CMP_BLOCK, 2 * LANES), lambda b: (0, 0)),
            pl.BlockSpec((CMP_BLOCK, 2 * LANES, 2 * LANES), lambda b: (0, 0, 0)),
            pl.BlockSpec((2 * LANES, 2 * LANES), lambda b: (0, 0)),
            pl.BlockSpec((2 * LANES, 2 * LANES), lambda b: (0, 0)),
            pl.BlockSpec((NSA_GROUPS, 1, LANES), lambda b: (0, 0, 0)),
        ],
        out_specs=pl.BlockSpec((1, 2, NSA_GROUPS, n_rows, LANES), lambda b: (b, 0, 0, 0, 0)),
        out_shape=jax.ShapeDtypeStruct((B, 2, NSA_GROUPS, n_rows, LANES), BF16),
        compiler_params=_cparams(("arbitrary",), 32),
    )(kvc, pe, w1, w2k, w2vt, knw)


SEL_CHUNK = 512


def _nsa_kernel(q_ref, kn_ref, vt_ref, cmp_ref, g_ref, selmap_ref, y_ref, sel_ref, acc_ref):
    tq = q_ref.shape[1]
    cols = NSA_HEADS * tq
    i = pl.program_id(1)
    t0 = i * tq
    qb = q_ref[0]
    qs = jnp.concatenate([qb[:, h * LANES:(h + 1) * LANES] for h in range(NSA_HEADS)], axis=0)
    qpos = t0 + jnp.bitwise_and(lax.broadcasted_iota(I32, (1, cols), 1), tq - 1)
    both = lambda kv: (cmp_ref[0, kv, 0].astype(F32) + cmp_ref[0, kv, 1].astype(F32)).astype(BF16)

    n_win = WINDOW + tq
    w_start = pl.multiple_of(jnp.maximum(t0 - WINDOW, 0), LANES)
    w_unit = w_start // LANES
    k_w = kn_ref[0, pl.ds(w_start, n_win), LANES:2 * LANES]
    vt_w = jnp.concatenate([vt_ref[w_unit + u, LANES:2 * LANES, :] for u in range(n_win // LANES)], axis=1)
    diff = (qpos - w_start) - lax.broadcasted_iota(I32, (n_win, cols), 0)
    in_window = lax.bitcast_convert_type(diff, jnp.uint32) < jnp.uint32(WINDOW)
    s_w = jnp.where(in_window, _dot_nt(k_w, qs), NEG)
    p_w = jnp.exp(s_w - jnp.max(s_w, axis=0, keepdims=True))
    l_w = jnp.sum(p_w, axis=0, keepdims=True)
    o_w = _dot(vt_w, p_w.astype(BF16))

    n_cmp = cmp_ref.shape[3]
    s = _dot_nt(both(0), qs)
    cmp_last = lax.broadcasted_iota(I32, (n_cmp, cols), 0) * CMP_STRIDE + (CMP_BLOCK - 1)
    vis_c = cmp_last <= qpos
    s = jnp.where(vis_c, s, NEG)
    m = jnp.max(s, axis=0, keepdims=True)
    m = jnp.where(m > 0.5 * NEG, m, 0.0)
    e = jnp.where(vis_c, jnp.exp(s - m), 0.0)
    d = jnp.sum(e, axis=0, keepdims=True)
    p_c = e / jnp.where(d > 0.0, d, 1.0)
    o_c = _dot(both(1), p_c.astype(BF16))

    n_sel = selmap_ref.shape[0]
    blk = lax.broadcasted_iota(I32, (n_sel, tq), 0)
    cur = jnp.right_shift(t0 + lax.broadcasted_iota(I32, (n_sel, tq), 1), 6)
    forced = (blk == 0) | (blk == cur) | (blk == cur - 1)
    chosen = []
    for g in range(NSA_GROUPS):
        psum = p_c[:, g * NSA_REP * tq:(g * NSA_REP + 1) * tq]
        for r in range(1, NSA_REP):
            psum = psum + p_c[:, (g * NSA_REP + r) * tq:(g * NSA_REP + r + 1) * tq]
        p_hi = psum.astype(BF16)
        p_lo = (psum - p_hi.astype(F32)).astype(BF16)
        imp = _dot(selmap_ref[...], p_hi) + _dot(selmap_ref[...], p_lo)
        imp = jnp.where(blk > cur, -jnp.inf, jnp.where(forced, jnp.inf, imp))
        rank = jnp.zeros((n_sel, tq), I32)
        for jp in range(n_sel):
            other = imp[jp:jp + 1, :]
            beats = (other > imp) | ((other == imp) & (blk > jp))
            rank = rank + beats.astype(I32)
        chosen += [rank < min(SEL_TOPN, n_sel)] * NSA_REP
    sel_ref[...] = jnp.where(jnp.concatenate(chosen, axis=1), qpos, -1)

    acc_ref[...] = jnp.zeros(acc_ref.shape, F32)
    units = SEL_CHUNK // LANES
    blocks = SEL_CHUNK // SEL_BLOCK
    koff = lax.broadcasted_iota(I32, (SEL_BLOCK, cols), 0)

    def sel_chunk(c, carry):
        m_old, l_old = carry
        start = pl.multiple_of(c * SEL_CHUNK, SEL_CHUNK)
        k = kn_ref[0, pl.ds(start, SEL_CHUNK), 0:LANES]
        vt = jnp.concatenate([vt_ref[c * units + u, 0:LANES, :] for u in range(units)], axis=1)
        sc = _dot_nt(k, qs)
        parts = []
        for b in range(blocks):
            limit = sel_ref[pl.ds(c * blocks + b, 1), :] - (start + b * SEL_BLOCK)
            parts.append(jnp.where(koff <= limit, sc[b * SEL_BLOCK:(b + 1) * SEL_BLOCK], NEG))
        sc = jnp.concatenate(parts, axis=0)
        m_new = jnp.maximum(m_old, jnp.max(sc, axis=0, keepdims=True))
        alpha = jnp.exp(m_old - m_new)
        p = jnp.exp(sc - m_new)
        acc_ref[...] = alpha * acc_ref[...] + _dot(vt, p.astype(BF16))
        return m_new, alpha * l_old + jnp.sum(p, axis=0, keepdims=True)

    init = (jnp.full((1, cols), NEG, F32), jnp.zeros((1, cols), F32))
    _, l_s = lax.fori_loop(0, (t0 + tq + SEL_CHUNK - 1) // SEL_CHUNK, sel_chunk, init)
    o_s = acc_ref[...]

    g_t = [g_ref[0, :, g * LANES:(g + 1) * LANES].T for g in range(NSA_GROUPS)]
    gate = lambda br: jnp.concatenate(
        [g_t[h // NSA_REP][3 * (h % NSA_REP) + br:3 * (h % NSA_REP) + br + 1, :] for h in range(NSA_HEADS)], axis=1)
    y_t = o_c * gate(0) + o_s * (gate(1) / l_s) + o_w * (gate(2) / l_w)
    for h in range(NSA_HEADS):
        y_ref[0, :, h * LANES:(h + 1) * LANES] = y_t[:, h * tq:(h + 1) * tq].T.astype(BF16)


def _nsa(q, kn, vt, kvcmp, gates, selmap, tq=128):
    B, S = q.shape[0], q.shape[1]
    n_sel = S // SEL_BLOCK
    cols = NSA_HEADS * tq
    units = S // LANES
    n_cmp = kvcmp.shape[3]
    return pl.pallas_call(
        _nsa_kernel,
        grid=(B, S // tq),
        in_specs=[
            pl.BlockSpec((1, tq, C_Q), lambda b, i: (b, i, 0)),
            pl.BlockSpec((1, S, 2 * LANES), lambda b, i: (b, 0, 0)),
            pl.BlockSpec((units, 2 * LANES, LANES), lambda b, i: (b, 0, 0)),
            pl.BlockSpec((1, 2, NSA_GROUPS, n_cmp, LANES), lambda b, i: (b, 0, 0, 0, 0)),
            pl.BlockSpec((1, tq, C_G), lambda b, i: (b, i, 0)),
            pl.BlockSpec((n_sel, n_cmp), lambda b, i: (0, 0)),
        ],
        out_specs=pl.BlockSpec((1, tq, C_Q), lambda b, i: (b, i, 0)),
        out_shape=jax.ShapeDtypeStruct((B, S, C_Q), BF16),
        scratch_shapes=[
            pltpu.VMEM((n_sel, cols), I32),
            pltpu.VMEM((LANES, cols), F32),
        ],
        compiler_params=_cparams(("arbitrary", "arbitrary"), 56),
    )(q, kn, vt, kvcmp, gates, selmap)


def _mem_kv_kernel(mem_ref, nw_ref, wk_ref, wv_ref, kw_ref, k_ref, v_ref):
    mn = _rms(mem_ref[0], nw_ref[...]).astype(BF16)
    k = _dot(mn, wk_ref[...])
    for h in range(X_HEADS):
        sl = slice(h * X_DH, (h + 1) * X_DH)
        k_ref[0, :, sl] = _rms(k[:, sl], kw_ref[...]).astype(BF16)
    v_ref[0] = _dot(mn, wv_ref[...]).astype(BF16)


def _mem_kv(mem, nw, wk, wv, kw):
    B, M, D = mem.shape
    blk = pl.BlockSpec((1, M, D), lambda b: (b, 0, 0))
    const = lambda b: (0, 0)
    return pl.pallas_call(
        _mem_kv_kernel,
        grid=(B,),
        in_specs=[blk, pl.BlockSpec((1, D), const), pl.BlockSpec((D, D), const),
                  pl.BlockSpec((D, D), const), pl.BlockSpec((1, X_DH), const)],
        out_specs=[blk, blk],
        out_shape=[jax.ShapeDtypeStruct((B, M, D), BF16)] * 2,
        compiler_params=_cparams(("arbitrary",), 32),
    )(mem, nw, wk, wv, kw)


def _post_mix_kernel(x_ref, yc_ref, yn_ref, woc_ref, won_ref, xnw_ref, xq_ref, xqw_ref, km_ref, vm_ref,
                     xo_ref, fnw_ref, pwq_ref, keys_ref, x2_ref, h3_ref, sc_ref):
    tm = x_ref.shape[0]
    n_streams = 2
    rows_per = tm // n_streams

    def stream(i):
        rs = slice(i * rows_per, (i + 1) * rows_per)
        x1 = x_ref[rs, :] + _dot(yc_ref[rs, :], woc_ref[...]) + _dot(yn_ref[rs, :], won_ref[...])
        yield
        h2 = _rms(x1, xnw_ref[...]).astype(BF16)
        qx = _dot(h2, xq_ref[...])
        yield
        heads = []
        for h in range(X_HEADS):
            sl = slice(h * X_DH, (h + 1) * X_DH)
            qh = _rms(qx[:, sl], xqw_ref[...]).astype(BF16)
            s = _dot_nt(qh, km_ref[0, :, sl])
            e = jnp.exp(s - jnp.max(s, axis=-1, keepdims=True))
            p = e / jnp.sum(e, axis=-1, keepdims=True)
            heads.append(_dot(p.astype(BF16), vm_ref[0, :, sl]))
        yield
        o = jnp.concatenate(heads, axis=1).astype(BF16)
        x2 = x1 + _dot(o, xo_ref[...])
        x2_ref[rs, :] = x2
        yield
        h3 = _rms(x2, fnw_ref[...]).astype(BF16)
        h3_ref[rs, :] = h3
        qp = _dot(h3, pwq_ref[...]).astype(BF16)
        for c in range(2 * PEER_HEADS):
            sl = slice(c * PEER_HALF, (c + 1) * PEER_HALF)
            for j in range(rows_per // LANES):
                sc_ref[c, i * (rows_per // LANES) + j] = _dot_nt(keys_ref[c], qp[j * LANES:(j + 1) * LANES, sl])
        yield

    first, second = stream(0), stream(1)
    next(first)
    for _ in range(4):
        next(first)
        next(second)
    next(second)


def _post_mix(x2d, yconv, ynsa, woc, won, xnw, xq, xqw, kmem, vmem, xo, fnw, pwq, keys, seq, tm=512):
    T, D = x2d.shape
    M = kmem.shape[1]
    n_sc = 2 * PEER_HEADS * PEER_NKEYS
    tile = lambda i: (i, 0)
    const = lambda i: (0, 0)
    per_batch = lambda i: ((i * tm) // seq, 0, 0)
    return pl.pallas_call(
        _post_mix_kernel,
        grid=(T // tm,),
        in_specs=[
            pl.BlockSpec((tm, D), tile),
            pl.BlockSpec((tm, CONV_WIDTH), tile),
            pl.BlockSpec((tm, C_Q), tile),
            pl.BlockSpec((CONV_WIDTH, D), const),
            pl.BlockSpec((C_Q, D), const),
            pl.BlockSpec((1, D), const),
            pl.BlockSpec((D, D), const),
            pl.BlockSpec((1, X_DH), const),
            pl.BlockSpec((1, M, D), per_batch),
            pl.BlockSpec((1, M, D), per_batch),
            pl.BlockSpec((D, D), const),
            pl.BlockSpec((1, D), const),
            pl.BlockSpec((D, n_sc), const),
            pl.BlockSpec((2 * PEER_HEADS, PEER_NKEYS, PEER_HALF), lambda i: (0, 0, 0)),
        ],
        out_specs=[pl.BlockSpec((tm, D), tile), pl.BlockSpec((tm, D), tile),
                   pl.BlockSpec((2 * PEER_HEADS, tm // LANES, PEER_NKEYS, LANES), lambda i: (0, i, 0, 0))],
        out_shape=[
            jax.ShapeDtypeStruct((T, D), F32),
            jax.ShapeDtypeStruct((T, D), BF16),
            jax.ShapeDtypeStruct((2 * PEER_HEADS, T // LANES, PEER_NKEYS, LANES), F32),
        ],
        compiler_params=_cparams(("arbitrary",), 56),
    )(x2d, yconv, ynsa, woc, won, xnw, xq, xqw, kmem, vmem, xo, fnw, pwq, keys)


STAIR = [(a, b) for a in range(PEER_TOPK) for b in range(PEER_TOPK // (a + 1))]
N_STAIR_VREGS = -(-len(STAIR) // SUBLANES)


def _stair_vregs(axis, k):
    return sorted({r // SUBLANES for r, ab in enumerate(STAIR) if ab[axis] == k})


def _stair_tables():
    rows = N_STAIR_VREGS * SUBLANES
    pad = rows - len(STAIR)
    ta = jnp.asarray([float(a) for a, _ in STAIR] + [-1.0] * pad, F32)
    tb = jnp.asarray([float(b) for _, b in STAIR] + [-1.0] * pad, F32)
    return jnp.broadcast_to(ta[:, None], (rows, LANES)), jnp.broadcast_to(tb[:, None], (rows, LANES))


def _make_topk(sc_ref, ta_ref, tb_ref, s_out, e_out):
    K = PEER_TOPK
    NV = N_STAIR_VREGS
    kidx = lax.broadcasted_iota(I32, (PEER_NKEYS, LANES), 0).astype(F32)
    ridx = [(lax.broadcasted_iota(I32, (SUBLANES, LANES), 0) + SUBLANES * j).astype(F32) for j in range(NV)]
    ta = [ta_ref[SUBLANES * j:SUBLANES * (j + 1), :] for j in range(NV)]
    tb = [tb_ref[SUBLANES * j:SUBLANES * (j + 1), :] for j in range(NV)]
    zero = jnp.zeros((SUBLANES, LANES), F32)

    def top1(s):
        m = jnp.max(s, axis=0, keepdims=True)
        idx = jnp.min(jnp.where(s == m, kidx, float(PEER_NKEYS)), axis=0, keepdims=True)
        return m, idx, jnp.where(kidx == idx, -jnp.inf, s)

    def sorted_top(s, table, axis):
        val = [zero] * NV
        key = [zero] * NV
        for k in range(K):
            m, idx, s = top1(s)
            for j in _stair_vregs(axis, k):
                hit = table[j] == float(k)
                val[j] = jnp.where(hit, m, val[j])
                key[j] = jnp.where(hit, idx, key[j])
        return val, key

    def top_half(h, g, which):
        return sorted_top(sc_ref[2 * h + which, g], (ta, tb)[which], which)

    def select(first, second):
        (v1, k1), (v2, k2) = first, second
        cand = [jnp.where(ta[j] >= 0.0, v1[j] + v2[j], -jnp.inf) for j in range(NV)]
        ce = [k1[j] * float(PEER_NKEYS) + k2[j] for j in range(NV)]
        sub = lax.broadcasted_iota(I32, (SUBLANES, LANES), 0)
        best_s = [zero] * (K // SUBLANES)
        best_e = [zero] * (K // SUBLANES)
        for k in range(K):
            mx = cand[0]
            for j in range(1, NV):
                mx = jnp.maximum(mx, cand[j])
            m = jnp.max(mx, axis=0, keepdims=True)
            ix = jnp.where(cand[0] == m, ridx[0], float(NV * SUBLANES))
            for j in range(1, NV):
                ix = jnp.minimum(ix, jnp.where(cand[j] == m, ridx[j], float(NV * SUBLANES)))
            idx = jnp.min(ix, axis=0, keepdims=True)
            hits = [ridx[j] == idx for j in range(NV)]
            es = jnp.where(hits[0], ce[0], 0.0)
            for j in range(1, NV):
                es = es + jnp.where(hits[j], ce[j], 0.0)
            cand = [jnp.where(hits[j], -jnp.inf, cand[j]) for j in range(NV)]
            here = sub == (k % SUBLANES)
            best_s[k // SUBLANES] = jnp.where(here, m, best_s[k // SUBLANES])
            best_e[k // SUBLANES] = jnp.where(here, jnp.sum(es, axis=0, keepdims=True), best_e[k // SUBLANES])
        return jnp.concatenate(best_s, axis=0), jnp.concatenate(best_e, axis=0)

    def store(h, best):
        rows = pl.ds(pl.multiple_of(h * K, K), K)
        s_out[rows, :] = best[0]
        e_out[rows, :] = best[1]

    def finish(rows, ei_dst, ej_dst, gate_dst):
        s_all = s_out[...]
        gates = []
        for h in range(PEER_HEADS):
            sh = s_all[h * K:(h + 1) * K]
            ex = jnp.exp(sh - sh[0:1])
            gates.append(ex / jnp.sum(ex, axis=0, keepdims=True))
        gate_dst[rows, :] = jnp.concatenate(gates, axis=0).T
        e_int = e_out[...].T.astype(I32)
        ei_dst[rows, :] = jnp.right_shift(e_int, 7)
        ej_dst[rows, :] = jnp.bitwise_and(e_int, PEER_NKEYS - 1)

    return top_half, select, store, finish


G_ROWS = PEER_NKEYS // 2
G_PITCH = G_ROWS + SUBLANES
HI_MASK = 0xFFFF0000


def _peer_kernel(sc_ref, ta_ref, tb_ref, h_ref, wd_ref, wu_ref, x2_ref, out_ref,
                 g_ref, ei_ref, ej_ref, gate_ref, s_out, e_out):
    tm = h_ref.shape[0]
    ec = wd_ref.shape[1]
    ipc = ec // LANES
    half = ipc // 2
    t = pl.program_id(0)
    c = pl.program_id(1)
    n_grp = tm // LANES
    steps_per_grp = PEER_EXPERTS // ec // n_grp
    heads_per_step = PEER_HEADS // steps_per_grp
    rd = (t + 1) % 2
    wr = t % 2
    ei_rd, ej_rd, gate_rd = ei_ref.at[rd], ej_ref.at[rd], gate_ref.at[rd]

    @pl.when((t == 0) & (c == 0))
    def _first_step():
        ei_ref[...] = jnp.zeros(ei_ref.shape, I32)
        ej_ref[...] = jnp.zeros(ej_ref.shape, I32)
        gate_ref[...] = jnp.zeros(gate_ref.shape, F32)

    @pl.when(c == 0)
    def _scatter():
        out_ref[...] = x2_ref[...]
        as_bf16 = lambda v: v.astype(F32).astype(BF16)
        i_of_m = as_bf16(lax.broadcasted_iota(I32, (PEER_NKEYS, 2 * LANES), 0))
        n = as_bf16(lax.broadcasted_iota(I32, (LANES, LANES), 0))
        one = jnp.ones((LANES, LANES), BF16)
        zero = jnp.zeros((LANES, LANES), BF16)

        def pair(u, carry):
            ta_, tb_ = pl.ds(2 * u, 1), pl.ds(2 * u + 1, 1)
            ri = as_bf16(jnp.concatenate([ei_rd[ta_, :], ei_rd[tb_, :]], axis=1))
            rg = (0.5 * jnp.concatenate([gate_rd[ta_, :], gate_rd[tb_, :]], axis=1)).astype(BF16)
            c_t = jnp.where(i_of_m == ri, rg, jnp.zeros_like(rg))
            q_a = jnp.where(n == as_bf16(ej_rd[ta_, :]), one, zero)
            q_b = jnp.where(n == as_bf16(ej_rd[tb_, :]), one, zero)
            q_t = jnp.concatenate([jnp.concatenate([q_a, zero], axis=1),
                                   jnp.concatenate([zero, q_b], axis=1)], axis=0)
            g2 = _dot_nt(c_t, q_t).astype(BF16)
            for tok in range(2):
                word = pltpu.bitcast(g2[:, tok * LANES:(tok + 1) * LANES], jnp.uint32)
                g_ref[pl.ds(pl.multiple_of((2 * u + tok) * G_PITCH, SUBLANES), G_ROWS), :] = word
            return carry

        lax.fori_loop(0, tm // 2, pair, 0, unroll=16)

    top_half, select, store, finish = _make_topk(sc_ref, ta_ref, tb_ref, s_out, e_out)
    grp = c // steps_per_grp
    h0 = (c % steps_per_grp) * heads_per_step

    def experts_slice(k):
        a = _dot(h_ref[...], wd_ref[:, 2 * k * LANES:2 * (k + 1) * LANES]).astype(BF16)
        word = g_ref[pl.ds(c * half + k, tm, stride=G_PITCH), :]
        g_lo = lax.bitcast_convert_type(jnp.left_shift(word, 16), F32).astype(BF16)
        g_hi = lax.bitcast_convert_type(jnp.bitwise_and(word, jnp.uint32(HI_MASK)), F32).astype(BF16)
        return jnp.concatenate([_gelu2(a[:, 0:LANES]) * g_lo, _gelu2(a[:, LANES:2 * LANES]) * g_hi], axis=1)

    assert half == 2 * heads_per_step
    halves, z = [], []
    for k in range(half):
        halves.append(top_half(h0 + k // 2, grp, k % 2))
        z.append(experts_slice(k))
    best = [select(halves[2 * hh], halves[2 * hh + 1]) for hh in range(heads_per_step - 1)]
    out_ref[...] += _dot(jnp.concatenate(z, axis=1), wu_ref[...])
    best.append(select(halves[-2], halves[-1]))
    for hh in range(heads_per_step):
        store(h0 + hh, best[hh])

    @pl.when(c % steps_per_grp == steps_per_grp - 1)
    def _group_done():
        rows = pl.ds(pl.multiple_of(grp * LANES, LANES), LANES)
        finish(rows, ei_ref.at[wr], ej_ref.at[wr], gate_ref.at[wr])


def _peer(sc, h3, wd_t, wu, x2, tm=512, ec=2048):
    T, D = h3.shape
    tm = min(tm, T)
    n_tiles = T // tm
    n_grp = tm // LANES
    n_chunks = wu.shape[0] // ec
    assert n_chunks % n_grp == 0 and PEER_HEADS % (n_chunks // n_grp) == 0
    ta, tb = _stair_tables()
    prev = lambda t, c: (jnp.maximum(t - 1, 0), 0)
    return pl.pallas_call(
        _peer_kernel,
        grid=(n_tiles + 1, n_chunks),
        in_specs=[
            pl.BlockSpec((2 * PEER_HEADS, n_grp, PEER_NKEYS, LANES),
                         lambda t, c: (0, jnp.minimum(t, n_tiles - 1), 0, 0)),
            pl.BlockSpec(ta.shape, lambda t, c: (0, 0)),
            pl.BlockSpec(tb.shape, lambda t, c: (0, 0)),
            pl.BlockSpec((tm, D), prev),
            pl.BlockSpec((D, ec), lambda t, c: (0, c)),
            pl.BlockSpec((ec, D), lambda t, c: (c, 0)),
            pl.BlockSpec((tm, D), prev),
        ],
        out_specs=pl.BlockSpec((tm, D), prev),
        out_shape=jax.ShapeDtypeStruct((T, D), F32),
        scratch_shapes=[
            pltpu.VMEM((tm * G_PITCH, LANES), jnp.uint32),
            pltpu.VMEM((2, tm, LANES), I32),
            pltpu.VMEM((2, tm, LANES), I32),
            pltpu.VMEM((2, tm, LANES), F32),
            pltpu.VMEM((PEER_HEADS * PEER_TOPK, LANES), F32),
            pltpu.VMEM((PEER_HEADS * PEER_TOPK, LANES), F32),
        ],
        compiler_params=_cparams(("arbitrary", "arbitrary"), 60),
    )(sc, ta, tb, h3, wd_t, wu, x2)


def _pad_half(a, g):
    z = jnp.zeros_like(a)
    return jnp.concatenate([a, z] if g == 0 else [z, a], axis=-1)


def kernel(x, mem, mix_norm_w, w_in, conv_w, conv_b, cmp_pe, cmp_w1, cmp_w2, q_norm_w, k_norm_w, w_out,
           xattn_norm_w, mem_norm_w, xq, xk, xv, xo, xq_norm_w, xk_norm_w, ffn_norm_w, peer_wq, peer_keys,
           peer_down, peer_up):
    B, S, D = x.shape
    T = B * S
    l = 0
    G, R, dh = NSA_GROUPS, NSA_REP, NSA_DH

    w = w_in[l]
    o_q = C_CONV
    o_kv = o_q + NSA_HEADS * dh
    o_g = o_kv + C_KV
    wq = w[:, o_q:o_kv].reshape(D, G, R, dh)
    wq_pad = jnp.concatenate([_pad_half(wq[:, g], g).reshape(D, R * LANES) for g in range(G)], axis=1)
    wg = w[:, o_g:].reshape(D, G, R * 3)
    wg_pad = jnp.pad(wg, ((0, 0), (0, 0), (0, LANES - R * 3))).reshape(D, G * LANES)
    wkv = w[:, o_kv:o_g].reshape(D, 6, LANES)
    wkn = wkv[:, jnp.array([0, 1, 2, 4])].reshape(D, C_KN)
    wvt = wkv[:, jnp.array([3, 5])].reshape(D, 2 * LANES).T.astype(BF16)
    w_all = jnp.concatenate([w[:, :o_q], wq_pad, wkn, wg_pad], axis=1).astype(BF16)
    qw = q_norm_w[l] * dh ** -0.5
    qw_pad = jnp.concatenate([jnp.tile(_pad_half(qw, g), R) for g in range(G)]).reshape(1, C_Q)
    kw = jnp.stack([jnp.tile(k_norm_w[l, 1], G), jnp.tile(k_norm_w[l, 2], G)])

    yconv, q, kvc, kn, vt, gates = _in_proj(
        x.reshape(T, D), mix_norm_w[l].reshape(1, D), w_all, wvt, conv_w[l], conv_b[l].reshape(1, CONV_WIDTH),
        qw_pad, kw, S)

    n_rows = S // CMP_STRIDE
    kvg = jnp.array([0, 0, 1, 1])
    pe = jnp.concatenate([cmp_pe[l, kv] for kv in (0, 0, 1, 1)], axis=-1)
    w1 = cmp_w1[l].reshape(2, CMP_BLOCK, dh, dh)[kvg]
    w1 = jnp.einsum('qp,qjdh->jqdph', jnp.eye(4, dtype=F32), w1).reshape(CMP_BLOCK, 4 * dh, 4 * dh).astype(BF16)
    w2k = jnp.zeros((4 * dh, G * LANES), F32)
    w2vt = jnp.zeros((G * LANES, 4 * dh), F32)
    for g in range(G):
        o = g * LANES + g * dh
        w2k = w2k.at[g * dh:(g + 1) * dh, o:o + dh].set(cmp_w2[l, 0])
        w2vt = w2vt.at[o:o + dh, (G + g) * dh:(G + g + 1) * dh].set(cmp_w2[l, 1].T)
    knw = jnp.stack([_pad_half(k_norm_w[l, 0], g) for g in range(G)]).reshape(G, 1, LANES)
    kvcmp = _compress(kvc, pe, w1, w2k.astype(BF16), w2vt.astype(BF16), knw, S)

    n_sel = S // SEL_BLOCK
    cmp_start = jnp.arange(n_rows) * CMP_STRIDE
    sel_start = jnp.arange(n_sel) * SEL_BLOCK
    selmap_t = ((cmp_start[None, :] < sel_start[:, None] + SEL_BLOCK)
                & (cmp_start[None, :] + CMP_BLOCK > sel_start[:, None])
                & (jnp.arange(n_rows)[None, :] < n_rows - 1)).astype(BF16)
    ynsa = _nsa(q.reshape(B, S, C_Q), kn.reshape(B, S, 2 * LANES), vt, kvcmp, gates.reshape(B, S, C_G), selmap_t)

    kmem, vmem = _mem_kv(mem, mem_norm_w[l].reshape(1, D), xk[l].astype(BF16), xv[l].astype(BF16),
                         xk_norm_w[l].reshape(1, X_DH))

    wo = w_out[l]
    won = wo[CONV_WIDTH:].reshape(G, R, dh, D)
    won_pad = jnp.concatenate(
        [jnp.concatenate([won[g], jnp.zeros_like(won[g])] if g == 0 else [jnp.zeros_like(won[g]), won[g]],
                         axis=1).reshape(R * LANES, D) for g in range(G)], axis=0).astype(BF16)
    keys = peer_keys[l].reshape(2 * PEER_HEADS, PEER_NKEYS, PEER_HALF).astype(BF16)
    x2, h3, sc = _post_mix(
        x.reshape(T, D), yconv, ynsa.reshape(T, C_Q), wo[:CONV_WIDTH].astype(BF16), won_pad,
        xattn_norm_w[l].reshape(1, D), xq[l].astype(BF16),
        (xq_norm_w[l] * X_DH ** -0.5).reshape(1, X_DH), kmem, vmem, xo[l].astype(BF16),
        ffn_norm_w[l].reshape(1, D), peer_wq[l].astype(BF16), keys, S)

    out = _peer(sc, h3, peer_down[l].T.astype(BF16), peer_up[l].astype(BF16), x2)
    return out.reshape(B, S, D)
```

```python
import functools

import jax
import jax.numpy as jnp
from jax import lax
from jax.experimental import pallas as pl
from jax.experimental.pallas import tpu as pltpu

F32 = jnp.float32
BF16 = jnp.bfloat16
I32 = jnp.int32

EPS = 1e-6
NEG = -1e30
LANES = 128
SUBLANES = 8

D_MODEL = 1024
CONV_WIDTH = 512
NSA_HEADS = 8
NSA_GROUPS = 2
NSA_REP = NSA_HEADS // NSA_GROUPS
NSA_DH = 64
CMP_BLOCK = 32
CMP_STRIDE = 16
SEL_BLOCK = 64
SEL_TOPN = 16
WINDOW = 512
X_HEADS = 4
X_DH = D_MODEL // X_HEADS
PEER_HEADS = 8
PEER_NKEYS = 128
PEER_HALF = 128
PEER_TOPK = 16
PEER_EXPERTS = PEER_NKEYS * PEER_NKEYS

NT_DIMS = (((1,), (1,)), ((), ()))


def _dot(a, b):
    return jnp.dot(a, b, preferred_element_type=F32)


def _dot_nt(a, b):
    return lax.dot_general(a, b, NT_DIMS, preferred_element_type=F32)


def _gelu2(x):
    return x * (1.0 + lax.erf(x * (2.0 ** -0.5)))


def _gelu(x):
    return 0.5 * _gelu2(x)


def _rms(x, w):
    ms = jnp.mean(x * x, axis=-1, keepdims=True)
    return x * lax.rsqrt(ms + EPS) * w


def _cparams(sem, vmem_mb):
    return pltpu.CompilerParams(dimension_semantics=sem, vmem_limit_bytes=vmem_mb * 1024 * 1024)


C_CONV = 3 * CONV_WIDTH
C_Q = NSA_HEADS * LANES
C_KV = 6 * NSA_GROUPS * NSA_DH
C_KN = 4 * LANES
C_G = NSA_GROUPS * LANES
C_ALL = C_CONV + C_Q + C_KN + C_G


def _in_proj_kernel(tiles_per_seq, x_ref, nw_ref, w_ref, wvt_ref, cw_ref, cb_ref, qw_ref, kw_ref,
                    yconv_ref, q_ref, kvc_ref, kn_ref, vt_ref, g_ref, carry_ref):
    tm = x_ref.shape[0]
    n_streams = 2
    rows = tm // n_streams

    @pl.when((pl.program_id(0) % tiles_per_seq) == 0)
    def _sequence_start():
        carry_ref[...] = jnp.zeros(carry_ref.shape, F32)

    tails = {}

    def stream(i):
        rs = slice(i * rows, (i + 1) * rows)
        hn = _rms(x_ref[rs, :], nw_ref[...]).astype(BF16)
        yield
        p = _dot(hn, w_ref[:, 0:C_CONV])
        b_g = p[:, 0:CONV_WIDTH]
        u = p[:, CONV_WIDTH:2 * CONV_WIDTH] * p[:, 2 * CONV_WIDTH:3 * CONV_WIDTH]
        prev = carry_ref[...] if i == 0 else tails[i - 1]
        p1 = prev[SUBLANES - 1:SUBLANES, :]
        p2 = prev[SUBLANES - 2:SUBLANES - 1, :]
        row = lax.broadcasted_iota(I32, u.shape, 0)
        u1 = jnp.where(row == 0, p1, pltpu.roll(u, 1, axis=0))
        u2 = jnp.where(row == 0, p2, jnp.where(row == 1, p1, pltpu.roll(u, 2, axis=0)))
        tails[i] = u[rows - SUBLANES:rows, :]
        if i == n_streams - 1:
            carry_ref[...] = tails[i]
        cw = cw_ref[...]
        y = b_g * (cw[0:1, :] * u2 + cw[1:2, :] * u1 + cw[2:3, :] * u + cb_ref[...])
        yconv_ref[rs, :] = y.astype(BF16)
        yield
        pq = _dot(hn, w_ref[:, C_CONV:C_CONV + C_Q])
        for h in range(NSA_HEADS):
            blk = pq[:, h * LANES:(h + 1) * LANES]
            ms = jnp.sum(blk * blk, axis=-1, keepdims=True) * (1.0 / NSA_DH)
            q_ref[rs, h * LANES:(h + 1) * LANES] = (
                blk * lax.rsqrt(ms + EPS) * qw_ref[:, h * LANES:(h + 1) * LANES]).astype(BF16)
        yield
        pkv = _dot(hn, w_ref[:, C_CONV + C_Q:C_CONV + C_Q + C_KN])
        kvc_ref[0, rs, :] = pkv[:, 0:LANES]
        kvc_ref[1, rs, :] = pkv[:, LANES:2 * LANES]
        lane = lax.broadcasted_iota(I32, (rows, LANES), 1)
        lo = lane < NSA_DH
        for j in range(2):
            blk = pkv[:, (2 + j) * LANES:(3 + j) * LANES]
            sq = blk * blk
            ms_lo = jnp.sum(jnp.where(lo, sq, 0.0), axis=-1, keepdims=True) * (1.0 / NSA_DH)
            ms_hi = jnp.sum(jnp.where(lo, 0.0, sq), axis=-1, keepdims=True) * (1.0 / NSA_DH)
            scale = jnp.where(lo, lax.rsqrt(ms_lo + EPS), lax.rsqrt(ms_hi + EPS))
            kn_ref[rs, j * LANES:(j + 1) * LANES] = (blk * scale * kw_ref[j:j + 1, :]).astype(BF16)
        yield
        for j in range(rows // LANES):
            vt_ref[i * (rows // LANES) + j] = _dot_nt(wvt_ref[...], hn[j * LANES:(j + 1) * LANES, :]).astype(BF16)
        pg = _dot(hn, w_ref[:, C_CONV + C_Q + C_KN:C_ALL])
        g_ref[rs, :] = jax.nn.sigmoid(pg)
        yield

    first, second = stream(0), stream(1)
    next(first)
    for _ in range(4):
        next(first)
        next(second)
    next(second)


def _in_proj(x2d, nw, w_all, wvt, cw, cb, qw, kw, seq, tm=512):
    T = x2d.shape[0]
    const = lambda i: (0, 0)
    tile = lambda i: (i, 0)
    return pl.pallas_call(
        functools.partial(_in_proj_kernel, seq // tm),
        grid=(T // tm,),
        in_specs=[
            pl.BlockSpec((tm, D_MODEL), tile),
            pl.BlockSpec((1, D_MODEL), const),
            pl.BlockSpec((D_MODEL, C_ALL), const),
            pl.BlockSpec((2 * LANES, D_MODEL), const),
            pl.BlockSpec((3, CONV_WIDTH), const),
            pl.BlockSpec((1, CONV_WIDTH), const),
            pl.BlockSpec((1, C_Q), const),
            pl.BlockSpec((2, LANES), const),
        ],
        out_specs=[
            pl.BlockSpec((tm, CONV_WIDTH), tile),
            pl.BlockSpec((tm, C_Q), tile),
            pl.BlockSpec((2, tm, LANES), lambda i: (0, i, 0)),
            pl.BlockSpec((tm, 2 * LANES), tile),
            pl.BlockSpec((tm // LANES, 2 * LANES, LANES), lambda i: (i, 0, 0)),
            pl.BlockSpec((tm, C_G), tile),
        ],
        out_shape=[
            jax.ShapeDtypeStruct((T, CONV_WIDTH), BF16),
            jax.ShapeDtypeStruct((T, C_Q), BF16),
            jax.ShapeDtypeStruct((2, T, LANES), F32),
            jax.ShapeDtypeStruct((T, 2 * LANES), BF16),
            jax.ShapeDtypeStruct((T // LANES, 2 * LANES, LANES), BF16),
            jax.ShapeDtypeStruct((T, C_G), F32),
        ],
        scratch_shapes=[pltpu.VMEM((SUBLANES, CONV_WIDTH), F32)],
        compiler_params=_cparams(("arbitrary",), 48),
    )(x2d, nw, w_all, wvt, cw, cb, qw, kw)


def _compress_kernel(kvc_ref, pe_ref, w1_ref, w2k_ref, w2vt_ref, knw_ref, out_ref):
    n_rows = out_ref.shape[3]
    acc_a = jnp.zeros((n_rows, 2 * LANES), F32)
    acc_b = jnp.zeros((n_rows, 2 * LANES), F32)
    for l in range(CMP_STRIDE):
        rows = pl.ds(l, n_rows, stride=CMP_STRIDE)
        x = jnp.concatenate([kvc_ref[0, rows, :], kvc_ref[1, rows, :]], axis=1)
        acc_a = acc_a + _dot((x + pe_ref[l:l + 1, :]).astype(BF16), w1_ref[l])
        acc_b = acc_b + _dot((x + pe_ref[CMP_STRIDE + l:CMP_STRIDE + l + 1, :]).astype(BF16),
                             w1_ref[CMP_STRIDE + l])
    hid = _gelu(acc_a + pltpu.roll(acc_b, n_rows - 1, axis=0)).astype(BF16)
    row = lax.broadcasted_iota(I32, (n_rows, LANES), 0)
    col = lax.broadcasted_iota(I32, (LANES, n_rows), 1)
    keys = _dot(hid, w2k_ref[...])
    vals = _dot_nt(w2vt_ref[...], hid)
    for g in range(NSA_GROUPS):
        out = keys[:, g * LANES:(g + 1) * LANES]
        ms = jnp.sum(out * out, axis=-1, keepdims=True) * (1.0 / NSA_DH)
        out = out * lax.rsqrt(ms + EPS) * knw_ref[g]
        out_ref[0, 0, g] = jnp.where(row < n_rows - 1, out, 0.0).astype(BF16)
        out_ref[0, 1, g] = jnp.where(col < n_rows - 1, vals[g * LANES:(g + 1) * LANES, :], 0.0).astype(BF16)


def _compress(kvc, pe, w1, w2k, w2vt, knw, S):
    B = kvc.shape[1] // S
    n_rows = S // CMP_STRIDE
    return pl.pallas_call(
        _compress_kernel,
        grid=(B,),
        in_specs=[
            pl.BlockSpec((2, S, LANES), lambda b: (0, b, 0)),
            pl.BlockSpec((CMP_BLOCK, 2 * LANES), lambda b: (0, 0)),
            pl.BlockSpec((CMP_BLOCK, 2 * LANES, 2 * LANES), lambda b: (0, 0, 0)),
            pl.BlockSpec((2 * LANES, 2 * LANES), lambda b: (0, 0)),
            pl.BlockSpec((2 * LANES, 2 * LANES), lambda b: (0, 0)),
            pl.BlockSpec((NSA_GROUPS, 1, LANES), lambda b: (0, 0, 0)),
        ],
        out_specs=pl.BlockSpec((1, 2, NSA_GROUPS, n_rows, LANES), lambda b: (b, 0, 0, 0, 0)),
        out_shape=jax.ShapeDtypeStruct((B, 2, NSA_GROUPS, n_rows, LANES), BF16),
        compiler_params=_cparams(("arbitrary",), 32),
    )(kvc, pe, w1, w2k, w2vt, knw)


SEL_CHUNK = 512


def _nsa_kernel(q_ref, kn_ref, vt_ref, cmp_ref, g_ref, selmap_ref, y_ref, sel_ref, acc_ref):
    tq = q_ref.shape[1]
    cols = NSA_HEADS * tq
    i = pl.program_id(1)
    t0 = i * tq
    qb = q_ref[0]
    qs = jnp.concatenate([qb[:, h * LANES:(h + 1) * LANES] for h in range(NSA_HEADS)], axis=0)
    qpos = t0 + jnp.bitwise_and(lax.broadcasted_iota(I32, (1, cols), 1), tq - 1)
    both = lambda kv: (cmp_ref[0, kv, 0].astype(F32) + cmp_ref[0, kv, 1].astype(F32)).astype(BF16)

    n_win = WINDOW + tq
    w_start = pl.multiple_of(jnp.maximum(t0 - WINDOW, 0), LANES)
    w_unit = w_start // LANES
    k_w = kn_ref[0, pl.ds(w_start, n_win), LANES:2 * LANES]
    vt_w = jnp.concatenate([vt_ref[w_unit + u, LANES:2 * LANES, :] for u in range(n_win // LANES)], axis=1)
    diff = (qpos - w_start) - lax.broadcasted_iota(I32, (n_win, cols), 0)
    in_window = lax.bitcast_convert_type(diff, jnp.uint32) < jnp.uint32(WINDOW)
    s_w = jnp.where(in_window, _dot_nt(k_w, qs), NEG)
    p_w = jnp.exp(s_w - jnp.max(s_w, axis=0, keepdims=True))
    l_w = jnp.sum(p_w, axis=0, keepdims=True)
    o_w = _dot(vt_w, p_w.astype(BF16))

    n_cmp = cmp_ref.shape[3]
    s = _dot_nt(both(0), qs)
    cmp_last = lax.broadcasted_iota(I32, (n_cmp, cols), 0) * CMP_STRIDE + (CMP_BLOCK - 1)
    vis_c = cmp_last <= qpos
    s = jnp.where(vis_c, s, NEG)
    m = jnp.max(s, axis=0, keepdims=True)
    m = jnp.where(m > 0.5 * NEG, m, 0.0)
    e = jnp.where(vis_c, jnp.exp(s - m), 0.0)
    d = jnp.sum(e, axis=0, keepdims=True)
    p_c = e / jnp.where(d > 0.0, d, 1.0)
    o_c = _dot(both(1), p_c.astype(BF16))

    n_sel = selmap_ref.shape[0]
    blk = lax.broadcasted_iota(I32, (n_sel, tq), 0)
    cur = jnp.right_shift(t0 + lax.broadcasted_iota(I32, (n_sel, tq), 1),
                          SEL_BLOCK.bit_length() - 1)
    forced = (blk == 0) | (blk == cur) | (blk == cur - 1)
    chosen = []
    for g in range(NSA_GROUPS):
        psum = p_c[:, g * NSA_REP * tq:(g * NSA_REP + 1) * tq]
        for r in range(1, NSA_REP):
            psum = psum + p_c[:, (g * NSA_REP + r) * tq:(g * NSA_REP + r + 1) * tq]
        p_hi = psum.astype(BF16)
        p_lo = (psum - p_hi.astype(F32)).astype(BF16)
        imp = _dot(selmap_ref[...], p_hi) + _dot(selmap_ref[...], p_lo)
        imp = jnp.where(blk > cur, -jnp.inf, jnp.where(forced, jnp.inf, imp))
        rank = jnp.zeros((n_sel, tq), I32)
        for jp in range(n_sel):
            other = imp[jp:jp + 1, :]
            beats = (other > imp) | ((other == imp) & (blk > jp))
            rank = rank + beats.astype(I32)
        chosen += [rank < min(SEL_TOPN, n_sel)] * NSA_REP
    sel_ref[...] = jnp.where(jnp.concatenate(chosen, axis=1), qpos, -1)

    acc_ref[...] = jnp.zeros(acc_ref.shape, F32)
    units = SEL_CHUNK // LANES
    blocks = SEL_CHUNK // SEL_BLOCK
    koff = lax.broadcasted_iota(I32, (SEL_BLOCK, cols), 0)

    def sel_chunk(c, carry):
        m_old, l_old = carry
        start = pl.multiple_of(c * SEL_CHUNK, SEL_CHUNK)
        k = kn_ref[0, pl.ds(start, SEL_CHUNK), 0:LANES]
        vt = jnp.concatenate([vt_ref[c * units + u, 0:LANES, :] for u in range(units)], axis=1)
        sc = _dot_nt(k, qs)
        parts = []
        for b in range(blocks):
            limit = sel_ref[pl.ds(c * blocks + b, 1), :] - (start + b * SEL_BLOCK)
            parts.append(jnp.where(koff <= limit, sc[b * SEL_BLOCK:(b + 1) * SEL_BLOCK], NEG))
        sc = jnp.concatenate(parts, axis=0)
        m_new = jnp.maximum(m_old, jnp.max(sc, axis=0, keepdims=True))
        alpha = jnp.exp(m_old - m_new)
        p = jnp.exp(sc - m_new)
        acc_ref[...] = alpha * acc_ref[...] + _dot(vt, p.astype(BF16))
        return m_new, alpha * l_old + jnp.sum(p, axis=0, keepdims=True)

    init = (jnp.full((1, cols), NEG, F32), jnp.zeros((1, cols), F32))
    _, l_s = lax.fori_loop(0, (t0 + tq + SEL_CHUNK - 1) // SEL_CHUNK, sel_chunk, init)
    o_s = acc_ref[...]

    g_t = [g_ref[0, :, g * LANES:(g + 1) * LANES].T for g in range(NSA_GROUPS)]
    gate = lambda br: jnp.concatenate(
        [g_t[h // NSA_REP][3 * (h % NSA_REP) + br:3 * (h % NSA_REP) + br + 1, :] for h in range(NSA_HEADS)], axis=1)
    y_t = o_c * gate(0) + o_s * (gate(1) / l_s) + o_w * (gate(2) / l_w)
    for h in range(NSA_HEADS):
        y_ref[0, :, h * LANES:(h + 1) * LANES] = y_t[:, h * tq:(h + 1) * tq].T.astype(BF16)


def _nsa(q, kn, vt, kvcmp, gates, selmap, tq=128):
    B, S = q.shape[0], q.shape[1]
    n_sel = S // SEL_BLOCK
    cols = NSA_HEADS * tq
    units = S // LANES
    n_cmp = kvcmp.shape[3]
    return pl.pallas_call(
        _nsa_kernel,
        grid=(B, S // tq),
        in_specs=[
            pl.BlockSpec((1, tq, C_Q), lambda b, i: (b, i, 0)),
            pl.BlockSpec((1, S, 2 * LANES), lambda b, i: (b, 0, 0)),
            pl.BlockSpec((units, 2 * LANES, LANES), lambda b, i: (b, 0, 0)),
            pl.BlockSpec((1, 2, NSA_GROUPS, n_cmp, LANES), lambda b, i: (b, 0, 0, 0, 0)),
            pl.BlockSpec((1, tq, C_G), lambda b, i: (b, i, 0)),
            pl.BlockSpec((n_sel, n_cmp), lambda b, i: (0, 0)),
        ],
        out_specs=pl.BlockSpec((1, tq, C_Q), lambda b, i: (b, i, 0)),
        out_shape=jax.ShapeDtypeStruct((B, S, C_Q), BF16),
        scratch_shapes=[
            pltpu.VMEM((n_sel, cols), I32),
            pltpu.VMEM((LANES, cols), F32),
        ],
        compiler_params=_cparams(("arbitrary", "arbitrary"), 56),
    )(q, kn, vt, kvcmp, gates, selmap)


def _mem_kv_kernel(mem_ref, nw_ref, wk_ref, wv_ref, kw_ref, k_ref, v_ref):
    mn = _rms(mem_ref[0], nw_ref[...]).astype(BF16)
    k = _dot(mn, wk_ref[...])
    for h in range(X_HEADS):
        sl = slice(h * X_DH, (h + 1) * X_DH)
        k_ref[0, :, sl] = _rms(k[:, sl], kw_ref[...]).astype(BF16)
    v_ref[0] = _dot(mn, wv_ref[...]).astype(BF16)


def _mem_kv(mem, nw, wk, wv, kw):
    B, M, D = mem.shape
    blk = pl.BlockSpec((1, M, D), lambda b: (b, 0, 0))
    const = lambda b: (0, 0)
    return pl.pallas_call(
        _mem_kv_kernel,
        grid=(B,),
        in_specs=[blk, pl.BlockSpec((1, D), const), pl.BlockSpec((D, D), const),
                  pl.BlockSpec((D, D), const), pl.BlockSpec((1, X_DH), const)],
        out_specs=[blk, blk],
        out_shape=[jax.ShapeDtypeStruct((B, M, D), BF16)] * 2,
        compiler_params=_cparams(("arbitrary",), 32),
    )(mem, nw, wk, wv, kw)


def _post_mix_kernel(x_ref, yc_ref, yn_ref, woc_ref, won_ref, xnw_ref, xq_ref, xqw_ref, km_ref, vm_ref,
                     xo_ref, fnw_ref, pwq_ref, keys_ref, x2_ref, h3_ref, sc_ref):
    tm = x_ref.shape[0]
    n_streams = 2
    rows_per = tm // n_streams

    def stream(i):
        rs = slice(i * rows_per, (i + 1) * rows_per)
        x1 = x_ref[rs, :] + _dot(yc_ref[rs, :], woc_ref[...]) + _dot(yn_ref[rs, :], won_ref[...])
        yield
        h2 = _rms(x1, xnw_ref[...]).astype(BF16)
        qx = _dot(h2, xq_ref[...])
        yield
        heads = []
        for h in range(X_HEADS):
            sl = slice(h * X_DH, (h + 1) * X_DH)
            qh = _rms(qx[:, sl], xqw_ref[...]).astype(BF16)
            s = _dot_nt(qh, km_ref[0, :, sl])
            e = jnp.exp(s - jnp.max(s, axis=-1, keepdims=True))
            p = e / jnp.sum(e, axis=-1, keepdims=True)
            heads.append(_dot(p.astype(BF16), vm_ref[0, :, sl]))
        yield
        o = jnp.concatenate(heads, axis=1).astype(BF16)
        x2 = x1 + _dot(o, xo_ref[...])
        x2_ref[rs, :] = x2
        yield
        h3 = _rms(x2, fnw_ref[...]).astype(BF16)
        h3_ref[rs, :] = h3
        qp = _dot(h3, pwq_ref[...]).astype(BF16)
        for c in range(2 * PEER_HEADS):
            sl = slice(c * PEER_HALF, (c + 1) * PEER_HALF)
            for j in range(rows_per // LANES):
                sc_ref[c, i * (rows_per // LANES) + j] = _dot_nt(keys_ref[c], qp[j * LANES:(j + 1) * LANES, sl])
        yield

    first, second = stream(0), stream(1)
    next(first)
    for _ in range(4):
        next(first)
        next(second)
    next(second)


def _post_mix(x2d, yconv, ynsa, woc, won, xnw, xq, xqw, kmem, vmem, xo, fnw, pwq, keys, seq, tm=512):
    T, D = x2d.shape
    M = kmem.shape[1]
    n_sc = 2 * PEER_HEADS * PEER_NKEYS
    tile = lambda i: (i, 0)
    const = lambda i: (0, 0)
    per_batch = lambda i: ((i * tm) // seq, 0, 0)
    return pl.pallas_call(
        _post_mix_kernel,
        grid=(T // tm,),
        in_specs=[
            pl.BlockSpec((tm, D), tile),
            pl.BlockSpec((tm, CONV_WIDTH), tile),
            pl.BlockSpec((tm, C_Q), tile),
            pl.BlockSpec((CONV_WIDTH, D), const),
            pl.BlockSpec((C_Q, D), const),
            pl.BlockSpec((1, D), const),
            pl.BlockSpec((D, D), const),
            pl.BlockSpec((1, X_DH), const),
            pl.BlockSpec((1, M, D), per_batch),
            pl.BlockSpec((1, M, D), per_batch),
            pl.BlockSpec((D, D), const),
            pl.BlockSpec((1, D), const),
            pl.BlockSpec((D, n_sc), const),
            pl.BlockSpec((2 * PEER_HEADS, PEER_NKEYS, PEER_HALF), lambda i: (0, 0, 0)),
        ],
        out_specs=[pl.BlockSpec((tm, D), tile), pl.BlockSpec((tm, D), tile),
                   pl.BlockSpec((2 * PEER_HEADS, tm // LANES, PEER_NKEYS, LANES), lambda i: (0, i, 0, 0))],
        out_shape=[
            jax.ShapeDtypeStruct((T, D), F32),
            jax.ShapeDtypeStruct((T, D), BF16),
            jax.ShapeDtypeStruct((2 * PEER_HEADS, T // LANES, PEER_NKEYS, LANES), F32),
        ],
        compiler_params=_cparams(("arbitrary",), 56),
    )(x2d, yconv, ynsa, woc, won, xnw, xq, xqw, kmem, vmem, xo, fnw, pwq, keys)


STAIR = [(a, b) for a in range(PEER_TOPK) for b in range(PEER_TOPK // (a + 1))]
N_STAIR_VREGS = -(-len(STAIR) // SUBLANES)


def _stair_vregs(axis, k):
    return sorted({r // SUBLANES for r, ab in enumerate(STAIR) if ab[axis] == k})


def _stair_tables():
    rows = N_STAIR_VREGS * SUBLANES
    pad = rows - len(STAIR)
    ta = jnp.asarray([float(a) for a, _ in STAIR] + [-1.0] * pad, F32)
    tb = jnp.asarray([float(b) for _, b in STAIR] + [-1.0] * pad, F32)
    return jnp.broadcast_to(ta[:, None], (rows, LANES)), jnp.broadcast_to(tb[:, None], (rows, LANES))


def _make_topk(sc_ref, ta_ref, tb_ref, s_out, e_out):
    K = PEER_TOPK
    NV = N_STAIR_VREGS
    kidx = lax.broadcasted_iota(I32, (PEER_NKEYS, LANES), 0).astype(F32)
    ridx = [(lax.broadcasted_iota(I32, (SUBLANES, LANES), 0) + SUBLANES * j).astype(F32) for j in range(NV)]
    ta = [ta_ref[SUBLANES * j:SUBLANES * (j + 1), :] for j in range(NV)]
    tb = [tb_ref[SUBLANES * j:SUBLANES * (j + 1), :] for j in range(NV)]
    zero = jnp.zeros((SUBLANES, LANES), F32)

    def top1(s):
        m = jnp.max(s, axis=0, keepdims=True)
        idx = jnp.min(jnp.where(s == m, kidx, float(PEER_NKEYS)), axis=0, keepdims=True)
        return m, idx, jnp.where(kidx == idx, -jnp.inf, s)

    def sorted_top(s, table, axis):
        val = [zero] * NV
        key = [zero] * NV
        for k in range(K):
            m, idx, s = top1(s)
            for j in _stair_vregs(axis, k):
                hit = table[j] == float(k)
                val[j] = jnp.where(hit, m, val[j])
                key[j] = jnp.where(hit, idx, key[j])
        return val, key

    def top_half(h, g, which):
        return sorted_top(sc_ref[2 * h + which, g], (ta, tb)[which], which)

    def select(first, second):
        (v1, k1), (v2, k2) = first, second
        cand = [jnp.where(ta[j] >= 0.0, v1[j] + v2[j], -jnp.inf) for j in range(NV)]
        ce = [k1[j] * float(PEER_NKEYS) + k2[j] for j in range(NV)]
        sub = lax.broadcasted_iota(I32, (SUBLANES, LANES), 0)
        best_s = [zero] * (K // SUBLANES)
        best_e = [zero] * (K // SUBLANES)
        for k in range(K):
            mx = cand[0]
            for j in range(1, NV):
                mx = jnp.maximum(mx, cand[j])
            m = jnp.max(mx, axis=0, keepdims=True)
            ix = jnp.where(cand[0] == m, ridx[0], float(NV * SUBLANES))
            for j in range(1, NV):
                ix = jnp.minimum(ix, jnp.where(cand[j] == m, ridx[j], float(NV * SUBLANES)))
            idx = jnp.min(ix, axis=0, keepdims=True)
            hits = [ridx[j] == idx for j in range(NV)]
            es = jnp.where(hits[0], ce[0], 0.0)
            for j in range(1, NV):
                es = es + jnp.where(hits[j], ce[j], 0.0)
            cand = [jnp.where(hits[j], -jnp.inf, cand[j]) for j in range(NV)]
            here = sub == (k % SUBLANES)
            best_s[k // SUBLANES] = jnp.where(here, m, best_s[k // SUBLANES])
            best_e[k // SUBLANES] = jnp.where(here, jnp.sum(es, axis=0, keepdims=True), best_e[k // SUBLANES])
        return jnp.concatenate(best_s, axis=0), jnp.concatenate(best_e, axis=0)

    def store(h, best):
        rows = pl.ds(pl.multiple_of(h * K, K), K)
        s_out[rows, :] = best[0]
        e_out[rows, :] = best[1]

    def finish(rows, ei_dst, ej_dst, gate_dst):
        s_all = s_out[...]
        gates = []
        for h in range(PEER_HEADS):
            sh = s_all[h * K:(h + 1) * K]
            ex = jnp.exp(sh - sh[0:1])
            gates.append(ex / jnp.sum(ex, axis=0, keepdims=True))
        gate_dst[rows, :] = jnp.concatenate(gates, axis=0).T
        e_int = e_out[...].T.astype(I32)
        ei_dst[rows, :] = jnp.right_shift(e_int, PEER_NKEYS.bit_length() - 1)
        ej_dst[rows, :] = jnp.bitwise_and(e_int, PEER_NKEYS - 1)

    return top_half, select, store, finish


G_ROWS = PEER_NKEYS // 2
G_PITCH = G_ROWS + SUBLANES
HI_MASK = 0xFFFF0000


def _peer_kernel(sc_ref, ta_ref, tb_ref, h_ref, wd_ref, wu_ref, x2_ref, out_ref,
                 g_ref, ei_ref, ej_ref, gate_ref, s_out, e_out):
    tm = h_ref.shape[0]
    ec = wd_ref.shape[1]
    ipc = ec // LANES
    half = ipc // 2
    t = pl.program_id(0)
    c = pl.program_id(1)
    n_grp = tm // LANES
    steps_per_grp = PEER_EXPERTS // ec // n_grp
    heads_per_step = PEER_HEADS // steps_per_grp
    rd = (t + 1) % 2
    wr = t % 2
    ei_rd, ej_rd, gate_rd = ei_ref.at[rd], ej_ref.at[rd], gate_ref.at[rd]

    @pl.when((t == 0) & (c == 0))
    def _first_step():
        ei_ref[...] = jnp.zeros(ei_ref.shape, I32)
        ej_ref[...] = jnp.zeros(ej_ref.shape, I32)
        gate_ref[...] = jnp.zeros(gate_ref.shape, F32)

    @pl.when(c == 0)
    def _scatter():
        out_ref[...] = x2_ref[...]
        as_bf16 = lambda v: v.astype(F32).astype(BF16)
        i_of_m = as_bf16(lax.broadcasted_iota(I32, (PEER_NKEYS, 2 * LANES), 0))
        n = as_bf16(lax.broadcasted_iota(I32, (LANES, LANES), 0))
        one = jnp.ones((LANES, LANES), BF16)
        zero = jnp.zeros((LANES, LANES), BF16)

        def pair(u, carry):
            ta_, tb_ = pl.ds(2 * u, 1), pl.ds(2 * u + 1, 1)
            ri = as_bf16(jnp.concatenate([ei_rd[ta_, :], ei_rd[tb_, :]], axis=1))
            rg = (0.5 * jnp.concatenate([gate_rd[ta_, :], gate_rd[tb_, :]], axis=1)).astype(BF16)
            c_t = jnp.where(i_of_m == ri, rg, jnp.zeros_like(rg))
            q_a = jnp.where(n == as_bf16(ej_rd[ta_, :]), one, zero)
            q_b = jnp.where(n == as_bf16(ej_rd[tb_, :]), one, zero)
            q_t = jnp.concatenate([jnp.concatenate([q_a, zero], axis=1),
                                   jnp.concatenate([zero, q_b], axis=1)], axis=0)
            g2 = _dot_nt(c_t, q_t).astype(BF16)
            for tok in range(2):
                word = pltpu.bitcast(g2[:, tok * LANES:(tok + 1) * LANES], jnp.uint32)
                g_ref[pl.ds(pl.multiple_of((2 * u + tok) * G_PITCH, SUBLANES), G_ROWS), :] = word
            return carry

        lax.fori_loop(0, tm // 2, pair, 0, unroll=16)

    top_half, select, store, finish = _make_topk(sc_ref, ta_ref, tb_ref, s_out, e_out)
    grp = c // steps_per_grp
    h0 = (c % steps_per_grp) * heads_per_step

    def experts_slice(k):
        a = _dot(h_ref[...], wd_ref[:, 2 * k * LANES:2 * (k + 1) * LANES]).astype(BF16)
        word = g_ref[pl.ds(c * half + k, tm, stride=G_PITCH), :]
        g_lo = lax.bitcast_convert_type(jnp.left_shift(word, 16), F32).astype(BF16)
        g_hi = lax.bitcast_convert_type(jnp.bitwise_and(word, jnp.uint32(HI_MASK)), F32).astype(BF16)
        return jnp.concatenate([_gelu2(a[:, 0:LANES]) * g_lo, _gelu2(a[:, LANES:2 * LANES]) * g_hi], axis=1)

    assert half == 2 * heads_per_step
    halves, z = [], []
    for k in range(half):
        halves.append(top_half(h0 + k // 2, grp, k % 2))
        z.append(experts_slice(k))
    best = [select(halves[2 * hh], halves[2 * hh + 1]) for hh in range(heads_per_step - 1)]
    out_ref[...] += _dot(jnp.concatenate(z, axis=1), wu_ref[...])
    best.append(select(halves[-2], halves[-1]))
    for hh in range(heads_per_step):
        store(h0 + hh, best[hh])

    @pl.when(c % steps_per_grp == steps_per_grp - 1)
    def _group_done():
        rows = pl.ds(pl.multiple_of(grp * LANES, LANES), LANES)
        finish(rows, ei_ref.at[wr], ej_ref.at[wr], gate_ref.at[wr])


def _peer(sc, h3, wd_t, wu, x2, tm=512, ec=2048):
    T, D = h3.shape
    tm = min(tm, T)
    n_tiles = T // tm
    n_grp = tm // LANES
    n_chunks = wu.shape[0] // ec
    assert n_chunks % n_grp == 0 and PEER_HEADS % (n_chunks // n_grp) == 0
    ta, tb = _stair_tables()
    prev = lambda t, c: (jnp.maximum(t - 1, 0), 0)
    return pl.pallas_call(
        _peer_kernel,
        grid=(n_tiles + 1, n_chunks),
        in_specs=[
            pl.BlockSpec((2 * PEER_HEADS, n_grp, PEER_NKEYS, LANES),
                         lambda t, c: (0, jnp.minimum(t, n_tiles - 1), 0, 0)),
            pl.BlockSpec(ta.shape, lambda t, c: (0, 0)),
            pl.BlockSpec(tb.shape, lambda t, c: (0, 0)),
            pl.BlockSpec((tm, D), prev),
            pl.BlockSpec((D, ec), lambda t, c: (0, c)),
            pl.BlockSpec((ec, D), lambda t, c: (c, 0)),
            pl.BlockSpec((tm, D), prev),
        ],
        out_specs=pl.BlockSpec((tm, D), prev),
        out_shape=jax.ShapeDtypeStruct((T, D), F32),
        scratch_shapes=[
            pltpu.VMEM((tm * G_PITCH, LANES), jnp.uint32),
            pltpu.VMEM((2, tm, LANES), I32),
            pltpu.VMEM((2, tm, LANES), I32),
            pltpu.VMEM((2, tm, LANES), F32),
            pltpu.VMEM((PEER_HEADS * PEER_TOPK, LANES), F32),
            pltpu.VMEM((PEER_HEADS * PEER_TOPK, LANES), F32),
        ],
        compiler_params=_cparams(("arbitrary", "arbitrary"), 60),
    )(sc, ta, tb, h3, wd_t, wu, x2)


def _pad_half(a, g):
    z = jnp.zeros_like(a)
    return jnp.concatenate([a, z] if g == 0 else [z, a], axis=-1)


def kernel(x, mem, mix_norm_w, w_in, conv_w, conv_b, cmp_pe, cmp_w1, cmp_w2, q_norm_w, k_norm_w, w_out,
           xattn_norm_w, mem_norm_w, xq, xk, xv, xo, xq_norm_w, xk_norm_w, ffn_norm_w, peer_wq, peer_keys,
           peer_down, peer_up):
    B, S, D = x.shape
    T = B * S
    l = 0
    G, R, dh = NSA_GROUPS, NSA_REP, NSA_DH

    w = w_in[l]
    o_q = C_CONV
    o_kv = o_q + NSA_HEADS * dh
    o_g = o_kv + C_KV
    wq = w[:, o_q:o_kv].reshape(D, G, R, dh)
    wq_pad = jnp.concatenate([_pad_half(wq[:, g], g).reshape(D, R * LANES) for g in range(G)], axis=1)
    wg = w[:, o_g:].reshape(D, G, R * 3)
    wg_pad = jnp.pad(wg, ((0, 0), (0, 0), (0, LANES - R * 3))).reshape(D, G * LANES)
    wkv = w[:, o_kv:o_g].reshape(D, 6, LANES)
    wkn = wkv[:, jnp.array([0, 1, 2, 4])].reshape(D, C_KN)
    wvt = wkv[:, jnp.array([3, 5])].reshape(D, 2 * LANES).T.astype(BF16)
    w_all = jnp.concatenate([w[:, :o_q], wq_pad, wkn, wg_pad], axis=1).astype(BF16)
    qw = q_norm_w[l] * dh ** -0.5
    qw_pad = jnp.concatenate([jnp.tile(_pad_half(qw, g), R) for g in range(G)]).reshape(1, C_Q)
    kw = jnp.stack([jnp.tile(k_norm_w[l, 1], G), jnp.tile(k_norm_w[l, 2], G)])

    yconv, q, kvc, kn, vt, gates = _in_proj(
        x.reshape(T, D), mix_norm_w[l].reshape(1, D), w_all, wvt, conv_w[l], conv_b[l].reshape(1, CONV_WIDTH),
        qw_pad, kw, S)

    n_rows = S // CMP_STRIDE
    kvg = jnp.array([0, 0, 1, 1])
    pe = jnp.concatenate([cmp_pe[l, kv] for kv in (0, 0, 1, 1)], axis=-1)
    w1 = cmp_w1[l].reshape(2, CMP_BLOCK, dh, dh)[kvg]
    w1 = jnp.einsum('qp,qjdh->jqdph', jnp.eye(4, dtype=F32), w1).reshape(CMP_BLOCK, 4 * dh, 4 * dh).astype(BF16)
    w2k = jnp.zeros((4 * dh, G * LANES), F32)
    w2vt = jnp.zeros((G * LANES, 4 * dh), F32)
    for g in range(G):
        o = g * LANES + g * dh
        w2k = w2k.at[g * dh:(g + 1) * dh, o:o + dh].set(cmp_w2[l, 0])
        w2vt = w2vt.at[o:o + dh, (G + g) * dh:(G + g + 1) * dh].set(cmp_w2[l, 1].T)
    knw = jnp.stack([_pad_half(k_norm_w[l, 0], g) for g in range(G)]).reshape(G, 1, LANES)
    kvcmp = _compress(kvc, pe, w1, w2k.astype(BF16), w2vt.astype(BF16), knw, S)

    n_sel = S // SEL_BLOCK
    cmp_start = jnp.arange(n_rows) * CMP_STRIDE
    sel_start = jnp.arange(n_sel) * SEL_BLOCK
    selmap_t = ((cmp_start[None, :] < sel_start[:, None] + SEL_BLOCK)
                & (cmp_start[None, :] + CMP_BLOCK > sel_start[:, None])
                & (jnp.arange(n_rows)[None, :] < n_rows - 1)).astype(BF16)
    ynsa = _nsa(q.reshape(B, S, C_Q), kn.reshape(B, S, 2 * LANES), vt, kvcmp, gates.reshape(B, S, C_G), selmap_t)

    kmem, vmem = _mem_kv(mem, mem_norm_w[l].reshape(1, D), xk[l].astype(BF16), xv[l].astype(BF16),
                         xk_norm_w[l].reshape(1, X_DH))

    wo = w_out[l]
    won = wo[CONV_WIDTH:].reshape(G, R, dh, D)
    won_pad = jnp.concatenate(
        [jnp.concatenate([won[g], jnp.zeros_like(won[g])] if g == 0 else [jnp.zeros_like(won[g]), won[g]],
                         axis=1).reshape(R * LANES, D) for g in range(G)], axis=0).astype(BF16)
    keys = peer_keys[l].reshape(2 * PEER_HEADS, PEER_NKEYS, PEER_HALF).astype(BF16)
    x2, h3, sc = _post_mix(
        x.reshape(T, D), yconv, ynsa.reshape(T, C_Q), wo[:CONV_WIDTH].astype(BF16), won_pad,
        xattn_norm_w[l].reshape(1, D), xq[l].astype(BF16),
        (xq_norm_w[l] * X_DH ** -0.5).reshape(1, X_DH), kmem, vmem, xo[l].astype(BF16),
        ffn_norm_w[l].reshape(1, D), peer_wq[l].astype(BF16), keys, S)

    out = _peer(sc, h3, peer_down[l].T.astype(BF16), peer_up[l].astype(BF16), x2)
    return out.reshape(B, S, D)
```

```python
import functools

import jax
import jax.numpy as jnp
from jax import lax
from jax.experimental import pallas as pl
from jax.experimental.pallas import tpu as pltpu

F32 = jnp.float32
BF16 = jnp.bfloat16
I32 = jnp.int32

EPS = 1e-6
NEG = -1e30
LANES = 128
SUBLANES = 8

D_MODEL = 1024
CONV_WIDTH = 512
NSA_HEADS = 8
NSA_GROUPS = 2
NSA_REP = NSA_HEADS // NSA_GROUPS
NSA_DH = 64
CMP_BLOCK = 32
CMP_STRIDE = 16
SEL_BLOCK = 64
SEL_TOPN = 16
WINDOW = 512
X_HEADS = 4
X_DH = D_MODEL // X_HEADS
PEER_HEADS = 8
PEER_NKEYS = 128
PEER_HALF = 128
PEER_TOPK = 16
PEER_EXPERTS = PEER_NKEYS * PEER_NKEYS

NT_DIMS = (((1,), (1,)), ((), ()))


def _dot(a, b):
    return jnp.dot(a, b, preferred_element_type=F32)


def _dot_nt(a, b):
    return lax.dot_general(a, b, NT_DIMS, preferred_element_type=F32)


def _gelu2(x):
    return x * (1.0 + lax.erf(x * (2.0 ** -0.5)))


def _gelu(x):
    return 0.5 * _gelu2(x)


def _rms(x, w):
    ms = jnp.mean(x * x, axis=-1, keepdims=True)
    return x * lax.rsqrt(ms + EPS) * w


def _cparams(sem, vmem_mb):
    return pltpu.CompilerParams(dimension_semantics=sem, vmem_limit_bytes=vmem_mb * 1024 * 1024)


C_CONV = 3 * CONV_WIDTH
C_Q = NSA_HEADS * LANES
C_KV = 6 * NSA_GROUPS * NSA_DH
C_KN = 4 * LANES
C_G = NSA_GROUPS * LANES
C_ALL = C_CONV + C_Q + C_KN + C_G


def _in_proj_kernel(tiles_per_seq, x_ref, nw_ref, w_ref, wvt_ref, cw_ref, cb_ref, qw_ref, kw_ref,
                    yconv_ref, q_ref, kvc_ref, kn_ref, vt_ref, g_ref, carry_ref):
    tm = x_ref.shape[0]
    n_streams = 2
    rows = tm // n_streams

    @pl.when((pl.program_id(0) % tiles_per_seq) == 0)
    def _sequence_start():
        carry_ref[...] = jnp.zeros(carry_ref.shape, F32)

    tails = {}

    def stream(i):
        rs = slice(i * rows, (i + 1) * rows)
        hn = _rms(x_ref[rs, :], nw_ref[...]).astype(BF16)
        yield
        p = _dot(hn, w_ref[:, 0:C_CONV])
        b_g = p[:, 0:CONV_WIDTH]
        u = p[:, CONV_WIDTH:2 * CONV_WIDTH] * p[:, 2 * CONV_WIDTH:3 * CONV_WIDTH]
        prev = carry_ref[...] if i == 0 else tails[i - 1]
        p1 = prev[SUBLANES - 1:SUBLANES, :]
        p2 = prev[SUBLANES - 2:SUBLANES - 1, :]
        row = lax.broadcasted_iota(I32, u.shape, 0)
        u1 = jnp.where(row == 0, p1, pltpu.roll(u, 1, axis=0))
        u2 = jnp.where(row == 0, p2, jnp.where(row == 1, p1, pltpu.roll(u, 2, axis=0)))
        tails[i] = u[rows - SUBLANES:rows, :]
        if i == n_streams - 1:
            carry_ref[...] = tails[i]
        cw = cw_ref[...]
        y = b_g * (cw[0:1, :] * u2 + cw[1:2, :] * u1 + cw[2:3, :] * u + cb_ref[...])
        yconv_ref[rs, :] = y.astype(BF16)
        yield
        pq = _dot(hn, w_ref[:, C_CONV:C_CONV + C_Q])
        for h in range(NSA_HEADS):
            blk = pq[:, h * LANES:(h + 1) * LANES]
            ms = jnp.sum(blk * blk, axis=-1, keepdims=True) * (1.0 / NSA_DH)
            q_ref[rs, h * LANES:(h + 1) * LANES] = (
                blk * lax.rsqrt(ms + EPS) * qw_ref[:, h * LANES:(h + 1) * LANES]).astype(BF16)
        yield
        pkv = _dot(hn, w_ref[:, C_CONV + C_Q:C_CONV + C_Q + C_KN])
        kvc_ref[0, rs, :] = pkv[:, 0:LANES]
        kvc_ref[1, rs, :] = pkv[:, LANES:2 * LANES]
        lane = lax.broadcasted_iota(I32, (rows, LANES), 1)
        lo = lane < NSA_DH
        for j in range(2):
            blk = pkv[:, (2 + j) * LANES:(3 + j) * LANES]
            sq = blk * blk
            ms_lo = jnp.sum(jnp.where(lo, sq, 0.0), axis=-1, keepdims=True) * (1.0 / NSA_DH)
            ms_hi = jnp.sum(jnp.where(lo, 0.0, sq), axis=-1, keepdims=True) * (1.0 / NSA_DH)
            scale = jnp.where(lo, lax.rsqrt(ms_lo + EPS), lax.rsqrt(ms_hi + EPS))
            kn_ref[rs, j * LANES:(j + 1) * LANES] = (blk * scale * kw_ref[j:j + 1, :]).astype(BF16)
        yield
        for j in range(rows // LANES):
            vt_ref[i * (rows // LANES) + j] = _dot_nt(wvt_ref[...], hn[j * LANES:(j + 1) * LANES, :]).astype(BF16)
        pg = _dot(hn, w_ref[:, C_CONV + C_Q + C_KN:C_ALL])
        g_ref[rs, :] = jax.nn.sigmoid(pg)
        yield

    first, second = stream(0), stream(1)
    next(first)
    for _ in range(4):
        next(first)
        next(second)
    next(second)


def _in_proj(x2d, nw, w_all, wvt, cw, cb, qw, kw, seq, tm=512):
    T = x2d.shape[0]
    const = lambda i: (0, 0)
    tile = lambda i: (i, 0)
    return pl.pallas_call(
        functools.partial(_in_proj_kernel, seq // tm),
        grid=(T // tm,),
        in_specs=[
            pl.BlockSpec((tm, D_MODEL), tile),
            pl.BlockSpec((1, D_MODEL), const),
            pl.BlockSpec((D_MODEL, C_ALL), const),
            pl.BlockSpec((2 * LANES, D_MODEL), const),
            pl.BlockSpec((3, CONV_WIDTH), const),
            pl.BlockSpec((1, CONV_WIDTH), const),
            pl.BlockSpec((1, C_Q), const),
            pl.BlockSpec((2, LANES), const),
        ],
        out_specs=[
            pl.BlockSpec((tm, CONV_WIDTH), tile),
            pl.BlockSpec((tm, C_Q), tile),
            pl.BlockSpec((2, tm, LANES), lambda i: (0, i, 0)),
            pl.BlockSpec((tm, 2 * LANES), tile),
            pl.BlockSpec((tm // LANES, 2 * LANES, LANES), lambda i: (i, 0, 0)),
            pl.BlockSpec((tm, C_G), tile),
        ],
        out_shape=[
            jax.ShapeDtypeStruct((T, CONV_WIDTH), BF16),
            jax.ShapeDtypeStruct((T, C_Q), BF16),
            jax.ShapeDtypeStruct((2, T, LANES), F32),
            jax.ShapeDtypeStruct((T, 2 * LANES), BF16),
            jax.ShapeDtypeStruct((T // LANES, 2 * LANES, LANES), BF16),
            jax.ShapeDtypeStruct((T, C_G), F32),
        ],
        scratch_shapes=[pltpu.VMEM((SUBLANES, CONV_WIDTH), F32)],
        compiler_params=_cparams(("arbitrary",), 48),
    )(x2d, nw, w_all, wvt, cw, cb, qw, kw)


def _compress_kernel(kvc_ref, pe_ref, w1_ref, w2k_ref, w2vt_ref, knw_ref, out_ref):
    n_rows = out_ref.shape[3]
    acc_a = jnp.zeros((n_rows, 2 * LANES), F32)
    acc_b = jnp.zeros((n_rows, 2 * LANES), F32)
    for l in range(CMP_STRIDE):
        rows = pl.ds(l, n_rows, stride=CMP_STRIDE)
        x = jnp.concatenate([kvc_ref[0, rows, :], kvc_ref[1, rows, :]], axis=1)
        acc_a = acc_a + _dot((x + pe_ref[l:l + 1, :]).astype(BF16), w1_ref[l])
        acc_b = acc_b + _dot((x + pe_ref[CMP_STRIDE + l:CMP_STRIDE + l + 1, :]).astype(BF16),
                             w1_ref[CMP_STRIDE + l])
    hid = _gelu(acc_a + pltpu.roll(acc_b, n_rows - 1, axis=0)).astype(BF16)
    row = lax.broadcasted_iota(I32, (n_rows, LANES), 0)
    col = lax.broadcasted_iota(I32, (LANES, n_rows), 1)
    keys = _dot(hid, w2k_ref[...])
    vals = _dot_nt(w2vt_ref[...], hid)
    for g in range(NSA_GROUPS):
        out = keys[:, g * LANES:(g + 1) * LANES]
        ms = jnp.sum(out * out, axis=-1, keepdims=True) * (1.0 / NSA_DH)
        out = out * lax.rsqrt(ms + EPS) * knw_ref[g]
        out_ref[0, 0, g] = jnp.where(row < n_rows - 1, out, 0.0).astype(BF16)
        out_ref[0, 1, g] = jnp.where(col < n_rows - 1, vals[g * LANES:(g + 1) * LANES, :], 0.0).astype(BF16)


def _compress(kvc, pe, w1, w2k, w2vt, knw, S):
    B = kvc.shape[1] // S
    n_rows = S // CMP_STRIDE
    return pl.pallas_call(
        _compress_kernel,
        grid=(B,),
        in_specs=[
            pl.BlockSpec((2, S, LANES), lambda b: (0, b, 0)),
            pl.BlockSpec((CMP_BLOCK, 2 * LANES), lambda b: (0, 0)),
            pl.BlockSpec((CMP_BLOCK, 2 * LANES, 2 * LANES), lambda b: (0, 0, 0)),
            pl.BlockSpec((2 * LANES, 2 * LANES), lambda b: (0, 0)),
            pl.BlockSpec((2 * LANES, 2 * LANES), lambda b: (0, 0)),
            pl.BlockSpec((NSA_GROUPS, 1, LANES), lambda b: (0, 0, 0)),
        ],
        out_specs=pl.BlockSpec((1, 2, NSA_GROUPS, n_rows, LANES), lambda b: (b, 0, 0, 0, 0)),
        out_shape=jax.ShapeDtypeStruct((B, 2, NSA_GROUPS, n_rows, LANES), BF16),
        compiler_params=_cparams(("arbitrary",), 32),
    )(kvc, pe, w1, w2k, w2vt, knw)


SEL_CHUNK = 512


def _nsa_kernel(q_ref, kn_ref, vt_ref, cmp_ref, g_ref, selmap_ref, y_ref, sel_ref, acc_ref):
    tq = q_ref.shape[1]
    cols = NSA_HEADS * tq
    i = pl.program_id(1)
    t0 = i * tq
    qb = q_ref[0]
    qs = jnp.concatenate([qb[:, h * LANES:(h + 1) * LANES] for h in range(NSA_HEADS)], axis=0)
    qpos = t0 + jnp.bitwise_and(lax.broadcasted_iota(I32, (1, cols), 1), tq - 1)
    both = lambda kv: (cmp_ref[0, kv, 0].astype(F32) + cmp_ref[0, kv, 1].astype(F32)).astype(BF16)

    n_win = WINDOW + tq
    w_start = pl.multiple_of(jnp.maximum(t0 - WINDOW, 0), LANES)
    w_unit = w_start // LANES
    k_w = kn_ref[0, pl.ds(w_start, n_win), LANES:2 * LANES]
    vt_w = jnp.concatenate([vt_ref[w_unit + u, LANES:2 * LANES, :] for u in range(n_win // LANES)], axis=1)
    diff = (qpos - w_start) - lax.broadcasted_iota(I32, (n_win, cols), 0)
    in_window = lax.bitcast_convert_type(diff, jnp.uint32) < jnp.uint32(WINDOW)
    s_w = jnp.where(in_window, _dot_nt(k_w, qs), NEG)
    p_w = jnp.exp(s_w - jnp.max(s_w, axis=0, keepdims=True))
    l_w = jnp.sum(p_w, axis=0, keepdims=True)
    o_w = _dot(vt_w, p_w.astype(BF16))

    n_cmp = cmp_ref.shape[3]
    s = _dot_nt(both(0), qs)
    cmp_last = lax.broadcasted_iota(I32, (n_cmp, cols), 0) * CMP_STRIDE + (CMP_BLOCK - 1)
    vis_c = cmp_last <= qpos
    s = jnp.where(vis_c, s, NEG)
    m = jnp.max(s, axis=0, keepdims=True)
    m = jnp.where(m > 0.5 * NEG, m, 0.0)
    e = jnp.where(vis_c, jnp.exp(s - m), 0.0)
    d = jnp.sum(e, axis=0, keepdims=True)
    p_c = e / jnp.where(d > 0.0, d, 1.0)
    o_c = _dot(both(1), p_c.astype(BF16))

    n_sel = selmap_ref.shape[0]
    blk = lax.broadcasted_iota(I32, (n_sel, tq), 0)
    cur = jnp.right_shift(t0 + lax.broadcasted_iota(I32, (n_sel, tq), 1),
                          SEL_BLOCK.bit_length() - 1)
    forced = (blk == 0) | (blk == cur) | (blk == cur - 1)
    chosen = []
    for g in range(NSA_GROUPS):
        psum = p_c[:, g * NSA_REP * tq:(g * NSA_REP + 1) * tq]
        for r in range(1, NSA_REP):
            psum = psum + p_c[:, (g * NSA_REP + r) * tq:(g * NSA_REP + r + 1) * tq]
        p_hi = psum.astype(BF16)
        p_lo = (psum - p_hi.astype(F32)).astype(BF16)
        imp = _dot(selmap_ref[...], p_hi) + _dot(selmap_ref[...], p_lo)
        imp = jnp.where(blk > cur, -jnp.inf, jnp.where(forced, jnp.inf, imp))
        rank = jnp.zeros((n_sel, tq), I32)
        for jp in range(n_sel):
            other = imp[jp:jp + 1, :]
            beats = (other > imp) | ((other == imp) & (blk > jp))
            rank = rank + beats.astype(I32)
        chosen += [rank < min(SEL_TOPN, n_sel)] * NSA_REP
    sel_ref[...] = jnp.where(jnp.concatenate(chosen, axis=1), qpos, -1)

    acc_ref[...] = jnp.zeros(acc_ref.shape, F32)
    units = SEL_CHUNK // LANES
    blocks = SEL_CHUNK // SEL_BLOCK
    koff = lax.broadcasted_iota(I32, (SEL_BLOCK, cols), 0)

    def sel_chunk(c, carry):
        m_old, l_old = carry
        start = pl.multiple_of(c * SEL_CHUNK, SEL_CHUNK)
        k = kn_ref[0, pl.ds(start, SEL_CHUNK), 0:LANES]
        vt = jnp.concatenate([vt_ref[c * units + u, 0:LANES, :] for u in range(units)], axis=1)
        sc = _dot_nt(k, qs)
        parts = []
        for b in range(blocks):
            limit = sel_ref[pl.ds(c * blocks + b, 1), :] - (start + b * SEL_BLOCK)
            parts.append(jnp.where(koff <= limit, sc[b * SEL_BLOCK:(b + 1) * SEL_BLOCK], NEG))
        sc = jnp.concatenate(parts, axis=0)
        m_new = jnp.maximum(m_old, jnp.max(sc, axis=0, keepdims=True))
        alpha = jnp.exp(m_old - m_new)
        p = jnp.exp(sc - m_new)
        acc_ref[...] = alpha * acc_ref[...] + _dot(vt, p.astype(BF16))
        return m_new, alpha * l_old + jnp.sum(p, axis=0, keepdims=True)

    init = (jnp.full((1, cols), NEG, F32), jnp.zeros((1, cols), F32))
    _, l_s = lax.fori_loop(0, (t0 + tq + SEL_CHUNK - 1) // SEL_CHUNK, sel_chunk, init)
    o_s = acc_ref[...]

    g_t = [g_ref[0, :, g * LANES:(g + 1) * LANES].T for g in range(NSA_GROUPS)]
    gate = lambda br: jnp.concatenate(
        [g_t[h // NSA_REP][3 * (h % NSA_REP) + br:3 * (h % NSA_REP) + br + 1, :] for h in range(NSA_HEADS)], axis=1)
    y_t = o_c * gate(0) + o_s * (gate(1) / l_s) + o_w * (gate(2) / l_w)
    for h in range(NSA_HEADS):
        y_ref[0, :, h * LANES:(h + 1) * LANES] = y_t[:, h * tq:(h + 1) * tq].T.astype(BF16)


def _nsa(q, kn, vt, kvcmp, gates, selmap, tq=256):
    B, S = q.shape[0], q.shape[1]
    n_sel = S // SEL_BLOCK
    cols = NSA_HEADS * tq
    units = S // LANES
    n_cmp = kvcmp.shape[3]
    return pl.pallas_call(
        _nsa_kernel,
        grid=(B, S // tq),
        in_specs=[
            pl.BlockSpec((1, tq, C_Q), lambda b, i: (b, i, 0)),
            pl.BlockSpec((1, S, 2 * LANES), lambda b, i: (b, 0, 0)),
            pl.BlockSpec((units, 2 * LANES, LANES), lambda b, i: (b, 0, 0)),
            pl.BlockSpec((1, 2, NSA_GROUPS, n_cmp, LANES), lambda b, i: (b, 0, 0, 0, 0)),
            pl.BlockSpec((1, tq, C_G), lambda b, i: (b, i, 0)),
            pl.BlockSpec((n_sel, n_cmp), lambda b, i: (0, 0)),
        ],
        out_specs=pl.BlockSpec((1, tq, C_Q), lambda b, i: (b, i, 0)),
        out_shape=jax.ShapeDtypeStruct((B, S, C_Q), BF16),
        scratch_shapes=[
            pltpu.VMEM((n_sel, cols), I32),
            pltpu.VMEM((LANES, cols), F32),
        ],
        compiler_params=_cparams(("arbitrary", "arbitrary"), 56),
    )(q, kn, vt, kvcmp, gates, selmap)


def _mem_kv_kernel(mem_ref, nw_ref, wk_ref, wv_ref, kw_ref, k_ref, v_ref):
    mn = _rms(mem_ref[0], nw_ref[...]).astype(BF16)
    k = _dot(mn, wk_ref[...])
    for h in range(X_HEADS):
        sl = slice(h * X_DH, (h + 1) * X_DH)
        k_ref[0, :, sl] = _rms(k[:, sl], kw_ref[...]).astype(BF16)
    v_ref[0] = _dot(mn, wv_ref[...]).astype(BF16)


def _mem_kv(mem, nw, wk, wv, kw):
    B, M, D = mem.shape
    blk = pl.BlockSpec((1, M, D), lambda b: (b, 0, 0))
    const = lambda b: (0, 0)
    return pl.pallas_call(
        _mem_kv_kernel,
        grid=(B,),
        in_specs=[blk, pl.BlockSpec((1, D), const), pl.BlockSpec((D, D), const),
                  pl.BlockSpec((D, D), const), pl.BlockSpec((1, X_DH), const)],
        out_specs=[blk, blk],
        out_shape=[jax.ShapeDtypeStruct((B, M, D), BF16)] * 2,
        compiler_params=_cparams(("arbitrary",), 32),
    )(mem, nw, wk, wv, kw)


def _post_mix_kernel(x_ref, yc_ref, yn_ref, woc_ref, won_ref, xnw_ref, xq_ref, xqw_ref, km_ref, vm_ref,
                     xo_ref, fnw_ref, pwq_ref, keys_ref, x2_ref, h3_ref, sc_ref):
    tm = x_ref.shape[0]
    n_streams = 2
    rows_per = tm // n_streams

    def stream(i):
        rs = slice(i * rows_per, (i + 1) * rows_per)
        x1 = x_ref[rs, :] + _dot(yc_ref[rs, :], woc_ref[...]) + _dot(yn_ref[rs, :], won_ref[...])
        yield
        h2 = _rms(x1, xnw_ref[...]).astype(BF16)
        qx = _dot(h2, xq_ref[...])
        yield
        heads = []
        for h in range(X_HEADS):
            sl = slice(h * X_DH, (h + 1) * X_DH)
            qh = _rms(qx[:, sl], xqw_ref[...]).astype(BF16)
            s = _dot_nt(qh, km_ref[0, :, sl])
            e = jnp.exp(s - jnp.max(s, axis=-1, keepdims=True))
            p = e / jnp.sum(e, axis=-1, keepdims=True)
            heads.append(_dot(p.astype(BF16), vm_ref[0, :, sl]))
        yield
        o = jnp.concatenate(heads, axis=1).astype(BF16)
        x2 = x1 + _dot(o, xo_ref[...])
        x2_ref[rs, :] = x2
        yield
        h3 = _rms(x2, fnw_ref[...]).astype(BF16)
        h3_ref[rs, :] = h3
        qp = _dot(h3, pwq_ref[...]).astype(BF16)
        for c in range(2 * PEER_HEADS):
            sl = slice(c * PEER_HALF, (c + 1) * PEER_HALF)
            for j in range(rows_per // LANES):
                sc_ref[c, i * (rows_per // LANES) + j] = _dot_nt(keys_ref[c], qp[j * LANES:(j + 1) * LANES, sl])
        yield

    first, second = stream(0), stream(1)
    next(first)
    for _ in range(4):
        next(first)
        next(second)
    next(second)


def _post_mix(x2d, yconv, ynsa, woc, won, xnw, xq, xqw, kmem, vmem, xo, fnw, pwq, keys, seq, tm=512):
    T, D = x2d.shape
    M = kmem.shape[1]
    n_sc = 2 * PEER_HEADS * PEER_NKEYS
    tile = lambda i: (i, 0)
    const = lambda i: (0, 0)
    per_batch = lambda i: ((i * tm) // seq, 0, 0)
    return pl.pallas_call(
        _post_mix_kernel,
        grid=(T // tm,),
        in_specs=[
            pl.BlockSpec((tm, D), tile),
            pl.BlockSpec((tm, CONV_WIDTH), tile),
            pl.BlockSpec((tm, C_Q), tile),
            pl.BlockSpec((CONV_WIDTH, D), const),
            pl.BlockSpec((C_Q, D), const),
            pl.BlockSpec((1, D), const),
            pl.BlockSpec((D, D), const),
            pl.BlockSpec((1, X_DH), const),
            pl.BlockSpec((1, M, D), per_batch),
            pl.BlockSpec((1, M, D), per_batch),
            pl.BlockSpec((D, D), const),
            pl.BlockSpec((1, D), const),
            pl.BlockSpec((D, n_sc), const),
            pl.BlockSpec((2 * PEER_HEADS, PEER_NKEYS, PEER_HALF), lambda i: (0, 0, 0)),
        ],
        out_specs=[pl.BlockSpec((tm, D), tile), pl.BlockSpec((tm, D), tile),
                   pl.BlockSpec((2 * PEER_HEADS, tm // LANES, PEER_NKEYS, LANES), lambda i: (0, i, 0, 0))],
        out_shape=[
            jax.ShapeDtypeStruct((T, D), F32),
            jax.ShapeDtypeStruct((T, D), BF16),
            jax.ShapeDtypeStruct((2 * PEER_HEADS, T // LANES, PEER_NKEYS, LANES), F32),
        ],
        compiler_params=_cparams(("arbitrary",), 56),
    )(x2d, yconv, ynsa, woc, won, xnw, xq, xqw, kmem, vmem, xo, fnw, pwq, keys)


STAIR = [(a, b) for a in range(PEER_TOPK) for b in range(PEER_TOPK // (a + 1))]
N_STAIR_VREGS = -(-len(STAIR) // SUBLANES)


def _stair_vregs(axis, k):
    return sorted({r // SUBLANES for r, ab in enumerate(STAIR) if ab[axis] == k})


def _stair_tables():
    rows = N_STAIR_VREGS * SUBLANES
    pad = rows - len(STAIR)
    ta = jnp.asarray([float(a) for a, _ in STAIR] + [-1.0] * pad, F32)
    tb = jnp.asarray([float(b) for _, b in STAIR] + [-1.0] * pad, F32)
    return jnp.broadcast_to(ta[:, None], (rows, LANES)), jnp.broadcast_to(tb[:, None], (rows, LANES))


def _make_topk(sc_ref, ta_ref, tb_ref, s_out, e_out):
    K = PEER_TOPK
    NV = N_STAIR_VREGS
    kidx = lax.broadcasted_iota(I32, (PEER_NKEYS, LANES), 0).astype(F32)
    ridx = [(lax.broadcasted_iota(I32, (SUBLANES, LANES), 0) + SUBLANES * j).astype(F32) for j in range(NV)]
    ta = [ta_ref[SUBLANES * j:SUBLANES * (j + 1), :] for j in range(NV)]
    tb = [tb_ref[SUBLANES * j:SUBLANES * (j + 1), :] for j in range(NV)]
    zero = jnp.zeros((SUBLANES, LANES), F32)

    def top1(s):
        m = jnp.max(s, axis=0, keepdims=True)
        idx = jnp.min(jnp.where(s == m, kidx, float(PEER_NKEYS)), axis=0, keepdims=True)
        return m, idx, jnp.where(kidx == idx, -jnp.inf, s)

    def sorted_top(s, table, axis):
        val = [zero] * NV
        key = [zero] * NV
        for k in range(K):
            m, idx, s = top1(s)
            for j in _stair_vregs(axis, k):
                hit = table[j] == float(k)
                val[j] = jnp.where(hit, m, val[j])
                key[j] = jnp.where(hit, idx, key[j])
        return val, key

    def top_half(h, g, which):
        return sorted_top(sc_ref[2 * h + which, g], (ta, tb)[which], which)

    def select(first, second):
        (v1, k1), (v2, k2) = first, second
        cand = [jnp.where(ta[j] >= 0.0, v1[j] + v2[j], -jnp.inf) for j in range(NV)]
        ce = [k1[j] * float(PEER_NKEYS) + k2[j] for j in range(NV)]
        sub = lax.broadcasted_iota(I32, (SUBLANES, LANES), 0)
        best_s = [zero] * (K // SUBLANES)
        best_e = [zero] * (K // SUBLANES)
        for k in range(K):
            mx = cand[0]
            for j in range(1, NV):
                mx = jnp.maximum(mx, cand[j])
            m = jnp.max(mx, axis=0, keepdims=True)
            ix = jnp.where(cand[0] == m, ridx[0], float(NV * SUBLANES))
            for j in range(1, NV):
                ix = jnp.minimum(ix, jnp.where(cand[j] == m, ridx[j], float(NV * SUBLANES)))
            idx = jnp.min(ix, axis=0, keepdims=True)
            hits = [ridx[j] == idx for j in range(NV)]
            es = jnp.where(hits[0], ce[0], 0.0)
            for j in range(1, NV):
                es = es + jnp.where(hits[j], ce[j], 0.0)
            cand = [jnp.where(hits[j], -jnp.inf, cand[j]) for j in range(NV)]
            here = sub == (k % SUBLANES)
            best_s[k // SUBLANES] = jnp.where(here, m, best_s[k // SUBLANES])
            best_e[k // SUBLANES] = jnp.where(here, jnp.sum(es, axis=0, keepdims=True), best_e[k // SUBLANES])
        return jnp.concatenate(best_s, axis=0), jnp.concatenate(best_e, axis=0)

    def store(h, best):
        rows = pl.ds(pl.multiple_of(h * K, K), K)
        s_out[rows, :] = best[0]
        e_out[rows, :] = best[1]

    def finish(rows, ei_dst, ej_dst, gate_dst):
        s_all = s_out[...]
        gates = []
        for h in range(PEER_HEADS):
            sh = s_all[h * K:(h + 1) * K]
            ex = jnp.exp(sh - sh[0:1])
            gates.append(ex / jnp.sum(ex, axis=0, keepdims=True))
        gate_dst[rows, :] = jnp.concatenate(gates, axis=0).T
        e_int = e_out[...].T.astype(I32)
        ei_dst[rows, :] = jnp.right_shift(e_int, PEER_NKEYS.bit_length() - 1)
        ej_dst[rows, :] = jnp.bitwise_and(e_int, PEER_NKEYS - 1)

    return top_half, select, store, finish


G_ROWS = PEER_NKEYS // 2
G_PITCH = G_ROWS + SUBLANES
HI_MASK = 0xFFFF0000


def _peer_kernel(sc_ref, ta_ref, tb_ref, h_ref, wd_ref, wu_ref, x2_ref, out_ref,
                 g_ref, ei_ref, ej_ref, gate_ref, s_out, e_out):
    tm = h_ref.shape[0]
    ec = wd_ref.shape[1]
    ipc = ec // LANES
    half = ipc // 2
    t = pl.program_id(0)
    c = pl.program_id(1)
    n_grp = tm // LANES
    steps_per_grp = PEER_EXPERTS // ec // n_grp
    heads_per_step = PEER_HEADS // steps_per_grp
    rd = (t + 1) % 2
    wr = t % 2
    ei_rd, ej_rd, gate_rd = ei_ref.at[rd], ej_ref.at[rd], gate_ref.at[rd]

    @pl.when((t == 0) & (c == 0))
    def _first_step():
        ei_ref[...] = jnp.zeros(ei_ref.shape, I32)
        ej_ref[...] = jnp.zeros(ej_ref.shape, I32)
        gate_ref[...] = jnp.zeros(gate_ref.shape, F32)

    @pl.when(c == 0)
    def _scatter():
        out_ref[...] = x2_ref[...]
        as_bf16 = lambda v: v.astype(F32).astype(BF16)
        i_of_m = as_bf16(lax.broadcasted_iota(I32, (PEER_NKEYS, 2 * LANES), 0))
        n = as_bf16(lax.broadcasted_iota(I32, (LANES, LANES), 0))
        one = jnp.ones((LANES, LANES), BF16)
        zero = jnp.zeros((LANES, LANES), BF16)

        def pair(u, carry):
            ta_, tb_ = pl.ds(2 * u, 1), pl.ds(2 * u + 1, 1)
            ri = as_bf16(jnp.concatenate([ei_rd[ta_, :], ei_rd[tb_, :]], axis=1))
            rg = (0.5 * jnp.concatenate([gate_rd[ta_, :], gate_rd[tb_, :]], axis=1)).astype(BF16)
            c_t = jnp.where(i_of_m == ri, rg, jnp.zeros_like(rg))
            q_a = jnp.where(n == as_bf16(ej_rd[ta_, :]), one, zero)
            q_b = jnp.where(n == as_bf16(ej_rd[tb_, :]), one, zero)
            q_t = jnp.concatenate([jnp.concatenate([q_a, zero], axis=1),
                                   jnp.concatenate([zero, q_b], axis=1)], axis=0)
            g2 = _dot_nt(c_t, q_t).astype(BF16)
            for tok in range(2):
                word = pltpu.bitcast(g2[:, tok * LANES:(tok + 1) * LANES], jnp.uint32)
                g_ref[pl.ds(pl.multiple_of((2 * u + tok) * G_PITCH, SUBLANES), G_ROWS), :] = word
            return carry

        lax.fori_loop(0, tm // 2, pair, 0, unroll=16)

    top_half, select, store, finish = _make_topk(sc_ref, ta_ref, tb_ref, s_out, e_out)
    grp = c // steps_per_grp
    h0 = (c % steps_per_grp) * heads_per_step

    def experts_slice(k):
        a = _dot(h_ref[...], wd_ref[:, 2 * k * LANES:2 * (k + 1) * LANES]).astype(BF16)
        word = g_ref[pl.ds(c * half + k, tm, stride=G_PITCH), :]
        g_lo = lax.bitcast_convert_type(jnp.left_shift(word, 16), F32).astype(BF16)
        g_hi = lax.bitcast_convert_type(jnp.bitwise_and(word, jnp.uint32(HI_MASK)), F32).astype(BF16)
        return jnp.concatenate([_gelu2(a[:, 0:LANES]) * g_lo, _gelu2(a[:, LANES:2 * LANES]) * g_hi], axis=1)

    assert half == 2 * heads_per_step
    halves, z = [], []
    for k in range(half):
        halves.append(top_half(h0 + k // 2, grp, k % 2))
        z.append(experts_slice(k))
    best = [select(halves[2 * hh], halves[2 * hh + 1]) for hh in range(heads_per_step - 1)]
    out_ref[...] += _dot(jnp.concatenate(z, axis=1), wu_ref[...])
    best.append(select(halves[-2], halves[-1]))
    for hh in range(heads_per_step):
        store(h0 + hh, best[hh])

    @pl.when(c % steps_per_grp == steps_per_grp - 1)
    def _group_done():
        rows = pl.ds(pl.multiple_of(grp * LANES, LANES), LANES)
        finish(rows, ei_ref.at[wr], ej_ref.at[wr], gate_ref.at[wr])


def _peer(sc, h3, wd_t, wu, x2, tm=512, ec=2048):
    T, D = h3.shape
    tm = min(tm, T)
    n_tiles = T // tm
    n_grp = tm // LANES
    n_chunks = wu.shape[0] // ec
    assert n_chunks % n_grp == 0 and PEER_HEADS % (n_chunks // n_grp) == 0
    ta, tb = _stair_tables()
    prev = lambda t, c: (jnp.maximum(t - 1, 0), 0)
    return pl.pallas_call(
        _peer_kernel,
        grid=(n_tiles + 1, n_chunks),
        in_specs=[
            pl.BlockSpec((2 * PEER_HEADS, n_grp, PEER_NKEYS, LANES),
                         lambda t, c: (0, jnp.minimum(t, n_tiles - 1), 0, 0)),
            pl.BlockSpec(ta.shape, lambda t, c: (0, 0)),
            pl.BlockSpec(tb.shape, lambda t, c: (0, 0)),
            pl.BlockSpec((tm, D), prev),
            pl.BlockSpec((D, ec), lambda t, c: (0, c)),
            pl.BlockSpec((ec, D), lambda t, c: (c, 0)),
            pl.BlockSpec((tm, D), prev),
        ],
        out_specs=pl.BlockSpec((tm, D), prev),
        out_shape=jax.ShapeDtypeStruct((T, D), F32),
        scratch_shapes=[
            pltpu.VMEM((tm * G_PITCH, LANES), jnp.uint32),
            pltpu.VMEM((2, tm, LANES), I32),
            pltpu.VMEM((2, tm, LANES), I32),
            pltpu.VMEM((2, tm, LANES), F32),
            pltpu.VMEM((PEER_HEADS * PEER_TOPK, LANES), F32),
            pltpu.VMEM((PEER_HEADS * PEER_TOPK, LANES), F32),
        ],
        compiler_params=_cparams(("arbitrary", "arbitrary"), 60),
    )(sc, ta, tb, h3, wd_t, wu, x2)


def _pad_half(a, g):
    z = jnp.zeros_like(a)
    return jnp.concatenate([a, z] if g == 0 else [z, a], axis=-1)


def kernel(x, mem, mix_norm_w, w_in, conv_w, conv_b, cmp_pe, cmp_w1, cmp_w2, q_norm_w, k_norm_w, w_out,
           xattn_norm_w, mem_norm_w, xq, xk, xv, xo, xq_norm_w, xk_norm_w, ffn_norm_w, peer_wq, peer_keys,
           peer_down, peer_up):
    B, S, D = x.shape
    T = B * S
    l = 0
    G, R, dh = NSA_GROUPS, NSA_REP, NSA_DH

    w = w_in[l]
    o_q = C_CONV
    o_kv = o_q + NSA_HEADS * dh
    o_g = o_kv + C_KV
    wq = w[:, o_q:o_kv].reshape(D, G, R, dh)
    wq_pad = jnp.concatenate([_pad_half(wq[:, g], g).reshape(D, R * LANES) for g in range(G)], axis=1)
    wg = w[:, o_g:].reshape(D, G, R * 3)
    wg_pad = jnp.pad(wg, ((0, 0), (0, 0), (0, LANES - R * 3))).reshape(D, G * LANES)
    wkv = w[:, o_kv:o_g].reshape(D, 6, LANES)
    wkn = wkv[:, jnp.array([0, 1, 2, 4])].reshape(D, C_KN)
    wvt = wkv[:, jnp.array([3, 5])].reshape(D, 2 * LANES).T.astype(BF16)
    w_all = jnp.concatenate([w[:, :o_q], wq_pad, wkn, wg_pad], axis=1).astype(BF16)
    qw = q_norm_w[l] * dh ** -0.5
    qw_pad = jnp.concatenate([jnp.tile(_pad_half(qw, g), R) for g in range(G)]).reshape(1, C_Q)
    kw = jnp.stack([jnp.tile(k_norm_w[l, 1], G), jnp.tile(k_norm_w[l, 2], G)])

    yconv, q, kvc, kn, vt, gates = _in_proj(
        x.reshape(T, D), mix_norm_w[l].reshape(1, D), w_all, wvt, conv_w[l], conv_b[l].reshape(1, CONV_WIDTH),
        qw_pad, kw, S)

    n_rows = S // CMP_STRIDE
    kvg = jnp.array([0, 0, 1, 1])
    pe = jnp.concatenate([cmp_pe[l, kv] for kv in (0, 0, 1, 1)], axis=-1)
    w1 = cmp_w1[l].reshape(2, CMP_BLOCK, dh, dh)[kvg]
    w1 = jnp.einsum('qp,qjdh->jqdph', jnp.eye(4, dtype=F32), w1).reshape(CMP_BLOCK, 4 * dh, 4 * dh).astype(BF16)
    w2k = jnp.zeros((4 * dh, G * LANES), F32)
    w2vt = jnp.zeros((G * LANES, 4 * dh), F32)
    for g in range(G):
        o = g * LANES + g * dh
        w2k = w2k.at[g * dh:(g + 1) * dh, o:o + dh].set(cmp_w2[l, 0])
        w2vt = w2vt.at[o:o + dh, (G + g) * dh:(G + g + 1) * dh].set(cmp_w2[l, 1].T)
    knw = jnp.stack([_pad_half(k_norm_w[l, 0], g) for g in range(G)]).reshape(G, 1, LANES)
    kvcmp = _compress(kvc, pe, w1, w2k.astype(BF16), w2vt.astype(BF16), knw, S)

    n_sel = S // SEL_BLOCK
    cmp_start = jnp.arange(n_rows) * CMP_STRIDE
    sel_start = jnp.arange(n_sel) * SEL_BLOCK
    selmap_t = ((cmp_start[None, :] < sel_start[:, None] + SEL_BLOCK)
                & (cmp_start[None, :] + CMP_BLOCK > sel_start[:, None])
                & (jnp.arange(n_rows)[None, :] < n_rows - 1)).astype(BF16)
    ynsa = _nsa(q.reshape(B, S, C_Q), kn.reshape(B, S, 2 * LANES), vt, kvcmp, gates.reshape(B, S, C_G), selmap_t)

    kmem, vmem = _mem_kv(mem, mem_norm_w[l].reshape(1, D), xk[l].astype(BF16), xv[l].astype(BF16),
                         xk_norm_w[l].reshape(1, X_DH))

    wo = w_out[l]
    won = wo[CONV_WIDTH:].reshape(G, R, dh, D)
    won_pad = jnp.concatenate(
        [jnp.concatenate([won[g], jnp.zeros_like(won[g])] if g == 0 else [jnp.zeros_like(won[g]), won[g]],
                         axis=1).reshape(R * LANES, D) for g in range(G)], axis=0).astype(BF16)
    keys = peer_keys[l].reshape(2 * PEER_HEADS, PEER_NKEYS, PEER_HALF).astype(BF16)
    x2, h3, sc = _post_mix(
        x.reshape(T, D), yconv, ynsa.reshape(T, C_Q), wo[:CONV_WIDTH].astype(BF16), won_pad,
        xattn_norm_w[l].reshape(1, D), xq[l].astype(BF16),
        (xq_norm_w[l] * X_DH ** -0.5).reshape(1, X_DH), kmem, vmem, xo[l].astype(BF16),
        ffn_norm_w[l].reshape(1, D), peer_wq[l].astype(BF16), keys, S)

    out = _peer(sc, h3, peer_down[l].T.astype(BF16), peer_up[l].astype(BF16), x2)
    return out.reshape(B, S, D)
```

```python
import functools

import jax
import jax.numpy as jnp
from jax import lax
from jax.experimental import pallas as pl
from jax.experimental.pallas import tpu as pltpu

F32 = jnp.float32
BF16 = jnp.bfloat16
I32 = jnp.int32

EPS = 1e-6
NEG = -1e30
LANES = 128
SUBLANES = 8

D_MODEL = 1024
CONV_WIDTH = 512
NSA_HEADS = 8
NSA_GROUPS = 2
NSA_REP = NSA_HEADS // NSA_GROUPS
NSA_DH = 64
CMP_BLOCK = 32
CMP_STRIDE = 16
SEL_BLOCK = 64
SEL_TOPN = 16
WINDOW = 512
X_HEADS = 4
X_DH = D_MODEL // X_HEADS
PEER_HEADS = 8
PEER_NKEYS = 128
PEER_HALF = 128
PEER_TOPK = 16
PEER_EXPERTS = PEER_NKEYS * PEER_NKEYS

NT_DIMS = (((1,), (1,)), ((), ()))


def _dot(a, b):
    return jnp.dot(a, b, preferred_element_type=F32)


def _dot_nt(a, b):
    return lax.dot_general(a, b, NT_DIMS, preferred_element_type=F32)


def _gelu2(x):
    return x * (1.0 + lax.erf(x * (2.0 ** -0.5)))


def _gelu(x):
    return 0.5 * _gelu2(x)


def _rms(x, w):
    ms = jnp.mean(x * x, axis=-1, keepdims=True)
    return x * lax.rsqrt(ms + EPS) * w


def _emit_staggered(streams, n_stages):
    for step in range(n_stages + len(streams) - 1):
        for k, stream in enumerate(streams):
            if 0 <= step - k < n_stages:
                next(stream)


def _cparams(sem, vmem_mb):
    return pltpu.CompilerParams(dimension_semantics=sem, vmem_limit_bytes=vmem_mb * 1024 * 1024)


C_CONV = 3 * CONV_WIDTH
C_Q = NSA_HEADS * LANES
C_KV = 6 * NSA_GROUPS * NSA_DH
C_KN = 4 * LANES
C_G = NSA_GROUPS * LANES
C_ALL = C_CONV + C_Q + C_KN + C_G


def _in_proj_kernel(tiles_per_seq, x_ref, nw_ref, w_ref, wvt_ref, cw_ref, cb_ref, qw_ref, kw_ref,
                    yconv_ref, q_ref, kvc_ref, kn_ref, vt_ref, g_ref, carry_ref):
    tm = x_ref.shape[0]
    n_streams = 2
    rows = tm // n_streams

    @pl.when((pl.program_id(0) % tiles_per_seq) == 0)
    def _sequence_start():
        carry_ref[...] = jnp.zeros(carry_ref.shape, F32)

    tails = {}

    def stream(i):
        rs = slice(i * rows, (i + 1) * rows)
        hn = _rms(x_ref[rs, :], nw_ref[...]).astype(BF16)
        yield
        p = _dot(hn, w_ref[:, 0:C_CONV])
        b_g = p[:, 0:CONV_WIDTH]
        u = p[:, CONV_WIDTH:2 * CONV_WIDTH] * p[:, 2 * CONV_WIDTH:3 * CONV_WIDTH]
        prev = carry_ref[...] if i == 0 else tails[i - 1]
        p1 = prev[SUBLANES - 1:SUBLANES, :]
        p2 = prev[SUBLANES - 2:SUBLANES - 1, :]
        row = lax.broadcasted_iota(I32, u.shape, 0)
        u1 = jnp.where(row == 0, p1, pltpu.roll(u, 1, axis=0))
        u2 = jnp.where(row == 0, p2, jnp.where(row == 1, p1, pltpu.roll(u, 2, axis=0)))
        tails[i] = u[rows - SUBLANES:rows, :]
        if i == n_streams - 1:
            carry_ref[...] = tails[i]
        cw = cw_ref[...]
        y = b_g * (cw[0:1, :] * u2 + cw[1:2, :] * u1 + cw[2:3, :] * u + cb_ref[...])
        yconv_ref[rs, :] = y.astype(BF16)
        yield
        pq = _dot(hn, w_ref[:, C_CONV:C_CONV + C_Q])
        for h in range(NSA_HEADS):
            blk = pq[:, h * LANES:(h + 1) * LANES]
            ms = jnp.sum(blk * blk, axis=-1, keepdims=True) * (1.0 / NSA_DH)
            q_ref[rs, h * LANES:(h + 1) * LANES] = (
                blk * lax.rsqrt(ms + EPS) * qw_ref[:, h * LANES:(h + 1) * LANES]).astype(BF16)
        yield
        pkv = _dot(hn, w_ref[:, C_CONV + C_Q:C_CONV + C_Q + C_KN])
        kvc_ref[0, rs, :] = pkv[:, 0:LANES]
        kvc_ref[1, rs, :] = pkv[:, LANES:2 * LANES]
        lane = lax.broadcasted_iota(I32, (rows, LANES), 1)
        lo = lane < NSA_DH
        for j in range(2):
            blk = pkv[:, (2 + j) * LANES:(3 + j) * LANES]
            sq = blk * blk
            ms_lo = jnp.sum(jnp.where(lo, sq, 0.0), axis=-1, keepdims=True) * (1.0 / NSA_DH)
            ms_hi = jnp.sum(jnp.where(lo, 0.0, sq), axis=-1, keepdims=True) * (1.0 / NSA_DH)
            scale = jnp.where(lo, lax.rsqrt(ms_lo + EPS), lax.rsqrt(ms_hi + EPS))
            kn_ref[rs, j * LANES:(j + 1) * LANES] = (blk * scale * kw_ref[j:j + 1, :]).astype(BF16)
        yield
        for j in range(rows // LANES):
            vt_ref[i * (rows // LANES) + j] = _dot_nt(wvt_ref[...], hn[j * LANES:(j + 1) * LANES, :]).astype(BF16)
        pg = _dot(hn, w_ref[:, C_CONV + C_Q + C_KN:C_ALL])
        g_ref[rs, :] = jax.nn.sigmoid(pg)
        yield

    _emit_staggered([stream(i) for i in range(n_streams)], n_stages=5)


def _in_proj(x2d, nw, w_all, wvt, cw, cb, qw, kw, seq, tm=1024):
    T = x2d.shape[0]
    const = lambda i: (0, 0)
    tile = lambda i: (i, 0)
    return pl.pallas_call(
        functools.partial(_in_proj_kernel, seq // tm),
        grid=(T // tm,),
        in_specs=[
            pl.BlockSpec((tm, D_MODEL), tile),
            pl.BlockSpec((1, D_MODEL), const),
            pl.BlockSpec((D_MODEL, C_ALL), const),
            pl.BlockSpec((2 * LANES, D_MODEL), const),
            pl.BlockSpec((3, CONV_WIDTH), const),
            pl.BlockSpec((1, CONV_WIDTH), const),
            pl.BlockSpec((1, C_Q), const),
            pl.BlockSpec((2, LANES), const),
        ],
        out_specs=[
            pl.BlockSpec((tm, CONV_WIDTH), tile),
            pl.BlockSpec((tm, C_Q), tile),
            pl.BlockSpec((2, tm, LANES), lambda i: (0, i, 0)),
            pl.BlockSpec((tm, 2 * LANES), tile),
            pl.BlockSpec((tm // LANES, 2 * LANES, LANES), lambda i: (i, 0, 0)),
            pl.BlockSpec((tm, C_G), tile),
        ],
        out_shape=[
            jax.ShapeDtypeStruct((T, CONV_WIDTH), BF16),
            jax.ShapeDtypeStruct((T, C_Q), BF16),
            jax.ShapeDtypeStruct((2, T, LANES), F32),
            jax.ShapeDtypeStruct((T, 2 * LANES), BF16),
            jax.ShapeDtypeStruct((T // LANES, 2 * LANES, LANES), BF16),
            jax.ShapeDtypeStruct((T, C_G), F32),
        ],
        scratch_shapes=[pltpu.VMEM((SUBLANES, CONV_WIDTH), F32)],
        compiler_params=_cparams(("arbitrary",), 48),
    )(x2d, nw, w_all, wvt, cw, cb, qw, kw)


def _compress_kernel(kvc_ref, pe_ref, w1_ref, w2k_ref, w2vt_ref, knw_ref, out_ref):
    n_rows = out_ref.shape[3]
    acc_a = jnp.zeros((n_rows, 2 * LANES), F32)
    acc_b = jnp.zeros((n_rows, 2 * LANES), F32)
    for l in range(CMP_STRIDE):
        rows = pl.ds(l, n_rows, stride=CMP_STRIDE)
        x = jnp.concatenate([kvc_ref[0, rows, :], kvc_ref[1, rows, :]], axis=1)
        acc_a = acc_a + _dot((x + pe_ref[l:l + 1, :]).astype(BF16), w1_ref[l])
        acc_b = acc_b + _dot((x + pe_ref[CMP_STRIDE + l:CMP_STRIDE + l + 1, :]).astype(BF16),
                             w1_ref[CMP_STRIDE + l])
    hid = _gelu(acc_a + pltpu.roll(acc_b, n_rows - 1, axis=0)).astype(BF16)
    row = lax.broadcasted_iota(I32, (n_rows, LANES), 0)
    col = lax.broadcasted_iota(I32, (LANES, n_rows), 1)
    keys = _dot(hid, w2k_ref[...])
    vals = _dot_nt(w2vt_ref[...], hid)
    for g in range(NSA_GROUPS):
        out = keys[:, g * LANES:(g + 1) * LANES]
        ms = jnp.sum(out * out, axis=-1, keepdims=True) * (1.0 / NSA_DH)
        out = out * lax.rsqrt(ms + EPS) * knw_ref[g]
        out_ref[0, 0, g] = jnp.where(row < n_rows - 1, out, 0.0).astype(BF16)
        out_ref[0, 1, g] = jnp.where(col < n_rows - 1, vals[g * LANES:(g + 1) * LANES, :], 0.0).astype(BF16)


def _compress(kvc, pe, w1, w2k, w2vt, knw, S):
    B = kvc.shape[1] // S
    n_rows = S // CMP_STRIDE
    return pl.pallas_call(
        _compress_kernel,
        grid=(B,),
        in_specs=[
            pl.BlockSpec((2, S, LANES), lambda b: (0, b, 0)),
            pl.BlockSpec((CMP_BLOCK, 2 * LANES), lambda b: (0, 0)),
            pl.BlockSpec((CMP_BLOCK, 2 * LANES, 2 * LANES), lambda b: (0, 0, 0)),
            pl.BlockSpec((2 * LANES, 2 * LANES), lambda b: (0, 0)),
            pl.BlockSpec((2 * LANES, 2 * LANES), lambda b: (0, 0)),
            pl.BlockSpec((NSA_GROUPS, 1, LANES), lambda b: (0, 0, 0)),
        ],
        out_specs=pl.BlockSpec((1, 2, NSA_GROUPS, n_rows, LANES), lambda b: (b, 0, 0, 0, 0)),
        out_shape=jax.ShapeDtypeStruct((B, 2, NSA_GROUPS, n_rows, LANES), BF16),
        compiler_params=_cparams(("arbitrary",), 32),
    )(kvc, pe, w1, w2k, w2vt, knw)


SEL_CHUNK = 512


def _nsa_kernel(q_ref, kn_ref, vt_ref, cmp_ref, g_ref, selmap_ref, y_ref, sel_ref, acc_ref):
    tq = q_ref.shape[1]
    cols = NSA_HEADS * tq
    i = pl.program_id(1)
    t0 = i * tq
    qb = q_ref[0]
    qs = jnp.concatenate([qb[:, h * LANES:(h + 1) * LANES] for h in range(NSA_HEADS)], axis=0)
    qpos = t0 + jnp.bitwise_and(lax.broadcasted_iota(I32, (1, cols), 1), tq - 1)
    both = lambda kv: (cmp_ref[0, kv, 0].astype(F32) + cmp_ref[0, kv, 1].astype(F32)).astype(BF16)

    n_win = WINDOW + tq
    w_start = pl.multiple_of(jnp.maximum(t0 - WINDOW, 0), LANES)
    w_unit = w_start // LANES
    k_w = kn_ref[0, pl.ds(w_start, n_win), LANES:2 * LANES]
    vt_w = jnp.concatenate([vt_ref[w_unit + u, LANES:2 * LANES, :] for u in range(n_win // LANES)], axis=1)
    diff = (qpos - w_start) - lax.broadcasted_iota(I32, (n_win, cols), 0)
    in_window = lax.bitcast_convert_type(diff, jnp.uint32) < jnp.uint32(WINDOW)
    s_w = jnp.where(in_window, _dot_nt(k_w, qs), NEG)
    p_w = jnp.exp(s_w - jnp.max(s_w, axis=0, keepdims=True))
    l_w = jnp.sum(p_w, axis=0, keepdims=True)
    o_w = _dot(vt_w, p_w.astype(BF16))

    n_cmp = cmp_ref.shape[3]
    s = _dot_nt(both(0), qs)
    cmp_last = lax.broadcasted_iota(I32, (n_cmp, cols), 0) * CMP_STRIDE + (CMP_BLOCK - 1)
    vis_c = cmp_last <= qpos
    s = jnp.where(vis_c, s, NEG)
    m = jnp.max(s, axis=0, keepdims=True)
    m = jnp.where(m > 0.5 * NEG, m, 0.0)
    e = jnp.where(vis_c, jnp.exp(s - m), 0.0)
    d = jnp.sum(e, axis=0, keepdims=True)
    p_c = e / jnp.where(d > 0.0, d, 1.0)
    o_c = _dot(both(1), p_c.astype(BF16))

    n_sel = selmap_ref.shape[0]
    blk = lax.broadcasted_iota(I32, (n_sel, tq), 0)
    cur = jnp.right_shift(t0 + lax.broadcasted_iota(I32, (n_sel, tq), 1),
                          SEL_BLOCK.bit_length() - 1)
    forced = (blk == 0) | (blk == cur) | (blk == cur - 1)
    chosen = []
    for g in range(NSA_GROUPS):
        psum = p_c[:, g * NSA_REP * tq:(g * NSA_REP + 1) * tq]
        for r in range(1, NSA_REP):
            psum = psum + p_c[:, (g * NSA_REP + r) * tq:(g * NSA_REP + r + 1) * tq]
        p_hi = psum.astype(BF16)
        p_lo = (psum - p_hi.astype(F32)).astype(BF16)
        imp = _dot(selmap_ref[...], p_hi) + _dot(selmap_ref[...], p_lo)
        imp = jnp.where(blk > cur, -jnp.inf, jnp.where(forced, jnp.inf, imp))
        rank = jnp.zeros((n_sel, tq), I32)
        for jp in range(n_sel):
            other = imp[jp:jp + 1, :]
            beats = (other > imp) | ((other == imp) & (blk > jp))
            rank = rank + beats.astype(I32)
        chosen += [rank < min(SEL_TOPN, n_sel)] * NSA_REP
    sel_ref[...] = jnp.where(jnp.concatenate(chosen, axis=1), qpos, -1)

    acc_ref[...] = jnp.zeros(acc_ref.shape, F32)
    units = SEL_CHUNK // LANES
    blocks = SEL_CHUNK // SEL_BLOCK
    koff = lax.broadcasted_iota(I32, (SEL_BLOCK, cols), 0)

    def sel_chunk(c, carry):
        m_old, l_old = carry
        start = pl.multiple_of(c * SEL_CHUNK, SEL_CHUNK)
        k = kn_ref[0, pl.ds(start, SEL_CHUNK), 0:LANES]
        vt = jnp.concatenate([vt_ref[c * units + u, 0:LANES, :] for u in range(units)], axis=1)
        sc = _dot_nt(k, qs)
        parts = []
        for b in range(blocks):
            limit = sel_ref[pl.ds(c * blocks + b, 1), :] - (start + b * SEL_BLOCK)
            parts.append(jnp.where(koff <= limit, sc[b * SEL_BLOCK:(b + 1) * SEL_BLOCK], NEG))
        sc = jnp.concatenate(parts, axis=0)
        m_new = jnp.maximum(m_old, jnp.max(sc, axis=0, keepdims=True))
        alpha = jnp.exp(m_old - m_new)
        p = jnp.exp(sc - m_new)
        acc_ref[...] = alpha * acc_ref[...] + _dot(vt, p.astype(BF16))
        return m_new, alpha * l_old + jnp.sum(p, axis=0, keepdims=True)

    init = (jnp.full((1, cols), NEG, F32), jnp.zeros((1, cols), F32))
    _, l_s = lax.fori_loop(0, (t0 + tq + SEL_CHUNK - 1) // SEL_CHUNK, sel_chunk, init)
    o_s = acc_ref[...]

    g_t = [g_ref[0, :, g * LANES:(g + 1) * LANES].T for g in range(NSA_GROUPS)]
    gate = lambda br: jnp.concatenate(
        [g_t[h // NSA_REP][3 * (h % NSA_REP) + br:3 * (h % NSA_REP) + br + 1, :] for h in range(NSA_HEADS)], axis=1)
    y_t = o_c * gate(0) + o_s * (gate(1) / l_s) + o_w * (gate(2) / l_w)
    for h in range(NSA_HEADS):
        y_ref[0, :, h * LANES:(h + 1) * LANES] = y_t[:, h * tq:(h + 1) * tq].T.astype(BF16)


def _nsa(q, kn, vt, kvcmp, gates, selmap, tq=256):
    B, S = q.shape[0], q.shape[1]
    n_sel = S // SEL_BLOCK
    cols = NSA_HEADS * tq
    units = S // LANES
    n_cmp = kvcmp.shape[3]
    return pl.pallas_call(
        _nsa_kernel,
        grid=(B, S // tq),
        in_specs=[
            pl.BlockSpec((1, tq, C_Q), lambda b, i: (b, i, 0)),
            pl.BlockSpec((1, S, 2 * LANES), lambda b, i: (b, 0, 0)),
            pl.BlockSpec((units, 2 * LANES, LANES), lambda b, i: (b, 0, 0)),
            pl.BlockSpec((1, 2, NSA_GROUPS, n_cmp, LANES), lambda b, i: (b, 0, 0, 0, 0)),
            pl.BlockSpec((1, tq, C_G), lambda b, i: (b, i, 0)),
            pl.BlockSpec((n_sel, n_cmp), lambda b, i: (0, 0)),
        ],
        out_specs=pl.BlockSpec((1, tq, C_Q), lambda b, i: (b, i, 0)),
        out_shape=jax.ShapeDtypeStruct((B, S, C_Q), BF16),
        scratch_shapes=[
            pltpu.VMEM((n_sel, cols), I32),
            pltpu.VMEM((LANES, cols), F32),
        ],
        compiler_params=_cparams(("arbitrary", "arbitrary"), 56),
    )(q, kn, vt, kvcmp, gates, selmap)


def _mem_kv_kernel(mem_ref, nw_ref, wk_ref, wv_ref, kw_ref, k_ref, v_ref):
    mn = _rms(mem_ref[0], nw_ref[...]).astype(BF16)
    k = _dot(mn, wk_ref[...])
    for h in range(X_HEADS):
        sl = slice(h * X_DH, (h + 1) * X_DH)
        k_ref[0, :, sl] = _rms(k[:, sl], kw_ref[...]).astype(BF16)
    v_ref[0] = _dot(mn, wv_ref[...]).astype(BF16)


def _mem_kv(mem, nw, wk, wv, kw):
    B, M, D = mem.shape
    blk = pl.BlockSpec((1, M, D), lambda b: (b, 0, 0))
    const = lambda b: (0, 0)
    return pl.pallas_call(
        _mem_kv_kernel,
        grid=(B,),
        in_specs=[blk, pl.BlockSpec((1, D), const), pl.BlockSpec((D, D), const),
                  pl.BlockSpec((D, D), const), pl.BlockSpec((1, X_DH), const)],
        out_specs=[blk, blk],
        out_shape=[jax.ShapeDtypeStruct((B, M, D), BF16)] * 2,
        compiler_params=_cparams(("arbitrary",), 32),
    )(mem, nw, wk, wv, kw)


def _post_mix_kernel(x_ref, yc_ref, yn_ref, woc_ref, won_ref, xnw_ref, xq_ref, xqw_ref, km_ref, vm_ref,
                     xo_ref, fnw_ref, pwq_ref, keys_ref, x2_ref, h3_ref, sc_ref):
    tm = x_ref.shape[0]
    n_streams = 2
    rows_per = tm // n_streams

    def stream(i):
        rs = slice(i * rows_per, (i + 1) * rows_per)
        x1 = x_ref[rs, :] + _dot(yc_ref[rs, :], woc_ref[...]) + _dot(yn_ref[rs, :], won_ref[...])
        yield
        h2 = _rms(x1, xnw_ref[...]).astype(BF16)
        qx = _dot(h2, xq_ref[...])
        yield
        heads = []
        for h in range(X_HEADS):
            sl = slice(h * X_DH, (h + 1) * X_DH)
            qh = _rms(qx[:, sl], xqw_ref[...]).astype(BF16)
            s = _dot_nt(qh, km_ref[0, :, sl])
            e = jnp.exp(s - jnp.max(s, axis=-1, keepdims=True))
            p = e / jnp.sum(e, axis=-1, keepdims=True)
            heads.append(_dot(p.astype(BF16), vm_ref[0, :, sl]))
        yield
        o = jnp.concatenate(heads, axis=1).astype(BF16)
        x2 = x1 + _dot(o, xo_ref[...])
        x2_ref[rs, :] = x2
        yield
        h3 = _rms(x2, fnw_ref[...]).astype(BF16)
        h3_ref[rs, :] = h3
        qp = _dot(h3, pwq_ref[...]).astype(BF16)
        for c in range(2 * PEER_HEADS):
            sl = slice(c * PEER_HALF, (c + 1) * PEER_HALF)
            for j in range(rows_per // LANES):
                sc_ref[c, i * (rows_per // LANES) + j] = _dot_nt(keys_ref[c], qp[j * LANES:(j + 1) * LANES, sl])
        yield

    _emit_staggered([stream(i) for i in range(n_streams)], n_stages=5)


def _post_mix(x2d, yconv, ynsa, woc, won, xnw, xq, xqw, kmem, vmem, xo, fnw, pwq, keys, seq, tm=512):
    T, D = x2d.shape
    M = kmem.shape[1]
    n_sc = 2 * PEER_HEADS * PEER_NKEYS
    tile = lambda i: (i, 0)
    const = lambda i: (0, 0)
    per_batch = lambda i: ((i * tm) // seq, 0, 0)
    return pl.pallas_call(
        _post_mix_kernel,
        grid=(T // tm,),
        in_specs=[
            pl.BlockSpec((tm, D), tile),
            pl.BlockSpec((tm, CONV_WIDTH), tile),
            pl.BlockSpec((tm, C_Q), tile),
            pl.BlockSpec((CONV_WIDTH, D), const),
            pl.BlockSpec((C_Q, D), const),
            pl.BlockSpec((1, D), const),
            pl.BlockSpec((D, D), const),
            pl.BlockSpec((1, X_DH), const),
            pl.BlockSpec((1, M, D), per_batch),
            pl.BlockSpec((1, M, D), per_batch),
            pl.BlockSpec((D, D), const),
            pl.BlockSpec((1, D), const),
            pl.BlockSpec((D, n_sc), const),
            pl.BlockSpec((2 * PEER_HEADS, PEER_NKEYS, PEER_HALF), lambda i: (0, 0, 0)),
        ],
        out_specs=[pl.BlockSpec((tm, D), tile), pl.BlockSpec((tm, D), tile),
                   pl.BlockSpec((2 * PEER_HEADS, tm // LANES, PEER_NKEYS, LANES), lambda i: (0, i, 0, 0))],
        out_shape=[
            jax.ShapeDtypeStruct((T, D), F32),
            jax.ShapeDtypeStruct((T, D), BF16),
            jax.ShapeDtypeStruct((2 * PEER_HEADS, T // LANES, PEER_NKEYS, LANES), F32),
        ],
        compiler_params=_cparams(("arbitrary",), 56),
    )(x2d, yconv, ynsa, woc, won, xnw, xq, xqw, kmem, vmem, xo, fnw, pwq, keys)


STAIR = [(a, b) for a in range(PEER_TOPK) for b in range(PEER_TOPK // (a + 1))]
N_STAIR_VREGS = -(-len(STAIR) // SUBLANES)


def _stair_vregs(axis, k):
    return sorted({r // SUBLANES for r, ab in enumerate(STAIR) if ab[axis] == k})


def _stair_tables():
    rows = N_STAIR_VREGS * SUBLANES
    pad = rows - len(STAIR)
    ta = jnp.asarray([float(a) for a, _ in STAIR] + [-1.0] * pad, F32)
    tb = jnp.asarray([float(b) for _, b in STAIR] + [-1.0] * pad, F32)
    return jnp.broadcast_to(ta[:, None], (rows, LANES)), jnp.broadcast_to(tb[:, None], (rows, LANES))


def _make_topk(sc_ref, ta_ref, tb_ref, s_out, e_out):
    K = PEER_TOPK
    NV = N_STAIR_VREGS
    kidx = lax.broadcasted_iota(I32, (PEER_NKEYS, LANES), 0).astype(F32)
    ridx = [(lax.broadcasted_iota(I32, (SUBLANES, LANES), 0) + SUBLANES * j).astype(F32) for j in range(NV)]
    ta = [ta_ref[SUBLANES * j:SUBLANES * (j + 1), :] for j in range(NV)]
    tb = [tb_ref[SUBLANES * j:SUBLANES * (j + 1), :] for j in range(NV)]
    zero = jnp.zeros((SUBLANES, LANES), F32)

    def top1(s):
        m = jnp.max(s, axis=0, keepdims=True)
        idx = jnp.min(jnp.where(s == m, kidx, float(PEER_NKEYS)), axis=0, keepdims=True)
        return m, idx, jnp.where(kidx == idx, -jnp.inf, s)

    def sorted_top(s, table, axis):
        val = [zero] * NV
        key = [zero] * NV
        for k in range(K):
            m, idx, s = top1(s)
            for j in _stair_vregs(axis, k):
                hit = table[j] == float(k)
                val[j] = jnp.where(hit, m, val[j])
                key[j] = jnp.where(hit, idx, key[j])
        return val, key

    def top_half(h, g, which):
        return sorted_top(sc_ref[2 * h + which, g], (ta, tb)[which], which)

    def select(first, second):
        (v1, k1), (v2, k2) = first, second
        cand = [jnp.where(ta[j] >= 0.0, v1[j] + v2[j], -jnp.inf) for j in range(NV)]
        ce = [k1[j] * float(PEER_NKEYS) + k2[j] for j in range(NV)]
        sub = lax.broadcasted_iota(I32, (SUBLANES, LANES), 0)
        best_s = [zero] * (K // SUBLANES)
        best_e = [zero] * (K // SUBLANES)
        for k in range(K):
            mx = cand[0]
            for j in range(1, NV):
                mx = jnp.maximum(mx, cand[j])
            m = jnp.max(mx, axis=0, keepdims=True)
            ix = jnp.where(cand[0] == m, ridx[0], float(NV * SUBLANES))
            for j in range(1, NV):
                ix = jnp.minimum(ix, jnp.where(cand[j] == m, ridx[j], float(NV * SUBLANES)))
            idx = jnp.min(ix, axis=0, keepdims=True)
            hits = [ridx[j] == idx for j in range(NV)]
            es = jnp.where(hits[0], ce[0], 0.0)
            for j in range(1, NV):
                es = es + jnp.where(hits[j], ce[j], 0.0)
            cand = [jnp.where(hits[j], -jnp.inf, cand[j]) for j in range(NV)]
            here = sub == (k % SUBLANES)
            best_s[k // SUBLANES] = jnp.where(here, m, best_s[k // SUBLANES])
            best_e[k // SUBLANES] = jnp.where(here, jnp.sum(es, axis=0, keepdims=True), best_e[k // SUBLANES])
        return jnp.concatenate(best_s, axis=0), jnp.concatenate(best_e, axis=0)

    def store(h, best):
        rows = pl.ds(pl.multiple_of(h * K, K), K)
        s_out[rows, :] = best[0]
        e_out[rows, :] = best[1]

    def finish(rows, ei_dst, ej_dst, gate_dst):
        s_all = s_out[...]
        gates = []
        for h in range(PEER_HEADS):
            sh = s_all[h * K:(h + 1) * K]
            ex = jnp.exp(sh - sh[0:1])
            gates.append(ex / jnp.sum(ex, axis=0, keepdims=True))
        gate_dst[rows, :] = jnp.concatenate(gates, axis=0).T
        e_int = e_out[...].T.astype(I32)
        ei_dst[rows, :] = jnp.right_shift(e_int, PEER_NKEYS.bit_length() - 1)
        ej_dst[rows, :] = jnp.bitwise_and(e_int, PEER_NKEYS - 1)

    return top_half, select, store, finish


G_ROWS = PEER_NKEYS // 2
G_PITCH = G_ROWS + SUBLANES
HI_MASK = 0xFFFF0000


def _peer_kernel(sc_ref, ta_ref, tb_ref, h_ref, wd_ref, wu_ref, x2_ref, out_ref,
                 g_ref, ei_ref, ej_ref, gate_ref, s_out, e_out):
    tm = h_ref.shape[0]
    ec = wd_ref.shape[1]
    ipc = ec // LANES
    half = ipc // 2
    t = pl.program_id(0)
    c = pl.program_id(1)
    n_grp = tm // LANES
    steps_per_grp = PEER_EXPERTS // ec // n_grp
    heads_per_step = PEER_HEADS // steps_per_grp
    rd = (t + 1) % 2
    wr = t % 2
    ei_rd, ej_rd, gate_rd = ei_ref.at[rd], ej_ref.at[rd], gate_ref.at[rd]

    @pl.when((t == 0) & (c == 0))
    def _first_step():
        ei_ref[...] = jnp.zeros(ei_ref.shape, I32)
        ej_ref[...] = jnp.zeros(ej_ref.shape, I32)
        gate_ref[...] = jnp.zeros(gate_ref.shape, F32)

    @pl.when(c == 0)
    def _scatter():
        out_ref[...] = x2_ref[...]
        as_bf16 = lambda v: v.astype(F32).astype(BF16)
        i_of_m = as_bf16(lax.broadcasted_iota(I32, (PEER_NKEYS, 2 * LANES), 0))
        n = as_bf16(lax.broadcasted_iota(I32, (LANES, LANES), 0))
        one = jnp.ones((LANES, LANES), BF16)
        zero = jnp.zeros((LANES, LANES), BF16)

        def pair(u, carry):
            ta_, tb_ = pl.ds(2 * u, 1), pl.ds(2 * u + 1, 1)
            ri = as_bf16(jnp.concatenate([ei_rd[ta_, :], ei_rd[tb_, :]], axis=1))
            rg = (0.5 * jnp.concatenate([gate_rd[ta_, :], gate_rd[tb_, :]], axis=1)).astype(BF16)
            c_t = jnp.where(i_of_m == ri, rg, jnp.zeros_like(rg))
            q_a = jnp.where(n == as_bf16(ej_rd[ta_, :]), one, zero)
            q_b = jnp.where(n == as_bf16(ej_rd[tb_, :]), one, zero)
            q_t = jnp.concatenate([jnp.concatenate([q_a, zero], axis=1),
                                   jnp.concatenate([zero, q_b], axis=1)], axis=0)
            g2 = _dot_nt(c_t, q_t).astype(BF16)
            for tok in range(2):
                word = pltpu.bitcast(g2[:, tok * LANES:(tok + 1) * LANES], jnp.uint32)
                g_ref[pl.ds(pl.multiple_of((2 * u + tok) * G_PITCH, SUBLANES), G_ROWS), :] = word
            return carry

        lax.fori_loop(0, tm // 2, pair, 0, unroll=16)

    top_half, select, store, finish = _make_topk(sc_ref, ta_ref, tb_ref, s_out, e_out)
    grp = c // steps_per_grp
    h0 = (c % steps_per_grp) * heads_per_step

    def experts_slice(k):
        a = _dot(h_ref[...], wd_ref[:, 2 * k * LANES:2 * (k + 1) * LANES]).astype(BF16)
        word = g_ref[pl.ds(c * half + k, tm, stride=G_PITCH), :]
        g_lo = lax.bitcast_convert_type(jnp.left_shift(word, 16), F32).astype(BF16)
        g_hi = lax.bitcast_convert_type(jnp.bitwise_and(word, jnp.uint32(HI_MASK)), F32).astype(BF16)
        return jnp.concatenate([_gelu2(a[:, 0:LANES]) * g_lo, _gelu2(a[:, LANES:2 * LANES]) * g_hi], axis=1)

    assert half == 2 * heads_per_step
    halves, z = [], []
    for k in range(half):
        halves.append(top_half(h0 + k // 2, grp, k % 2))
        z.append(experts_slice(k))
    best = [select(halves[2 * hh], halves[2 * hh + 1]) for hh in range(heads_per_step - 1)]
    out_ref[...] += _dot(jnp.concatenate(z, axis=1), wu_ref[...])
    best.append(select(halves[-2], halves[-1]))
    for hh in range(heads_per_step):
        store(h0 + hh, best[hh])

    @pl.when(c % steps_per_grp == steps_per_grp - 1)
    def _group_done():
        rows = pl.ds(pl.multiple_of(grp * LANES, LANES), LANES)
        finish(rows, ei_ref.at[wr], ej_ref.at[wr], gate_ref.at[wr])


def _peer(sc, h3, wd_t, wu, x2, tm=512, ec=2048):
    T, D = h3.shape
    tm = min(tm, T)
    n_tiles = T // tm
    n_grp = tm // LANES
    n_chunks = wu.shape[0] // ec
    assert n_chunks % n_grp == 0 and PEER_HEADS % (n_chunks // n_grp) == 0
    ta, tb = _stair_tables()
    prev = lambda t, c: (jnp.maximum(t - 1, 0), 0)
    return pl.pallas_call(
        _peer_kernel,
        grid=(n_tiles + 1, n_chunks),
        in_specs=[
            pl.BlockSpec((2 * PEER_HEADS, n_grp, PEER_NKEYS, LANES),
                         lambda t, c: (0, jnp.minimum(t, n_tiles - 1), 0, 0)),
            pl.BlockSpec(ta.shape, lambda t, c: (0, 0)),
            pl.BlockSpec(tb.shape, lambda t, c: (0, 0)),
            pl.BlockSpec((tm, D), prev),
            pl.BlockSpec((D, ec), lambda t, c: (0, c)),
            pl.BlockSpec((ec, D), lambda t, c: (c, 0)),
            pl.BlockSpec((tm, D), prev),
        ],
        out_specs=pl.BlockSpec((tm, D), prev),
        out_shape=jax.ShapeDtypeStruct((T, D), F32),
        scratch_shapes=[
            pltpu.VMEM((tm * G_PITCH, LANES), jnp.uint32),
            pltpu.VMEM((2, tm, LANES), I32),
            pltpu.VMEM((2, tm, LANES), I32),
            pltpu.VMEM((2, tm, LANES), F32),
            pltpu.VMEM((PEER_HEADS * PEER_TOPK, LANES), F32),
            pltpu.VMEM((PEER_HEADS * PEER_TOPK, LANES), F32),
        ],
        compiler_params=_cparams(("arbitrary", "arbitrary"), 60),
    )(sc, ta, tb, h3, wd_t, wu, x2)


def _pad_half(a, g):
    z = jnp.zeros_like(a)
    return jnp.concatenate([a, z] if g == 0 else [z, a], axis=-1)


def kernel(x, mem, mix_norm_w, w_in, conv_w, conv_b, cmp_pe, cmp_w1, cmp_w2, q_norm_w, k_norm_w, w_out,
           xattn_norm_w, mem_norm_w, xq, xk, xv, xo, xq_norm_w, xk_norm_w, ffn_norm_w, peer_wq, peer_keys,
           peer_down, peer_up):
    B, S, D = x.shape
    T = B * S
    l = 0
    G, R, dh = NSA_GROUPS, NSA_REP, NSA_DH

    w = w_in[l]
    o_q = C_CONV
    o_kv = o_q + NSA_HEADS * dh
    o_g = o_kv + C_KV
    wq = w[:, o_q:o_kv].reshape(D, G, R, dh)
    wq_pad = jnp.concatenate([_pad_half(wq[:, g], g).reshape(D, R * LANES) for g in range(G)], axis=1)
    wg = w[:, o_g:].reshape(D, G, R * 3)
    wg_pad = jnp.pad(wg, ((0, 0), (0, 0), (0, LANES - R * 3))).reshape(D, G * LANES)
    wkv = w[:, o_kv:o_g].reshape(D, 6, LANES)
    wkn = wkv[:, jnp.array([0, 1, 2, 4])].reshape(D, C_KN)
    wvt = wkv[:, jnp.array([3, 5])].reshape(D, 2 * LANES).T.astype(BF16)
    w_all = jnp.concatenate([w[:, :o_q], wq_pad, wkn, wg_pad], axis=1).astype(BF16)
    qw = q_norm_w[l] * dh ** -0.5
    qw_pad = jnp.concatenate([jnp.tile(_pad_half(qw, g), R) for g in range(G)]).reshape(1, C_Q)
    kw = jnp.stack([jnp.tile(k_norm_w[l, 1], G), jnp.tile(k_norm_w[l, 2], G)])

    yconv, q, kvc, kn, vt, gates = _in_proj(
        x.reshape(T, D), mix_norm_w[l].reshape(1, D), w_all, wvt, conv_w[l], conv_b[l].reshape(1, CONV_WIDTH),
        qw_pad, kw, S)

    n_rows = S // CMP_STRIDE
    kvg = jnp.array([0, 0, 1, 1])
    pe = jnp.concatenate([cmp_pe[l, kv] for kv in (0, 0, 1, 1)], axis=-1)
    w1 = cmp_w1[l].reshape(2, CMP_BLOCK, dh, dh)[kvg]
    w1 = jnp.einsum('qp,qjdh->jqdph', jnp.eye(4, dtype=F32), w1).reshape(CMP_BLOCK, 4 * dh, 4 * dh).astype(BF16)
    w2k = jnp.zeros((4 * dh, G * LANES), F32)
    w2vt = jnp.zeros((G * LANES, 4 * dh), F32)
    for g in range(G):
        o = g * LANES + g * dh
        w2k = w2k.at[g * dh:(g + 1) * dh, o:o + dh].set(cmp_w2[l, 0])
        w2vt = w2vt.at[o:o + dh, (G + g) * dh:(G + g + 1) * dh].set(cmp_w2[l, 1].T)
    knw = jnp.stack([_pad_half(k_norm_w[l, 0], g) for g in range(G)]).reshape(G, 1, LANES)
    kvcmp = _compress(kvc, pe, w1, w2k.astype(BF16), w2vt.astype(BF16), knw, S)

    n_sel = S // SEL_BLOCK
    cmp_start = jnp.arange(n_rows) * CMP_STRIDE
    sel_start = jnp.arange(n_sel) * SEL_BLOCK
    selmap_t = ((cmp_start[None, :] < sel_start[:, None] + SEL_BLOCK)
                & (cmp_start[None, :] + CMP_BLOCK > sel_start[:, None])
                & (jnp.arange(n_rows)[None, :] < n_rows - 1)).astype(BF16)
    ynsa = _nsa(q.reshape(B, S, C_Q), kn.reshape(B, S, 2 * LANES), vt, kvcmp, gates.reshape(B, S, C_G), selmap_t)

    kmem, vmem = _mem_kv(mem, mem_norm_w[l].reshape(1, D), xk[l].astype(BF16), xv[l].astype(BF16),
                         xk_norm_w[l].reshape(1, X_DH))

    wo = w_out[l]
    won = wo[CONV_WIDTH:].reshape(G, R, dh, D)
    won_pad = jnp.concatenate(
        [jnp.concatenate([won[g], jnp.zeros_like(won[g])] if g == 0 else [jnp.zeros_like(won[g]), won[g]],
                         axis=1).reshape(R * LANES, D) for g in range(G)], axis=0).astype(BF16)
    keys = peer_keys[l].reshape(2 * PEER_HEADS, PEER_NKEYS, PEER_HALF).astype(BF16)
    x2, h3, sc = _post_mix(
        x.reshape(T, D), yconv, ynsa.reshape(T, C_Q), wo[:CONV_WIDTH].astype(BF16), won_pad,
        xattn_norm_w[l].reshape(1, D), xq[l].astype(BF16),
        (xq_norm_w[l] * X_DH ** -0.5).reshape(1, X_DH), kmem, vmem, xo[l].astype(BF16),
        ffn_norm_w[l].reshape(1, D), peer_wq[l].astype(BF16), keys, S)

    out = _peer(sc, h3, peer_down[l].T.astype(BF16), peer_up[l].astype(BF16), x2)
    return out.reshape(B, S, D)
```

```python
import functools

import jax
import jax.numpy as jnp
from jax import lax
from jax.experimental import pallas as pl
from jax.experimental.pallas import tpu as pltpu

F32 = jnp.float32
BF16 = jnp.bfloat16
I32 = jnp.int32

EPS = 1e-6
NEG = -1e30
LANES = 128
SUBLANES = 8

D_MODEL = 1024
CONV_WIDTH = 512
NSA_HEADS = 8
NSA_GROUPS = 2
NSA_REP = NSA_HEADS // NSA_GROUPS
NSA_DH = 64
CMP_BLOCK = 32
CMP_STRIDE = 16
SEL_BLOCK = 64
SEL_TOPN = 16
WINDOW = 512
X_HEADS = 4
X_DH = D_MODEL // X_HEADS
PEER_HEADS = 8
PEER_NKEYS = 128
PEER_HALF = 128
PEER_TOPK = 16
PEER_EXPERTS = PEER_NKEYS * PEER_NKEYS

NT_DIMS = (((1,), (1,)), ((), ()))


def _dot(a, b):
    return jnp.dot(a, b, preferred_element_type=F32)


def _dot_nt(a, b):
    return lax.dot_general(a, b, NT_DIMS, preferred_element_type=F32)


def _gelu2(x):
    return x * (1.0 + lax.erf(x * (2.0 ** -0.5)))


def _gelu(x):
    return 0.5 * _gelu2(x)


def _rms(x, w):
    ms = jnp.mean(x * x, axis=-1, keepdims=True)
    return x * lax.rsqrt(ms + EPS) * w


def _emit_staggered(streams, n_stages):
    for step in range(n_stages + len(streams) - 1):
        for k, stream in enumerate(streams):
            if 0 <= step - k < n_stages:
                next(stream)


def _cparams(sem, vmem_mb):
    return pltpu.CompilerParams(dimension_semantics=sem, vmem_limit_bytes=vmem_mb * 1024 * 1024)


C_CONV = 3 * CONV_WIDTH
C_Q = NSA_HEADS * LANES
C_KV = 6 * NSA_GROUPS * NSA_DH
C_KN = 4 * LANES
C_G = NSA_GROUPS * LANES
C_ALL = C_CONV + C_Q + C_KN + C_G


def _in_proj_kernel(tiles_per_seq, x_ref, nw_ref, w_ref, wvt_ref, cw_ref, cb_ref, qw_ref, kw_ref,
                    yconv_ref, q_ref, kvc_ref, kn_ref, vt_ref, g_ref, carry_ref):
    tm = x_ref.shape[0]
    n_streams = 2
    rows = tm // n_streams

    @pl.when((pl.program_id(0) % tiles_per_seq) == 0)
    def _sequence_start():
        carry_ref[...] = jnp.zeros(carry_ref.shape, F32)

    tails = {}

    def stream(i):
        rs = slice(i * rows, (i + 1) * rows)
        hn = _rms(x_ref[rs, :], nw_ref[...]).astype(BF16)
        yield
        p = _dot(hn, w_ref[:, 0:C_CONV])
        b_g = p[:, 0:CONV_WIDTH]
        u = p[:, CONV_WIDTH:2 * CONV_WIDTH] * p[:, 2 * CONV_WIDTH:3 * CONV_WIDTH]
        prev = carry_ref[...] if i == 0 else tails[i - 1]
        p1 = prev[SUBLANES - 1:SUBLANES, :]
        p2 = prev[SUBLANES - 2:SUBLANES - 1, :]
        row = lax.broadcasted_iota(I32, u.shape, 0)
        u1 = jnp.where(row == 0, p1, pltpu.roll(u, 1, axis=0))
        u2 = jnp.where(row == 0, p2, jnp.where(row == 1, p1, pltpu.roll(u, 2, axis=0)))
        tails[i] = u[rows - SUBLANES:rows, :]
        if i == n_streams - 1:
            carry_ref[...] = tails[i]
        cw = cw_ref[...]
        y = b_g * (cw[0:1, :] * u2 + cw[1:2, :] * u1 + cw[2:3, :] * u + cb_ref[...])
        yconv_ref[rs, :] = y.astype(BF16)
        yield
        pq = _dot(hn, w_ref[:, C_CONV:C_CONV + C_Q])
        for h in range(NSA_HEADS):
            blk = pq[:, h * LANES:(h + 1) * LANES]
            ms = jnp.sum(blk * blk, axis=-1, keepdims=True) * (1.0 / NSA_DH)
            q_ref[rs, h * LANES:(h + 1) * LANES] = (
                blk * lax.rsqrt(ms + EPS) * qw_ref[:, h * LANES:(h + 1) * LANES]).astype(BF16)
        yield
        pkv = _dot(hn, w_ref[:, C_CONV + C_Q:C_CONV + C_Q + C_KN])
        kvc_ref[0, rs, :] = pkv[:, 0:LANES]
        kvc_ref[1, rs, :] = pkv[:, LANES:2 * LANES]
        lane = lax.broadcasted_iota(I32, (rows, LANES), 1)
        lo = lane < NSA_DH
        for j in range(2):
            blk = pkv[:, (2 + j) * LANES:(3 + j) * LANES]
            sq = blk * blk
            ms_lo = jnp.sum(jnp.where(lo, sq, 0.0), axis=-1, keepdims=True) * (1.0 / NSA_DH)
            ms_hi = jnp.sum(jnp.where(lo, 0.0, sq), axis=-1, keepdims=True) * (1.0 / NSA_DH)
            scale = jnp.where(lo, lax.rsqrt(ms_lo + EPS), lax.rsqrt(ms_hi + EPS))
            kn_ref[rs, j * LANES:(j + 1) * LANES] = (blk * scale * kw_ref[j:j + 1, :]).astype(BF16)
        yield
        for j in range(rows // LANES):
            vt_ref[i * (rows // LANES) + j] = _dot_nt(wvt_ref[...], hn[j * LANES:(j + 1) * LANES, :]).astype(BF16)
        pg = _dot(hn, w_ref[:, C_CONV + C_Q + C_KN:C_ALL])
        g_ref[rs, :] = jax.nn.sigmoid(pg)
        yield

    _emit_staggered([stream(i) for i in range(n_streams)], n_stages=5)


def _in_proj(x2d, nw, w_all, wvt, cw, cb, qw, kw, seq, tm=1024):
    T = x2d.shape[0]
    const = lambda i: (0, 0)
    tile = lambda i: (i, 0)
    return pl.pallas_call(
        functools.partial(_in_proj_kernel, seq // tm),
        grid=(T // tm,),
        in_specs=[
            pl.BlockSpec((tm, D_MODEL), tile),
            pl.BlockSpec((1, D_MODEL), const),
            pl.BlockSpec((D_MODEL, C_ALL), const),
            pl.BlockSpec((2 * LANES, D_MODEL), const),
            pl.BlockSpec((3, CONV_WIDTH), const),
            pl.BlockSpec((1, CONV_WIDTH), const),
            pl.BlockSpec((1, C_Q), const),
            pl.BlockSpec((2, LANES), const),
        ],
        out_specs=[
            pl.BlockSpec((tm, CONV_WIDTH), tile),
            pl.BlockSpec((tm, C_Q), tile),
            pl.BlockSpec((2, tm, LANES), lambda i: (0, i, 0)),
            pl.BlockSpec((tm, 2 * LANES), tile),
            pl.BlockSpec((tm // LANES, 2 * LANES, LANES), lambda i: (i, 0, 0)),
            pl.BlockSpec((tm, C_G), tile),
        ],
        out_shape=[
            jax.ShapeDtypeStruct((T, CONV_WIDTH), BF16),
            jax.ShapeDtypeStruct((T, C_Q), BF16),
            jax.ShapeDtypeStruct((2, T, LANES), F32),
            jax.ShapeDtypeStruct((T, 2 * LANES), BF16),
            jax.ShapeDtypeStruct((T // LANES, 2 * LANES, LANES), BF16),
            jax.ShapeDtypeStruct((T, C_G), F32),
        ],
        scratch_shapes=[pltpu.VMEM((SUBLANES, CONV_WIDTH), F32)],
        compiler_params=_cparams(("arbitrary",), 48),
    )(x2d, nw, w_all, wvt, cw, cb, qw, kw)


def _compress_kernel(kvc_ref, pe_ref, w1_ref, w2k_ref, w2vt_ref, knw_ref, out_ref):
    n_rows = out_ref.shape[3]
    acc_a = jnp.zeros((n_rows, 2 * LANES), F32)
    acc_b = jnp.zeros((n_rows, 2 * LANES), F32)
    for l in range(CMP_STRIDE):
        rows = pl.ds(l, n_rows, stride=CMP_STRIDE)
        x = jnp.concatenate([kvc_ref[0, rows, :], kvc_ref[1, rows, :]], axis=1)
        acc_a = acc_a + _dot((x + pe_ref[l:l + 1, :]).astype(BF16), w1_ref[l])
        acc_b = acc_b + _dot((x + pe_ref[CMP_STRIDE + l:CMP_STRIDE + l + 1, :]).astype(BF16),
                             w1_ref[CMP_STRIDE + l])
    hid = _gelu(acc_a + pltpu.roll(acc_b, n_rows - 1, axis=0)).astype(BF16)
    row = lax.broadcasted_iota(I32, (n_rows, LANES), 0)
    col = lax.broadcasted_iota(I32, (LANES, n_rows), 1)
    keys = _dot(hid, w2k_ref[...])
    vals = _dot_nt(w2vt_ref[...], hid)
    for g in range(NSA_GROUPS):
        out = keys[:, g * LANES:(g + 1) * LANES]
        ms = jnp.sum(out * out, axis=-1, keepdims=True) * (1.0 / NSA_DH)
        out = out * lax.rsqrt(ms + EPS) * knw_ref[g]
        out_ref[0, 0, g] = jnp.where(row < n_rows - 1, out, 0.0).astype(BF16)
        out_ref[0, 1, g] = jnp.where(col < n_rows - 1, vals[g * LANES:(g + 1) * LANES, :], 0.0).astype(BF16)


def _compress(kvc, pe, w1, w2k, w2vt, knw, S):
    B = kvc.shape[1] // S
    n_rows = S // CMP_STRIDE
    return pl.pallas_call(
        _compress_kernel,
        grid=(B,),
        in_specs=[
            pl.BlockSpec((2, S, LANES), lambda b: (0, b, 0)),
            pl.BlockSpec((CMP_BLOCK, 2 * LANES), lambda b: (0, 0)),
            pl.BlockSpec((CMP_BLOCK, 2 * LANES, 2 * LANES), lambda b: (0, 0, 0)),
            pl.BlockSpec((2 * LANES, 2 * LANES), lambda b: (0, 0)),
            pl.BlockSpec((2 * LANES, 2 * LANES), lambda b: (0, 0)),
            pl.BlockSpec((NSA_GROUPS, 1, LANES), lambda b: (0, 0, 0)),
        ],
        out_specs=pl.BlockSpec((1, 2, NSA_GROUPS, n_rows, LANES), lambda b: (b, 0, 0, 0, 0)),
        out_shape=jax.ShapeDtypeStruct((B, 2, NSA_GROUPS, n_rows, LANES), BF16),
        compiler_params=_cparams(("arbitrary",), 32),
    )(kvc, pe, w1, w2k, w2vt, knw)


SEL_CHUNK = 512


def _nsa_kernel(q_ref, kn_ref, vt_ref, cmp_ref, g_ref, selmap_ref, y_ref, sel_ref, acc_ref):
    tq = q_ref.shape[1]
    cols = NSA_HEADS * tq
    i = pl.program_id(1)
    t0 = i * tq
    qb = q_ref[0]
    qs = jnp.concatenate([qb[:, h * LANES:(h + 1) * LANES] for h in range(NSA_HEADS)], axis=0)
    qpos = t0 + jnp.bitwise_and(lax.broadcasted_iota(I32, (1, cols), 1), tq - 1)
    both = lambda kv: (cmp_ref[0, kv, 0].astype(F32) + cmp_ref[0, kv, 1].astype(F32)).astype(BF16)

    n_win = WINDOW + tq
    w_start = pl.multiple_of(jnp.maximum(t0 - WINDOW, 0), LANES)
    w_unit = w_start // LANES
    k_w = kn_ref[0, pl.ds(w_start, n_win), LANES:2 * LANES]
    vt_w = jnp.concatenate([vt_ref[w_unit + u, LANES:2 * LANES, :] for u in range(n_win // LANES)], axis=1)
    diff = (qpos - w_start) - lax.broadcasted_iota(I32, (n_win, cols), 0)
    in_window = lax.bitcast_convert_type(diff, jnp.uint32) < jnp.uint32(WINDOW)
    s_w = jnp.where(in_window, _dot_nt(k_w, qs), NEG)
    p_w = jnp.exp(s_w - jnp.max(s_w, axis=0, keepdims=True))
    l_w = jnp.sum(p_w, axis=0, keepdims=True)
    o_w = _dot(vt_w, p_w.astype(BF16))

    n_cmp = cmp_ref.shape[3]
    s = _dot_nt(both(0), qs)
    cmp_last = lax.broadcasted_iota(I32, (n_cmp, cols), 0) * CMP_STRIDE + (CMP_BLOCK - 1)
    vis_c = cmp_last <= qpos
    s = jnp.where(vis_c, s, NEG)
    m = jnp.max(s, axis=0, keepdims=True)
    m = jnp.where(m > 0.5 * NEG, m, 0.0)
    e = jnp.where(vis_c, jnp.exp(s - m), 0.0)
    d = jnp.sum(e, axis=0, keepdims=True)
    p_c = e / jnp.where(d > 0.0, d, 1.0)
    o_c = _dot(both(1), p_c.astype(BF16))

    n_sel = selmap_ref.shape[0]
    blk = lax.broadcasted_iota(I32, (n_sel, tq), 0)
    cur = jnp.right_shift(t0 + lax.broadcasted_iota(I32, (n_sel, tq), 1),
                          SEL_BLOCK.bit_length() - 1)
    forced = (blk == 0) | (blk == cur) | (blk == cur - 1)
    chosen = []
    for g in range(NSA_GROUPS):
        psum = p_c[:, g * NSA_REP * tq:(g * NSA_REP + 1) * tq]
        for r in range(1, NSA_REP):
            psum = psum + p_c[:, (g * NSA_REP + r) * tq:(g * NSA_REP + r + 1) * tq]
        p_hi = psum.astype(BF16)
        p_lo = (psum - p_hi.astype(F32)).astype(BF16)
        imp = _dot(selmap_ref[...], p_hi) + _dot(selmap_ref[...], p_lo)
        imp = jnp.where(blk > cur, -jnp.inf, jnp.where(forced, jnp.inf, imp))
        rank = jnp.zeros((n_sel, tq), I32)
        for jp in range(n_sel):
            other = imp[jp:jp + 1, :]
            beats = (other > imp) | ((other == imp) & (blk > jp))
            rank = rank + beats.astype(I32)
        chosen += [rank < min(SEL_TOPN, n_sel)] * NSA_REP
    sel_ref[...] = jnp.where(jnp.concatenate(chosen, axis=1), qpos, -1)

    acc_ref[...] = jnp.zeros(acc_ref.shape, F32)
    units = SEL_CHUNK // LANES
    blocks = SEL_CHUNK // SEL_BLOCK
    koff = lax.broadcasted_iota(I32, (SEL_BLOCK, cols), 0)

    def sel_chunk(c, carry):
        m_old, l_old = carry
        start = pl.multiple_of(c * SEL_CHUNK, SEL_CHUNK)
        k = kn_ref[0, pl.ds(start, SEL_CHUNK), 0:LANES]
        vt = jnp.concatenate([vt_ref[c * units + u, 0:LANES, :] for u in range(units)], axis=1)
        sc = _dot_nt(k, qs)
        parts = []
        for b in range(blocks):
            limit = sel_ref[pl.ds(c * blocks + b, 1), :] - (start + b * SEL_BLOCK)
            parts.append(jnp.where(koff <= limit, sc[b * SEL_BLOCK:(b + 1) * SEL_BLOCK], NEG))
        sc = jnp.concatenate(parts, axis=0)
        m_new = jnp.maximum(m_old, jnp.max(sc, axis=0, keepdims=True))
        alpha = jnp.exp(m_old - m_new)
        p = jnp.exp(sc - m_new)
        acc_ref[...] = alpha * acc_ref[...] + _dot(vt, p.astype(BF16))
        return m_new, alpha * l_old + jnp.sum(p, axis=0, keepdims=True)

    init = (jnp.full((1, cols), NEG, F32), jnp.zeros((1, cols), F32))
    _, l_s = lax.fori_loop(0, (t0 + tq + SEL_CHUNK - 1) // SEL_CHUNK, sel_chunk, init)
    o_s = acc_ref[...]

    g_t = [g_ref[0, :, g * LANES:(g + 1) * LANES].T for g in range(NSA_GROUPS)]
    gate = lambda br: jnp.concatenate(
        [g_t[h // NSA_REP][3 * (h % NSA_REP) + br:3 * (h % NSA_REP) + br + 1, :] for h in range(NSA_HEADS)], axis=1)
    y_t = o_c * gate(0) + o_s * (gate(1) / l_s) + o_w * (gate(2) / l_w)
    for h in range(NSA_HEADS):
        y_ref[0, :, h * LANES:(h + 1) * LANES] = y_t[:, h * tq:(h + 1) * tq].T.astype(BF16)


def _nsa(q, kn, vt, kvcmp, gates, selmap, tq=256):
    B, S = q.shape[0], q.shape[1]
    n_sel = S // SEL_BLOCK
    cols = NSA_HEADS * tq
    units = S // LANES
    n_cmp = kvcmp.shape[3]
    return pl.pallas_call(
        _nsa_kernel,
        grid=(B, S // tq),
        in_specs=[
            pl.BlockSpec((1, tq, C_Q), lambda b, i: (b, i, 0)),
            pl.BlockSpec((1, S, 2 * LANES), lambda b, i: (b, 0, 0)),
            pl.BlockSpec((units, 2 * LANES, LANES), lambda b, i: (b, 0, 0)),
            pl.BlockSpec((1, 2, NSA_GROUPS, n_cmp, LANES), lambda b, i: (b, 0, 0, 0, 0)),
            pl.BlockSpec((1, tq, C_G), lambda b, i: (b, i, 0)),
            pl.BlockSpec((n_sel, n_cmp), lambda b, i: (0, 0)),
        ],
        out_specs=pl.BlockSpec((1, tq, C_Q), lambda b, i: (b, i, 0)),
        out_shape=jax.ShapeDtypeStruct((B, S, C_Q), BF16),
        scratch_shapes=[
            pltpu.VMEM((n_sel, cols), I32),
            pltpu.VMEM((LANES, cols), F32),
        ],
        compiler_params=_cparams(("arbitrary", "arbitrary"), 56),
    )(q, kn, vt, kvcmp, gates, selmap)


def _mem_kv_kernel(mem_ref, nw_ref, wk_ref, wv_ref, kw_ref, k_ref, v_ref):
    mn = _rms(mem_ref[0], nw_ref[...]).astype(BF16)
    k = _dot(mn, wk_ref[...])
    for h in range(X_HEADS):
        sl = slice(h * X_DH, (h + 1) * X_DH)
        k_ref[0, :, sl] = _rms(k[:, sl], kw_ref[...]).astype(BF16)
    v_ref[0] = _dot(mn, wv_ref[...]).astype(BF16)


def _mem_kv(mem, nw, wk, wv, kw):
    B, M, D = mem.shape
    blk = pl.BlockSpec((1, M, D), lambda b: (b, 0, 0))
    const = lambda b: (0, 0)
    return pl.pallas_call(
        _mem_kv_kernel,
        grid=(B,),
        in_specs=[blk, pl.BlockSpec((1, D), const), pl.BlockSpec((D, D), const),
                  pl.BlockSpec((D, D), const), pl.BlockSpec((1, X_DH), const)],
        out_specs=[blk, blk],
        out_shape=[jax.ShapeDtypeStruct((B, M, D), BF16)] * 2,
        compiler_params=_cparams(("arbitrary",), 32),
    )(mem, nw, wk, wv, kw)


def _post_mix_kernel(x_ref, yc_ref, yn_ref, woc_ref, won_ref, xnw_ref, xq_ref, xqw_ref, km_ref, vm_ref,
                     xo_ref, fnw_ref, pwq_ref, keys_ref, x2_ref, h3_ref, sc_ref):
    tm = x_ref.shape[0]
    n_streams = 2
    rows_per = tm // n_streams

    def stream(i):
        rs = slice(i * rows_per, (i + 1) * rows_per)
        x1 = x_ref[rs, :] + _dot(yc_ref[rs, :], woc_ref[...]) + _dot(yn_ref[rs, :], won_ref[...])
        yield
        h2 = _rms(x1, xnw_ref[...]).astype(BF16)
        qx = _dot(h2, xq_ref[...])
        yield
        heads = []
        for h in range(X_HEADS):
            sl = slice(h * X_DH, (h + 1) * X_DH)
            qh = _rms(qx[:, sl], xqw_ref[...]).astype(BF16)
            s = _dot_nt(qh, km_ref[0, :, sl])
            e = jnp.exp(s - jnp.max(s, axis=-1, keepdims=True))
            p = e / jnp.sum(e, axis=-1, keepdims=True)
            heads.append(_dot(p.astype(BF16), vm_ref[0, :, sl]))
        yield
        o = jnp.concatenate(heads, axis=1).astype(BF16)
        x2 = x1 + _dot(o, xo_ref[...])
        x2_ref[rs, :] = x2
        yield
        h3 = _rms(x2, fnw_ref[...]).astype(BF16)
        h3_ref[rs, :] = h3
        qp = _dot(h3, pwq_ref[...]).astype(BF16)
        for c in range(2 * PEER_HEADS):
            sl = slice(c * PEER_HALF, (c + 1) * PEER_HALF)
            for j in range(rows_per // LANES):
                sc_ref[c, i * (rows_per // LANES) + j] = _dot_nt(keys_ref[c], qp[j * LANES:(j + 1) * LANES, sl])
        yield

    _emit_staggered([stream(i) for i in range(n_streams)], n_stages=5)


def _post_mix(x2d, yconv, ynsa, woc, won, xnw, xq, xqw, kmem, vmem, xo, fnw, pwq, keys, seq, tm=512):
    T, D = x2d.shape
    M = kmem.shape[1]
    n_sc = 2 * PEER_HEADS * PEER_NKEYS
    tile = lambda i: (i, 0)
    const = lambda i: (0, 0)
    per_batch = lambda i: ((i * tm) // seq, 0, 0)
    return pl.pallas_call(
        _post_mix_kernel,
        grid=(T // tm,),
        in_specs=[
            pl.BlockSpec((tm, D), tile),
            pl.BlockSpec((tm, CONV_WIDTH), tile),
            pl.BlockSpec((tm, C_Q), tile),
            pl.BlockSpec((CONV_WIDTH, D), const),
            pl.BlockSpec((C_Q, D), const),
            pl.BlockSpec((1, D), const),
            pl.BlockSpec((D, D), const),
            pl.BlockSpec((1, X_DH), const),
            pl.BlockSpec((1, M, D), per_batch),
            pl.BlockSpec((1, M, D), per_batch),
            pl.BlockSpec((D, D), const),
            pl.BlockSpec((1, D), const),
            pl.BlockSpec((D, n_sc), const),
            pl.BlockSpec((2 * PEER_HEADS, PEER_NKEYS, PEER_HALF), lambda i: (0, 0, 0)),
        ],
        out_specs=[pl.BlockSpec((tm, D), tile), pl.BlockSpec((tm, D), tile),
                   pl.BlockSpec((2 * PEER_HEADS, tm // LANES, PEER_NKEYS, LANES), lambda i: (0, i, 0, 0))],
        out_shape=[
            jax.ShapeDtypeStruct((T, D), F32),
            jax.ShapeDtypeStruct((T, D), BF16),
            jax.ShapeDtypeStruct((2 * PEER_HEADS, T // LANES, PEER_NKEYS, LANES), F32),
        ],
        compiler_params=_cparams(("arbitrary",), 56),
    )(x2d, yconv, ynsa, woc, won, xnw, xq, xqw, kmem, vmem, xo, fnw, pwq, keys)


STAIR = [(a, b) for a in range(PEER_TOPK) for b in range(PEER_TOPK // (a + 1))]
N_STAIR_VREGS = -(-len(STAIR) // SUBLANES)


def _stair_vregs(axis, k):
    return sorted({r // SUBLANES for r, ab in enumerate(STAIR) if ab[axis] == k})


def _stair_tables():
    rows = N_STAIR_VREGS * SUBLANES
    pad = rows - len(STAIR)
    ta = jnp.asarray([float(a) for a, _ in STAIR] + [-1.0] * pad, F32)
    tb = jnp.asarray([float(b) for _, b in STAIR] + [-1.0] * pad, F32)
    return jnp.broadcast_to(ta[:, None], (rows, LANES)), jnp.broadcast_to(tb[:, None], (rows, LANES))


def _make_topk(sc_ref, ta_ref, tb_ref, s_out, e_out):
    K = PEER_TOPK
    NV = N_STAIR_VREGS
    kidx = lax.broadcasted_iota(I32, (PEER_NKEYS, LANES), 0).astype(F32)
    ridx = [(lax.broadcasted_iota(I32, (SUBLANES, LANES), 0) + SUBLANES * j).astype(F32) for j in range(NV)]
    ta = [ta_ref[SUBLANES * j:SUBLANES * (j + 1), :] for j in range(NV)]
    tb = [tb_ref[SUBLANES * j:SUBLANES * (j + 1), :] for j in range(NV)]
    zero = jnp.zeros((SUBLANES, LANES), F32)

    def top1(s):
        m = jnp.max(s, axis=0, keepdims=True)
        idx = jnp.min(jnp.where(s == m, kidx, float(PEER_NKEYS)), axis=0, keepdims=True)
        return m, idx, jnp.where(kidx == idx, -jnp.inf, s)

    def sorted_top(s, table, axis):
        val = [zero] * NV
        key = [zero] * NV
        for k in range(K):
            m, idx, s = top1(s)
            for j in _stair_vregs(axis, k):
                hit = table[j] == float(k)
                val[j] = jnp.where(hit, m, val[j])
                key[j] = jnp.where(hit, idx, key[j])
        return val, key

    def top_half(h, g, which):
        return sorted_top(sc_ref[2 * h + which, g], (ta, tb)[which], which)

    def select(first, second):
        (v1, k1), (v2, k2) = first, second
        cand = [jnp.where(ta[j] >= 0.0, v1[j] + v2[j], -jnp.inf) for j in range(NV)]
        ce = [k1[j] * float(PEER_NKEYS) + k2[j] for j in range(NV)]
        sub = lax.broadcasted_iota(I32, (SUBLANES, LANES), 0)
        best_s = [zero] * (K // SUBLANES)
        best_e = [zero] * (K // SUBLANES)
        for k in range(K):
            mx = cand[0]
            for j in range(1, NV):
                mx = jnp.maximum(mx, cand[j])
            m = jnp.max(mx, axis=0, keepdims=True)
            ix = jnp.where(cand[0] == m, ridx[0], float(NV * SUBLANES))
            for j in range(1, NV):
                ix = jnp.minimum(ix, jnp.where(cand[j] == m, ridx[j], float(NV * SUBLANES)))
            idx = jnp.min(ix, axis=0, keepdims=True)
            hits = [ridx[j] == idx for j in range(NV)]
            es = jnp.where(hits[0], ce[0], 0.0)
            for j in range(1, NV):
                es = es + jnp.where(hits[j], ce[j], 0.0)
            cand = [jnp.where(hits[j], -jnp.inf, cand[j]) for j in range(NV)]
            here = sub == (k % SUBLANES)
            best_s[k // SUBLANES] = jnp.where(here, m, best_s[k // SUBLANES])
            best_e[k // SUBLANES] = jnp.where(here, jnp.sum(es, axis=0, keepdims=True), best_e[k // SUBLANES])
        return jnp.concatenate(best_s, axis=0), jnp.concatenate(best_e, axis=0)

    def store(h, best):
        rows = pl.ds(pl.multiple_of(h * K, K), K)
        s_out[rows, :] = best[0]
        e_out[rows, :] = best[1]

    def finish(rows, ei_dst, ej_dst, gate_dst):
        s_all = s_out[...]
        gates = []
        for h in range(PEER_HEADS):
            sh = s_all[h * K:(h + 1) * K]
            ex = jnp.exp(sh - sh[0:1])
            gates.append(ex / jnp.sum(ex, axis=0, keepdims=True))
        gate_dst[rows, :] = jnp.concatenate(gates, axis=0).T
        e_int = e_out[...].T.astype(I32)
        ei_dst[rows, :] = jnp.right_shift(e_int, PEER_NKEYS.bit_length() - 1)
        ej_dst[rows, :] = jnp.bitwise_and(e_int, PEER_NKEYS - 1)

    return top_half, select, store, finish


G_ROWS = PEER_NKEYS // 2
G_PITCH = G_ROWS + SUBLANES
HI_MASK = 0xFFFF0000


def _peer_kernel(sc_ref, ta_ref, tb_ref, h_ref, wd_ref, wu_ref, x2_ref, out_ref,
                 g_ref, ei_ref, ej_ref, gate_ref, s_out, e_out):
    tm = h_ref.shape[0]
    ec = wd_ref.shape[1]
    ipc = ec // LANES
    half = ipc // 2
    t = pl.program_id(0)
    c = pl.program_id(1)
    n_grp = tm // LANES
    steps_per_grp = PEER_EXPERTS // ec // n_grp
    heads_per_step = PEER_HEADS // steps_per_grp
    rd = (t + 1) % 2
    wr = t % 2
    ei_rd, ej_rd, gate_rd = ei_ref.at[rd], ej_ref.at[rd], gate_ref.at[rd]

    @pl.when((t == 0) & (c == 0))
    def _first_step():
        ei_ref[...] = jnp.zeros(ei_ref.shape, I32)
        ej_ref[...] = jnp.zeros(ej_ref.shape, I32)
        gate_ref[...] = jnp.zeros(gate_ref.shape, F32)

    @pl.when(c == 0)
    def _scatter():
        out_ref[...] = x2_ref[...]
        as_bf16 = lambda v: v.astype(F32).astype(BF16)
        i_of_m = as_bf16(lax.broadcasted_iota(I32, (PEER_NKEYS, 2 * LANES), 0))
        n = as_bf16(lax.broadcasted_iota(I32, (LANES, LANES), 0))
        one = jnp.ones((LANES, LANES), BF16)
        zero = jnp.zeros((LANES, LANES), BF16)

        def pair(u, carry):
            ta_, tb_ = pl.ds(2 * u, 1), pl.ds(2 * u + 1, 1)
            ri = as_bf16(jnp.concatenate([ei_rd[ta_, :], ei_rd[tb_, :]], axis=1))
            rg = (0.5 * jnp.concatenate([gate_rd[ta_, :], gate_rd[tb_, :]], axis=1)).astype(BF16)
            c_t = jnp.where(i_of_m == ri, rg, jnp.zeros_like(rg))
            q_a = jnp.where(n == as_bf16(ej_rd[ta_, :]), one, zero)
            q_b = jnp.where(n == as_bf16(ej_rd[tb_, :]), one, zero)
            q_t = jnp.concatenate([jnp.concatenate([q_a, zero], axis=1),
                                   jnp.concatenate([zero, q_b], axis=1)], axis=0)
            g2 = _dot_nt(c_t, q_t).astype(BF16)
            for tok in range(2):
                word = pltpu.bitcast(g2[:, tok * LANES:(tok + 1) * LANES], jnp.uint32)
                g_ref[pl.ds(pl.multiple_of((2 * u + tok) * G_PITCH, SUBLANES), G_ROWS), :] = word
            return carry

        lax.fori_loop(0, tm // 2, pair, 0, unroll=64)

    top_half, select, store, finish = _make_topk(sc_ref, ta_ref, tb_ref, s_out, e_out)
    grp = c // steps_per_grp
    h0 = (c % steps_per_grp) * heads_per_step

    def experts_slice(k):
        a = _dot(h_ref[...], wd_ref[:, 2 * k * LANES:2 * (k + 1) * LANES]).astype(BF16)
        word = g_ref[pl.ds(c * half + k, tm, stride=G_PITCH), :]
        g_lo = lax.bitcast_convert_type(jnp.left_shift(word, 16), F32).astype(BF16)
        g_hi = lax.bitcast_convert_type(jnp.bitwise_and(word, jnp.uint32(HI_MASK)), F32).astype(BF16)
        return jnp.concatenate([_gelu2(a[:, 0:LANES]) * g_lo, _gelu2(a[:, LANES:2 * LANES]) * g_hi], axis=1)

    assert half == 2 * heads_per_step
    halves, z = [], []
    for k in range(half):
        halves.append(top_half(h0 + k // 2, grp, k % 2))
        z.append(experts_slice(k))
    best = [select(halves[2 * hh], halves[2 * hh + 1]) for hh in range(heads_per_step - 1)]
    out_ref[...] += _dot(jnp.concatenate(z, axis=1), wu_ref[...])
    best.append(select(halves[-2], halves[-1]))
    for hh in range(heads_per_step):
        store(h0 + hh, best[hh])

    @pl.when(c % steps_per_grp == steps_per_grp - 1)
    def _group_done():
        rows = pl.ds(pl.multiple_of(grp * LANES, LANES), LANES)
        finish(rows, ei_ref.at[wr], ej_ref.at[wr], gate_ref.at[wr])


def _peer(sc, h3, wd_t, wu, x2, tm=512, ec=2048):
    T, D = h3.shape
    tm = min(tm, T)
    n_tiles = T // tm
    n_grp = tm // LANES
    n_chunks = wu.shape[0] // ec
    assert n_chunks % n_grp == 0 and PEER_HEADS % (n_chunks // n_grp) == 0
    ta, tb = _stair_tables()
    prev = lambda t, c: (jnp.maximum(t - 1, 0), 0)
    return pl.pallas_call(
        _peer_kernel,
        grid=(n_tiles + 1, n_chunks),
        in_specs=[
            pl.BlockSpec((2 * PEER_HEADS, n_grp, PEER_NKEYS, LANES),
                         lambda t, c: (0, jnp.minimum(t, n_tiles - 1), 0, 0)),
            pl.BlockSpec(ta.shape, lambda t, c: (0, 0)),
            pl.BlockSpec(tb.shape, lambda t, c: (0, 0)),
            pl.BlockSpec((tm, D), prev),
            pl.BlockSpec((D, ec), lambda t, c: (0, c)),
            pl.BlockSpec((ec, D), lambda t, c: (c, 0)),
            pl.BlockSpec((tm, D), prev),
        ],
        out_specs=pl.BlockSpec((tm, D), prev),
        out_shape=jax.ShapeDtypeStruct((T, D), F32),
        scratch_shapes=[
            pltpu.VMEM((tm * G_PITCH, LANES), jnp.uint32),
            pltpu.VMEM((2, tm, LANES), I32),
            pltpu.VMEM((2, tm, LANES), I32),
            pltpu.VMEM((2, tm, LANES), F32),
            pltpu.VMEM((PEER_HEADS * PEER_TOPK, LANES), F32),
            pltpu.VMEM((PEER_HEADS * PEER_TOPK, LANES), F32),
        ],
        compiler_params=_cparams(("arbitrary", "arbitrary"), 60),
    )(sc, ta, tb, h3, wd_t, wu, x2)


def _pad_half(a, g):
    z = jnp.zeros_like(a)
    return jnp.concatenate([a, z] if g == 0 else [z, a], axis=-1)


def kernel(x, mem, mix_norm_w, w_in, conv_w, conv_b, cmp_pe, cmp_w1, cmp_w2, q_norm_w, k_norm_w, w_out,
           xattn_norm_w, mem_norm_w, xq, xk, xv, xo, xq_norm_w, xk_norm_w, ffn_norm_w, peer_wq, peer_keys,
           peer_down, peer_up):
    B, S, D = x.shape
    T = B * S
    l = 0
    G, R, dh = NSA_GROUPS, NSA_REP, NSA_DH

    w = w_in[l]
    o_q = C_CONV
    o_kv = o_q + NSA_HEADS * dh
    o_g = o_kv + C_KV
    wq = w[:, o_q:o_kv].reshape(D, G, R, dh)
    wq_pad = jnp.concatenate([_pad_half(wq[:, g], g).reshape(D, R * LANES) for g in range(G)], axis=1)
    wg = w[:, o_g:].reshape(D, G, R * 3)
    wg_pad = jnp.pad(wg, ((0, 0), (0, 0), (0, LANES - R * 3))).reshape(D, G * LANES)
    wkv = w[:, o_kv:o_g].reshape(D, 6, LANES)
    wkn = wkv[:, jnp.array([0, 1, 2, 4])].reshape(D, C_KN)
    wvt = wkv[:, jnp.array([3, 5])].reshape(D, 2 * LANES).T.astype(BF16)
    w_all = jnp.concatenate([w[:, :o_q], wq_pad, wkn, wg_pad], axis=1).astype(BF16)
    qw = q_norm_w[l] * dh ** -0.5
    qw_pad = jnp.concatenate([jnp.tile(_pad_half(qw, g), R) for g in range(G)]).reshape(1, C_Q)
    kw = jnp.stack([jnp.tile(k_norm_w[l, 1], G), jnp.tile(k_norm_w[l, 2], G)])

    yconv, q, kvc, kn, vt, gates = _in_proj(
        x.reshape(T, D), mix_norm_w[l].reshape(1, D), w_all, wvt, conv_w[l], conv_b[l].reshape(1, CONV_WIDTH),
        qw_pad, kw, S)

    n_rows = S // CMP_STRIDE
    kvg = jnp.array([0, 0, 1, 1])
    pe = jnp.concatenate([cmp_pe[l, kv] for kv in (0, 0, 1, 1)], axis=-1)
    w1 = cmp_w1[l].reshape(2, CMP_BLOCK, dh, dh)[kvg]
    w1 = jnp.einsum('qp,qjdh->jqdph', jnp.eye(4, dtype=F32), w1).reshape(CMP_BLOCK, 4 * dh, 4 * dh).astype(BF16)
    w2k = jnp.zeros((4 * dh, G * LANES), F32)
    w2vt = jnp.zeros((G * LANES, 4 * dh), F32)
    for g in range(G):
        o = g * LANES + g * dh
        w2k = w2k.at[g * dh:(g + 1) * dh, o:o + dh].set(cmp_w2[l, 0])
        w2vt = w2vt.at[o:o + dh, (G + g) * dh:(G + g + 1) * dh].set(cmp_w2[l, 1].T)
    knw = jnp.stack([_pad_half(k_norm_w[l, 0], g) for g in range(G)]).reshape(G, 1, LANES)
    kvcmp = _compress(kvc, pe, w1, w2k.astype(BF16), w2vt.astype(BF16), knw, S)

    n_sel = S // SEL_BLOCK
    cmp_start = jnp.arange(n_rows) * CMP_STRIDE
    sel_start = jnp.arange(n_sel) * SEL_BLOCK
    selmap_t = ((cmp_start[None, :] < sel_start[:, None] + SEL_BLOCK)
                & (cmp_start[None, :] + CMP_BLOCK > sel_start[:, None])
                & (jnp.arange(n_rows)[None, :] < n_rows - 1)).astype(BF16)
    ynsa = _nsa(q.reshape(B, S, C_Q), kn.reshape(B, S, 2 * LANES), vt, kvcmp, gates.reshape(B, S, C_G), selmap_t)

    kmem, vmem = _mem_kv(mem, mem_norm_w[l].reshape(1, D), xk[l].astype(BF16), xv[l].astype(BF16),
                         xk_norm_w[l].reshape(1, X_DH))

    wo = w_out[l]
    won = wo[CONV_WIDTH:].reshape(G, R, dh, D)
    won_pad = jnp.concatenate(
        [jnp.concatenate([won[g], jnp.zeros_like(won[g])] if g == 0 else [jnp.zeros_like(won[g]), won[g]],
                         axis=1).reshape(R * LANES, D) for g in range(G)], axis=0).astype(BF16)
    keys = peer_keys[l].reshape(2 * PEER_HEADS, PEER_NKEYS, PEER_HALF).astype(BF16)
    x2, h3, sc = _post_mix(
        x.reshape(T, D), yconv, ynsa.reshape(T, C_Q), wo[:CONV_WIDTH].astype(BF16), won_pad,
        xattn_norm_w[l].reshape(1, D), xq[l].astype(BF16),
        (xq_norm_w[l] * X_DH ** -0.5).reshape(1, X_DH), kmem, vmem, xo[l].astype(BF16),
        ffn_norm_w[l].reshape(1, D), peer_wq[l].astype(BF16), keys, S)

    out = _peer(sc, h3, peer_down[l].T.astype(BF16), peer_up[l].astype(BF16), x2)
    return out.reshape(B, S, D)
```

```python
import functools

import jax
import jax.numpy as jnp
from jax import lax
from jax.experimental import pallas as pl
from jax.experimental.pallas import tpu as pltpu

F32 = jnp.float32
BF16 = jnp.bfloat16
I32 = jnp.int32

EPS = 1e-6
NEG = -1e30
LANES = 128
SUBLANES = 8

D_MODEL = 1024
CONV_WIDTH = 512
NSA_HEADS = 8
NSA_GROUPS = 2
NSA_REP = NSA_HEADS // NSA_GROUPS
NSA_DH = 64
CMP_BLOCK = 32
CMP_STRIDE = 16
SEL_BLOCK = 64
SEL_TOPN = 16
WINDOW = 512
X_HEADS = 4
X_DH = D_MODEL // X_HEADS
PEER_HEADS = 8
PEER_NKEYS = 128
PEER_HALF = 128
PEER_TOPK = 16
PEER_EXPERTS = PEER_NKEYS * PEER_NKEYS

NT_DIMS = (((1,), (1,)), ((), ()))


def _dot(a, b):
    return jnp.dot(a, b, preferred_element_type=F32)


def _dot_nt(a, b):
    return lax.dot_general(a, b, NT_DIMS, preferred_element_type=F32)


def _gelu2(x):
    return x * (1.0 + lax.erf(x * (2.0 ** -0.5)))


def _gelu(x):
    return 0.5 * _gelu2(x)


def _rms(x, w):
    ms = jnp.mean(x * x, axis=-1, keepdims=True)
    return x * lax.rsqrt(ms + EPS) * w


def _emit_staggered(streams, n_stages):
    for step in range(n_stages + len(streams) - 1):
        for k, stream in enumerate(streams):
            if 0 <= step - k < n_stages:
                next(stream)


def _cparams(sem, vmem_mb):
    return pltpu.CompilerParams(dimension_semantics=sem, vmem_limit_bytes=vmem_mb * 1024 * 1024)


C_CONV = 3 * CONV_WIDTH
C_Q = NSA_HEADS * LANES
C_KV = 6 * NSA_GROUPS * NSA_DH
C_KN = 4 * LANES
C_G = NSA_GROUPS * LANES
C_ALL = C_CONV + C_Q + C_KN + C_G


def _in_proj_kernel(tiles_per_seq, x_ref, nw_ref, w_ref, wvt_ref, cw_ref, cb_ref, qw_ref, kw_ref,
                    yconv_ref, q_ref, kvc_ref, kn_ref, vt_ref, g_ref, carry_ref):
    tm = x_ref.shape[0]
    n_streams = 2
    rows = tm // n_streams

    @pl.when((pl.program_id(0) % tiles_per_seq) == 0)
    def _sequence_start():
        carry_ref[...] = jnp.zeros(carry_ref.shape, F32)

    tails = {}

    def stream(i):
        rs = slice(i * rows, (i + 1) * rows)
        hn = _rms(x_ref[rs, :], nw_ref[...]).astype(BF16)
        yield
        p = _dot(hn, w_ref[:, 0:C_CONV])
        b_g = p[:, 0:CONV_WIDTH]
        u = p[:, CONV_WIDTH:2 * CONV_WIDTH] * p[:, 2 * CONV_WIDTH:3 * CONV_WIDTH]
        prev = carry_ref[...] if i == 0 else tails[i - 1]
        p1 = prev[SUBLANES - 1:SUBLANES, :]
        p2 = prev[SUBLANES - 2:SUBLANES - 1, :]
        row = lax.broadcasted_iota(I32, u.shape, 0)
        u1 = jnp.where(row == 0, p1, pltpu.roll(u, 1, axis=0))
        u2 = jnp.where(row == 0, p2, jnp.where(row == 1, p1, pltpu.roll(u, 2, axis=0)))
        tails[i] = u[rows - SUBLANES:rows, :]
        if i == n_streams - 1:
            carry_ref[...] = tails[i]
        cw = cw_ref[...]
        y = b_g * (cw[0:1, :] * u2 + cw[1:2, :] * u1 + cw[2:3, :] * u + cb_ref[...])
        yconv_ref[rs, :] = y.astype(BF16)
        yield
        pq = _dot(hn, w_ref[:, C_CONV:C_CONV + C_Q])
        for h in range(NSA_HEADS):
            blk = pq[:, h * LANES:(h + 1) * LANES]
            ms = jnp.sum(blk * blk, axis=-1, keepdims=True) * (1.0 / NSA_DH)
            q_ref[rs, h * LANES:(h + 1) * LANES] = (
                blk * lax.rsqrt(ms + EPS) * qw_ref[:, h * LANES:(h + 1) * LANES]).astype(BF16)
        yield
        pkv = _dot(hn, w_ref[:, C_CONV + C_Q:C_CONV + C_Q + C_KN])
        kvc_ref[0, rs, :] = pkv[:, 0:LANES]
        kvc_ref[1, rs, :] = pkv[:, LANES:2 * LANES]
        lane = lax.broadcasted_iota(I32, (rows, LANES), 1)
        lo = lane < NSA_DH
        for j in range(2):
            blk = pkv[:, (2 + j) * LANES:(3 + j) * LANES]
            sq = blk * blk
            ms_lo = jnp.sum(jnp.where(lo, sq, 0.0), axis=-1, keepdims=True) * (1.0 / NSA_DH)
            ms_hi = jnp.sum(jnp.where(lo, 0.0, sq), axis=-1, keepdims=True) * (1.0 / NSA_DH)
            scale = jnp.where(lo, lax.rsqrt(ms_lo + EPS), lax.rsqrt(ms_hi + EPS))
            kn_ref[rs, j * LANES:(j + 1) * LANES] = (blk * scale * kw_ref[j:j + 1, :]).astype(BF16)
        yield
        for j in range(rows // LANES):
            vt_ref[i * (rows // LANES) + j] = _dot_nt(wvt_ref[...], hn[j * LANES:(j + 1) * LANES, :]).astype(BF16)
        pg = _dot(hn, w_ref[:, C_CONV + C_Q + C_KN:C_ALL])
        g_ref[rs, :] = jax.nn.sigmoid(pg)
        yield

    _emit_staggered([stream(i) for i in range(n_streams)], n_stages=5)


def _in_proj(x2d, nw, w_all, wvt, cw, cb, qw, kw, seq, tm=1024):
    T = x2d.shape[0]
    const = lambda i: (0, 0)
    tile = lambda i: (i, 0)
    return pl.pallas_call(
        functools.partial(_in_proj_kernel, seq // tm),
        grid=(T // tm,),
        in_specs=[
            pl.BlockSpec((tm, D_MODEL), tile),
            pl.BlockSpec((1, D_MODEL), const),
            pl.BlockSpec((D_MODEL, C_ALL), const),
            pl.BlockSpec((2 * LANES, D_MODEL), const),
            pl.BlockSpec((3, CONV_WIDTH), const),
            pl.BlockSpec((1, CONV_WIDTH), const),
            pl.BlockSpec((1, C_Q), const),
            pl.BlockSpec((2, LANES), const),
        ],
        out_specs=[
            pl.BlockSpec((tm, CONV_WIDTH), tile),
            pl.BlockSpec((tm, C_Q), tile),
            pl.BlockSpec((2, tm, LANES), lambda i: (0, i, 0)),
            pl.BlockSpec((tm, 2 * LANES), tile),
            pl.BlockSpec((tm // LANES, 2 * LANES, LANES), lambda i: (i, 0, 0)),
            pl.BlockSpec((tm, C_G), tile),
        ],
        out_shape=[
            jax.ShapeDtypeStruct((T, CONV_WIDTH), BF16),
            jax.ShapeDtypeStruct((T, C_Q), BF16),
            jax.ShapeDtypeStruct((2, T, LANES), F32),
            jax.ShapeDtypeStruct((T, 2 * LANES), BF16),
            jax.ShapeDtypeStruct((T // LANES, 2 * LANES, LANES), BF16),
            jax.ShapeDtypeStruct((T, C_G), F32),
        ],
        scratch_shapes=[pltpu.VMEM((SUBLANES, CONV_WIDTH), F32)],
        compiler_params=_cparams(("arbitrary",), 48),
    )(x2d, nw, w_all, wvt, cw, cb, qw, kw)


def _compress_kernel(kvc_ref, pe_ref, w1_ref, w2k_ref, w2vt_ref, knw_ref, out_ref):
    n_rows = out_ref.shape[3]
    acc_a = jnp.zeros((n_rows, 2 * LANES), F32)
    acc_b = jnp.zeros((n_rows, 2 * LANES), F32)
    for l in range(CMP_STRIDE):
        rows = pl.ds(l, n_rows, stride=CMP_STRIDE)
        x = jnp.concatenate([kvc_ref[0, rows, :], kvc_ref[1, rows, :]], axis=1)
        acc_a = acc_a + _dot((x + pe_ref[l:l + 1, :]).astype(BF16), w1_ref[l])
        acc_b = acc_b + _dot((x + pe_ref[CMP_STRIDE + l:CMP_STRIDE + l + 1, :]).astype(BF16),
                             w1_ref[CMP_STRIDE + l])
    hid = _gelu(acc_a + pltpu.roll(acc_b, n_rows - 1, axis=0)).astype(BF16)
    row = lax.broadcasted_iota(I32, (n_rows, LANES), 0)
    col = lax.broadcasted_iota(I32, (LANES, n_rows), 1)
    keys = _dot(hid, w2k_ref[...])
    vals = _dot_nt(w2vt_ref[...], hid)
    for g in range(NSA_GROUPS):
        out = keys[:, g * LANES:(g + 1) * LANES]
        ms = jnp.sum(out * out, axis=-1, keepdims=True) * (1.0 / NSA_DH)
        out = out * lax.rsqrt(ms + EPS) * knw_ref[g]
        out_ref[0, 0, g] = jnp.where(row < n_rows - 1, out, 0.0).astype(BF16)
        out_ref[0, 1, g] = jnp.where(col < n_rows - 1, vals[g * LANES:(g + 1) * LANES, :], 0.0).astype(BF16)


def _compress(kvc, pe, w1, w2k, w2vt, knw, S):
    B = kvc.shape[1] // S
    n_rows = S // CMP_STRIDE
    return pl.pallas_call(
        _compress_kernel,
        grid=(B,),
        in_specs=[
            pl.BlockSpec((2, S, LANES), lambda b: (0, b, 0)),
            pl.BlockSpec((CMP_BLOCK, 2 * LANES), lambda b: (0, 0)),
            pl.BlockSpec((CMP_BLOCK, 2 * LANES, 2 * LANES), lambda b: (0, 0, 0)),
            pl.BlockSpec((2 * LANES, 2 * LANES), lambda b: (0, 0)),
            pl.BlockSpec((2 * LANES, 2 * LANES), lambda b: (0, 0)),
            pl.BlockSpec((NSA_GROUPS, 1, LANES), lambda b: (0, 0, 0)),
        ],
        out_specs=pl.BlockSpec((1, 2, NSA_GROUPS, n_rows, LANES), lambda b: (b, 0, 0, 0, 0)),
        out_shape=jax.ShapeDtypeStruct((B, 2, NSA_GROUPS, n_rows, LANES), BF16),
        compiler_params=_cparams(("arbitrary",), 32),
    )(kvc, pe, w1, w2k, w2vt, knw)


SEL_CHUNK = 512


def _nsa_kernel(q_ref, kn_ref, vt_ref, cmp_ref, g_ref, selmap_ref, y_ref, sel_ref, acc_ref):
    tq = q_ref.shape[1]
    cols = NSA_HEADS * tq
    i = pl.program_id(1)
    t0 = i * tq
    qb = q_ref[0]
    qs = jnp.concatenate([qb[:, h * LANES:(h + 1) * LANES] for h in range(NSA_HEADS)], axis=0)
    qpos = t0 + jnp.bitwise_and(lax.broadcasted_iota(I32, (1, cols), 1), tq - 1)
    both = lambda kv: (cmp_ref[0, kv, 0].astype(F32) + cmp_ref[0, kv, 1].astype(F32)).astype(BF16)

    n_win = WINDOW + tq
    w_start = pl.multiple_of(jnp.maximum(t0 - WINDOW, 0), LANES)
    w_unit = w_start // LANES
    k_w = kn_ref[0, pl.ds(w_start, n_win), LANES:2 * LANES]
    vt_w = jnp.concatenate([vt_ref[w_unit + u, LANES:2 * LANES, :] for u in range(n_win // LANES)], axis=1)
    diff = (qpos - w_start) - lax.broadcasted_iota(I32, (n_win, cols), 0)
    in_window = lax.bitcast_convert_type(diff, jnp.uint32) < jnp.uint32(WINDOW)
    s_w = jnp.where(in_window, _dot_nt(k_w, qs), NEG)
    p_w = jnp.exp(s_w - jnp.max(s_w, axis=0, keepdims=True))
    l_w = jnp.sum(p_w, axis=0, keepdims=True)
    o_w = _dot(vt_w, p_w.astype(BF16))

    n_cmp = cmp_ref.shape[3]
    s = _dot_nt(both(0), qs)
    cmp_last = lax.broadcasted_iota(I32, (n_cmp, cols), 0) * CMP_STRIDE + (CMP_BLOCK - 1)
    vis_c = cmp_last <= qpos
    s = jnp.where(vis_c, s, NEG)
    m = jnp.max(s, axis=0, keepdims=True)
    m = jnp.where(m > 0.5 * NEG, m, 0.0)
    e = jnp.where(vis_c, jnp.exp(s - m), 0.0)
    d = jnp.sum(e, axis=0, keepdims=True)
    p_c = e / jnp.where(d > 0.0, d, 1.0)
    o_c = _dot(both(1), p_c.astype(BF16))

    n_sel = selmap_ref.shape[0]
    blk = lax.broadcasted_iota(I32, (n_sel, tq), 0)
    cur = jnp.right_shift(t0 + lax.broadcasted_iota(I32, (n_sel, tq), 1),
                          SEL_BLOCK.bit_length() - 1)
    forced = (blk == 0) | (blk == cur) | (blk == cur - 1)
    chosen = []
    for g in range(NSA_GROUPS):
        psum = p_c[:, g * NSA_REP * tq:(g * NSA_REP + 1) * tq]
        for r in range(1, NSA_REP):
            psum = psum + p_c[:, (g * NSA_REP + r) * tq:(g * NSA_REP + r + 1) * tq]
        p_hi = psum.astype(BF16)
        p_lo = (psum - p_hi.astype(F32)).astype(BF16)
        imp = _dot(selmap_ref[...], p_hi) + _dot(selmap_ref[...], p_lo)
        imp = jnp.where(blk > cur, -jnp.inf, jnp.where(forced, jnp.inf, imp))
        rank = jnp.zeros((n_sel, tq), I32)
        for jp in range(n_sel):
            other = imp[jp:jp + 1, :]
            beats = (other > imp) | ((other == imp) & (blk > jp))
            rank = rank + beats.astype(I32)
        chosen += [rank < min(SEL_TOPN, n_sel)] * NSA_REP
    sel_ref[...] = jnp.where(jnp.concatenate(chosen, axis=1), qpos, -1)

    acc_ref[...] = jnp.zeros(acc_ref.shape, F32)
    units = SEL_CHUNK // LANES
    blocks = SEL_CHUNK // SEL_BLOCK
    koff = lax.broadcasted_iota(I32, (SEL_BLOCK, cols), 0)

    def sel_chunk(c, carry):
        m_old, l_old = carry
        start = pl.multiple_of(c * SEL_CHUNK, SEL_CHUNK)
        k = kn_ref[0, pl.ds(start, SEL_CHUNK), 0:LANES]
        vt = jnp.concatenate([vt_ref[c * units + u, 0:LANES, :] for u in range(units)], axis=1)
        sc = _dot_nt(k, qs)
        parts = []
        for b in range(blocks):
            limit = sel_ref[pl.ds(c * blocks + b, 1), :] - (start + b * SEL_BLOCK)
            parts.append(jnp.where(koff <= limit, sc[b * SEL_BLOCK:(b + 1) * SEL_BLOCK], NEG))
        sc = jnp.concatenate(parts, axis=0)
        m_new = jnp.maximum(m_old, jnp.max(sc, axis=0, keepdims=True))
        alpha = jnp.exp(m_old - m_new)
        p = jnp.exp(sc - m_new)
        acc_ref[...] = alpha * acc_ref[...] + _dot(vt, p.astype(BF16))
        return m_new, alpha * l_old + jnp.sum(p, axis=0, keepdims=True)

    init = (jnp.full((1, cols), NEG, F32), jnp.zeros((1, cols), F32))
    _, l_s = lax.fori_loop(0, (t0 + tq + SEL_CHUNK - 1) // SEL_CHUNK, sel_chunk, init)
    o_s = acc_ref[...]

    g_t = [g_ref[0, :, g * LANES:(g + 1) * LANES].T for g in range(NSA_GROUPS)]
    gate = lambda br: jnp.concatenate(
        [g_t[h // NSA_REP][3 * (h % NSA_REP) + br:3 * (h % NSA_REP) + br + 1, :] for h in range(NSA_HEADS)], axis=1)
    y_t = o_c * gate(0) + o_s * (gate(1) / l_s) + o_w * (gate(2) / l_w)
    for h in range(NSA_HEADS):
        y_ref[0, :, h * LANES:(h + 1) * LANES] = y_t[:, h * tq:(h + 1) * tq].T.astype(BF16)


def _nsa(q, kn, vt, kvcmp, gates, selmap, tq=256):
    B, S = q.shape[0], q.shape[1]
    n_sel = S // SEL_BLOCK
    cols = NSA_HEADS * tq
    units = S // LANES
    n_cmp = kvcmp.shape[3]
    return pl.pallas_call(
        _nsa_kernel,
        grid=(B, S // tq),
        in_specs=[
            pl.BlockSpec((1, tq, C_Q), lambda b, i: (b, i, 0)),
            pl.BlockSpec((1, S, 2 * LANES), lambda b, i: (b, 0, 0)),
            pl.BlockSpec((units, 2 * LANES, LANES), lambda b, i: (b, 0, 0)),
            pl.BlockSpec((1, 2, NSA_GROUPS, n_cmp, LANES), lambda b, i: (b, 0, 0, 0, 0)),
            pl.BlockSpec((1, tq, C_G), lambda b, i: (b, i, 0)),
            pl.BlockSpec((n_sel, n_cmp), lambda b, i: (0, 0)),
        ],
        out_specs=pl.BlockSpec((1, tq, C_Q), lambda b, i: (b, i, 0)),
        out_shape=jax.ShapeDtypeStruct((B, S, C_Q), BF16),
        scratch_shapes=[
            pltpu.VMEM((n_sel, cols), I32),
            pltpu.VMEM((LANES, cols), F32),
        ],
        compiler_params=_cparams(("arbitrary", "arbitrary"), 56),
    )(q, kn, vt, kvcmp, gates, selmap)


def _mem_kv_kernel(mem_ref, nw_ref, wk_ref, wv_ref, kw_ref, k_ref, v_ref):
    mn = _rms(mem_ref[0], nw_ref[...]).astype(BF16)
    k = _dot(mn, wk_ref[...])
    for h in range(X_HEADS):
        sl = slice(h * X_DH, (h + 1) * X_DH)
        k_ref[0, :, sl] = _rms(k[:, sl], kw_ref[...]).astype(BF16)
    v_ref[0] = _dot(mn, wv_ref[...]).astype(BF16)


def _mem_kv(mem, nw, wk, wv, kw):
    B, M, D = mem.shape
    blk = pl.BlockSpec((1, M, D), lambda b: (b, 0, 0))
    const = lambda b: (0, 0)
    return pl.pallas_call(
        _mem_kv_kernel,
        grid=(B,),
        in_specs=[blk, pl.BlockSpec((1, D), const), pl.BlockSpec((D, D), const),
                  pl.BlockSpec((D, D), const), pl.BlockSpec((1, X_DH), const)],
        out_specs=[blk, blk],
        out_shape=[jax.ShapeDtypeStruct((B, M, D), BF16)] * 2,
        compiler_params=_cparams(("arbitrary",), 32),
    )(mem, nw, wk, wv, kw)


def _post_mix_kernel(x_ref, yc_ref, yn_ref, woc_ref, won_ref, xnw_ref, xq_ref, xqw_ref, km_ref, vm_ref,
                     xo_ref, fnw_ref, pwq_ref, keys_ref, x2_ref, h3_ref, sc_ref):
    tm = x_ref.shape[0]
    n_streams = 2
    rows_per = tm // n_streams

    def stream(i):
        rs = slice(i * rows_per, (i + 1) * rows_per)
        x1 = x_ref[rs, :] + _dot(yc_ref[rs, :], woc_ref[...]) + _dot(yn_ref[rs, :], won_ref[...])
        yield
        h2 = _rms(x1, xnw_ref[...]).astype(BF16)
        qx = _dot(h2, xq_ref[...])
        yield
        heads = []
        for h in range(X_HEADS):
            sl = slice(h * X_DH, (h + 1) * X_DH)
            qh = _rms(qx[:, sl], xqw_ref[...]).astype(BF16)
            s = _dot_nt(qh, km_ref[0, :, sl])
            e = jnp.exp(s - jnp.max(s, axis=-1, keepdims=True))
            p = e / jnp.sum(e, axis=-1, keepdims=True)
            heads.append(_dot(p.astype(BF16), vm_ref[0, :, sl]))
        yield
        o = jnp.concatenate(heads, axis=1).astype(BF16)
        x2 = x1 + _dot(o, xo_ref[...])
        x2_ref[rs, :] = x2
        yield
        h3 = _rms(x2, fnw_ref[...]).astype(BF16)
        h3_ref[rs, :] = h3
        qp = _dot(h3, pwq_ref[...]).astype(BF16)
        for c in range(2 * PEER_HEADS):
            sl = slice(c * PEER_HALF, (c + 1) * PEER_HALF)
            for j in range(rows_per // LANES):
                sc_ref[c, i * (rows_per // LANES) + j] = _dot_nt(keys_ref[c], qp[j * LANES:(j + 1) * LANES, sl])
        yield

    _emit_staggered([stream(i) for i in range(n_streams)], n_stages=5)


def _post_mix(x2d, yconv, ynsa, woc, won, xnw, xq, xqw, kmem, vmem, xo, fnw, pwq, keys, seq, tm=512):
    T, D = x2d.shape
    M = kmem.shape[1]
    n_sc = 2 * PEER_HEADS * PEER_NKEYS
    tile = lambda i: (i, 0)
    const = lambda i: (0, 0)
    per_batch = lambda i: ((i * tm) // seq, 0, 0)
    return pl.pallas_call(
        _post_mix_kernel,
        grid=(T // tm,),
        in_specs=[
            pl.BlockSpec((tm, D), tile),
            pl.BlockSpec((tm, CONV_WIDTH), tile),
            pl.BlockSpec((tm, C_Q), tile),
            pl.BlockSpec((CONV_WIDTH, D), const),
            pl.BlockSpec((C_Q, D), const),
            pl.BlockSpec((1, D), const),
            pl.BlockSpec((D, D), const),
            pl.BlockSpec((1, X_DH), const),
            pl.BlockSpec((1, M, D), per_batch),
            pl.BlockSpec((1, M, D), per_batch),
            pl.BlockSpec((D, D), const),
            pl.BlockSpec((1, D), const),
            pl.BlockSpec((D, n_sc), const),
            pl.BlockSpec((2 * PEER_HEADS, PEER_NKEYS, PEER_HALF), lambda i: (0, 0, 0)),
        ],
        out_specs=[pl.BlockSpec((tm, D), tile), pl.BlockSpec((tm, D), tile),
                   pl.BlockSpec((2 * PEER_HEADS, tm // LANES, PEER_NKEYS, LANES), lambda i: (0, i, 0, 0))],
        out_shape=[
            jax.ShapeDtypeStruct((T, D), F32),
            jax.ShapeDtypeStruct((T, D), BF16),
            jax.ShapeDtypeStruct((2 * PEER_HEADS, T // LANES, PEER_NKEYS, LANES), F32),
        ],
        compiler_params=_cparams(("arbitrary",), 56),
    )(x2d, yconv, ynsa, woc, won, xnw, xq, xqw, kmem, vmem, xo, fnw, pwq, keys)


STAIR = [(a, b) for a in range(PEER_TOPK) for b in range(PEER_TOPK // (a + 1))]
N_STAIR_VREGS = -(-len(STAIR) // SUBLANES)


def _stair_vregs(axis, k):
    return sorted({r // SUBLANES for r, ab in enumerate(STAIR) if ab[axis] == k})


def _stair_tables():
    rows = N_STAIR_VREGS * SUBLANES
    pad = rows - len(STAIR)
    ta = jnp.asarray([float(a) for a, _ in STAIR] + [-1.0] * pad, F32)
    tb = jnp.asarray([float(b) for _, b in STAIR] + [-1.0] * pad, F32)
    return jnp.broadcast_to(ta[:, None], (rows, LANES)), jnp.broadcast_to(tb[:, None], (rows, LANES))


def _make_topk(sc_ref, ta_ref, tb_ref, s_out, e_out):
    K = PEER_TOPK
    NV = N_STAIR_VREGS
    kidx = lax.broadcasted_iota(I32, (PEER_NKEYS, LANES), 0).astype(F32)
    ridx = [(lax.broadcasted_iota(I32, (SUBLANES, LANES), 0) + SUBLANES * j).astype(F32) for j in range(NV)]
    ta = [ta_ref[SUBLANES * j:SUBLANES * (j + 1), :] for j in range(NV)]
    tb = [tb_ref[SUBLANES * j:SUBLANES * (j + 1), :] for j in range(NV)]
    zero = jnp.zeros((SUBLANES, LANES), F32)

    def top1(s):
        m = jnp.max(s, axis=0, keepdims=True)
        idx = jnp.min(jnp.where(s == m, kidx, float(PEER_NKEYS)), axis=0, keepdims=True)
        return m, idx, jnp.where(kidx == idx, -jnp.inf, s)

    def sorted_top(s, table, axis):
        val = [zero] * NV
        key = [zero] * NV
        for k in range(K):
            m, idx, s = top1(s)
            for j in _stair_vregs(axis, k):
                hit = table[j] == float(k)
                val[j] = jnp.where(hit, m, val[j])
                key[j] = jnp.where(hit, idx, key[j])
        return val, key

    def top_half(h, g, which):
        return sorted_top(sc_ref[2 * h + which, g], (ta, tb)[which], which)

    def select(first, second):
        (v1, k1), (v2, k2) = first, second
        cand = [jnp.where(ta[j] >= 0.0, v1[j] + v2[j], -jnp.inf) for j in range(NV)]
        ce = [k1[j] * float(PEER_NKEYS) + k2[j] for j in range(NV)]
        sub = lax.broadcasted_iota(I32, (SUBLANES, LANES), 0)
        best_s = [zero] * (K // SUBLANES)
        best_e = [zero] * (K // SUBLANES)
        for k in range(K):
            mx = cand[0]
            for j in range(1, NV):
                mx = jnp.maximum(mx, cand[j])
            m = jnp.max(mx, axis=0, keepdims=True)
            ix = jnp.where(cand[0] == m, ridx[0], float(NV * SUBLANES))
            for j in range(1, NV):
                ix = jnp.minimum(ix, jnp.where(cand[j] == m, ridx[j], float(NV * SUBLANES)))
            idx = jnp.min(ix, axis=0, keepdims=True)
            hits = [ridx[j] == idx for j in range(NV)]
            es = jnp.where(hits[0], ce[0], 0.0)
            for j in range(1, NV):
                es = es + jnp.where(hits[j], ce[j], 0.0)
            cand = [jnp.where(hits[j], -jnp.inf, cand[j]) for j in range(NV)]
            here = sub == (k % SUBLANES)
            best_s[k // SUBLANES] = jnp.where(here, m, best_s[k // SUBLANES])
            best_e[k // SUBLANES] = jnp.where(here, jnp.sum(es, axis=0, keepdims=True), best_e[k // SUBLANES])
        return jnp.concatenate(best_s, axis=0), jnp.concatenate(best_e, axis=0)

    def store(h, best):
        rows = pl.ds(pl.multiple_of(h * K, K), K)
        s_out[rows, :] = best[0]
        e_out[rows, :] = best[1]

    def finish(rows, ei_dst, ej_dst, gate_dst):
        s_all = s_out[...]
        gates = []
        for h in range(PEER_HEADS):
            sh = s_all[h * K:(h + 1) * K]
            ex = jnp.exp(sh - sh[0:1])
            gates.append(ex / jnp.sum(ex, axis=0, keepdims=True))
        gate_dst[rows, :] = jnp.concatenate(gates, axis=0).T
        e_int = e_out[...].T.astype(I32)
        ei_dst[rows, :] = jnp.right_shift(e_int, PEER_NKEYS.bit_length() - 1)
        ej_dst[rows, :] = jnp.bitwise_and(e_int, PEER_NKEYS - 1)

    return top_half, select, store, finish


G_ROWS = PEER_NKEYS // 2
G_PITCH = G_ROWS + 4
HI_MASK = 0xFFFF0000


def _peer_kernel(sc_ref, ta_ref, tb_ref, h_ref, wd_ref, wu_ref, x2_ref, out_ref,
                 g_ref, ei_ref, ej_ref, gate_ref, s_out, e_out):
    tm = h_ref.shape[0]
    ec = wd_ref.shape[1]
    ipc = ec // LANES
    half = ipc // 2
    t = pl.program_id(0)
    c = pl.program_id(1)
    n_grp = tm // LANES
    steps_per_grp = PEER_EXPERTS // ec // n_grp
    heads_per_step = PEER_HEADS // steps_per_grp
    rd = (t + 1) % 2
    wr = t % 2
    ei_rd, ej_rd, gate_rd = ei_ref.at[rd], ej_ref.at[rd], gate_ref.at[rd]

    @pl.when((t == 0) & (c == 0))
    def _first_step():
        ei_ref[...] = jnp.zeros(ei_ref.shape, I32)
        ej_ref[...] = jnp.zeros(ej_ref.shape, I32)
        gate_ref[...] = jnp.zeros(gate_ref.shape, F32)

    @pl.when(c == 0)
    def _scatter():
        out_ref[...] = x2_ref[...]
        as_bf16 = lambda v: v.astype(F32).astype(BF16)
        i_of_m = as_bf16(lax.broadcasted_iota(I32, (PEER_NKEYS, 2 * LANES), 0))
        n = as_bf16(lax.broadcasted_iota(I32, (LANES, LANES), 0))
        one = jnp.ones((LANES, LANES), BF16)
        zero = jnp.zeros((LANES, LANES), BF16)

        def pair(u, carry):
            ta_, tb_ = pl.ds(2 * u, 1), pl.ds(2 * u + 1, 1)
            ri = as_bf16(jnp.concatenate([ei_rd[ta_, :], ei_rd[tb_, :]], axis=1))
            rg = (0.5 * jnp.concatenate([gate_rd[ta_, :], gate_rd[tb_, :]], axis=1)).astype(BF16)
            c_t = jnp.where(i_of_m == ri, rg, jnp.zeros_like(rg))
            q_a = jnp.where(n == as_bf16(ej_rd[ta_, :]), one, zero)
            q_b = jnp.where(n == as_bf16(ej_rd[tb_, :]), one, zero)
            q_t = jnp.concatenate([jnp.concatenate([q_a, zero], axis=1),
                                   jnp.concatenate([zero, q_b], axis=1)], axis=0)
            g2 = _dot_nt(c_t, q_t).astype(BF16)
            for tok in range(2):
                word = pltpu.bitcast(g2[:, tok * LANES:(tok + 1) * LANES], jnp.uint32)
                g_ref[pl.ds((2 * u + tok) * G_PITCH, G_ROWS), :] = word
            return carry

        lax.fori_loop(0, tm // 2, pair, 0, unroll=64)

    top_half, select, store, finish = _make_topk(sc_ref, ta_ref, tb_ref, s_out, e_out)
    grp = c // steps_per_grp
    h0 = (c % steps_per_grp) * heads_per_step

    def experts_slice(k):
        a = _dot(h_ref[...], wd_ref[:, 2 * k * LANES:2 * (k + 1) * LANES]).astype(BF16)
        word = g_ref[pl.ds(c * half + k, tm, stride=G_PITCH), :]
        g_lo = lax.bitcast_convert_type(jnp.left_shift(word, 16), F32).astype(BF16)
        g_hi = lax.bitcast_convert_type(jnp.bitwise_and(word, jnp.uint32(HI_MASK)), F32).astype(BF16)
        return jnp.concatenate([_gelu2(a[:, 0:LANES]) * g_lo, _gelu2(a[:, LANES:2 * LANES]) * g_hi], axis=1)

    assert half == 2 * heads_per_step
    halves, z = [], []
    for k in range(half):
        halves.append(top_half(h0 + k // 2, grp, k % 2))
        z.append(experts_slice(k))
    best = [select(halves[2 * hh], halves[2 * hh + 1]) for hh in range(heads_per_step - 1)]
    out_ref[...] += _dot(jnp.concatenate(z, axis=1), wu_ref[...])
    best.append(select(halves[-2], halves[-1]))
    for hh in range(heads_per_step):
        store(h0 + hh, best[hh])

    @pl.when(c % steps_per_grp == steps_per_grp - 1)
    def _group_done():
        rows = pl.ds(pl.multiple_of(grp * LANES, LANES), LANES)
        finish(rows, ei_ref.at[wr], ej_ref.at[wr], gate_ref.at[wr])


def _peer(sc, h3, wd_t, wu, x2, tm=512, ec=2048):
    T, D = h3.shape
    tm = min(tm, T)
    n_tiles = T // tm
    n_grp = tm // LANES
    n_chunks = wu.shape[0] // ec
    assert n_chunks % n_grp == 0 and PEER_HEADS % (n_chunks // n_grp) == 0
    ta, tb = _stair_tables()
    prev = lambda t, c: (jnp.maximum(t - 1, 0), 0)
    return pl.pallas_call(
        _peer_kernel,
        grid=(n_tiles + 1, n_chunks),
        in_specs=[
            pl.BlockSpec((2 * PEER_HEADS, n_grp, PEER_NKEYS, LANES),
                         lambda t, c: (0, jnp.minimum(t, n_tiles - 1), 0, 0)),
            pl.BlockSpec(ta.shape, lambda t, c: (0, 0)),
            pl.BlockSpec(tb.shape, lambda t, c: (0, 0)),
            pl.BlockSpec((tm, D), prev),
            pl.BlockSpec((D, ec), lambda t, c: (0, c)),
            pl.BlockSpec((ec, D), lambda t, c: (c, 0)),
            pl.BlockSpec((tm, D), prev),
        ],
        out_specs=pl.BlockSpec((tm, D), prev),
        out_shape=jax.ShapeDtypeStruct((T, D), F32),
        scratch_shapes=[
            pltpu.VMEM((tm * G_PITCH, LANES), jnp.uint32),
            pltpu.VMEM((2, tm, LANES), I32),
            pltpu.VMEM((2, tm, LANES), I32),
            pltpu.VMEM((2, tm, LANES), F32),
            pltpu.VMEM((PEER_HEADS * PEER_TOPK, LANES), F32),
            pltpu.VMEM((PEER_HEADS * PEER_TOPK, LANES), F32),
        ],
        compiler_params=_cparams(("arbitrary", "arbitrary"), 60),
    )(sc, ta, tb, h3, wd_t, wu, x2)


def _pad_half(a, g):
    z = jnp.zeros_like(a)
    return jnp.concatenate([a, z] if g == 0 else [z, a], axis=-1)


def kernel(x, mem, mix_norm_w, w_in, conv_w, conv_b, cmp_pe, cmp_w1, cmp_w2, q_norm_w, k_norm_w, w_out,
           xattn_norm_w, mem_norm_w, xq, xk, xv, xo, xq_norm_w, xk_norm_w, ffn_norm_w, peer_wq, peer_keys,
           peer_down, peer_up):
    B, S, D = x.shape
    T = B * S
    l = 0
    G, R, dh = NSA_GROUPS, NSA_REP, NSA_DH

    w = w_in[l]
    o_q = C_CONV
    o_kv = o_q + NSA_HEADS * dh
    o_g = o_kv + C_KV
    wq = w[:, o_q:o_kv].reshape(D, G, R, dh)
    wq_pad = jnp.concatenate([_pad_half(wq[:, g], g).reshape(D, R * LANES) for g in range(G)], axis=1)
    wg = w[:, o_g:].reshape(D, G, R * 3)
    wg_pad = jnp.pad(wg, ((0, 0), (0, 0), (0, LANES - R * 3))).reshape(D, G * LANES)
    wkv = w[:, o_kv:o_g].reshape(D, 6, LANES)
    wkn = wkv[:, jnp.array([0, 1, 2, 4])].reshape(D, C_KN)
    wvt = wkv[:, jnp.array([3, 5])].reshape(D, 2 * LANES).T.astype(BF16)
    w_all = jnp.concatenate([w[:, :o_q], wq_pad, wkn, wg_pad], axis=1).astype(BF16)
    qw = q_norm_w[l] * dh ** -0.5
    qw_pad = jnp.concatenate([jnp.tile(_pad_half(qw, g), R) for g in range(G)]).reshape(1, C_Q)
    kw = jnp.stack([jnp.tile(k_norm_w[l, 1], G), jnp.tile(k_norm_w[l, 2], G)])

    yconv, q, kvc, kn, vt, gates = _in_proj(
        x.reshape(T, D), mix_norm_w[l].reshape(1, D), w_all, wvt, conv_w[l], conv_b[l].reshape(1, CONV_WIDTH),
        qw_pad, kw, S)

    n_rows = S // CMP_STRIDE
    kvg = jnp.array([0, 0, 1, 1])
    pe = jnp.concatenate([cmp_pe[l, kv] for kv in (0, 0, 1, 1)], axis=-1)
    w1 = cmp_w1[l].reshape(2, CMP_BLOCK, dh, dh)[kvg]
    w1 = jnp.einsum('qp,qjdh->jqdph', jnp.eye(4, dtype=F32), w1).reshape(CMP_BLOCK, 4 * dh, 4 * dh).astype(BF16)
    w2k = jnp.zeros((4 * dh, G * LANES), F32)
    w2vt = jnp.zeros((G * LANES, 4 * dh), F32)
    for g in range(G):
        o = g * LANES + g * dh
        w2k = w2k.at[g * dh:(g + 1) * dh, o:o + dh].set(cmp_w2[l, 0])
        w2vt = w2vt.at[o:o + dh, (G + g) * dh:(G + g + 1) * dh].set(cmp_w2[l, 1].T)
    knw = jnp.stack([_pad_half(k_norm_w[l, 0], g) for g in range(G)]).reshape(G, 1, LANES)
    kvcmp = _compress(kvc, pe, w1, w2k.astype(BF16), w2vt.astype(BF16), knw, S)

    n_sel = S // SEL_BLOCK
    cmp_start = jnp.arange(n_rows) * CMP_STRIDE
    sel_start = jnp.arange(n_sel) * SEL_BLOCK
    selmap_t = ((cmp_start[None, :] < sel_start[:, None] + SEL_BLOCK)
                & (cmp_start[None, :] + CMP_BLOCK > sel_start[:, None])
                & (jnp.arange(n_rows)[None, :] < n_rows - 1)).astype(BF16)
    ynsa = _nsa(q.reshape(B, S, C_Q), kn.reshape(B, S, 2 * LANES), vt, kvcmp, gates.reshape(B, S, C_G), selmap_t)

    kmem, vmem = _mem_kv(mem, mem_norm_w[l].reshape(1, D), xk[l].astype(BF16), xv[l].astype(BF16),
                         xk_norm_w[l].reshape(1, X_DH))

    wo = w_out[l]
    won = wo[CONV_WIDTH:].reshape(G, R, dh, D)
    won_pad = jnp.concatenate(
        [jnp.concatenate([won[g], jnp.zeros_like(won[g])] if g == 0 else [jnp.zeros_like(won[g]), won[g]],
                         axis=1).reshape(R * LANES, D) for g in range(G)], axis=0).astype(BF16)
    keys = peer_keys[l].reshape(2 * PEER_HEADS, PEER_NKEYS, PEER_HALF).astype(BF16)
    x2, h3, sc = _post_mix(
        x.reshape(T, D), yconv, ynsa.reshape(T, C_Q), wo[:CONV_WIDTH].astype(BF16), won_pad,
        xattn_norm_w[l].reshape(1, D), xq[l].astype(BF16),
        (xq_norm_w[l] * X_DH ** -0.5).reshape(1, X_DH), kmem, vmem, xo[l].astype(BF16),
        ffn_norm_w[l].reshape(1, D), peer_wq[l].astype(BF16), keys, S)

    out = _peer(sc, h3, peer_down[l].T.astype(BF16), peer_up[l].astype(BF16), x2)
    return out.reshape(B, S, D)
```
